```python
import jax, jax.numpy as jnp
from jax import lax
import numpy as np

D_MODEL = 1024
BATCH = 4
SEQ = 4096
DEPTH = 1

D_ATTN = D_MODEL // 2
HEAD_DIM_A = 64
N_HEADS_A = D_ATTN // HEAD_DIM_A
DILATED_PAIRS = ((128, 1), (512, 4), (2048, 16))
D_REC = D_MODEL - D_ATTN
HGRN_EXPAND = 128
N_HEADS_R = D_REC // HGRN_EXPAND
HEAD_DIM_R = D_REC // N_HEADS_R
HGRN_CHUNK = 32
D_FORGET = N_HEADS_R * HGRN_EXPAND
D_IN = 3 * D_ATTN + 2 * D_FORGET + 2 * D_REC
N_EXPERTS = 32
TOP_K = 4
D_FF = D_MODEL
SWIGLU_LIMIT = 7.0
SWIGLU_ALPHA = 1.702
MOE_BLOCK = 128
EPS = 1e-6

kernel_name = "hymba_style_dilated_attn_hgrn2_moe_block"


def rms_norm(x, w):
    xf = x.astype(jnp.float32)
    y = xf * lax.rsqrt(jnp.mean(xf * xf, axis=-1, keepdims=True) + EPS)
    return (y * w.astype(jnp.float32)).astype(x.dtype)


def modulate(n, shift, scale):
    return n * (1.0 + scale[:, None, :]) + shift[:, None, :]


def dilated_causal_attention(q, k, v, window, dil):
    B, S, H, Dh = q.shape
    bw = window // dil
    n = S // dil
    nb = -(-n // bw)
    pad = nb * bw - n

    def sub(t):
        t = t.reshape(B, n, dil, H, Dh).transpose(0, 2, 1, 3, 4)
        return jnp.pad(t, ((0, 0), (0, 0), (0, pad), (0, 0), (0, 0)))

    def band(t):
        tp = jnp.pad(t, ((0, 0), (0, 0), (bw, 0), (0, 0), (0, 0)))
        prev = tp[:, :, :nb * bw].reshape(B, dil, nb, bw, H, Dh)
        cur = t.reshape(B, dil, nb, bw, H, Dh)
        return jnp.concatenate([prev, cur], axis=3)

    qb = sub(q).reshape(B, dil, nb, bw, H, Dh).astype(jnp.float32)
    kb = band(sub(k)).astype(jnp.float32)
    vb = band(sub(v)).astype(jnp.float32)

    r = jnp.arange(bw)[:, None]
    j = jnp.arange(2 * bw)[None, :]
    diff = bw + r - j
    kidx = (jnp.arange(nb)[:, None, None] - 1) * bw + j[None]
    mask = (diff >= 0) & (diff <= bw) & (kidx >= 0)

    s = jnp.einsum('bgiqhd,bgikhd->bgihqk', qb, kb) * (Dh ** -0.5)
    s = jnp.where(mask[None, None, :, None], s, -jnp.inf)
    m = jnp.max(s, axis=-1, keepdims=True)
    p = jnp.exp(s - m)
    l = jnp.sum(p, axis=-1, keepdims=True)
    o = jnp.einsum('bgihqk,bgikhd->bgiqhd', p, vb) / jnp.swapaxes(l, 3, 4)
    lse = jnp.swapaxes((m + jnp.log(l))[..., 0], 3, 4)

    o = o.reshape(B, dil, nb * bw, H, Dh)[:, :, :n].transpose(0, 2, 1, 3, 4).reshape(B, S, H, Dh)
    lse = lse.reshape(B, dil, nb * bw, H)[:, :, :n].transpose(0, 2, 1, 3).reshape(B, S, H)
    return o, lse


def hgrn2_chunkwise(q, k, v, log_f):
    B, S, H, DK = q.shape
    DV = v.shape[-1]
    C = HGRN_CHUNK
    nc = S // C

    def chunk(t):
        return t.astype(jnp.float32).reshape(B, nc, C, H, t.shape[-1]).transpose(0, 3, 1, 2, 4)

    q, k, v, log_f = chunk(q), chunk(k), chunk(v), chunk(log_f)
    b = jnp.cumsum(log_f, axis=3)
    qe = q * jnp.exp(b)
    ke = k * jnp.exp(-b)
    causal = jnp.tril(jnp.ones((C, C), dtype=bool))
    a = jnp.where(causal, jnp.einsum('bhncd,bhnsd->bhncs', qe, ke), 0.0)
    o_intra = jnp.einsum('bhncs,bhnse->bhnce', a, v)

    b_last = b[:, :, :, -1:, :]
    delta = jnp.einsum('bhncd,bhnce->bhnde', k * jnp.exp(b_last - b), v)
    decay = jnp.exp(b_last[:, :, :, 0, :])

    def step(state, inp):
        dec, dl = inp
        return dec[..., None] * state + dl, state

    s0 = jnp.zeros((B, H, DK, DV), jnp.float32)
    _, s_prev = lax.scan(step, s0, (jnp.moveaxis(decay, 2, 0), jnp.moveaxis(delta, 2, 0)))
    s_prev = jnp.moveaxis(s_prev, 0, 2)
    o_inter = jnp.einsum('bhncd,bhnde->bhnce', qe, s_prev)
    return (o_intra + o_inter).transpose(0, 2, 3, 1, 4).reshape(B, S, H, DV)


def hybrid_mixer(h, w_in, attn_norm_w, lb, hgrn_norm_w, w_out):
    B, S, _ = h.shape
    proj = h @ w_in
    widths = [D_ATTN, D_ATTN, D_ATTN, D_FORGET, D_FORGET, D_REC, D_REC]
    cuts = [int(v) for v in np.cumsum(widths)[:-1]]
    qa, ka, va, qr, fr, ir, gr = jnp.split(proj, cuts, axis=-1)

    qa = qa.reshape(B, S, N_HEADS_A, HEAD_DIM_A)
    ka = ka.reshape(B, S, N_HEADS_A, HEAD_DIM_A)
    va = va.reshape(B, S, N_HEADS_A, HEAD_DIM_A)
    outs, lses = [], []
    for window, dil in DILATED_PAIRS:
        o, lse = dilated_causal_attention(qa, ka, va, window, dil)
        outs.append(o)
        lses.append(lse)
    wts = jax.nn.softmax(jnp.stack(lses, axis=0), axis=0)
    o_a = jnp.einsum('nbsh,nbshd->bshd', wts, jnp.stack(outs, axis=0)).astype(h.dtype)
    y_a = rms_norm(o_a, attn_norm_w.reshape(N_HEADS_A, HEAD_DIM_A)).reshape(B, S, D_ATTN)

    q_r = jax.nn.silu(qr.astype(jnp.float32)).reshape(B, S, N_HEADS_R, HGRN_EXPAND)
    lb_h = lb.reshape(N_HEADS_R, HGRN_EXPAND)
    f = lb_h + (1.0 - lb_h) * jax.nn.sigmoid(fr.astype(jnp.float32).reshape(B, S, N_HEADS_R, HGRN_EXPAND))
    v_r = ir.reshape(B, S, N_HEADS_R, HEAD_DIM_R)
    o_r = hgrn2_chunkwise(q_r, 1.0 - f, v_r, jnp.log(f)).astype(h.dtype)
    o_r = rms_norm(o_r, hgrn_norm_w.reshape(N_HEADS_R, HEAD_DIM_R)).reshape(B, S, D_REC)
    y_r = o_r * jax.nn.silu(gr)

    return jnp.concatenate([y_a, y_r], axis=-1) @ w_out


def moe_ffn(h, w_router, b_router, w_gu, b_gu, w_down, b_down):
    B, S, D = h.shape
    T = B * S
    ht = h.reshape(T, D)
    logits = (ht @ w_router + b_router).astype(jnp.float32)
    top_val, top_idx = lax.top_k(logits, TOP_K)
    gates = jax.nn.softmax(top_val, axis=-1)

    flat_e = top_idx.reshape(-1)
    flat_tok = jnp.arange(T * TOP_K, dtype=jnp.int32) // TOP_K
    flat_gate = gates.reshape(-1)
    order = jnp.argsort(flat_e)
    se, stok, sgate = flat_e[order], flat_tok[order], flat_gate[order]
    counts = jnp.zeros((N_EXPERTS,), jnp.int32).at[flat_e].add(1)
    padded = ((counts + MOE_BLOCK - 1) // MOE_BLOCK) * MOE_BLOCK
    pend = jnp.cumsum(padded)
    pstart = pend - padded
    sstart = jnp.cumsum(counts) - counts
    rank = jnp.arange(T * TOP_K, dtype=jnp.int32) - sstart[se]
    dest = pstart[se] + rank
    P = T * TOP_K + N_EXPERTS * MOE_BLOCK
    nblk = P // MOE_BLOCK
    tok_pad = jnp.full((P,), T, jnp.int32).at[dest].set(stok)
    gate_pad = jnp.zeros((P,), h.dtype).at[dest].set(sgate.astype(h.dtype))
    blk_e = jnp.minimum(jnp.searchsorted(pend, jnp.arange(nblk) * MOE_BLOCK, side='right'),
                        N_EXPERTS - 1).astype(jnp.int32)
    x_pad = jnp.concatenate([ht, jnp.zeros((1, D), ht.dtype)], axis=0)
    xb = x_pad[tok_pad].reshape(nblk, MOE_BLOCK, D)

    def expert_block(args):
        xblk, e = args
        hg = xblk @ w_gu[e] + b_gu[e]
        x_glu, x_lin = hg[:, :D_FF], hg[:, D_FF:]
        x_glu = jnp.minimum(x_glu, SWIGLU_LIMIT)
        x_lin = jnp.clip(x_lin, -SWIGLU_LIMIT, SWIGLU_LIMIT)
        act = x_glu * jax.nn.sigmoid(SWIGLU_ALPHA * x_glu) * (x_lin + 1.0)
        return act @ w_down[e] + b_down[e]

    yb = lax.map(expert_block, (xb, blk_e))
    y = yb.reshape(P, D) * gate_pad[:, None]
    out = jnp.zeros((T + 1, D), y.dtype).at[tok_pad].add(y)[:T]
    return out.reshape(B, S, D)


def setup_inputs(seed: int = 0) -> dict:
    key = jax.random.key(seed)
    ks = jax.random.split(key, 20)

    def nrm(k, shape, scale):
        return jax.random.normal(k, shape, jnp.float32) * scale

    return {
        "x": nrm(ks[0], (BATCH, SEQ, D_MODEL), 1.0),
        "c": nrm(ks[1], (BATCH, D_MODEL), 1.0),
        "w_ada": nrm(ks[2], (DEPTH, D_MODEL, 6 * D_MODEL), 0.5 * D_MODEL ** -0.5),
        "b_ada": nrm(ks[3], (DEPTH, 6 * D_MODEL), 0.02),
        "g_pre_mix": 1.0 + nrm(ks[4], (DEPTH, D_MODEL), 0.05),
        "g_post_mix": 1.0 + nrm(ks[5], (DEPTH, D_MODEL), 0.05),
        "w_in": nrm(ks[6], (DEPTH, D_MODEL, D_IN), D_MODEL ** -0.5),
        "attn_norm_w": 1.0 + nrm(ks[7], (DEPTH, D_ATTN), 0.05),
        "hgrn_lb": nrm(ks[8], (DEPTH + 1, D_FORGET), 0.5),
        "hgrn_norm_w": 1.0 + nrm(ks[9], (DEPTH, D_REC), 0.05),
        "w_out": nrm(ks[10], (DEPTH, D_MODEL, D_MODEL), D_MODEL ** -0.5),
        "g_pre_ffn": 1.0 + nrm(ks[11], (DEPTH, D_MODEL), 0.05),
        "g_post_ffn": 1.0 + nrm(ks[12], (DEPTH, D_MODEL), 0.05),
        "w_router": nrm(ks[13], (DEPTH, D_MODEL, N_EXPERTS), D_MODEL ** -0.5),
        "b_router": nrm(ks[14], (DEPTH, N_EXPERTS), 0.01),
        "w_gu": nrm(ks[15], (DEPTH, N_EXPERTS, D_MODEL, 2 * D_FF), D_MODEL ** -0.5),
        "b_gu": nrm(ks[16], (DEPTH, N_EXPERTS, 2 * D_FF), 0.02),
        "w_down": nrm(ks[17], (DEPTH, N_EXPERTS, D_FF, D_MODEL), D_FF ** -0.5),
        "b_down": nrm(ks[18], (DEPTH, N_EXPERTS, D_MODEL), 0.02),
    }


def reference(x, c, w_ada, b_ada, g_pre_mix, g_post_mix, w_in, attn_norm_w, hgrn_lb,
              hgrn_norm_w, w_out, g_pre_ffn, g_post_ffn, w_router, b_router, w_gu, b_gu,
              w_down, b_down):
    lb_table = jnp.cumsum(jax.nn.softmax(hgrn_lb.astype(jnp.float32), axis=0), axis=0)
    cond = jax.nn.silu(c)
    for l in range(DEPTH):
        mod = cond @ w_ada[l] + b_ada[l]
        shift_m, scale_m, gate_m, shift_f, scale_f, gate_f = jnp.split(mod, 6, axis=-1)
        h = modulate(rms_norm(x, g_pre_mix[l]), shift_m, scale_m)
        y = hybrid_mixer(h, w_in[l], attn_norm_w[l], lb_table[l], hgrn_norm_w[l], w_out[l])
        x = x + gate_m[:, None, :] * rms_norm(y, g_post_mix[l])
        h = modulate(rms_norm(x, g_pre_ffn[l]), shift_f, scale_f)
        y = moe_ffn(h, w_router[l], b_router[l], w_gu[l], b_gu[l], w_down[l], b_down[l])
        x = x + gate_f[:, None, :] * rms_norm(y, g_post_ffn[l])
    return x
```

```python
import functools

import jax
import jax.numpy as jnp
from jax import lax
from jax.experimental import pallas as pl
from jax.experimental.pallas import tpu as pltpu
from jax.experimental.pallas import tpu_sc as plsc

F32 = jnp.float32
BF16 = jnp.bfloat16
I32 = jnp.int32
HIGHEST = lax.Precision.HIGHEST

D_MODEL = 1024
BATCH = 4
SEQ = 4096
N_TOK = BATCH * SEQ
D_ATTN = 512
HEAD_DIM_A = 64
DILATIONS = (1, 4, 16)
ATT_BLOCK = 128
D_REC = 512
HGRN_CHUNK = 32
HGRN_SUPER = 256
N_EXPERTS = 32
TOP_K = 4
D_FF = 1024
SWIGLU_LIMIT = 7.0
SWIGLU_ALPHA = 1.702
EPS = 1e-6

LANES = 128
SUBLANES = 8
ROW_TILES = D_MODEL // LANES

TOK_TILE = 512
MOE_BLOCK = 512
N_SLOTS = N_TOK * TOP_K + N_EXPERTS * MOE_BLOCK
N_BLOCKS = N_SLOTS // MOE_BLOCK
SC_WINDOW = 32
VMEM_LIMIT = 56 * 1024 * 1024


def _sigmoid(x):
    return 1.0 / (1.0 + jnp.exp(-x))


def _dot(a, b):
    return jnp.dot(a, b, preferred_element_type=F32)


def _dot_nt(a, b):
    return lax.dot_general(a, b, (((1,), (1,)), ((), ())), preferred_element_type=F32)


def _ada_kernel(c_ref, w_ref, b_ref, o_ref):
    c = c_ref[...]
    cond = c * _sigmoid(c)
    o_ref[...] = jnp.dot(cond, w_ref[...], precision=HIGHEST,
                         preferred_element_type=F32) + b_ref[...]


def _ada_mod(c_pad, w_ada, b_ada):
    n = w_ada.shape[1]
    tn = 1536
    return pl.pallas_call(
        _ada_kernel,
        grid=(n // tn,),
        in_specs=[pl.BlockSpec((SUBLANES, D_MODEL), lambda j: (0, 0)),
                  pl.BlockSpec((D_MODEL, tn), lambda j: (0, j)),
                  pl.BlockSpec((1, tn), lambda j: (0, j))],
        out_specs=pl.BlockSpec((SUBLANES, tn), lambda j: (0, j)),
        out_shape=jax.ShapeDtypeStruct((SUBLANES, n), F32),
        compiler_params=pltpu.CompilerParams(vmem_limit_bytes=VMEM_LIMIT),
        name="ada_mod",
    )(c_pad, w_ada, b_ada)


def _rms(x):
    return x * lax.rsqrt(jnp.mean(x * x, axis=-1, keepdims=True) + EPS)


def _inproj_kernel(x_ref, mod_ref, g_ref, w_ref, *out_refs):
    mod = mod_ref[0]
    shift, scale = mod[0:1], mod[1:2]
    h = _rms(x_ref[...]) * g_ref[...] * (1.0 + scale) + shift
    hb = h.astype(BF16)
    for j, o_ref in enumerate(out_refs):
        r = _dot(hb, w_ref[:, j * D_ATTN:(j + 1) * D_ATTN])
        if j == 0:
            r = r * (HEAD_DIM_A ** -0.5)
        o_ref[...] = r


def _inproj(x2, mod3, g_pre, w_in_bf16):
    n_out = w_in_bf16.shape[1] // D_ATTN
    tiles_per_seq = SEQ // TOK_TILE
    return pl.pallas_call(
        _inproj_kernel,
        grid=(N_TOK // TOK_TILE,),
        in_specs=[pl.BlockSpec((TOK_TILE, D_MODEL), lambda i: (i, 0)),
                  pl.BlockSpec((1, 6, D_MODEL), lambda i: (i // tiles_per_seq, 0, 0)),
                  pl.BlockSpec((1, D_MODEL), lambda i: (0, 0)),
                  pl.BlockSpec(w_in_bf16.shape, lambda i: (0, 0))],
        out_specs=[pl.BlockSpec((TOK_TILE, D_ATTN), lambda i: (i, 0))] * n_out,
        out_shape=[jax.ShapeDtypeStruct((N_TOK, D_ATTN), F32)] * n_out,
        compiler_params=pltpu.CompilerParams(vmem_limit_bytes=VMEM_LIMIT),
        name="inproj",
    )(x2, mod3, g_pre, w_in_bf16)


def _attn_kernel(q_ref, k_ref, v_ref, nw_ref, o_ref, o_scr, lse_scr):
    bw = ATT_BLOCK
    lane = lax.broadcasted_iota(I32, (bw, LANES), 1)
    head0 = lane < HEAD_DIM_A
    diff0 = (lax.broadcasted_iota(I32, (bw, 2 * bw), 0)
             - lax.broadcasted_iota(I32, (bw, 2 * bw), 1))
    units = SEQ // bw

    for n, dil in enumerate(DILATIONS):
        def unit(u, carry, n=n, dil=dil):
            r = u % dil
            blk = u // dil
            kblk = jnp.maximum(blk - 1, 0)
            qstart = blk * (bw * dil) + r
            kstart = kblk * (bw * dil) + r
            q = q_ref[pl.ds(qstart, bw, stride=dil), :]
            k = k_ref[pl.ds(kstart, 2 * bw, stride=dil), :].astype(BF16)
            v = v_ref[pl.ds(kstart, 2 * bw, stride=dil), :].astype(BF16)
            diff = diff0 + (blk - kblk) * bw
            valid = (diff >= 0) & (diff <= bw)
            outs, lses = [], []
            for hsel in (True, False):
                qh = jnp.where(head0 == hsel, q, 0.0).astype(BF16)
                s = jnp.where(valid, _dot_nt(qh, k), -1e30)
                m = jnp.max(s, axis=-1, keepdims=True)
                p = jnp.exp(s - m)
                l = jnp.sum(p, axis=-1, keepdims=True)
                outs.append(_dot(p.astype(BF16), v) / l)
                lses.append(m + jnp.log(l))
            o_scr[n, pl.ds(qstart, bw, stride=dil), :] = jnp.where(head0, outs[0], outs[1])
            lse_scr[n, pl.ds(qstart, bw, stride=dil), :] = jnp.where(
                head0, jnp.broadcast_to(lses[0], (bw, LANES)), jnp.broadcast_to(lses[1], (bw, LANES)))
            return carry

        lax.fori_loop(0, units, unit, 0)

    rows = 256
    hi = lax.broadcasted_iota(I32, (LANES, LANES), 0) // HEAD_DIM_A
    hj = lax.broadcasted_iota(I32, (LANES, LANES), 1) // HEAD_DIM_A
    head_sum = jnp.where(hi == hj, 1.0, 0.0).astype(F32)
    nw = nw_ref[...]

    def merge(t, carry):
        sl = pl.ds(pl.multiple_of(t * rows, rows), rows)
        l0, l1, l2 = lse_scr[0, sl, :], lse_scr[1, sl, :], lse_scr[2, sl, :]
        mx = jnp.maximum(jnp.maximum(l0, l1), l2)
        w0, w1, w2 = jnp.exp(l0 - mx), jnp.exp(l1 - mx), jnp.exp(l2 - mx)
        o = (w0 * o_scr[0, sl, :] + w1 * o_scr[1, sl, :] + w2 * o_scr[2, sl, :]) / (w0 + w1 + w2)
        ms = jnp.dot(o * o, head_sum, precision=HIGHEST,
                     preferred_element_type=F32) * (1.0 / HEAD_DIM_A)
        o_ref[sl, :] = (o * lax.rsqrt(ms + EPS) * nw).astype(o_ref.dtype)
        return carry

    lax.fori_loop(0, SEQ // rows, merge, 0)


def _attention(q, k, v, attn_norm_w):
    hp = D_ATTN // LANES
    blk = pl.BlockSpec((None, SEQ, LANES), lambda b, h: (b, 0, h))
    return pl.pallas_call(
        _attn_kernel,
        grid=(BATCH, hp),
        in_specs=[blk, blk, blk, pl.BlockSpec((1, LANES), lambda b, h: (0, h))],
        out_specs=blk,
        out_shape=jax.ShapeDtypeStruct((BATCH, SEQ, D_ATTN), BF16),
        scratch_shapes=[pltpu.VMEM((len(DILATIONS), SEQ, LANES), F32),
                        pltpu.VMEM((len(DILATIONS), SEQ, LANES), F32)],
        compiler_params=pltpu.CompilerParams(vmem_limit_bytes=VMEM_LIMIT),
        name="dilated_attn",
    )(q, k, v, attn_norm_w)


def _hgrn_kernel(qr_ref, fr_ref, ir_ref, gr_ref, lb_ref, nw_ref, o_ref):
    sup, c = HGRN_SUPER, HGRN_CHUNK
    nch = sup // c
    lbp = lb_ref[...]
    lmx = jnp.max(lbp, axis=0, keepdims=True)
    ex = jnp.exp(lbp - lmx)
    lb = ex[0:1] / (ex[0:1] + ex[1:2])
    nw = nw_ref[...]

    ri = lax.broadcasted_iota(I32, (sup, sup), 0)
    ci = lax.broadcasted_iota(I32, (sup, sup), 1)
    same_chunk = (ri // c) == (ci // c)
    causal = same_chunk & (ci <= ri)
    tri = jnp.where(causal, 1.0, 0.0).astype(F32)
    last = jnp.where(same_chunk & (ci % c == c - 1), 1.0, 0.0).astype(F32)
    row_chunk = lax.broadcasted_iota(I32, (sup, LANES), 0) // c

    def step(t, st):
        sl = pl.ds(pl.multiple_of(t * sup, sup), sup)
        q = qr_ref[sl, :]
        q = q * _sigmoid(q)
        f = lb + (1.0 - lb) * _sigmoid(fr_ref[sl, :])
        kk = 1.0 - f
        logf = jnp.log(f)
        v = ir_ref[sl, :]
        vb = v.astype(BF16)
        b = jnp.dot(tri, logf, precision=HIGHEST, preferred_element_type=F32)
        b_last = jnp.dot(last, b, precision=HIGHEST, preferred_element_type=F32)
        qe = q * jnp.exp(b)
        ke = kk * jnp.exp(-b)
        kd = (kk * jnp.exp(b_last - b)).astype(BF16)
        qeb = qe.astype(BF16)
        a = jnp.where(causal, _dot_nt(qeb, ke.astype(BF16)), 0.0)
        o_intra = _dot(a.astype(BF16), vb)
        vt = v.T.astype(BF16)
        dec = jnp.exp(b_last)
        o_parts = []
        for ch in range(nch):
            rs = slice(ch * c, (ch + 1) * c)
            o_parts.append(_dot_nt(qeb[rs], st.astype(BF16)))
            delta_t = _dot(vt, jnp.where(row_chunk == ch, kd, jnp.zeros_like(kd)))
            st = st * dec[ch * c:ch * c + 1] + delta_t
        o = o_intra + jnp.concatenate(o_parts, axis=0)
        y = _rms(o) * nw
        g = gr_ref[sl, :]
        o_ref[sl, :] = (y * (g * _sigmoid(g))).astype(o_ref.dtype)
        return st

    lax.fori_loop(0, SEQ // sup, step, jnp.zeros((LANES, LANES), F32))


def _hgrn(qr, fr, ir, gr, hgrn_lb, hgrn_norm_w):
    nh = D_REC // LANES
    blk = pl.BlockSpec((None, SEQ, LANES), lambda b, h: (b, 0, h))
    return pl.pallas_call(
        _hgrn_kernel,
        grid=(BATCH, nh),
        in_specs=[blk, blk, blk, blk,
                  pl.BlockSpec((2, LANES), lambda b, h: (0, h)),
                  pl.BlockSpec((1, LANES), lambda b, h: (0, h))],
        out_specs=blk,
        out_shape=jax.ShapeDtypeStruct((BATCH, SEQ, D_REC), BF16),
        compiler_params=pltpu.CompilerParams(vmem_limit_bytes=VMEM_LIMIT),
        name="hgrn2",
    )(qr, fr, ir, gr, hgrn_lb, hgrn_norm_w)


def _store_row_tiles(ref, val):
    rows = val.shape[0]
    for j in range(ROW_TILES):
        ref[pl.ds(j, rows, stride=ROW_TILES), :] = val[:, j * LANES:(j + 1) * LANES]


def _load_row_tiles(ref, rows):
    return jnp.concatenate(
        [ref[pl.ds(j, rows, stride=ROW_TILES), :] for j in range(ROW_TILES)], axis=1)


def _mid_kernel(ya_ref, yr_ref, x_ref, mod_ref, gpost_ref, gpre_ref, wo_ref, wr_ref, br_ref,
                x1_ref, h2_ref, idx_ref, gate_ref, rank_ref, cnt_ref, carry_ref):
    i = pl.program_id(0)

    @pl.when(i == 0)
    def _():
        carry_ref[...] = jnp.zeros_like(carry_ref)

    mod = mod_ref[0]
    gate_m, shift_f, scale_f = mod[2:3], mod[3:4], mod[4:5]
    y = _dot(ya_ref[...], wo_ref[0:D_ATTN, :]) + _dot(yr_ref[...], wo_ref[D_ATTN:, :])
    x1 = x_ref[...] + gate_m * (_rms(y) * gpost_ref[...])
    x1_ref[...] = x1
    h2 = _rms(x1) * gpre_ref[...] * (1.0 + scale_f) + shift_f
    _store_row_tiles(h2_ref, h2)

    logits = jnp.dot(h2, wr_ref[...], precision=HIGHEST, preferred_element_type=F32) + br_ref[...]
    tm = logits.shape[0]
    eidx = lax.broadcasted_iota(I32, (tm, N_EXPERTS), 1)
    work = logits
    vals, idxs = [], []
    onehot = jnp.zeros((tm, N_EXPERTS), F32)
    for _ in range(TOP_K):
        m = jnp.max(work, axis=-1, keepdims=True)
        sel = jnp.min(jnp.where(work == m, eidx, N_EXPERTS), axis=-1, keepdims=True)
        hit = eidx == sel
        work = jnp.where(hit, -jnp.inf, work)
        onehot = jnp.where(hit, 1.0, onehot)
        vals.append(m)
        idxs.append(sel)
    ex = [jnp.exp(vv - vals[0]) for vv in vals]
    den = ex[0] + ex[1] + ex[2] + ex[3]

    ri = lax.broadcasted_iota(I32, (tm, tm), 0)
    ci = lax.broadcasted_iota(I32, (tm, tm), 1)
    strict_lower = jnp.where(ci < ri, 1.0, 0.0).astype(BF16)
    before = _dot(strict_lower, onehot.astype(BF16)) + carry_ref[...]
    lane4 = lax.broadcasted_iota(I32, (tm, TOP_K), 1)
    idx_o = jnp.zeros((tm, TOP_K), I32)
    gate_o = jnp.zeros((tm, TOP_K), F32)
    rank_o = jnp.zeros((tm, TOP_K), I32)
    for kk in range(TOP_K):
        rk = jnp.sum(jnp.where(eidx == idxs[kk], before, 0.0), axis=-1, keepdims=True)
        idx_o = jnp.where(lane4 == kk, idxs[kk], idx_o)
        gate_o = jnp.where(lane4 == kk, ex[kk] / den, gate_o)
        rank_o = jnp.where(lane4 == kk, rk.astype(I32), rank_o)
    idx_ref[...] = idx_o
    gate_ref[...] = gate_o
    rank_ref[...] = rank_o
    total = carry_ref[...] + jnp.sum(onehot, axis=0, keepdims=True)
    carry_ref[...] = total
    cnt_ref[...] = total.astype(I32)


def _mid(ya, yr, x2, mod3, g_post, g_pre, w_out_bf16, w_router, b_router):
    tiles_per_seq = SEQ // TOK_TILE
    tok = lambda w: pl.BlockSpec((TOK_TILE, w), lambda i: (i, 0))
    const = lambda s: pl.BlockSpec(s, lambda i: (0,) * len(s))
    return pl.pallas_call(
        _mid_kernel,
        grid=(N_TOK // TOK_TILE,),
        in_specs=[tok(D_ATTN), tok(D_REC), tok(D_MODEL),
                  pl.BlockSpec((1, 6, D_MODEL), lambda i: (i // tiles_per_seq, 0, 0)),
                  const((1, D_MODEL)), const((1, D_MODEL)),
                  const((D_MODEL, D_MODEL)), const((D_MODEL, N_EXPERTS)), const((1, N_EXPERTS))],
        out_specs=[tok(D_MODEL),
                   pl.BlockSpec((TOK_TILE * ROW_TILES, LANES), lambda i: (i, 0)),
                   tok(TOP_K), tok(TOP_K), tok(TOP_K), const((1, N_EXPERTS))],
        out_shape=[jax.ShapeDtypeStruct((N_TOK, D_MODEL), F32),
                   jax.ShapeDtypeStruct((N_TOK * ROW_TILES, LANES), F32),
                   jax.ShapeDtypeStruct((N_TOK, TOP_K), I32),
                   jax.ShapeDtypeStruct((N_TOK, TOP_K), F32),
                   jax.ShapeDtypeStruct((N_TOK, TOP_K), I32),
                   jax.ShapeDtypeStruct((1, N_EXPERTS), I32)],
        scratch_shapes=[pltpu.VMEM((1, N_EXPERTS), F32)],
        compiler_params=pltpu.CompilerParams(dimension_semantics=("arbitrary",),
                                             vmem_limit_bytes=VMEM_LIMIT),
        name="outproj_router",
    )(ya, yr, x2, mod3, g_post, g_pre, w_out_bf16, w_router, b_router)


def _sc_mesh():
    return plsc.VectorSubcoreMesh(core_axis_name="c", subcore_axis_name="s")


def _sc_worker_count():
    info = plsc.get_sparse_core_info()
    return info.num_cores, info.num_cores * info.num_subcores


def _sc_dispatch(h_rows, dest_win):
    n_cores, n_workers = _sc_worker_count()
    n_win = N_TOK // SC_WINDOW
    per_worker = n_win // n_workers

    @functools.partial(
        pl.kernel, mesh=_sc_mesh(),
        out_type=jax.ShapeDtypeStruct((N_SLOTS, ROW_TILES, LANES), F32),
        scratch_types=[pltpu.VMEM((TOP_K, SC_WINDOW), I32),
                       pltpu.VMEM((SC_WINDOW, ROW_TILES, LANES), F32),
                       pltpu.SemaphoreType.DMA],
        name="sc_dispatch")
    def run(h_hbm, dest_hbm, xs_hbm, idx_v, rows_v, sem):
        wid = lax.axis_index("s") * n_cores + lax.axis_index("c")

        @pl.loop(0, per_worker)
        def _(j):
            win = wid * per_worker + j
            pltpu.sync_copy(dest_hbm.at[win], idx_v)
            pltpu.sync_copy(h_hbm.at[pl.ds(win * SC_WINDOW, SC_WINDOW)], rows_v)
            copies = [pltpu.async_copy(rows_v, xs_hbm.at[idx_v.at[kk]], sem)
                      for kk in range(TOP_K)]
            for cp in copies:
                cp.wait()

    return run(h_rows, dest_win)


def _sc_collect(y_rows, dest_win):
    n_cores, n_workers = _sc_worker_count()
    n_win = N_TOK // SC_WINDOW
    per_worker = n_win // n_workers

    @functools.partial(
        pl.kernel, mesh=_sc_mesh(),
        out_type=jax.ShapeDtypeStruct((TOP_K, N_TOK, ROW_TILES, LANES), F32),
        scratch_types=[pltpu.VMEM((TOP_K, SC_WINDOW), I32),
                       pltpu.VMEM((SC_WINDOW, ROW_TILES, LANES), F32),
                       pltpu.SemaphoreType.DMA],
        name="sc_collect")
    def run(y_hbm, dest_hbm, yg_hbm, idx_v, rows_v, sem):
        wid = lax.axis_index("s") * n_cores + lax.axis_index("c")

        @pl.loop(0, per_worker)
        def _(j):
            win = wid * per_worker + j
            pltpu.sync_copy(dest_hbm.at[win], idx_v)
            for kk in range(TOP_K):
                pltpu.async_copy(y_hbm.at[idx_v.at[kk]], rows_v, sem).wait()
                pltpu.sync_copy(rows_v, yg_hbm.at[kk, pl.ds(win * SC_WINDOW, SC_WINDOW)])

    return run(y_rows, dest_win)


def _expert_kernel(be_ref, nu_ref, x_ref, wgu_ref, bgu_ref, wd_ref, bd_ref, y_ref, wgu16, wd16):
    i = pl.program_id(0)
    e = be_ref[i]
    prev = be_ref[jnp.maximum(i - 1, 0)]

    @pl.when((i == 0) | (e != prev))
    def _():
        rows = 128

        def cast(r, carry):
            sl = pl.ds(pl.multiple_of(r * rows, rows), rows)
            wgu16[sl, :] = wgu_ref[0, sl, :].astype(BF16)
            wd16[sl, :] = wd_ref[0, sl, :].astype(BF16)
            return carry

        lax.fori_loop(0, D_MODEL // rows, cast, 0)

    @pl.when(i < nu_ref[0])
    def _():
        x = _load_row_tiles(x_ref, MOE_BLOCK).astype(BF16)
        bgu = bgu_ref[0]
        glu = _dot(x, wgu16[:, 0:D_FF]) + bgu[:, 0:D_FF]
        lin = _dot(x, wgu16[:, D_FF:]) + bgu[:, D_FF:]
        glu = jnp.minimum(glu, SWIGLU_LIMIT)
        lin = jnp.clip(lin, -SWIGLU_LIMIT, SWIGLU_LIMIT)
        act = glu * _sigmoid(SWIGLU_ALPHA * glu) * (lin + 1.0)
        y = _dot(act.astype(BF16), wd16[...]) + bd_ref[0]
        _store_row_tiles(y_ref, y)


def _experts(blk_e, n_used, xs2, w_gu, b_gu3, w_down, b_down3):
    row_blk = pl.BlockSpec((MOE_BLOCK * ROW_TILES, LANES),
                           lambda i, be, nu: (jnp.minimum(i, nu[0] - 1), 0))
    grid_spec = pltpu.PrefetchScalarGridSpec(
        num_scalar_prefetch=2,
        grid=(N_BLOCKS,),
        in_specs=[row_blk,
                  pl.BlockSpec((1, D_MODEL, 2 * D_FF), lambda i, be, nu: (be[i], 0, 0)),
                  pl.BlockSpec((1, 1, 2 * D_FF), lambda i, be, nu: (be[i], 0, 0)),
                  pl.BlockSpec((1, D_FF, D_MODEL), lambda i, be, nu: (be[i], 0, 0)),
                  pl.BlockSpec((1, 1, D_MODEL), lambda i, be, nu: (be[i], 0, 0))],
        out_specs=row_blk,
        scratch_shapes=[pltpu.VMEM((D_MODEL, 2 * D_FF), BF16),
                        pltpu.VMEM((D_FF, D_MODEL), BF16)],
    )
    return pl.pallas_call(
        _expert_kernel,
        grid_spec=grid_spec,
        out_shape=jax.ShapeDtypeStruct((N_SLOTS * ROW_TILES, LANES), F32),
        compiler_params=pltpu.CompilerParams(dimension_semantics=("arbitrary",),
                                             vmem_limit_bytes=VMEM_LIMIT),
        name="experts",
    )(blk_e, n_used, xs2, w_gu, b_gu3, w_down, b_down3)


def _final_kernel(yg_ref, gate_ref, x1_ref, mod_ref, gpost_ref, o_ref):
    gates = gate_ref[...]
    y = None
    for kk in range(TOP_K):
        part = _load_row_tiles(yg_ref.at[kk], TOK_TILE) * gates[:, kk:kk + 1]
        y = part if y is None else y + part
    gate_f = mod_ref[0][5:6]
    o_ref[...] = x1_ref[...] + gate_f * (_rms(y) * gpost_ref[...])


def _final(yg3, gates, x1, mod3, g_post):
    tiles_per_seq = SEQ // TOK_TILE
    return pl.pallas_call(
        _final_kernel,
        grid=(N_TOK // TOK_TILE,),
        in_specs=[pl.BlockSpec((TOP_K, TOK_TILE * ROW_TILES, LANES), lambda i: (0, i, 0)),
                  pl.BlockSpec((TOK_TILE, TOP_K), lambda i: (i, 0)),
                  pl.BlockSpec((TOK_TILE, D_MODEL), lambda i: (i, 0)),
                  pl.BlockSpec((1, 6, D_MODEL), lambda i: (i // tiles_per_seq, 0, 0)),
                  pl.BlockSpec((1, D_MODEL), lambda i: (0, 0))],
        out_specs=pl.BlockSpec((TOK_TILE, D_MODEL), lambda i: (i, 0)),
        out_shape=jax.ShapeDtypeStruct((N_TOK, D_MODEL), F32),
        compiler_params=pltpu.CompilerParams(vmem_limit_bytes=VMEM_LIMIT),
        name="combine_final",
    )(yg3, gates, x1, mod3, g_post)


def _routing_tables(idx, rank, counts):
    counts = counts.reshape(N_EXPERTS)
    padded = ((counts + MOE_BLOCK - 1) // MOE_BLOCK) * MOE_BLOCK
    pend = jnp.cumsum(padded)
    pstart = pend - padded
    dest = pstart[idx] + rank
    dest_win = dest.reshape(N_TOK // SC_WINDOW, SC_WINDOW, TOP_K).transpose(0, 2, 1)
    n_used = (pend[-1] // MOE_BLOCK).astype(I32).reshape(1)
    blk_start = jnp.arange(N_BLOCKS, dtype=I32) * MOE_BLOCK
    blk_e = jnp.sum(blk_start[:, None] >= pend[None, :], axis=1).astype(I32)
    last_e = jnp.max(jnp.where(counts > 0, jnp.arange(N_EXPERTS, dtype=I32), 0))
    blk_e = jnp.minimum(blk_e, last_e)
    return dest_win, blk_e, n_used


def kernel(x, c, w_ada, b_ada, g_pre_mix, g_post_mix, w_in, attn_norm_w, hgrn_lb, hgrn_norm_w,
           w_out, g_pre_ffn, g_post_ffn, w_router, b_router, w_gu, b_gu, w_down, b_down):
    x2 = x.reshape(N_TOK, D_MODEL)
    c_pad = jnp.pad(c, ((0, SUBLANES - BATCH), (0, 0)))
    mod = _ada_mod(c_pad, w_ada[0], b_ada)
    mod3 = mod[:BATCH].reshape(BATCH, 6, D_MODEL)

    qa, ka, va, qr, fr, ir, gr = _inproj(x2, mod3, g_pre_mix, w_in[0].astype(BF16))
    seq = lambda t: t.reshape(BATCH, SEQ, t.shape[-1])
    ya = _attention(seq(qa), seq(ka), seq(va), attn_norm_w)
    yr = _hgrn(seq(qr), seq(fr), seq(ir), seq(gr), hgrn_lb, hgrn_norm_w)

    x1, h2, idx, gates, rank, counts = _mid(
        ya.reshape(N_TOK, D_ATTN), yr.reshape(N_TOK, D_REC), x2, mod3, g_post_mix, g_pre_ffn,
        w_out[0].astype(BF16), w_router[0], b_router)

    dest_win, blk_e, n_used = _routing_tables(idx, rank, counts)
    xs = _sc_dispatch(h2.reshape(N_TOK, ROW_TILES, LANES), dest_win)
    ys = _experts(blk_e, n_used, xs.reshape(N_SLOTS * ROW_TILES, LANES),
                  w_gu[0], b_gu[0].reshape(N_EXPERTS, 1, 2 * D_FF),
                  w_down[0], b_down[0].reshape(N_EXPERTS, 1, D_MODEL))
    yg = _sc_collect(ys.reshape(N_SLOTS, ROW_TILES, LANES), dest_win)
    out = _final(yg.reshape(TOP_K, N_TOK * ROW_TILES, LANES), gates, x1, mod3, g_post_ffn)
    return out.reshape(BATCH, SEQ, D_MODEL)
```

```python
import functools

import jax
import jax.numpy as jnp
from jax import lax
from jax.experimental import pallas as pl
from jax.experimental.pallas import tpu as pltpu
from jax.experimental.pallas import tpu_sc as plsc

F32 = jnp.float32
BF16 = jnp.bfloat16
I32 = jnp.int32
HIGHEST = lax.Precision.HIGHEST

D_MODEL = 1024
BATCH = 4
SEQ = 4096
N_TOK = BATCH * SEQ
D_ATTN = 512
HEAD_DIM_A = 64
DILATIONS = (1, 4, 16)
ATT_BLOCK = 128
D_REC = 512
HGRN_CHUNK = 32
HGRN_SUPER = 256
N_EXPERTS = 32
TOP_K = 4
D_FF = 1024
SWIGLU_LIMIT = 7.0
SWIGLU_ALPHA = 1.702
EPS = 1e-6

LANES = 128
SUBLANES = 8
ROW_TILES = D_MODEL // LANES

TOK_TILE = 512
MOE_BLOCK = 512
N_SLOTS = N_TOK * TOP_K + N_EXPERTS * MOE_BLOCK
N_BLOCKS = N_SLOTS // MOE_BLOCK
SC_WINDOW = 32
VMEM_LIMIT = 56 * 1024 * 1024


def _sigmoid(x):
    return 1.0 / (1.0 + jnp.exp(-x))


def _dot(a, b):
    return jnp.dot(a, b, preferred_element_type=F32)


def _dot_nt(a, b):
    return lax.dot_general(a, b, (((1,), (1,)), ((), ())), preferred_element_type=F32)


def _ada_kernel(c_ref, w_ref, b_ref, o_ref):
    c = c_ref[...]
    cond = c * _sigmoid(c)
    o_ref[...] = jnp.dot(cond, w_ref[...], precision=HIGHEST,
                         preferred_element_type=F32) + b_ref[...]


def _ada_mod(c_pad, w_ada, b_ada):
    n = w_ada.shape[1]
    tn = 1536
    return pl.pallas_call(
        _ada_kernel,
        grid=(n // tn,),
        in_specs=[pl.BlockSpec((SUBLANES, D_MODEL), lambda j: (0, 0)),
                  pl.BlockSpec((D_MODEL, tn), lambda j: (0, j)),
                  pl.BlockSpec((1, tn), lambda j: (0, j))],
        out_specs=pl.BlockSpec((SUBLANES, tn), lambda j: (0, j)),
        out_shape=jax.ShapeDtypeStruct((SUBLANES, n), F32),
        compiler_params=pltpu.CompilerParams(vmem_limit_bytes=VMEM_LIMIT),
        name="ada_mod",
    )(c_pad, w_ada, b_ada)


def _rms(x):
    return x * lax.rsqrt(jnp.mean(x * x, axis=-1, keepdims=True) + EPS)


def _inproj_kernel(x_ref, mod_ref, g_ref, w_ref, *out_refs):
    mod = mod_ref[0]
    shift, scale = mod[0:1], mod[1:2]
    h = _rms(x_ref[...]) * g_ref[...] * (1.0 + scale) + shift
    hb = h.astype(BF16)
    for j, o_ref in enumerate(out_refs):
        r = _dot(hb, w_ref[:, j * D_ATTN:(j + 1) * D_ATTN])
        if j == 0:
            r = r * (HEAD_DIM_A ** -0.5)
        o_ref[...] = r


def _inproj(x2, mod3, g_pre, w_in_bf16):
    n_out = w_in_bf16.shape[1] // D_ATTN
    tiles_per_seq = SEQ // TOK_TILE
    return pl.pallas_call(
        _inproj_kernel,
        grid=(N_TOK // TOK_TILE,),
        in_specs=[pl.BlockSpec((TOK_TILE, D_MODEL), lambda i: (i, 0)),
                  pl.BlockSpec((1, 6, D_MODEL), lambda i: (i // tiles_per_seq, 0, 0)),
                  pl.BlockSpec((1, D_MODEL), lambda i: (0, 0)),
                  pl.BlockSpec(w_in_bf16.shape, lambda i: (0, 0))],
        out_specs=[pl.BlockSpec((TOK_TILE, D_ATTN), lambda i: (i, 0))] * n_out,
        out_shape=[jax.ShapeDtypeStruct((N_TOK, D_ATTN), F32)] * n_out,
        compiler_params=pltpu.CompilerParams(vmem_limit_bytes=VMEM_LIMIT),
        name="inproj",
    )(x2, mod3, g_pre, w_in_bf16)


def _attn_kernel(q_ref, k_ref, v_ref, nw_ref, o_ref, o_scr, lse_scr):
    bw = ATT_BLOCK
    lane = lax.broadcasted_iota(I32, (bw, LANES), 1)
    head0 = lane < HEAD_DIM_A
    diff0 = (lax.broadcasted_iota(I32, (bw, 2 * bw), 0)
             - lax.broadcasted_iota(I32, (bw, 2 * bw), 1))
    units = SEQ // bw

    for n, dil in enumerate(DILATIONS):
        def unit(u, carry, n=n, dil=dil):
            r = u % dil
            blk = u // dil
            kblk = jnp.maximum(blk - 1, 0)
            qstart = blk * (bw * dil) + r
            kstart = kblk * (bw * dil) + r
            q = q_ref[pl.ds(qstart, bw, stride=dil), :]
            k = k_ref[pl.ds(kstart, 2 * bw, stride=dil), :].astype(BF16)
            v = v_ref[pl.ds(kstart, 2 * bw, stride=dil), :].astype(BF16)
            diff = diff0 + (blk - kblk) * bw
            valid = (diff >= 0) & (diff <= bw)
            outs, lses = [], []
            for hsel in (True, False):
                qh = jnp.where(head0 == hsel, q, 0.0).astype(BF16)
                s = jnp.where(valid, _dot_nt(qh, k), -1e30)
                m = jnp.max(s, axis=-1, keepdims=True)
                p = jnp.exp(s - m)
                l = jnp.sum(p, axis=-1, keepdims=True)
                outs.append(_dot(p.astype(BF16), v) / l)
                lses.append(m + jnp.log(l))
            o_scr[n, pl.ds(qstart, bw, stride=dil), :] = jnp.where(head0, outs[0], outs[1])
            lse_scr[n, pl.ds(qstart, bw, stride=dil), :] = jnp.where(
                head0, jnp.broadcast_to(lses[0], (bw, LANES)), jnp.broadcast_to(lses[1], (bw, LANES)))
            return carry

        lax.fori_loop(0, units, unit, 0, unroll=4)

    rows = 256
    hi = lax.broadcasted_iota(I32, (LANES, LANES), 0) // HEAD_DIM_A
    hj = lax.broadcasted_iota(I32, (LANES, LANES), 1) // HEAD_DIM_A
    head_sum = jnp.where(hi == hj, 1.0, 0.0).astype(F32)
    nw = nw_ref[...]

    def merge(t, carry):
        sl = pl.ds(pl.multiple_of(t * rows, rows), rows)
        l0, l1, l2 = lse_scr[0, sl, :], lse_scr[1, sl, :], lse_scr[2, sl, :]
        mx = jnp.maximum(jnp.maximum(l0, l1), l2)
        w0, w1, w2 = jnp.exp(l0 - mx), jnp.exp(l1 - mx), jnp.exp(l2 - mx)
        o = (w0 * o_scr[0, sl, :] + w1 * o_scr[1, sl, :] + w2 * o_scr[2, sl, :]) / (w0 + w1 + w2)
        ms = jnp.dot(o * o, head_sum, precision=HIGHEST,
                     preferred_element_type=F32) * (1.0 / HEAD_DIM_A)
        o_ref[sl, :] = (o * lax.rsqrt(ms + EPS) * nw).astype(o_ref.dtype)
        return carry

    lax.fori_loop(0, SEQ // rows, merge, 0)


def _attention(q, k, v, attn_norm_w):
    hp = D_ATTN // LANES
    blk = pl.BlockSpec((None, SEQ, LANES), lambda b, h: (b, 0, h))
    return pl.pallas_call(
        _attn_kernel,
        grid=(BATCH, hp),
        in_specs=[blk, blk, blk, pl.BlockSpec((1, LANES), lambda b, h: (0, h))],
        out_specs=blk,
        out_shape=jax.ShapeDtypeStruct((BATCH, SEQ, D_ATTN), BF16),
        scratch_shapes=[pltpu.VMEM((len(DILATIONS), SEQ, LANES), F32),
                        pltpu.VMEM((len(DILATIONS), SEQ, LANES), F32)],
        compiler_params=pltpu.CompilerParams(vmem_limit_bytes=VMEM_LIMIT),
        name="dilated_attn",
    )(q, k, v, attn_norm_w)


def _hgrn_kernel(qr_ref, fr_ref, ir_ref, gr_ref, lb_ref, nw_ref, o_ref):
    sup, c = HGRN_SUPER, HGRN_CHUNK
    nch = sup // c
    lbp = lb_ref[...]
    lmx = jnp.max(lbp, axis=0, keepdims=True)
    ex = jnp.exp(lbp - lmx)
    lb = ex[0:1] / (ex[0:1] + ex[1:2])
    nw = nw_ref[...]

    ri = lax.broadcasted_iota(I32, (sup, sup), 0)
    ci = lax.broadcasted_iota(I32, (sup, sup), 1)
    same_chunk = (ri // c) == (ci // c)
    causal = same_chunk & (ci <= ri)
    cum_ops = jnp.concatenate([jnp.where(causal, 1.0, 0.0), jnp.where(same_chunk, 1.0, 0.0)],
                              axis=0).astype(BF16)
    row_chunk = lax.broadcasted_iota(I32, (sup, LANES), 0) // c

    def step(t, st):
        sl = pl.ds(pl.multiple_of(t * sup, sup), sup)
        q = qr_ref[sl, :]
        q = q * _sigmoid(q)
        f = lb + (1.0 - lb) * _sigmoid(fr_ref[sl, :])
        kk = 1.0 - f
        logf = jnp.log(f)
        v = ir_ref[sl, :]
        vb = v.astype(BF16)
        logf_hi = logf.astype(BF16)
        logf_lo = (logf - logf_hi.astype(F32)).astype(BF16)
        sums = _dot(cum_ops, logf_hi) + _dot(cum_ops, logf_lo)
        b, b_last = sums[0:sup], sums[sup:]
        qe = q * jnp.exp(b)
        ke = kk * jnp.exp(-b)
        kd = (kk * jnp.exp(b_last - b)).astype(BF16)
        qeb = qe.astype(BF16)
        a = jnp.where(causal, _dot_nt(qeb, ke.astype(BF16)), 0.0)
        o_intra = _dot(a.astype(BF16), vb)
        vt = v.T.astype(BF16)
        dec = jnp.exp(b_last)
        o_parts = []
        for ch in range(nch):
            rs = slice(ch * c, (ch + 1) * c)
            o_parts.append(_dot_nt(qeb[rs], st.astype(BF16)))
            delta_t = _dot(vt, jnp.where(row_chunk == ch, kd, jnp.zeros_like(kd)))
            st = st * dec[ch * c:ch * c + 1] + delta_t
        o = o_intra + jnp.concatenate(o_parts, axis=0)
        y = _rms(o) * nw
        g = gr_ref[sl, :]
        o_ref[sl, :] = (y * (g * _sigmoid(g))).astype(o_ref.dtype)
        return st

    lax.fori_loop(0, SEQ // sup, step, jnp.zeros((LANES, LANES), F32), unroll=2)


def _hgrn(qr, fr, ir, gr, hgrn_lb, hgrn_norm_w):
    nh = D_REC // LANES
    blk = pl.BlockSpec((None, SEQ, LANES), lambda b, h: (b, 0, h))
    return pl.pallas_call(
        _hgrn_kernel,
        grid=(BATCH, nh),
        in_specs=[blk, blk, blk, blk,
                  pl.BlockSpec((2, LANES), lambda b, h: (0, h)),
                  pl.BlockSpec((1, LANES), lambda b, h: (0, h))],
        out_specs=blk,
        out_shape=jax.ShapeDtypeStruct((BATCH, SEQ, D_REC), BF16),
        compiler_params=pltpu.CompilerParams(vmem_limit_bytes=VMEM_LIMIT),
        name="hgrn2",
    )(qr, fr, ir, gr, hgrn_lb, hgrn_norm_w)


def _store_row_tiles(ref, val):
    rows = val.shape[0]
    for j in range(ROW_TILES):
        ref[pl.ds(j, rows, stride=ROW_TILES), :] = val[:, j * LANES:(j + 1) * LANES]


def _load_row_tiles(ref, rows):
    return jnp.concatenate(
        [ref[pl.ds(j, rows, stride=ROW_TILES), :] for j in range(ROW_TILES)], axis=1)


def _mid_kernel(ya_ref, yr_ref, x_ref, mod_ref, gpost_ref, gpre_ref, wo_ref, wr_ref, br_ref,
                x1_ref, h2_ref, idx_ref, gate_ref, rank_ref, cnt_ref, carry_ref):
    i = pl.program_id(0)

    @pl.when(i == 0)
    def _():
        carry_ref[...] = jnp.zeros_like(carry_ref)

    mod = mod_ref[0]
    gate_m, shift_f, scale_f = mod[2:3], mod[3:4], mod[4:5]
    y = _dot(ya_ref[...], wo_ref[0:D_ATTN, :]) + _dot(yr_ref[...], wo_ref[D_ATTN:, :])
    x1 = x_ref[...] + gate_m * (_rms(y) * gpost_ref[...])
    x1_ref[...] = x1
    h2 = _rms(x1) * gpre_ref[...] * (1.0 + scale_f) + shift_f
    _store_row_tiles(h2_ref, h2)

    wr = wr_ref[...]
    h2_hi, wr_hi = h2.astype(BF16), wr.astype(BF16)
    h2_lo = (h2 - h2_hi.astype(F32)).astype(BF16)
    wr_lo = (wr - wr_hi.astype(F32)).astype(BF16)
    logits = _dot(h2_hi, wr_hi) + _dot(h2_lo, wr_hi) + _dot(h2_hi, wr_lo) + br_ref[...]
    tm = logits.shape[0]
    eidx = lax.broadcasted_iota(I32, (tm, N_EXPERTS), 1)
    work = logits
    vals, idxs = [], []
    onehot = jnp.zeros((tm, N_EXPERTS), F32)
    for _ in range(TOP_K):
        m = jnp.max(work, axis=-1, keepdims=True)
        sel = jnp.min(jnp.where(work == m, eidx, N_EXPERTS), axis=-1, keepdims=True)
        hit = eidx == sel
        work = jnp.where(hit, -jnp.inf, work)
        onehot = jnp.where(hit, 1.0, onehot)
        vals.append(m)
        idxs.append(sel)
    ex = [jnp.exp(vv - vals[0]) for vv in vals]
    den = ex[0] + ex[1] + ex[2] + ex[3]

    ri = lax.broadcasted_iota(I32, (tm, tm), 0)
    ci = lax.broadcasted_iota(I32, (tm, tm), 1)
    strict_lower = jnp.where(ci < ri, 1.0, 0.0).astype(BF16)
    before = _dot(strict_lower, onehot.astype(BF16)) + carry_ref[...]
    lane4 = lax.broadcasted_iota(I32, (tm, TOP_K), 1)
    idx_o = jnp.zeros((tm, TOP_K), I32)
    gate_o = jnp.zeros((tm, TOP_K), F32)
    rank_o = jnp.zeros((tm, TOP_K), I32)
    for kk in range(TOP_K):
        rk = jnp.sum(jnp.where(eidx == idxs[kk], before, 0.0), axis=-1, keepdims=True)
        idx_o = jnp.where(lane4 == kk, idxs[kk], idx_o)
        gate_o = jnp.where(lane4 == kk, ex[kk] / den, gate_o)
        rank_o = jnp.where(lane4 == kk, rk.astype(I32), rank_o)
    idx_ref[...] = idx_o
    gate_ref[...] = gate_o
    rank_ref[...] = rank_o
    total = carry_ref[...] + jnp.sum(onehot, axis=0, keepdims=True)
    carry_ref[...] = total
    cnt_ref[...] = total.astype(I32)


def _mid(ya, yr, x2, mod3, g_post, g_pre, w_out_bf16, w_router, b_router):
    tiles_per_seq = SEQ // TOK_TILE
    tok = lambda w: pl.BlockSpec((TOK_TILE, w), lambda i: (i, 0))
    const = lambda s: pl.BlockSpec(s, lambda i: (0,) * len(s))
    return pl.pallas_call(
        _mid_kernel,
        grid=(N_TOK // TOK_TILE,),
        in_specs=[tok(D_ATTN), tok(D_REC), tok(D_MODEL),
                  pl.BlockSpec((1, 6, D_MODEL), lambda i: (i // tiles_per_seq, 0, 0)),
                  const((1, D_MODEL)), const((1, D_MODEL)),
                  const((D_MODEL, D_MODEL)), const((D_MODEL, N_EXPERTS)), const((1, N_EXPERTS))],
        out_specs=[tok(D_MODEL),
                   pl.BlockSpec((TOK_TILE * ROW_TILES, LANES), lambda i: (i, 0)),
                   tok(TOP_K), tok(TOP_K), tok(TOP_K), const((1, N_EXPERTS))],
        out_shape=[jax.ShapeDtypeStruct((N_TOK, D_MODEL), F32),
                   jax.ShapeDtypeStruct((N_TOK * ROW_TILES, LANES), F32),
                   jax.ShapeDtypeStruct((N_TOK, TOP_K), I32),
                   jax.ShapeDtypeStruct((N_TOK, TOP_K), F32),
                   jax.ShapeDtypeStruct((N_TOK, TOP_K), I32),
                   jax.ShapeDtypeStruct((1, N_EXPERTS), I32)],
        scratch_shapes=[pltpu.VMEM((1, N_EXPERTS), F32)],
        compiler_params=pltpu.CompilerParams(dimension_semantics=("arbitrary",),
                                             vmem_limit_bytes=VMEM_LIMIT),
        name="outproj_router",
    )(ya, yr, x2, mod3, g_post, g_pre, w_out_bf16, w_router, b_router)


def _sc_mesh():
    return plsc.VectorSubcoreMesh(core_axis_name="c", subcore_axis_name="s")


def _sc_worker_count():
    info = plsc.get_sparse_core_info()
    return info.num_cores, info.num_cores * info.num_subcores


def _sc_dispatch(h_rows, dest_win):
    n_cores, n_workers = _sc_worker_count()
    n_win = N_TOK // SC_WINDOW
    per_worker = n_win // n_workers

    @functools.partial(
        pl.kernel, mesh=_sc_mesh(),
        out_type=jax.ShapeDtypeStruct((N_SLOTS, ROW_TILES, LANES), F32),
        scratch_types=[pltpu.VMEM((TOP_K, SC_WINDOW), I32),
                       pltpu.VMEM((SC_WINDOW, ROW_TILES, LANES), F32),
                       pltpu.SemaphoreType.DMA],
        name="sc_dispatch")
    def run(h_hbm, dest_hbm, xs_hbm, idx_v, rows_v, sem):
        wid = lax.axis_index("s") * n_cores + lax.axis_index("c")

        @pl.loop(0, per_worker)
        def _(j):
            win = wid * per_worker + j
            pltpu.sync_copy(dest_hbm.at[win], idx_v)
            pltpu.sync_copy(h_hbm.at[pl.ds(win * SC_WINDOW, SC_WINDOW)], rows_v)
            copies = [pltpu.async_copy(rows_v, xs_hbm.at[idx_v.at[kk]], sem)
                      for kk in range(TOP_K)]
            for cp in copies:
                cp.wait()

    return run(h_rows, dest_win)


def _sc_collect(y_rows, dest_win):
    n_cores, n_workers = _sc_worker_count()
    n_win = N_TOK // SC_WINDOW
    per_worker = n_win // n_workers

    @functools.partial(
        pl.kernel, mesh=_sc_mesh(),
        out_type=jax.ShapeDtypeStruct((TOP_K, N_TOK, ROW_TILES, LANES), F32),
        scratch_types=[pltpu.VMEM((TOP_K, SC_WINDOW), I32),
                       pltpu.VMEM((SC_WINDOW, ROW_TILES, LANES), F32),
                       pltpu.SemaphoreType.DMA],
        name="sc_collect")
    def run(y_hbm, dest_hbm, yg_hbm, idx_v, rows_v, sem):
        wid = lax.axis_index("s") * n_cores + lax.axis_index("c")

        @pl.loop(0, per_worker)
        def _(j):
            win = wid * per_worker + j
            pltpu.sync_copy(dest_hbm.at[win], idx_v)
            for kk in range(TOP_K):
                pltpu.async_copy(y_hbm.at[idx_v.at[kk]], rows_v, sem).wait()
                pltpu.sync_copy(rows_v, yg_hbm.at[kk, pl.ds(win * SC_WINDOW, SC_WINDOW)])

    return run(y_rows, dest_win)


def _expert_kernel(be_ref, nu_ref, x_ref, wgu_ref, bgu_ref, wd_ref, bd_ref, y_ref, wgu16, wd16):
    i = pl.program_id(0)
    e = be_ref[i]
    prev = be_ref[jnp.maximum(i - 1, 0)]

    @pl.when((i == 0) | (e != prev))
    def _():
        rows = 128

        def cast(r, carry):
            sl = pl.ds(pl.multiple_of(r * rows, rows), rows)
            wgu16[sl, :] = wgu_ref[0, sl, :].astype(BF16)
            wd16[sl, :] = wd_ref[0, sl, :].astype(BF16)
            return carry

        lax.fori_loop(0, D_MODEL // rows, cast, 0)

    @pl.when(i < nu_ref[0])
    def _():
        x = _load_row_tiles(x_ref, MOE_BLOCK).astype(BF16)
        bgu = bgu_ref[0]
        glu = _dot(x, wgu16[:, 0:D_FF]) + bgu[:, 0:D_FF]
        lin = _dot(x, wgu16[:, D_FF:]) + bgu[:, D_FF:]
        glu = jnp.minimum(glu, SWIGLU_LIMIT)
        lin = jnp.clip(lin, -SWIGLU_LIMIT, SWIGLU_LIMIT)
        act = glu * _sigmoid(SWIGLU_ALPHA * glu) * (lin + 1.0)
        y = _dot(act.astype(BF16), wd16[...]) + bd_ref[0]
        _store_row_tiles(y_ref, y)


def _experts(blk_e, n_used, xs2, w_gu, b_gu3, w_down, b_down3):
    row_blk = pl.BlockSpec((MOE_BLOCK * ROW_TILES, LANES),
                           lambda i, be, nu: (jnp.minimum(i, nu[0] - 1), 0))
    grid_spec = pltpu.PrefetchScalarGridSpec(
        num_scalar_prefetch=2,
        grid=(N_BLOCKS,),
        in_specs=[row_blk,
                  pl.BlockSpec((1, D_MODEL, 2 * D_FF), lambda i, be, nu: (be[i], 0, 0)),
                  pl.BlockSpec((1, 1, 2 * D_FF), lambda i, be, nu: (be[i], 0, 0)),
                  pl.BlockSpec((1, D_FF, D_MODEL), lambda i, be, nu: (be[i], 0, 0)),
                  pl.BlockSpec((1, 1, D_MODEL), lambda i, be, nu: (be[i], 0, 0))],
        out_specs=row_blk,
        scratch_shapes=[pltpu.VMEM((D_MODEL, 2 * D_FF), BF16),
                        pltpu.VMEM((D_FF, D_MODEL), BF16)],
    )
    return pl.pallas_call(
        _expert_kernel,
        grid_spec=grid_spec,
        out_shape=jax.ShapeDtypeStruct((N_SLOTS * ROW_TILES, LANES), F32),
        compiler_params=pltpu.CompilerParams(dimension_semantics=("arbitrary",),
                                             vmem_limit_bytes=VMEM_LIMIT),
        name="experts",
    )(blk_e, n_used, xs2, w_gu, b_gu3, w_down, b_down3)


def _final_kernel(yg_ref, gate_ref, x1_ref, mod_ref, gpost_ref, o_ref):
    gates = gate_ref[...]
    y = None
    for kk in range(TOP_K):
        part = _load_row_tiles(yg_ref.at[kk], TOK_TILE) * gates[:, kk:kk + 1]
        y = part if y is None else y + part
    gate_f = mod_ref[0][5:6]
    o_ref[...] = x1_ref[...] + gate_f * (_rms(y) * gpost_ref[...])


def _final(yg3, gates, x1, mod3, g_post):
    tiles_per_seq = SEQ // TOK_TILE
    return pl.pallas_call(
        _final_kernel,
        grid=(N_TOK // TOK_TILE,),
        in_specs=[pl.BlockSpec((TOP_K, TOK_TILE * ROW_TILES, LANES), lambda i: (0, i, 0)),
                  pl.BlockSpec((TOK_TILE, TOP_K), lambda i: (i, 0)),
                  pl.BlockSpec((TOK_TILE, D_MODEL), lambda i: (i, 0)),
                  pl.BlockSpec((1, 6, D_MODEL), lambda i: (i // tiles_per_seq, 0, 0)),
                  pl.BlockSpec((1, D_MODEL), lambda i: (0, 0))],
        out_specs=pl.BlockSpec((TOK_TILE, D_MODEL), lambda i: (i, 0)),
        out_shape=jax.ShapeDtypeStruct((N_TOK, D_MODEL), F32),
        compiler_params=pltpu.CompilerParams(vmem_limit_bytes=VMEM_LIMIT),
        name="combine_final",
    )(yg3, gates, x1, mod3, g_post)


def _routing_tables(idx, rank, counts):
    counts = counts.reshape(N_EXPERTS)
    padded = ((counts + MOE_BLOCK - 1) // MOE_BLOCK) * MOE_BLOCK
    pend = jnp.cumsum(padded)
    pstart = pend - padded
    dest = pstart[idx] + rank
    dest_win = dest.reshape(N_TOK // SC_WINDOW, SC_WINDOW, TOP_K).transpose(0, 2, 1)
    n_used = (pend[-1] // MOE_BLOCK).astype(I32).reshape(1)
    blk_start = jnp.arange(N_BLOCKS, dtype=I32) * MOE_BLOCK
    blk_e = jnp.sum(blk_start[:, None] >= pend[None, :], axis=1).astype(I32)
    last_e = jnp.max(jnp.where(counts > 0, jnp.arange(N_EXPERTS, dtype=I32), 0))
    blk_e = jnp.minimum(blk_e, last_e)
    return dest_win, blk_e, n_used


def kernel(x, c, w_ada, b_ada, g_pre_mix, g_post_mix, w_in, attn_norm_w, hgrn_lb, hgrn_norm_w,
           w_out, g_pre_ffn, g_post_ffn, w_router, b_router, w_gu, b_gu, w_down, b_down):
    x2 = x.reshape(N_TOK, D_MODEL)
    c_pad = jnp.pad(c, ((0, SUBLANES - BATCH), (0, 0)))
    mod = _ada_mod(c_pad, w_ada[0], b_ada)
    mod3 = mod[:BATCH].reshape(BATCH, 6, D_MODEL)

    qa, ka, va, qr, fr, ir, gr = _inproj(x2, mod3, g_pre_mix, w_in[0].astype(BF16))
    seq = lambda t: t.reshape(BATCH, SEQ, t.shape[-1])
    ya = _attention(seq(qa), seq(ka), seq(va), attn_norm_w)
    yr = _hgrn(seq(qr), seq(fr), seq(ir), seq(gr), hgrn_lb, hgrn_norm_w)

    x1, h2, idx, gates, rank, counts = _mid(
        ya.reshape(N_TOK, D_ATTN), yr.reshape(N_TOK, D_REC), x2, mod3, g_post_mix, g_pre_ffn,
        w_out[0].astype(BF16), w_router[0], b_router)

    dest_win, blk_e, n_used = _routing_tables(idx, rank, counts)
    xs = _sc_dispatch(h2.reshape(N_TOK, ROW_TILES, LANES), dest_win)
    ys = _experts(blk_e, n_used, xs.reshape(N_SLOTS * ROW_TILES, LANES),
                  w_gu[0], b_gu[0].reshape(N_EXPERTS, 1, 2 * D_FF),
                  w_down[0], b_down[0].reshape(N_EXPERTS, 1, D_MODEL))
    yg = _sc_collect(ys.reshape(N_SLOTS, ROW_TILES, LANES), dest_win)
    out = _final(yg.reshape(TOP_K, N_TOK * ROW_TILES, LANES), gates, x1, mod3, g_post_ffn)
    return out.reshape(BATCH, SEQ, D_MODEL)
```

```python
import functools
import math

import jax
import jax.numpy as jnp
from jax import lax
from jax.experimental import pallas as pl
from jax.experimental.pallas import tpu as pltpu
from jax.experimental.pallas import tpu_sc as plsc

F32 = jnp.float32
BF16 = jnp.bfloat16
I32 = jnp.int32
HIGHEST = lax.Precision.HIGHEST

D_MODEL = 1024
BATCH = 4
SEQ = 4096
N_TOK = BATCH * SEQ
D_ATTN = 512
HEAD_DIM_A = 64
ATT_BLOCK = 128
DIL_MID = 4
DIL_MAX = 16
SUB_MID = SEQ // DIL_MID
SUB_MAX = SEQ // DIL_MAX
D_REC = 512
HGRN_CHUNK = 32
HGRN_SUPER = 256
N_EXPERTS = 32
TOP_K = 4
D_FF = 1024
SWIGLU_LIMIT = 7.0
SWIGLU_ALPHA = 1.702
EPS = 1e-6
NEG_BIG = -1e30
Q_SCALE = HEAD_DIM_A ** -0.5 * math.log2(math.e)

LANES = 128
SUBLANES = 8
ROW_TILES = D_MODEL // LANES

TOK_TILE = 512
MOE_BLOCK = 512
N_SLOTS = N_TOK * TOP_K + N_EXPERTS * MOE_BLOCK
N_BLOCKS = N_SLOTS // MOE_BLOCK
SC_WINDOW = 32
VMEM_LIMIT = 56 * 1024 * 1024


def _sigmoid(x):
    return 1.0 / (1.0 + jnp.exp(-x))


def _dot(a, b):
    return jnp.dot(a, b, preferred_element_type=F32)


def _dot_nt(a, b):
    return lax.dot_general(a, b, (((1,), (1,)), ((), ())), preferred_element_type=F32)


def _split_bf16(x):
    hi = x.astype(BF16)
    return hi, (x - hi.astype(F32)).astype(BF16)


def _rms(x):
    return x * lax.rsqrt(jnp.mean(x * x, axis=-1, keepdims=True) + EPS)


def _ada_kernel(c_ref, w_ref, b_ref, o_ref):
    c = c_ref[...]
    cond = c * _sigmoid(c)
    o_ref[...] = jnp.dot(cond, w_ref[...], precision=HIGHEST,
                         preferred_element_type=F32) + b_ref[...]


def _ada_mod(c_pad, w_ada, b_ada):
    n = w_ada.shape[1]
    tn = 1536
    return pl.pallas_call(
        _ada_kernel,
        grid=(n // tn,),
        in_specs=[pl.BlockSpec((SUBLANES, D_MODEL), lambda j: (0, 0)),
                  pl.BlockSpec((D_MODEL, tn), lambda j: (0, j)),
                  pl.BlockSpec((1, tn), lambda j: (0, j))],
        out_specs=pl.BlockSpec((SUBLANES, tn), lambda j: (0, j)),
        out_shape=jax.ShapeDtypeStruct((SUBLANES, n), F32),
        compiler_params=pltpu.CompilerParams(vmem_limit_bytes=VMEM_LIMIT),
        name="ada_mod",
    )(c_pad, w_ada, b_ada)


def _inproj_kernel(x_ref, mod_ref, g_ref, w_ref, q4, k4, v4, q16, k16, v16, qr, fr, ir, gr):
    shift, scale = mod_ref[:, 0:1, :], mod_ref[:, 1:2, :]
    x = x_ref[...]
    h = x * lax.rsqrt(jnp.mean(x * x, axis=-1, keepdims=True) + EPS)
    h = h * g_ref[...] * (1.0 + scale) + shift
    hb = h.reshape(2 * SUB_MAX, D_MODEL).astype(BF16)

    def proj(j):
        return _dot(hb, w_ref[:, j * D_ATTN:(j + 1) * D_ATTN])

    for j, (o_mid, o_max) in enumerate(((q4, q16), (k4, k16), (v4, v16))):
        r = proj(j)
        if j == 0:
            r = r * Q_SCALE
        rb = r.astype(BF16).reshape(2, SUB_MAX, D_ATTN)
        o_mid[...] = rb
        o_max[...] = rb
    for j, o_ref in enumerate((qr, fr, ir, gr)):
        o_ref[...] = proj(3 + j).reshape(2, SUB_MAX, D_REC)


def _inproj(x, mod3, g_pre, w_in_bf16):
    xv = x.reshape(BATCH, SUB_MAX, DIL_MAX * D_MODEL)
    mid = pl.BlockSpec((2, None, SUB_MAX, D_ATTN), lambda bp, r: (bp, r % DIL_MID, 0, r // DIL_MID))
    mx = pl.BlockSpec((2, None, SUB_MAX, D_ATTN), lambda bp, r: (bp, r, 0, 0))
    nat = pl.BlockSpec((2, SUB_MAX, D_REC), lambda bp, r: (bp, 0, r))
    mid_shape = jax.ShapeDtypeStruct((BATCH, DIL_MID, SUB_MAX, DIL_MID * D_ATTN), BF16)
    mx_shape = jax.ShapeDtypeStruct((BATCH, DIL_MAX, SUB_MAX, D_ATTN), BF16)
    nat_shape = jax.ShapeDtypeStruct((BATCH, SUB_MAX, DIL_MAX * D_REC), F32)
    return pl.pallas_call(
        _inproj_kernel,
        grid=(BATCH // 2, DIL_MAX),
        in_specs=[pl.BlockSpec((2, SUB_MAX, D_MODEL), lambda bp, r: (bp, 0, r)),
                  pl.BlockSpec((2, 6, D_MODEL), lambda bp, r: (bp, 0, 0)),
                  pl.BlockSpec((1, D_MODEL), lambda bp, r: (0, 0)),
                  pl.BlockSpec(w_in_bf16.shape, lambda bp, r: (0, 0))],
        out_specs=[mid, mid, mid, mx, mx, mx, nat, nat, nat, nat],
        out_shape=[mid_shape] * 3 + [mx_shape] * 3 + [nat_shape] * 4,
        compiler_params=pltpu.CompilerParams(vmem_limit_bytes=VMEM_LIMIT),
        name="inproj",
    )(xv, mod3, g_pre, w_in_bf16)


def _attn_kernel(q4_ref, k4_ref, v4_ref, q16_ref, k16_ref, v16_ref, nw_ref, o_ref,
                 bias_scr, o_scr, lse_scr):
    bw = ATT_BLOCK
    lane = lax.broadcasted_iota(I32, (bw, LANES), 1)
    head0 = lane < HEAD_DIM_A
    head_masks = (jnp.where(head0, 1.0, 0.0).astype(BF16), jnp.where(head0, 0.0, 1.0).astype(BF16))
    eye = jnp.where(lax.broadcasted_iota(I32, (bw, bw), 0) == lax.broadcasted_iota(I32, (bw, bw), 1),
                    1.0, 0.0).astype(BF16)

    jr = lax.broadcasted_iota(I32, (bw, 2 * bw), 1)
    rc = lax.broadcasted_iota(I32, (bw, 2 * bw), 0)
    jp = DIL_MID * (jr % (2 * bw // DIL_MID)) + jr // (2 * bw // DIL_MID)
    rp = DIL_MID * (rc % (bw // DIL_MID)) + rc // (bw // DIL_MID)
    for var, (j, r) in enumerate(((jr, rc), (jr, rc), (jp, rp), (jp, rp))):
        valid = (j <= r) if var % 2 else ((j >= r) & (j <= r + bw))
        bias_scr[var] = jnp.where(valid, 0.0, NEG_BIG).astype(BF16)

    def attend_group(loaded):
        scores = [_dot_nt(q * hm, k) + _dot(eye, bias)
                  for (q, k, _, bias) in loaded for hm in head_masks]
        probs = []
        for s in scores:
            m = jnp.max(s, axis=-1, keepdims=True)
            p = jnp.exp2(s - m)
            probs.append((p.astype(BF16), jnp.sum(p, axis=-1, keepdims=True), m))
        pvs = [_dot(p, loaded[n // 2][2]) for n, (p, _, _) in enumerate(probs)]
        results = []
        for u in range(len(loaded)):
            (_, l0, m0), (_, l1, m1) = probs[2 * u], probs[2 * u + 1]
            o = jnp.where(head0, pvs[2 * u] * (1.0 / l0), pvs[2 * u + 1] * (1.0 / l1))
            lse = jnp.where(head0, jnp.broadcast_to(m0 + jnp.log2(l0), (bw, LANES)),
                            jnp.broadcast_to(m1 + jnp.log2(l1), (bw, LANES)))
            results.append((o, lse))
        return results

    piece = bw // DIL_MID
    group = 4

    def group_d1(g, carry):
        loaded, starts = [], []
        for i in range(group):
            blk = g * group + i
            qs = pl.multiple_of(blk * piece, piece)
            ks = pl.multiple_of(jnp.maximum(blk - 1, 0) * piece, piece)
            q = jnp.concatenate([q4_ref[r, pl.ds(qs, piece), :] for r in range(DIL_MID)], axis=0)
            k = jnp.concatenate([k4_ref[r, pl.ds(ks, 2 * piece), :] for r in range(DIL_MID)], axis=0)
            v = jnp.concatenate([v4_ref[r, pl.ds(ks, 2 * piece), :] for r in range(DIL_MID)], axis=0)
            loaded.append((q, k, v, bias_scr[jnp.where(blk == 0, 3, 2)]))
            starts.append(qs)
        for qs, (o, lse) in zip(starts, attend_group(loaded)):
            for r in range(DIL_MID):
                o_scr[0, r, pl.ds(qs, piece), :] = o[r * piece:(r + 1) * piece]
                lse_scr[0, r, pl.ds(qs, piece), :] = lse[r * piece:(r + 1) * piece]
        return carry

    def group_d4(blk, carry):
        qs = pl.multiple_of(blk * bw, bw)
        ks = pl.multiple_of(jnp.maximum(blk - 1, 0) * bw, bw)
        bias = bias_scr[jnp.where(blk == 0, 1, 0)]
        loaded = [(q4_ref[r, pl.ds(qs, bw), :], k4_ref[r, pl.ds(ks, 2 * bw), :],
                   v4_ref[r, pl.ds(ks, 2 * bw), :], bias) for r in range(DIL_MID)]
        for r, (o, lse) in enumerate(attend_group(loaded)):
            o_scr[1, r, pl.ds(qs, bw), :] = o
            lse_scr[1, r, pl.ds(qs, bw), :] = lse
        return carry

    def group_d16(g, carry):
        loaded, dsts = [], []
        for i in range(group):
            r, blk = g * (group // 2) + i // 2, i % 2
            loaded.append((q16_ref[r, blk * bw:(blk + 1) * bw, :], k16_ref[r], v16_ref[r],
                           bias_scr[1 - blk]))
            dsts.append((r % DIL_MID, pl.ds(blk * (bw * DIL_MID) + r // DIL_MID, bw, stride=DIL_MID)))
        for (sub, dst), (o, lse) in zip(dsts, attend_group(loaded)):
            o_scr[2, sub, dst, :] = o
            lse_scr[2, sub, dst, :] = lse
        return carry

    units = SEQ // bw
    lax.fori_loop(0, units // group, group_d1, 0)
    lax.fori_loop(0, units // DIL_MID, group_d4, 0)
    lax.fori_loop(0, units // group, group_d16, 0)

    rows = 256
    hi = lax.broadcasted_iota(I32, (LANES, LANES), 0) // HEAD_DIM_A
    hj = lax.broadcasted_iota(I32, (LANES, LANES), 1) // HEAD_DIM_A
    head_sum = jnp.where(hi == hj, 1.0, 0.0).astype(BF16)
    nw = nw_ref[...]

    def merge(t, carry):
        r = t // (SUB_MID // rows)
        sl = pl.ds(pl.multiple_of((t % (SUB_MID // rows)) * rows, rows), rows)
        l0, l1, l2 = lse_scr[0, r, sl, :], lse_scr[1, r, sl, :], lse_scr[2, r, sl, :]
        mx = jnp.maximum(jnp.maximum(l0, l1), l2)
        w0, w1, w2 = jnp.exp2(l0 - mx), jnp.exp2(l1 - mx), jnp.exp2(l2 - mx)
        o = (w0 * o_scr[0, r, sl, :] + w1 * o_scr[1, r, sl, :] + w2 * o_scr[2, r, sl, :]) / (w0 + w1 + w2)
        sq_hi, sq_lo = _split_bf16(o * o)
        ms = (_dot(sq_hi, head_sum) + _dot(sq_lo, head_sum)) * (1.0 / HEAD_DIM_A)
        o_ref[r, sl, :] = (o * lax.rsqrt(ms + EPS) * nw).astype(o_ref.dtype)
        return carry

    lax.fori_loop(0, SEQ // rows, merge, 0, unroll=2)


def _attention(q4, k4, v4, q16, k16, v16, attn_norm_w):
    hp = D_ATTN // LANES
    mid = pl.BlockSpec((None, DIL_MID, SUB_MID, LANES), lambda b, h: (b, 0, 0, h))
    mx = pl.BlockSpec((None, DIL_MAX, SUB_MAX, LANES), lambda b, h: (b, 0, 0, h))
    scr = pltpu.VMEM((3, DIL_MID, SUB_MID, LANES), F32)
    return pl.pallas_call(
        _attn_kernel,
        grid=(BATCH, hp),
        in_specs=[mid, mid, mid, mx, mx, mx, pl.BlockSpec((1, LANES), lambda b, h: (0, h))],
        out_specs=mid,
        out_shape=jax.ShapeDtypeStruct((BATCH, DIL_MID, SUB_MID, D_ATTN), BF16),
        scratch_shapes=[pltpu.VMEM((4, ATT_BLOCK, 2 * ATT_BLOCK), BF16), scr, scr],
        compiler_params=pltpu.CompilerParams(vmem_limit_bytes=VMEM_LIMIT),
        name="dilated_attn",
    )(q4, k4, v4, q16, k16, v16, attn_norm_w)


def _hgrn_kernel(qr_ref, fr_ref, ir_ref, gr_ref, lb_ref, nw_ref, o_ref,
                 qe_scr, oi_scr, delta_scr, dec_scr, st_scr):
    sup, c = HGRN_SUPER, HGRN_CHUNK
    nch = sup // c
    n_sup = SEQ // sup
    lbp = lb_ref[...]
    lmx = jnp.max(lbp, axis=0, keepdims=True)
    ex = jnp.exp(lbp - lmx)
    lb = ex[0:1] / (ex[0:1] + ex[1:2])
    nw = nw_ref[...]

    ri = lax.broadcasted_iota(I32, (sup, sup), 0)
    ci = lax.broadcasted_iota(I32, (sup, sup), 1)
    same_chunk = (ri // c) == (ci // c)
    causal = same_chunk & (ci <= ri)
    cum_op = jnp.where(causal, 1.0, 0.0).astype(BF16)

    def chunk_rows(kd, ch):
        parts = []
        if ch > 0:
            parts.append(jnp.zeros((ch * c, LANES), BF16))
        parts.append(kd[ch * c:(ch + 1) * c])
        if ch < nch - 1:
            parts.append(jnp.zeros(((nch - 1 - ch) * c, LANES), BF16))
        return jnp.concatenate(parts, axis=0)

    group = 2

    def independent(g, carry):
        ts = [g * group + i for i in range(group)]
        sls = [pl.ds(pl.multiple_of(t * sup, sup), sup) for t in ts]
        pre = []
        for sl in sls:
            f = lb + (1.0 - lb) * _sigmoid(fr_ref[sl, :])
            logf_hi, logf_lo = _split_bf16(jnp.log(f))
            pre.append((1.0 - f, _dot(cum_op, logf_hi) + _dot(cum_op, logf_lo)))
        mid = []
        for t, sl, (kk, b) in zip(ts, sls, pre):
            b_last = jnp.concatenate(
                [jnp.broadcast_to(b[(ch + 1) * c - 1:(ch + 1) * c], (c, LANES)) for ch in range(nch)],
                axis=0)
            q = qr_ref[sl, :]
            qeb = (q * _sigmoid(q) * jnp.exp(b)).astype(BF16)
            ke = (kk * jnp.exp(-b)).astype(BF16)
            kd = (kk * jnp.exp(b_last - b)).astype(BF16)
            qe_scr[sl, :] = qeb
            dec_rows = jnp.concatenate([b_last[ch * c:ch * c + 1] for ch in range(nch)], axis=0)
            dec_scr[pl.ds(pl.multiple_of(t * nch, nch), nch), :] = jnp.exp(dec_rows)
            v = ir_ref[sl, :]
            vt = v.T.astype(BF16)
            scores = _dot_nt(qeb, ke)
            for pair in range(nch // 2):
                rhs = jnp.concatenate([chunk_rows(kd, 2 * pair), chunk_rows(kd, 2 * pair + 1)], axis=1)
                d2 = _dot(vt, rhs)
                delta_scr[t * nch + 2 * pair] = d2[:, 0:LANES]
                delta_scr[t * nch + 2 * pair + 1] = d2[:, LANES:]
            mid.append((scores, v.astype(BF16)))
        for sl, (scores, vb) in zip(sls, mid):
            a = jnp.where(causal, scores, 0.0)
            oi_scr[sl, :] = _dot(a.astype(BF16), vb)
        return carry

    lax.fori_loop(0, n_sup // group, independent, 0)

    def recur(ch, st):
        st_scr[ch] = st.astype(BF16)
        return st * dec_scr[pl.ds(ch, 1), :] + delta_scr[ch]

    lax.fori_loop(0, SEQ // c, recur, jnp.zeros((LANES, LANES), F32), unroll=8)

    def finish(t, carry):
        sl = pl.ds(pl.multiple_of(t * sup, sup), sup)
        parts = [_dot_nt(qe_scr[pl.ds(pl.multiple_of(t * sup + ch * c, c), c), :], st_scr[t * nch + ch])
                 for ch in range(nch)]
        o = oi_scr[sl, :] + jnp.concatenate(parts, axis=0)
        g = gr_ref[sl, :]
        o_ref[sl, :] = (_rms(o) * nw * (g * _sigmoid(g))).astype(o_ref.dtype)
        return carry

    lax.fori_loop(0, n_sup, finish, 0, unroll=2)


def _hgrn(qr, fr, ir, gr, hgrn_lb, hgrn_norm_w):
    nh = D_REC // LANES
    n_chunks = SEQ // HGRN_CHUNK
    blk = pl.BlockSpec((None, SEQ, LANES), lambda b, h: (b, 0, h))
    return pl.pallas_call(
        _hgrn_kernel,
        grid=(BATCH, nh),
        in_specs=[blk, blk, blk, blk,
                  pl.BlockSpec((2, LANES), lambda b, h: (0, h)),
                  pl.BlockSpec((1, LANES), lambda b, h: (0, h))],
        out_specs=blk,
        out_shape=jax.ShapeDtypeStruct((BATCH, SEQ, D_REC), BF16),
        scratch_shapes=[pltpu.VMEM((SEQ, LANES), BF16),
                        pltpu.VMEM((SEQ, LANES), F32),
                        pltpu.VMEM((n_chunks, LANES, LANES), F32),
                        pltpu.VMEM((n_chunks, LANES), F32),
                        pltpu.VMEM((n_chunks, LANES, LANES), BF16)],
        compiler_params=pltpu.CompilerParams(vmem_limit_bytes=VMEM_LIMIT),
        name="hgrn2",
    )(qr, fr, ir, gr, hgrn_lb, hgrn_norm_w)


def _tile_batch(i):
    return i // (SEQ // TOK_TILE)


def _tile_sub(i):
    return (i // (SUB_MID // TOK_TILE)) % DIL_MID


def _tile_part(i):
    return i % (SUB_MID // TOK_TILE)


def _store_row_tiles(ref, val):
    rows = val.shape[0]
    for j in range(ROW_TILES):
        ref[pl.ds(j, rows, stride=ROW_TILES), :] = val[:, j * LANES:(j + 1) * LANES]


def _load_row_tiles(ref, rows):
    return jnp.concatenate(
        [ref[pl.ds(j, rows, stride=ROW_TILES), :] for j in range(ROW_TILES)], axis=1)


def _mid_kernel(ya_ref, yr_ref, x_ref, mod_ref, gpost_ref, gpre_ref, wo_ref, wr_ref, br_ref,
                x1_ref, h2_ref, idx_ref, gate_ref, rank_ref, cnt_ref, carry_ref):
    i = pl.program_id(0)

    @pl.when(i == 0)
    def _():
        carry_ref[...] = jnp.zeros_like(carry_ref)

    mod = mod_ref[0]
    gate_m, shift_f, scale_f = mod[2:3], mod[3:4], mod[4:5]
    y = _dot(ya_ref[...], wo_ref[0:D_ATTN, :]) + _dot(yr_ref[...], wo_ref[D_ATTN:, :])
    x1 = x_ref[...] + gate_m * (_rms(y) * gpost_ref[...])
    x1_ref[...] = x1
    h2 = _rms(x1) * gpre_ref[...] * (1.0 + scale_f) + shift_f
    _store_row_tiles(h2_ref, h2)

    h2_hi, h2_lo = _split_bf16(h2)
    wr_hi, wr_lo = _split_bf16(wr_ref[...])
    logits = _dot(h2_hi, wr_hi) + _dot(h2_lo, wr_hi) + _dot(h2_hi, wr_lo) + br_ref[...]
    tm = logits.shape[0]
    eidx = lax.broadcasted_iota(I32, (tm, N_EXPERTS), 1)
    work = logits
    vals, idxs = [], []
    onehot = jnp.zeros((tm, N_EXPERTS), F32)
    for _ in range(TOP_K):
        m = jnp.max(work, axis=-1, keepdims=True)
        sel = jnp.min(jnp.where(work == m, eidx, N_EXPERTS), axis=-1, keepdims=True)
        hit = eidx == sel
        work = jnp.where(hit, -jnp.inf, work)
        onehot = jnp.where(hit, 1.0, onehot)
        vals.append(m)
        idxs.append(sel)
    ex = [jnp.exp(vv - vals[0]) for vv in vals]
    den = ex[0] + ex[1] + ex[2] + ex[3]

    ri = lax.broadcasted_iota(I32, (tm, tm), 0)
    ci = lax.broadcasted_iota(I32, (tm, tm), 1)
    strict_lower = jnp.where(ci < ri, 1.0, 0.0).astype(BF16)
    before = _dot(strict_lower, onehot.astype(BF16)) + carry_ref[...]
    lane4 = lax.broadcasted_iota(I32, (tm, TOP_K), 1)
    idx_o = jnp.zeros((tm, TOP_K), I32)
    gate_o = jnp.zeros((tm, TOP_K), F32)
    rank_o = jnp.zeros((tm, TOP_K), I32)
    for kk in range(TOP_K):
        rk = jnp.sum(jnp.where(eidx == idxs[kk], before, 0.0), axis=-1, keepdims=True)
        idx_o = jnp.where(lane4 == kk, idxs[kk], idx_o)
        gate_o = jnp.where(lane4 == kk, ex[kk] / den, gate_o)
        rank_o = jnp.where(lane4 == kk, rk.astype(I32), rank_o)
    idx_ref[...] = idx_o
    gate_ref[...] = gate_o
    rank_ref[...] = rank_o
    total = carry_ref[...] + jnp.sum(onehot, axis=0, keepdims=True)
    carry_ref[...] = total
    cnt_ref[...] = total.astype(I32)


def _mid(ya4, yr, x, mod3, g_post, g_pre, w_out_bf16, w_router, b_router):
    tok = lambda w: pl.BlockSpec((TOK_TILE, w), lambda i: (i, 0))
    const = lambda s: pl.BlockSpec(s, lambda i: (0,) * len(s))
    sub_view = lambda w: pl.BlockSpec((None, TOK_TILE, w),
                                      lambda i: (_tile_batch(i), _tile_part(i), _tile_sub(i)))
    return pl.pallas_call(
        _mid_kernel,
        grid=(N_TOK // TOK_TILE,),
        in_specs=[pl.BlockSpec((None, None, TOK_TILE, D_ATTN),
                               lambda i: (_tile_batch(i), _tile_sub(i), _tile_part(i), 0)),
                  sub_view(D_REC), sub_view(D_MODEL),
                  pl.BlockSpec((1, 6, D_MODEL), lambda i: (_tile_batch(i), 0, 0)),
                  const((1, D_MODEL)), const((1, D_MODEL)),
                  const((D_MODEL, D_MODEL)), const((D_MODEL, N_EXPERTS)), const((1, N_EXPERTS))],
        out_specs=[tok(D_MODEL),
                   pl.BlockSpec((TOK_TILE * ROW_TILES, LANES), lambda i: (i, 0)),
                   tok(TOP_K), tok(TOP_K), tok(TOP_K), const((1, N_EXPERTS))],
        out_shape=[jax.ShapeDtypeStruct((N_TOK, D_MODEL), F32),
                   jax.ShapeDtypeStruct((N_TOK * ROW_TILES, LANES), F32),
                   jax.ShapeDtypeStruct((N_TOK, TOP_K), I32),
                   jax.ShapeDtypeStruct((N_TOK, TOP_K), F32),
                   jax.ShapeDtypeStruct((N_TOK, TOP_K), I32),
                   jax.ShapeDtypeStruct((1, N_EXPERTS), I32)],
        scratch_shapes=[pltpu.VMEM((1, N_EXPERTS), F32)],
        compiler_params=pltpu.CompilerParams(dimension_semantics=("arbitrary",),
                                             vmem_limit_bytes=VMEM_LIMIT),
        name="outproj_router",
    )(ya4, yr.reshape(BATCH, SUB_MID, DIL_MID * D_REC), x.reshape(BATCH, SUB_MID, DIL_MID * D_MODEL),
      mod3, g_post, g_pre, w_out_bf16, w_router, b_router)


def _sc_mesh():
    return plsc.VectorSubcoreMesh(core_axis_name="c", subcore_axis_name="s")


def _sc_worker_count():
    info = plsc.get_sparse_core_info()
    return info.num_cores, info.num_cores * info.num_subcores


def _sc_dispatch(h_rows, dest_win):
    n_cores, n_workers = _sc_worker_count()
    n_win = N_TOK // SC_WINDOW
    per_worker = n_win // n_workers

    @functools.partial(
        pl.kernel, mesh=_sc_mesh(),
        out_type=jax.ShapeDtypeStruct((N_SLOTS, ROW_TILES, LANES), F32),
        scratch_types=[pltpu.VMEM((TOP_K, SC_WINDOW), I32),
                       pltpu.VMEM((SC_WINDOW, ROW_TILES, LANES), F32),
                       pltpu.SemaphoreType.DMA],
        name="sc_dispatch")
    def run(h_hbm, dest_hbm, xs_hbm, idx_v, rows_v, sem):
        wid = lax.axis_index("s") * n_cores + lax.axis_index("c")

        @pl.loop(0, per_worker)
        def _(j):
            win = wid * per_worker + j
            pltpu.sync_copy(dest_hbm.at[win], idx_v)
            pltpu.sync_copy(h_hbm.at[pl.ds(win * SC_WINDOW, SC_WINDOW)], rows_v)
            copies = [pltpu.async_copy(rows_v, xs_hbm.at[idx_v.at[kk]], sem)
                      for kk in range(TOP_K)]
            for cp in copies:
                cp.wait()

    return run(h_rows, dest_win)


def _sc_collect(y_rows, dest_win):
    n_cores, n_workers = _sc_worker_count()
    n_win = N_TOK // SC_WINDOW
    per_worker = n_win // n_workers

    @functools.partial(
        pl.kernel, mesh=_sc_mesh(),
        out_type=jax.ShapeDtypeStruct((TOP_K, N_TOK, ROW_TILES, LANES), F32),
        scratch_types=[pltpu.VMEM((TOP_K, SC_WINDOW), I32),
                       pltpu.VMEM((SC_WINDOW, ROW_TILES, LANES), F32),
                       pltpu.SemaphoreType.DMA],
        name="sc_collect")
    def run(y_hbm, dest_hbm, yg_hbm, idx_v, rows_v, sem):
        wid = lax.axis_index("s") * n_cores + lax.axis_index("c")

        @pl.loop(0, per_worker)
        def _(j):
            win = wid * per_worker + j
            pltpu.sync_copy(dest_hbm.at[win], idx_v)
            for kk in range(TOP_K):
                pltpu.async_copy(y_hbm.at[idx_v.at[kk]], rows_v, sem).wait()
                pltpu.sync_copy(rows_v, yg_hbm.at[kk, pl.ds(win * SC_WINDOW, SC_WINDOW)])

    return run(y_rows, dest_win)


def _expert_kernel(be_ref, nu_ref, x_ref, wgu_ref, bgu_ref, wd_ref, bd_ref, y_ref, wgu16, wd16):
    i = pl.program_id(0)
    e = be_ref[i]
    prev = be_ref[jnp.maximum(i - 1, 0)]

    @pl.when((i == 0) | (e != prev))
    def _():
        rows = 128

        def cast(r, carry):
            sl = pl.ds(pl.multiple_of(r * rows, rows), rows)
            wgu16[sl, :] = wgu_ref[0, sl, :].astype(BF16)
            wd16[sl, :] = wd_ref[0, sl, :].astype(BF16)
            return carry

        lax.fori_loop(0, D_MODEL // rows, cast, 0)

    @pl.when(i < nu_ref[0])
    def _():
        x = _load_row_tiles(x_ref, MOE_BLOCK).astype(BF16)
        bgu = bgu_ref[0]
        glu = _dot(x, wgu16[:, 0:D_FF]) + bgu[:, 0:D_FF]
        lin = _dot(x, wgu16[:, D_FF:]) + bgu[:, D_FF:]
        glu = jnp.minimum(glu, SWIGLU_LIMIT)
        lin = jnp.clip(lin, -SWIGLU_LIMIT, SWIGLU_LIMIT)
        act = glu * _sigmoid(SWIGLU_ALPHA * glu) * (lin + 1.0)
        y = _dot(act.astype(BF16), wd16[...]) + bd_ref[0]
        _store_row_tiles(y_ref, y)


def _experts(blk_e, n_used, xs2, w_gu, b_gu3, w_down, b_down3):
    row_blk = pl.BlockSpec((MOE_BLOCK * ROW_TILES, LANES),
                           lambda i, be, nu: (jnp.minimum(i, nu[0] - 1), 0))
    grid_spec = pltpu.PrefetchScalarGridSpec(
        num_scalar_prefetch=2,
        grid=(N_BLOCKS,),
        in_specs=[row_blk,
                  pl.BlockSpec((1, D_MODEL, 2 * D_FF), lambda i, be, nu: (be[i], 0, 0)),
                  pl.BlockSpec((1, 1, 2 * D_FF), lambda i, be, nu: (be[i], 0, 0)),
                  pl.BlockSpec((1, D_FF, D_MODEL), lambda i, be, nu: (be[i], 0, 0)),
                  pl.BlockSpec((1, 1, D_MODEL), lambda i, be, nu: (be[i], 0, 0))],
        out_specs=row_blk,
        scratch_shapes=[pltpu.VMEM((D_MODEL, 2 * D_FF), BF16),
                        pltpu.VMEM((D_FF, D_MODEL), BF16)],
    )
    return pl.pallas_call(
        _expert_kernel,
        grid_spec=grid_spec,
        out_shape=jax.ShapeDtypeStruct((N_SLOTS * ROW_TILES, LANES), F32),
        compiler_params=pltpu.CompilerParams(dimension_semantics=("arbitrary",),
                                             vmem_limit_bytes=VMEM_LIMIT),
        name="experts",
    )(blk_e, n_used, xs2, w_gu, b_gu3, w_down, b_down3)


def _final_kernel(yg_ref, gate_ref, x1_ref, mod_ref, gpost_ref, o_ref):
    gates = gate_ref[...]
    y = None
    for kk in range(TOP_K):
        part = _load_row_tiles(yg_ref.at[kk], TOK_TILE) * gates[:, kk:kk + 1]
        y = part if y is None else y + part
    gate_f = mod_ref[0][5:6]
    o_ref[...] = x1_ref[...] + gate_f * (_rms(y) * gpost_ref[...])


def _final(yg3, gates, x1, mod3, g_post):
    return pl.pallas_call(
        _final_kernel,
        grid=(N_TOK // TOK_TILE,),
        in_specs=[pl.BlockSpec((TOP_K, TOK_TILE * ROW_TILES, LANES), lambda i: (0, i, 0)),
                  pl.BlockSpec((TOK_TILE, TOP_K), lambda i: (i, 0)),
                  pl.BlockSpec((TOK_TILE, D_MODEL), lambda i: (i, 0)),
                  pl.BlockSpec((1, 6, D_MODEL), lambda i: (_tile_batch(i), 0, 0)),
                  pl.BlockSpec((1, D_MODEL), lambda i: (0, 0))],
        out_specs=pl.BlockSpec((None, TOK_TILE, D_MODEL),
                               lambda i: (_tile_batch(i), _tile_part(i), _tile_sub(i))),
        out_shape=jax.ShapeDtypeStruct((BATCH, SUB_MID, DIL_MID * D_MODEL), F32),
        compiler_params=pltpu.CompilerParams(vmem_limit_bytes=VMEM_LIMIT),
        name="combine_final",
    )(yg3, gates, x1, mod3, g_post)


def _routing_tables(idx, rank, counts):
    counts = counts.reshape(N_EXPERTS)
    padded = ((counts + MOE_BLOCK - 1) // MOE_BLOCK) * MOE_BLOCK
    pend = jnp.cumsum(padded)
    pstart = pend - padded
    dest = pstart[idx] + rank
    dest_win = dest.reshape(N_TOK // SC_WINDOW, SC_WINDOW, TOP_K).transpose(0, 2, 1)
    n_used = (pend[-1] // MOE_BLOCK).astype(I32).reshape(1)
    blk_start = jnp.arange(N_BLOCKS, dtype=I32) * MOE_BLOCK
    blk_e = jnp.sum(blk_start[:, None] >= pend[None, :], axis=1).astype(I32)
    last_e = jnp.max(jnp.where(counts > 0, jnp.arange(N_EXPERTS, dtype=I32), 0))
    blk_e = jnp.minimum(blk_e, last_e)
    return dest_win, blk_e, n_used


def kernel(x, c, w_ada, b_ada, g_pre_mix, g_post_mix, w_in, attn_norm_w, hgrn_lb, hgrn_norm_w,
           w_out, g_pre_ffn, g_post_ffn, w_router, b_router, w_gu, b_gu, w_down, b_down):
    c_pad = jnp.pad(c, ((0, SUBLANES - BATCH), (0, 0)))
    mod = _ada_mod(c_pad, w_ada[0], b_ada)
    mod3 = mod[:BATCH].reshape(BATCH, 6, D_MODEL)

    q4, k4, v4, q16, k16, v16, qr, fr, ir, gr = _inproj(x, mod3, g_pre_mix, w_in[0].astype(BF16))
    mid = lambda t: t.reshape(BATCH, DIL_MID, SUB_MID, D_ATTN)
    nat = lambda t: t.reshape(BATCH, SEQ, D_REC)
    ya4 = _attention(mid(q4), mid(k4), mid(v4), q16, k16, v16, attn_norm_w)
    yr = _hgrn(nat(qr), nat(fr), nat(ir), nat(gr), hgrn_lb, hgrn_norm_w)

    x1, h2, idx, gates, rank, counts = _mid(
        ya4, yr, x, mod3, g_post_mix, g_pre_ffn, w_out[0].astype(BF16), w_router[0], b_router)

    dest_win, blk_e, n_used = _routing_tables(idx, rank, counts)
    xs = _sc_dispatch(h2.reshape(N_TOK, ROW_TILES, LANES), dest_win)
    ys = _experts(blk_e, n_used, xs.reshape(N_SLOTS * ROW_TILES, LANES),
                  w_gu[0], b_gu[0].reshape(N_EXPERTS, 1, 2 * D_FF),
                  w_down[0], b_down[0].reshape(N_EXPERTS, 1, D_MODEL))
    yg = _sc_collect(ys.reshape(N_SLOTS, ROW_TILES, LANES), dest_win)
    out = _final(yg.reshape(TOP_K, N_TOK * ROW_TILES, LANES), gates, x1, mod3, g_post_ffn)
    return out.reshape(BATCH, SEQ, D_MODEL)
```

```python
import functools
import math

import jax
import jax.numpy as jnp
from jax import lax
from jax.experimental import pallas as pl
from jax.experimental.pallas import tpu as pltpu
from jax.experimental.pallas import tpu_sc as plsc

F32 = jnp.float32
BF16 = jnp.bfloat16
I32 = jnp.int32
HIGHEST = lax.Precision.HIGHEST

D_MODEL = 1024
BATCH = 4
SEQ = 4096
N_TOK = BATCH * SEQ
D_ATTN = 512
HEAD_DIM_A = 64
ATT_BLOCK = 128
DIL_MID = 4
DIL_MAX = 16
SUB_MID = SEQ // DIL_MID
SUB_MAX = SEQ // DIL_MAX
D_REC = 512
HGRN_CHUNK = 32
HGRN_SUPER = 256
N_EXPERTS = 32
TOP_K = 4
D_FF = 1024
SWIGLU_LIMIT = 7.0
SWIGLU_ALPHA = 1.702
EPS = 1e-6
NEG_BIG = -1e30
Q_SCALE = HEAD_DIM_A ** -0.5 * math.log2(math.e)

LANES = 128
SUBLANES = 8
ROW_TILES = D_MODEL // LANES

TOK_TILE = 512
MOE_BLOCK = 512
N_SLOTS = N_TOK * TOP_K + N_EXPERTS * MOE_BLOCK
N_BLOCKS = N_SLOTS // MOE_BLOCK
SC_WINDOW = 32
VMEM_LIMIT = 56 * 1024 * 1024


def _sigmoid(x):
    return 1.0 / (1.0 + jnp.exp(-x))


def _dot(a, b):
    return jnp.dot(a, b, preferred_element_type=F32)


def _dot_nt(a, b):
    return lax.dot_general(a, b, (((1,), (1,)), ((), ())), preferred_element_type=F32)


def _split_bf16(x):
    hi = x.astype(BF16)
    return hi, (x - hi.astype(F32)).astype(BF16)


def _rms(x):
    return x * lax.rsqrt(jnp.mean(x * x, axis=-1, keepdims=True) + EPS)


def _ada_kernel(c_ref, w_ref, b_ref, o_ref):
    c = c_ref[...]
    cond = c * _sigmoid(c)
    o_ref[...] = jnp.dot(cond, w_ref[...], precision=HIGHEST,
                         preferred_element_type=F32) + b_ref[...]


def _ada_mod(c_pad, w_ada, b_ada):
    n = w_ada.shape[1]
    tn = 1536
    return pl.pallas_call(
        _ada_kernel,
        grid=(n // tn,),
        in_specs=[pl.BlockSpec((SUBLANES, D_MODEL), lambda j: (0, 0)),
                  pl.BlockSpec((D_MODEL, tn), lambda j: (0, j)),
                  pl.BlockSpec((1, tn), lambda j: (0, j))],
        out_specs=pl.BlockSpec((SUBLANES, tn), lambda j: (0, j)),
        out_shape=jax.ShapeDtypeStruct((SUBLANES, n), F32),
        compiler_params=pltpu.CompilerParams(vmem_limit_bytes=VMEM_LIMIT),
        name="ada_mod",
    )(c_pad, w_ada, b_ada)


def _inproj_kernel(x_ref, mod_ref, g_ref, w_ref, q4, k4, v4, q16, k16, v16, qr, fr, ir, gr,
                   stage_nat, stage_mid):
    mod = mod_ref[0]
    shift, scale = mod[0:1], mod[1:2]
    h = _rms(x_ref[...]) * g_ref[...] * (1.0 + scale) + shift
    hb = h.astype(BF16)
    slabs = D_ATTN // LANES
    rows_mid = TOK_TILE // DIL_MID
    rows_max = TOK_TILE // DIL_MAX

    def proj(j):
        return _dot(hb, w_ref[:, j * D_ATTN:(j + 1) * D_ATTN])

    for j, (o_mid, o_max) in enumerate(((q4, q16), (k4, k16), (v4, v16))):
        r = proj(j)
        if j == 0:
            r = r * Q_SCALE
        for cs in range(slabs):
            stage_nat[cs] = r[:, cs * LANES:(cs + 1) * LANES]
        for cs in range(slabs):
            lanes = slice(cs * LANES, (cs + 1) * LANES)
            for sub in range(DIL_MID):
                piece = stage_nat[cs, pl.ds(sub, rows_mid, stride=DIL_MID), :]
                o_mid[sub, :, lanes] = piece.astype(BF16)
                stage_mid[cs, sub] = piece
            for sub in range(DIL_MAX):
                piece = stage_mid[cs, sub % DIL_MID, pl.ds(sub // DIL_MID, rows_max, stride=DIL_MID), :]
                o_max[sub, :, lanes] = piece.astype(BF16)
    for j, o_ref in enumerate((qr, fr, ir, gr)):
        o_ref[...] = proj(3 + j)


def _inproj(x2, mod3, g_pre, w_in_bf16):
    tiles_per_seq = SEQ // TOK_TILE
    rows_mid = TOK_TILE // DIL_MID
    rows_max = TOK_TILE // DIL_MAX
    mid = pl.BlockSpec((None, DIL_MID, rows_mid, D_ATTN),
                       lambda i: (i // tiles_per_seq, 0, i % tiles_per_seq, 0))
    mx = pl.BlockSpec((None, DIL_MAX, rows_max, D_ATTN),
                      lambda i: (i // tiles_per_seq, 0, i % tiles_per_seq, 0))
    nat = pl.BlockSpec((TOK_TILE, D_REC), lambda i: (i, 0))
    mid_shape = jax.ShapeDtypeStruct((BATCH, DIL_MID, SUB_MID, D_ATTN), BF16)
    mx_shape = jax.ShapeDtypeStruct((BATCH, DIL_MAX, SUB_MAX, D_ATTN), BF16)
    nat_shape = jax.ShapeDtypeStruct((N_TOK, D_REC), F32)
    return pl.pallas_call(
        _inproj_kernel,
        grid=(N_TOK // TOK_TILE,),
        in_specs=[pl.BlockSpec((TOK_TILE, D_MODEL), lambda i: (i, 0)),
                  pl.BlockSpec((1, 6, D_MODEL), lambda i: (i // tiles_per_seq, 0, 0)),
                  pl.BlockSpec((1, D_MODEL), lambda i: (0, 0)),
                  pl.BlockSpec(w_in_bf16.shape, lambda i: (0, 0))],
        out_specs=[mid, mid, mid, mx, mx, mx, nat, nat, nat, nat],
        out_shape=[mid_shape] * 3 + [mx_shape] * 3 + [nat_shape] * 4,
        scratch_shapes=[pltpu.VMEM((D_ATTN // LANES, TOK_TILE, LANES), F32),
                        pltpu.VMEM((D_ATTN // LANES, DIL_MID, rows_mid, LANES), F32)],
        compiler_params=pltpu.CompilerParams(vmem_limit_bytes=VMEM_LIMIT),
        name="inproj",
    )(x2, mod3, g_pre, w_in_bf16)


def _attn_kernel(q4_ref, k4_ref, v4_ref, q16_ref, k16_ref, v16_ref, nw_ref, o_ref,
                 bias_scr, o_scr, lse_scr):
    bw = ATT_BLOCK
    lane = lax.broadcasted_iota(I32, (bw, LANES), 1)
    head0 = lane < HEAD_DIM_A
    head_masks = (jnp.where(head0, 1.0, 0.0).astype(BF16), jnp.where(head0, 0.0, 1.0).astype(BF16))
    eye = jnp.where(lax.broadcasted_iota(I32, (bw, bw), 0) == lax.broadcasted_iota(I32, (bw, bw), 1),
                    1.0, 0.0).astype(BF16)

    jr = lax.broadcasted_iota(I32, (bw, 2 * bw), 1)
    rc = lax.broadcasted_iota(I32, (bw, 2 * bw), 0)
    jp = DIL_MID * (jr % (2 * bw // DIL_MID)) + jr // (2 * bw // DIL_MID)
    rp = DIL_MID * (rc % (bw // DIL_MID)) + rc // (bw // DIL_MID)
    for var, (j, r) in enumerate(((jr, rc), (jr, rc), (jp, rp), (jp, rp))):
        valid = (j <= r) if var % 2 else ((j >= r) & (j <= r + bw))
        bias_scr[var] = jnp.where(valid, 0.0, NEG_BIG).astype(BF16)

    def attend_group(loaded):
        scores = [_dot_nt(q * hm, k) + _dot(eye, bias)
                  for (q, k, _, bias) in loaded for hm in head_masks]
        probs = []
        for s in scores:
            m = jnp.max(s, axis=-1, keepdims=True)
            p = jnp.exp2(s - m)
            probs.append((p.astype(BF16), jnp.sum(p, axis=-1, keepdims=True), m))
        pvs = [_dot(p, loaded[n // 2][2]) for n, (p, _, _) in enumerate(probs)]
        results = []
        for u in range(len(loaded)):
            (_, l0, m0), (_, l1, m1) = probs[2 * u], probs[2 * u + 1]
            o = jnp.where(head0, pvs[2 * u] * (1.0 / l0), pvs[2 * u + 1] * (1.0 / l1))
            lse = jnp.where(head0, jnp.broadcast_to(m0 + jnp.log2(l0), (bw, LANES)),
                            jnp.broadcast_to(m1 + jnp.log2(l1), (bw, LANES)))
            results.append((o, lse))
        return results

    piece = bw // DIL_MID
    group = 4

    def group_d1(g, carry):
        loaded, starts = [], []
        for i in range(group):
            blk = g * group + i
            qs = pl.multiple_of(blk * piece, piece)
            ks = pl.multiple_of(jnp.maximum(blk - 1, 0) * piece, piece)
            q = jnp.concatenate([q4_ref[r, pl.ds(qs, piece), :] for r in range(DIL_MID)], axis=0)
            k = jnp.concatenate([k4_ref[r, pl.ds(ks, 2 * piece), :] for r in range(DIL_MID)], axis=0)
            v = jnp.concatenate([v4_ref[r, pl.ds(ks, 2 * piece), :] for r in range(DIL_MID)], axis=0)
            loaded.append((q, k, v, bias_scr[jnp.where(blk == 0, 3, 2)]))
            starts.append(qs)
        for qs, (o, lse) in zip(starts, attend_group(loaded)):
            for r in range(DIL_MID):
                o_scr[0, r, pl.ds(qs, piece), :] = o[r * piece:(r + 1) * piece]
                lse_scr[0, r, pl.ds(qs, piece), :] = lse[r * piece:(r + 1) * piece]
        return carry

    def group_d4(blk, carry):
        qs = pl.multiple_of(blk * bw, bw)
        ks = pl.multiple_of(jnp.maximum(blk - 1, 0) * bw, bw)
        bias = bias_scr[jnp.where(blk == 0, 1, 0)]
        loaded = [(q4_ref[r, pl.ds(qs, bw), :], k4_ref[r, pl.ds(ks, 2 * bw), :],
                   v4_ref[r, pl.ds(ks, 2 * bw), :], bias) for r in range(DIL_MID)]
        for r, (o, lse) in enumerate(attend_group(loaded)):
            o_scr[1, r, pl.ds(qs, bw), :] = o
            lse_scr[1, r, pl.ds(qs, bw), :] = lse
        return carry

    def group_d16(g, carry):
        loaded, dsts = [], []
        for i in range(group):
            r, blk = g * (group // 2) + i // 2, i % 2
            loaded.append((q16_ref[r, blk * bw:(blk + 1) * bw, :], k16_ref[r], v16_ref[r],
                           bias_scr[1 - blk]))
            dsts.append((r % DIL_MID, pl.ds(blk * (bw * DIL_MID) + r // DIL_MID, bw, stride=DIL_MID)))
        for (sub, dst), (o, lse) in zip(dsts, attend_group(loaded)):
            o_scr[2, sub, dst, :] = o
            lse_scr[2, sub, dst, :] = lse
        return carry

    units = SEQ // bw
    lax.fori_loop(0, units // group, group_d1, 0)
    lax.fori_loop(0, units // DIL_MID, group_d4, 0)
    lax.fori_loop(0, units // group, group_d16, 0)

    rows = 256
    hi = lax.broadcasted_iota(I32, (LANES, LANES), 0) // HEAD_DIM_A
    hj = lax.broadcasted_iota(I32, (LANES, LANES), 1) // HEAD_DIM_A
    head_sum = jnp.where(hi == hj, 1.0, 0.0).astype(BF16)
    nw = nw_ref[...]

    def merge(t, carry):
        r = t // (SUB_MID // rows)
        start = pl.multiple_of((t % (SUB_MID // rows)) * rows, rows)
        sl = pl.ds(start, rows)
        l0, l1, l2 = lse_scr[0, r, sl, :], lse_scr[1, r, sl, :], lse_scr[2, r, sl, :]
        mx = jnp.maximum(jnp.maximum(l0, l1), l2)
        w0, w1, w2 = jnp.exp2(l0 - mx), jnp.exp2(l1 - mx), jnp.exp2(l2 - mx)
        o = (w0 * o_scr[0, r, sl, :] + w1 * o_scr[1, r, sl, :] + w2 * o_scr[2, r, sl, :]) / (w0 + w1 + w2)
        sq_hi, sq_lo = _split_bf16(o * o)
        ms = (_dot(sq_hi, head_sum) + _dot(sq_lo, head_sum)) * (1.0 / HEAD_DIM_A)
        o_ref[pl.ds(start * DIL_MID + r, rows, stride=DIL_MID), :] = o * lax.rsqrt(ms + EPS) * nw
        return carry

    lax.fori_loop(0, SEQ // rows, merge, 0, unroll=2)


def _attention(q4, k4, v4, q16, k16, v16, attn_norm_w):
    hp = D_ATTN // LANES
    mid = pl.BlockSpec((None, DIL_MID, SUB_MID, LANES), lambda b, h: (b, 0, 0, h))
    mx = pl.BlockSpec((None, DIL_MAX, SUB_MAX, LANES), lambda b, h: (b, 0, 0, h))
    scr = pltpu.VMEM((3, DIL_MID, SUB_MID, LANES), F32)
    return pl.pallas_call(
        _attn_kernel,
        grid=(BATCH, hp),
        in_specs=[mid, mid, mid, mx, mx, mx, pl.BlockSpec((1, LANES), lambda b, h: (0, h))],
        out_specs=pl.BlockSpec((None, SEQ, LANES), lambda b, h: (b, 0, h)),
        out_shape=jax.ShapeDtypeStruct((BATCH, SEQ, D_ATTN), F32),
        scratch_shapes=[pltpu.VMEM((4, ATT_BLOCK, 2 * ATT_BLOCK), BF16), scr, scr],
        compiler_params=pltpu.CompilerParams(vmem_limit_bytes=VMEM_LIMIT),
        name="dilated_attn",
    )(q4, k4, v4, q16, k16, v16, attn_norm_w)


def _hgrn_kernel(qr_ref, fr_ref, ir_ref, gr_ref, lb_ref, nw_ref, o_ref,
                 qe_scr, oi_scr, delta_scr, dec_scr, st_scr):
    sup, c = HGRN_SUPER, HGRN_CHUNK
    nch = sup // c
    n_sup = SEQ // sup
    lbp = lb_ref[...]
    lmx = jnp.max(lbp, axis=0, keepdims=True)
    ex = jnp.exp(lbp - lmx)
    lb = ex[0:1] / (ex[0:1] + ex[1:2])
    nw = nw_ref[...]

    ri = lax.broadcasted_iota(I32, (sup, sup), 0)
    ci = lax.broadcasted_iota(I32, (sup, sup), 1)
    same_chunk = (ri // c) == (ci // c)
    causal = same_chunk & (ci <= ri)
    cum_op = jnp.where(causal, 1.0, 0.0).astype(BF16)

    def chunk_rows(kd, ch):
        parts = []
        if ch > 0:
            parts.append(jnp.zeros((ch * c, LANES), BF16))
        parts.append(kd[ch * c:(ch + 1) * c])
        if ch < nch - 1:
            parts.append(jnp.zeros(((nch - 1 - ch) * c, LANES), BF16))
        return jnp.concatenate(parts, axis=0)

    group = 2

    def independent(g, carry):
        ts = [g * group + i for i in range(group)]
        sls = [pl.ds(pl.multiple_of(t * sup, sup), sup) for t in ts]
        pre = []
        for sl in sls:
            f = lb + (1.0 - lb) * _sigmoid(fr_ref[sl, :])
            logf_hi, logf_lo = _split_bf16(jnp.log(f))
            pre.append((1.0 - f, _dot(cum_op, logf_hi) + _dot(cum_op, logf_lo)))
        mid = []
        for t, sl, (kk, b) in zip(ts, sls, pre):
            b_last = jnp.concatenate(
                [jnp.broadcast_to(b[(ch + 1) * c - 1:(ch + 1) * c], (c, LANES)) for ch in range(nch)],
                axis=0)
            q = qr_ref[sl, :]
            qeb = (q * _sigmoid(q) * jnp.exp(b)).astype(BF16)
            ke = (kk * jnp.exp(-b)).astype(BF16)
            kd = (kk * jnp.exp(b_last - b)).astype(BF16)
            qe_scr[sl, :] = qeb
            dec_rows = jnp.concatenate([b_last[ch * c:ch * c + 1] for ch in range(nch)], axis=0)
            dec_scr[pl.ds(pl.multiple_of(t * nch, nch), nch), :] = jnp.exp(dec_rows)
            v = ir_ref[sl, :]
            vt = v.T.astype(BF16)
            scores = _dot_nt(qeb, ke)
            for pair in range(nch // 2):
                rhs = jnp.concatenate([chunk_rows(kd, 2 * pair), chunk_rows(kd, 2 * pair + 1)], axis=1)
                d2 = _dot(vt, rhs)
                delta_scr[t * nch + 2 * pair] = d2[:, 0:LANES]
                delta_scr[t * nch + 2 * pair + 1] = d2[:, LANES:]
            mid.append((scores, v.astype(BF16)))
        for sl, (scores, vb) in zip(sls, mid):
            a = jnp.where(causal, scores, 0.0)
            oi_scr[sl, :] = _dot(a.astype(BF16), vb)
        return carry

    lax.fori_loop(0, n_sup // group, independent, 0)

    def recur(ch, st):
        st_scr[ch] = st.astype(BF16)
        return st * dec_scr[pl.ds(ch, 1), :] + delta_scr[ch]

    lax.fori_loop(0, SEQ // c, recur, jnp.zeros((LANES, LANES), F32), unroll=8)

    def finish(t, carry):
        sl = pl.ds(pl.multiple_of(t * sup, sup), sup)
        parts = [_dot_nt(qe_scr[pl.ds(pl.multiple_of(t * sup + ch * c, c), c), :], st_scr[t * nch + ch])
                 for ch in range(nch)]
        o = oi_scr[sl, :] + jnp.concatenate(parts, axis=0)
        g = gr_ref[sl, :]
        o_ref[sl, :] = (_rms(o) * nw * (g * _sigmoid(g))).astype(o_ref.dtype)
        return carry

    lax.fori_loop(0, n_sup, finish, 0, unroll=2)


def _hgrn(qr, fr, ir, gr, hgrn_lb, hgrn_norm_w):
    nh = D_REC // LANES
    n_chunks = SEQ // HGRN_CHUNK
    blk = pl.BlockSpec((None, SEQ, LANES), lambda b, h: (b, 0, h))
    return pl.pallas_call(
        _hgrn_kernel,
        grid=(BATCH, nh),
        in_specs=[blk, blk, blk, blk,
                  pl.BlockSpec((2, LANES), lambda b, h: (0, h)),
                  pl.BlockSpec((1, LANES), lambda b, h: (0, h))],
        out_specs=blk,
        out_shape=jax.ShapeDtypeStruct((BATCH, SEQ, D_REC), BF16),
        scratch_shapes=[pltpu.VMEM((SEQ, LANES), BF16),
                        pltpu.VMEM((SEQ, LANES), F32),
                        pltpu.VMEM((n_chunks, LANES, LANES), F32),
                        pltpu.VMEM((n_chunks, LANES), F32),
                        pltpu.VMEM((n_chunks, LANES, LANES), BF16)],
        compiler_params=pltpu.CompilerParams(vmem_limit_bytes=VMEM_LIMIT),
        name="hgrn2",
    )(qr, fr, ir, gr, hgrn_lb, hgrn_norm_w)


def _tile_batch(i):
    return i // (SEQ // TOK_TILE)


def _store_row_tiles(ref, val):
    rows = val.shape[0]
    for j in range(ROW_TILES):
        ref[pl.ds(j, rows, stride=ROW_TILES), :] = val[:, j * LANES:(j + 1) * LANES]


def _load_row_tiles(ref, rows):
    return jnp.concatenate(
        [ref[pl.ds(j, rows, stride=ROW_TILES), :] for j in range(ROW_TILES)], axis=1)


def _mid_kernel(ya_ref, yr_ref, x_ref, mod_ref, gpost_ref, gpre_ref, wo_ref, wr_ref, br_ref,
                x1_ref, h2_ref, idx_ref, gate_ref, rank_ref, cnt_ref, carry_ref):
    i = pl.program_id(0)

    @pl.when(i == 0)
    def _():
        carry_ref[...] = jnp.zeros_like(carry_ref)

    mod = mod_ref[0]
    gate_m, shift_f, scale_f = mod[2:3], mod[3:4], mod[4:5]
    y = _dot(ya_ref[...].astype(BF16), wo_ref[0:D_ATTN, :]) + _dot(yr_ref[...], wo_ref[D_ATTN:, :])
    x1 = x_ref[...] + gate_m * (_rms(y) * gpost_ref[...])
    x1_ref[...] = x1
    h2 = _rms(x1) * gpre_ref[...] * (1.0 + scale_f) + shift_f
    _store_row_tiles(h2_ref, h2)

    h2_hi, h2_lo = _split_bf16(h2)
    wr_hi, wr_lo = _split_bf16(wr_ref[...])
    logits = _dot(h2_hi, wr_hi) + _dot(h2_lo, wr_hi) + _dot(h2_hi, wr_lo) + br_ref[...]
    tm = logits.shape[0]
    eidx = lax.broadcasted_iota(I32, (tm, N_EXPERTS), 1)
    work = logits
    vals, idxs = [], []
    onehot = jnp.zeros((tm, N_EXPERTS), F32)
    for _ in range(TOP_K):
        m = jnp.max(work, axis=-1, keepdims=True)
        sel = jnp.min(jnp.where(work == m, eidx, N_EXPERTS), axis=-1, keepdims=True)
        hit = eidx == sel
        work = jnp.where(hit, -jnp.inf, work)
        onehot = jnp.where(hit, 1.0, onehot)
        vals.append(m)
        idxs.append(sel)
    ex = [jnp.exp(vv - vals[0]) for vv in vals]
    den = ex[0] + ex[1] + ex[2] + ex[3]

    ri = lax.broadcasted_iota(I32, (tm, tm), 0)
    ci = lax.broadcasted_iota(I32, (tm, tm), 1)
    strict_lower = jnp.where(ci < ri, 1.0, 0.0).astype(BF16)
    before = _dot(strict_lower, onehot.astype(BF16)) + carry_ref[...]
    lane4 = lax.broadcasted_iota(I32, (tm, TOP_K), 1)
    idx_o = jnp.zeros((tm, TOP_K), I32)
    gate_o = jnp.zeros((tm, TOP_K), F32)
    rank_o = jnp.zeros((tm, TOP_K), I32)
    for kk in range(TOP_K):
        rk = jnp.sum(jnp.where(eidx == idxs[kk], before, 0.0), axis=-1, keepdims=True)
        idx_o = jnp.where(lane4 == kk, idxs[kk], idx_o)
        gate_o = jnp.where(lane4 == kk, ex[kk] / den, gate_o)
        rank_o = jnp.where(lane4 == kk, rk.astype(I32), rank_o)
    idx_ref[...] = idx_o
    gate_ref[...] = gate_o
    rank_ref[...] = rank_o
    total = carry_ref[...] + jnp.sum(onehot, axis=0, keepdims=True)
    carry_ref[...] = total
    cnt_ref[...] = total.astype(I32)


def _mid(ya, yr, x2, mod3, g_post, g_pre, w_out_bf16, w_router, b_router):
    tok = lambda w: pl.BlockSpec((TOK_TILE, w), lambda i: (i, 0))
    const = lambda s: pl.BlockSpec(s, lambda i: (0,) * len(s))
    return pl.pallas_call(
        _mid_kernel,
        grid=(N_TOK // TOK_TILE,),
        in_specs=[tok(D_ATTN), tok(D_REC), tok(D_MODEL),
                  pl.BlockSpec((1, 6, D_MODEL), lambda i: (_tile_batch(i), 0, 0)),
                  const((1, D_MODEL)), const((1, D_MODEL)),
                  const((D_MODEL, D_MODEL)), const((D_MODEL, N_EXPERTS)), const((1, N_EXPERTS))],
        out_specs=[tok(D_MODEL),
                   pl.BlockSpec((TOK_TILE * ROW_TILES, LANES), lambda i: (i, 0)),
                   tok(TOP_K), tok(TOP_K), tok(TOP_K), const((1, N_EXPERTS))],
        out_shape=[jax.ShapeDtypeStruct((N_TOK, D_MODEL), F32),
                   jax.ShapeDtypeStruct((N_TOK * ROW_TILES, LANES), F32),
                   jax.ShapeDtypeStruct((N_TOK, TOP_K), I32),
                   jax.ShapeDtypeStruct((N_TOK, TOP_K), F32),
                   jax.ShapeDtypeStruct((N_TOK, TOP_K), I32),
                   jax.ShapeDtypeStruct((1, N_EXPERTS), I32)],
        scratch_shapes=[pltpu.VMEM((1, N_EXPERTS), F32)],
        compiler_params=pltpu.CompilerParams(dimension_semantics=("arbitrary",),
                                             vmem_limit_bytes=VMEM_LIMIT),
        name="outproj_router",
    )(ya, yr, x2, mod3, g_post, g_pre, w_out_bf16, w_router, b_router)


def _sc_mesh():
    return plsc.VectorSubcoreMesh(core_axis_name="c", subcore_axis_name="s")


def _sc_worker_count():
    info = plsc.get_sparse_core_info()
    return info.num_cores, info.num_cores * info.num_subcores


def _sc_dispatch(h_rows, dest_win):
    n_cores, n_workers = _sc_worker_count()
    n_win = N_TOK // SC_WINDOW
    per_worker = n_win // n_workers

    @functools.partial(
        pl.kernel, mesh=_sc_mesh(),
        out_type=jax.ShapeDtypeStruct((N_SLOTS, ROW_TILES, LANES), F32),
        scratch_types=[pltpu.VMEM((TOP_K, SC_WINDOW), I32),
                       pltpu.VMEM((SC_WINDOW, ROW_TILES, LANES), F32),
                       pltpu.SemaphoreType.DMA],
        name="sc_dispatch")
    def run(h_hbm, dest_hbm, xs_hbm, idx_v, rows_v, sem):
        wid = lax.axis_index("s") * n_cores + lax.axis_index("c")

        @pl.loop(0, per_worker)
        def _(j):
            win = wid * per_worker + j
            pltpu.sync_copy(dest_hbm.at[win], idx_v)
            pltpu.sync_copy(h_hbm.at[pl.ds(win * SC_WINDOW, SC_WINDOW)], rows_v)
            copies = [pltpu.async_copy(rows_v, xs_hbm.at[idx_v.at[kk]], sem)
                      for kk in range(TOP_K)]
            for cp in copies:
                cp.wait()

    return run(h_rows, dest_win)


def _sc_collect(y_rows, dest_win):
    n_cores, n_workers = _sc_worker_count()
    n_win = N_TOK // SC_WINDOW
    per_worker = n_win // n_workers

    @functools.partial(
        pl.kernel, mesh=_sc_mesh(),
        out_type=jax.ShapeDtypeStruct((TOP_K, N_TOK, ROW_TILES, LANES), F32),
        scratch_types=[pltpu.VMEM((TOP_K, SC_WINDOW), I32),
                       pltpu.VMEM((SC_WINDOW, ROW_TILES, LANES), F32),
                       pltpu.SemaphoreType.DMA],
        name="sc_collect")
    def run(y_hbm, dest_hbm, yg_hbm, idx_v, rows_v, sem):
        wid = lax.axis_index("s") * n_cores + lax.axis_index("c")

        @pl.loop(0, per_worker)
        def _(j):
            win = wid * per_worker + j
            pltpu.sync_copy(dest_hbm.at[win], idx_v)
            for kk in range(TOP_K):
                pltpu.async_copy(y_hbm.at[idx_v.at[kk]], rows_v, sem).wait()
                pltpu.sync_copy(rows_v, yg_hbm.at[kk, pl.ds(win * SC_WINDOW, SC_WINDOW)])

    return run(y_rows, dest_win)


def _expert_kernel(be_ref, nu_ref, x_ref, wgu_ref, bgu_ref, wd_ref, bd_ref, y_ref, wgu16, wd16):
    i = pl.program_id(0)
    e = be_ref[i]
    prev = be_ref[jnp.maximum(i - 1, 0)]

    @pl.when((i == 0) | (e != prev))
    def _():
        rows = 128

        def cast(r, carry):
            sl = pl.ds(pl.multiple_of(r * rows, rows), rows)
            wgu16[sl, :] = wgu_ref[0, sl, :].astype(BF16)
            wd16[sl, :] = wd_ref[0, sl, :].astype(BF16)
            return carry

        lax.fori_loop(0, D_MODEL // rows, cast, 0)

    @pl.when(i < nu_ref[0])
    def _():
        x = _load_row_tiles(x_ref, MOE_BLOCK).astype(BF16)
        bgu = bgu_ref[0]
        glu = _dot(x, wgu16[:, 0:D_FF]) + bgu[:, 0:D_FF]
        lin = _dot(x, wgu16[:, D_FF:]) + bgu[:, D_FF:]
        glu = jnp.minimum(glu, SWIGLU_LIMIT)
        lin = jnp.clip(lin, -SWIGLU_LIMIT, SWIGLU_LIMIT)
        act = glu * _sigmoid(SWIGLU_ALPHA * glu) * (lin + 1.0)
        y = _dot(act.astype(BF16), wd16[...]) + bd_ref[0]
        _store_row_tiles(y_ref, y)


def _experts(blk_e, n_used, xs2, w_gu, b_gu3, w_down, b_down3):
    row_blk = pl.BlockSpec((MOE_BLOCK * ROW_TILES, LANES),
                           lambda i, be, nu: (jnp.minimum(i, nu[0] - 1), 0))
    grid_spec = pltpu.PrefetchScalarGridSpec(
        num_scalar_prefetch=2,
        grid=(N_BLOCKS,),
        in_specs=[row_blk,
                  pl.BlockSpec((1, D_MODEL, 2 * D_FF), lambda i, be, nu: (be[i], 0, 0)),
                  pl.BlockSpec((1, 1, 2 * D_FF), lambda i, be, nu: (be[i], 0, 0)),
                  pl.BlockSpec((1, D_FF, D_MODEL), lambda i, be, nu: (be[i], 0, 0)),
                  pl.BlockSpec((1, 1, D_MODEL), lambda i, be, nu: (be[i], 0, 0))],
        out_specs=row_blk,
        scratch_shapes=[pltpu.VMEM((D_MODEL, 2 * D_FF), BF16),
                        pltpu.VMEM((D_FF, D_MODEL), BF16)],
    )
    return pl.pallas_call(
        _expert_kernel,
        grid_spec=grid_spec,
        out_shape=jax.ShapeDtypeStruct((N_SLOTS * ROW_TILES, LANES), F32),
        compiler_params=pltpu.CompilerParams(dimension_semantics=("arbitrary",),
                                             vmem_limit_bytes=VMEM_LIMIT),
        name="experts",
    )(blk_e, n_used, xs2, w_gu, b_gu3, w_down, b_down3)


def _final_kernel(yg_ref, gate_ref, x1_ref, mod_ref, gpost_ref, o_ref):
    gates = gate_ref[...]
    y = None
    for kk in range(TOP_K):
        part = _load_row_tiles(yg_ref.at[kk], TOK_TILE) * gates[:, kk:kk + 1]
        y = part if y is None else y + part
    gate_f = mod_ref[0][5:6]
    o_ref[...] = x1_ref[...] + gate_f * (_rms(y) * gpost_ref[...])


def _final(yg3, gates, x1, mod3, g_post):
    return pl.pallas_call(
        _final_kernel,
        grid=(N_TOK // TOK_TILE,),
        in_specs=[pl.BlockSpec((TOP_K, TOK_TILE * ROW_TILES, LANES), lambda i: (0, i, 0)),
                  pl.BlockSpec((TOK_TILE, TOP_K), lambda i: (i, 0)),
                  pl.BlockSpec((TOK_TILE, D_MODEL), lambda i: (i, 0)),
                  pl.BlockSpec((1, 6, D_MODEL), lambda i: (_tile_batch(i), 0, 0)),
                  pl.BlockSpec((1, D_MODEL), lambda i: (0, 0))],
        out_specs=pl.BlockSpec((TOK_TILE, D_MODEL), lambda i: (i, 0)),
        out_shape=jax.ShapeDtypeStruct((N_TOK, D_MODEL), F32),
        compiler_params=pltpu.CompilerParams(vmem_limit_bytes=VMEM_LIMIT),
        name="combine_final",
    )(yg3, gates, x1, mod3, g_post)


def _routing_tables(idx, rank, counts):
    counts = counts.reshape(N_EXPERTS)
    padded = ((counts + MOE_BLOCK - 1) // MOE_BLOCK) * MOE_BLOCK
    pend = jnp.cumsum(padded)
    pstart = pend - padded
    dest = pstart[idx] + rank
    dest_win = dest.reshape(N_TOK // SC_WINDOW, SC_WINDOW, TOP_K).transpose(0, 2, 1)
    n_used = (pend[-1] // MOE_BLOCK).astype(I32).reshape(1)
    blk_start = jnp.arange(N_BLOCKS, dtype=I32) * MOE_BLOCK
    blk_e = jnp.sum(blk_start[:, None] >= pend[None, :], axis=1).astype(I32)
    last_e = jnp.max(jnp.where(counts > 0, jnp.arange(N_EXPERTS, dtype=I32), 0))
    blk_e = jnp.minimum(blk_e, last_e)
    return dest_win, blk_e, n_used


def kernel(x, c, w_ada, b_ada, g_pre_mix, g_post_mix, w_in, attn_norm_w, hgrn_lb, hgrn_norm_w,
           w_out, g_pre_ffn, g_post_ffn, w_router, b_router, w_gu, b_gu, w_down, b_down):
    c_pad = jnp.pad(c, ((0, SUBLANES - BATCH), (0, 0)))
    mod = _ada_mod(c_pad, w_ada[0], b_ada)
    mod3 = mod[:BATCH].reshape(BATCH, 6, D_MODEL)

    x2 = x.reshape(N_TOK, D_MODEL)
    q4, k4, v4, q16, k16, v16, qr, fr, ir, gr = _inproj(x2, mod3, g_pre_mix, w_in[0].astype(BF16))
    nat = lambda t: t.reshape(BATCH, SEQ, D_REC)
    ya = _attention(q4, k4, v4, q16, k16, v16, attn_norm_w)
    yr = _hgrn(nat(qr), nat(fr), nat(ir), nat(gr), hgrn_lb, hgrn_norm_w)

    x1, h2, idx, gates, rank, counts = _mid(
        ya.reshape(N_TOK, D_ATTN), yr.reshape(N_TOK, D_REC), x2, mod3, g_post_mix, g_pre_ffn,
        w_out[0].astype(BF16), w_router[0], b_router)

    dest_win, blk_e, n_used = _routing_tables(idx, rank, counts)
    xs = _sc_dispatch(h2.reshape(N_TOK, ROW_TILES, LANES), dest_win)
    ys = _experts(blk_e, n_used, xs.reshape(N_SLOTS * ROW_TILES, LANES),
                  w_gu[0], b_gu[0].reshape(N_EXPERTS, 1, 2 * D_FF),
                  w_down[0], b_down[0].reshape(N_EXPERTS, 1, D_MODEL))
    yg = _sc_collect(ys.reshape(N_SLOTS, ROW_TILES, LANES), dest_win)
    out = _final(yg.reshape(TOP_K, N_TOK * ROW_TILES, LANES), gates, x1, mod3, g_post_ffn)
    return out.reshape(BATCH, SEQ, D_MODEL)
```

```python
import functools
import math

import jax
import jax.numpy as jnp
from jax import lax
from jax.experimental import pallas as pl
from jax.experimental.pallas import tpu as pltpu
from jax.experimental.pallas import tpu_sc as plsc

F32 = jnp.float32
BF16 = jnp.bfloat16
I32 = jnp.int32
HIGHEST = lax.Precision.HIGHEST

D_MODEL = 1024
BATCH = 4
SEQ = 4096
N_TOK = BATCH * SEQ
D_ATTN = 512
HEAD_DIM_A = 64
ATT_BLOCK = 128
DIL_MID = 4
DIL_MAX = 16
SUB_MID = SEQ // DIL_MID
SUB_MAX = SEQ // DIL_MAX
D_REC = 512
HGRN_CHUNK = 32
HGRN_SUPER = 256
N_EXPERTS = 32
TOP_K = 4
D_FF = 1024
SWIGLU_LIMIT = 7.0
SWIGLU_ALPHA = 1.702
EPS = 1e-6
NEG_BIG = -1e30
Q_SCALE = HEAD_DIM_A ** -0.5 * math.log2(math.e)

LANES = 128
SUBLANES = 8
ROW_TILES = D_MODEL // LANES

TOK_TILE = 512
MOE_BLOCK = 512
N_SLOTS = N_TOK * TOP_K + N_EXPERTS * MOE_BLOCK
N_BLOCKS = N_SLOTS // MOE_BLOCK
SC_WINDOW = 32
VMEM_LIMIT = 56 * 1024 * 1024


def _sigmoid(x):
    return 1.0 / (1.0 + jnp.exp(-x))


def _dot(a, b):
    return jnp.dot(a, b, preferred_element_type=F32)


def _dot_nt(a, b):
    return lax.dot_general(a, b, (((1,), (1,)), ((), ())), preferred_element_type=F32)


def _split_bf16(x):
    hi = x.astype(BF16)
    return hi, (x - hi.astype(F32)).astype(BF16)


def _rms(x):
    return x * lax.rsqrt(jnp.mean(x * x, axis=-1, keepdims=True) + EPS)


def _ada_kernel(c_ref, w_ref, b_ref, o_ref):
    c = c_ref[...]
    cond = c * _sigmoid(c)
    o_ref[...] = jnp.dot(cond, w_ref[...], precision=HIGHEST,
                         preferred_element_type=F32) + b_ref[...]


def _ada_mod(c_pad, w_ada, b_ada):
    n = w_ada.shape[1]
    tn = 1536
    return pl.pallas_call(
        _ada_kernel,
        grid=(n // tn,),
        in_specs=[pl.BlockSpec((SUBLANES, D_MODEL), lambda j: (0, 0)),
                  pl.BlockSpec((D_MODEL, tn), lambda j: (0, j)),
                  pl.BlockSpec((1, tn), lambda j: (0, j))],
        out_specs=pl.BlockSpec((SUBLANES, tn), lambda j: (0, j)),
        out_shape=jax.ShapeDtypeStruct((SUBLANES, n), F32),
        compiler_params=pltpu.CompilerParams(vmem_limit_bytes=VMEM_LIMIT),
        name="ada_mod",
    )(c_pad, w_ada, b_ada)


def _inproj_kernel(x_ref, mod_ref, g_ref, w_ref, q4, k4, v4, q16, k16, v16, qr, fr, ir, gr,
                   stage_nat, stage_mid):
    mod = mod_ref[0]
    shift, scale = mod[0:1], mod[1:2]
    h = _rms(x_ref[...]) * g_ref[...] * (1.0 + scale) + shift
    hb = h.astype(BF16)
    slabs = D_ATTN // LANES
    rows_mid = TOK_TILE // DIL_MID
    rows_max = TOK_TILE // DIL_MAX

    def proj(j):
        return _dot(hb, w_ref[:, j * D_ATTN:(j + 1) * D_ATTN])

    for j, (o_mid, o_max) in enumerate(((q4, q16), (k4, k16), (v4, v16))):
        r = proj(j)
        if j == 0:
            r = r * Q_SCALE
        for cs in range(slabs):
            stage_nat[cs] = r[:, cs * LANES:(cs + 1) * LANES]
        for cs in range(slabs):
            lanes = slice(cs * LANES, (cs + 1) * LANES)
            for sub in range(DIL_MID):
                piece = stage_nat[cs, pl.ds(sub, rows_mid, stride=DIL_MID), :]
                o_mid[sub, :, lanes] = piece.astype(BF16)
                stage_mid[cs, sub] = piece
            for sub in range(DIL_MAX):
                piece = stage_mid[cs, sub % DIL_MID, pl.ds(sub // DIL_MID, rows_max, stride=DIL_MID), :]
                o_max[sub, :, lanes] = piece.astype(BF16)
    for j, o_ref in enumerate((qr, fr, ir, gr)):
        o_ref[...] = proj(3 + j)


def _inproj(x2, mod3, g_pre, w_in_bf16):
    tiles_per_seq = SEQ // TOK_TILE
    rows_mid = TOK_TILE // DIL_MID
    rows_max = TOK_TILE // DIL_MAX
    mid = pl.BlockSpec((None, DIL_MID, rows_mid, D_ATTN),
                       lambda i: (i // tiles_per_seq, 0, i % tiles_per_seq, 0))
    mx = pl.BlockSpec((None, DIL_MAX, rows_max, D_ATTN),
                      lambda i: (i // tiles_per_seq, 0, i % tiles_per_seq, 0))
    nat = pl.BlockSpec((TOK_TILE, D_REC), lambda i: (i, 0))
    mid_shape = jax.ShapeDtypeStruct((BATCH, DIL_MID, SUB_MID, D_ATTN), BF16)
    mx_shape = jax.ShapeDtypeStruct((BATCH, DIL_MAX, SUB_MAX, D_ATTN), BF16)
    nat_shape = jax.ShapeDtypeStruct((N_TOK, D_REC), F32)
    return pl.pallas_call(
        _inproj_kernel,
        grid=(N_TOK // TOK_TILE,),
        in_specs=[pl.BlockSpec((TOK_TILE, D_MODEL), lambda i: (i, 0)),
                  pl.BlockSpec((1, 6, D_MODEL), lambda i: (i // tiles_per_seq, 0, 0)),
                  pl.BlockSpec((1, D_MODEL), lambda i: (0, 0)),
                  pl.BlockSpec(w_in_bf16.shape, lambda i: (0, 0))],
        out_specs=[mid, mid, mid, mx, mx, mx, nat, nat, nat, nat],
        out_shape=[mid_shape] * 3 + [mx_shape] * 3 + [nat_shape] * 4,
        scratch_shapes=[pltpu.VMEM((D_ATTN // LANES, TOK_TILE, LANES), F32),
                        pltpu.VMEM((D_ATTN // LANES, DIL_MID, rows_mid, LANES), F32)],
        compiler_params=pltpu.CompilerParams(vmem_limit_bytes=VMEM_LIMIT),
        name="inproj",
    )(x2, mod3, g_pre, w_in_bf16)


def _attn_kernel(q4_ref, k4_ref, v4_ref, q16_ref, k16_ref, v16_ref, nw_ref, o_ref,
                 bias_scr, o_scr, lse_scr):
    bw = ATT_BLOCK
    lane = lax.broadcasted_iota(I32, (bw, LANES), 1)
    head0 = lane < HEAD_DIM_A
    head_masks = (jnp.where(head0, 1.0, 0.0).astype(BF16), jnp.where(head0, 0.0, 1.0).astype(BF16))
    eye = jnp.where(lax.broadcasted_iota(I32, (bw, bw), 0) == lax.broadcasted_iota(I32, (bw, bw), 1),
                    1.0, 0.0).astype(BF16)

    jr = lax.broadcasted_iota(I32, (bw, 2 * bw), 1)
    rc = lax.broadcasted_iota(I32, (bw, 2 * bw), 0)
    jp = DIL_MID * (jr % (2 * bw // DIL_MID)) + jr // (2 * bw // DIL_MID)
    rp = DIL_MID * (rc % (bw // DIL_MID)) + rc // (bw // DIL_MID)
    for var, (j, r) in enumerate(((jr, rc), (jr, rc), (jp, rp), (jp, rp))):
        valid = (j <= r) if var % 2 else ((j >= r) & (j <= r + bw))
        bias_scr[var] = jnp.where(valid, 0.0, NEG_BIG).astype(BF16)

    def attend_group(loaded):
        scores = [_dot_nt(q * hm, k) + _dot(eye, bias)
                  for (q, k, _, bias) in loaded for hm in head_masks]
        probs = []
        for s in scores:
            m = jnp.max(s, axis=-1, keepdims=True)
            p = jnp.exp2(s - m)
            probs.append((p.astype(BF16), jnp.sum(p, axis=-1, keepdims=True), m))
        pvs = [_dot(p, loaded[n // 2][2]) for n, (p, _, _) in enumerate(probs)]
        results = []
        for u in range(len(loaded)):
            (_, l0, m0), (_, l1, m1) = probs[2 * u], probs[2 * u + 1]
            o = jnp.where(head0, pvs[2 * u] * (1.0 / l0), pvs[2 * u + 1] * (1.0 / l1))
            lse = jnp.where(head0, jnp.broadcast_to(m0 + jnp.log2(l0), (bw, LANES)),
                            jnp.broadcast_to(m1 + jnp.log2(l1), (bw, LANES)))
            results.append((o, lse))
        return results

    piece = bw // DIL_MID
    group = 4

    def group_d1(g, carry):
        loaded, starts = [], []
        for i in range(group):
            blk = g * group + i
            qs = pl.multiple_of(blk * piece, piece)
            ks = pl.multiple_of(jnp.maximum(blk - 1, 0) * piece, piece)
            q = jnp.concatenate([q4_ref[r, pl.ds(qs, piece), :] for r in range(DIL_MID)], axis=0)
            k = jnp.concatenate([k4_ref[r, pl.ds(ks, 2 * piece), :] for r in range(DIL_MID)], axis=0)
            v = jnp.concatenate([v4_ref[r, pl.ds(ks, 2 * piece), :] for r in range(DIL_MID)], axis=0)
            loaded.append((q, k, v, bias_scr[jnp.where(blk == 0, 3, 2)]))
            starts.append(qs)
        for qs, (o, lse) in zip(starts, attend_group(loaded)):
            for r in range(DIL_MID):
                o_scr[0, r, pl.ds(qs, piece), :] = o[r * piece:(r + 1) * piece]
                lse_scr[0, r, pl.ds(qs, piece), :] = lse[r * piece:(r + 1) * piece]
        return carry

    def group_d4(blk, carry):
        qs = pl.multiple_of(blk * bw, bw)
        ks = pl.multiple_of(jnp.maximum(blk - 1, 0) * bw, bw)
        bias = bias_scr[jnp.where(blk == 0, 1, 0)]
        loaded = [(q4_ref[r, pl.ds(qs, bw), :], k4_ref[r, pl.ds(ks, 2 * bw), :],
                   v4_ref[r, pl.ds(ks, 2 * bw), :], bias) for r in range(DIL_MID)]
        for r, (o, lse) in enumerate(attend_group(loaded)):
            o_scr[1, r, pl.ds(qs, bw), :] = o
            lse_scr[1, r, pl.ds(qs, bw), :] = lse
        return carry

    def group_d16(g, carry):
        loaded, dsts = [], []
        for i in range(group):
            r, blk = g * (group // 2) + i // 2, i % 2
            loaded.append((q16_ref[r, blk * bw:(blk + 1) * bw, :], k16_ref[r], v16_ref[r],
                           bias_scr[1 - blk]))
            dsts.append((r % DIL_MID, pl.ds(blk * (bw * DIL_MID) + r // DIL_MID, bw, stride=DIL_MID)))
        for (sub, dst), (o, lse) in zip(dsts, attend_group(loaded)):
            o_scr[2, sub, dst, :] = o
            lse_scr[2, sub, dst, :] = lse
        return carry

    units = SEQ // bw
    lax.fori_loop(0, units // group, group_d1, 0)
    lax.fori_loop(0, units // DIL_MID, group_d4, 0)
    lax.fori_loop(0, units // group, group_d16, 0)

    rows = 256
    hi = lax.broadcasted_iota(I32, (LANES, LANES), 0) // HEAD_DIM_A
    hj = lax.broadcasted_iota(I32, (LANES, LANES), 1) // HEAD_DIM_A
    head_sum = jnp.where(hi == hj, 1.0, 0.0).astype(BF16)
    nw = nw_ref[...]

    def merge(t, carry):
        r = t // (SUB_MID // rows)
        start = pl.multiple_of((t % (SUB_MID // rows)) * rows, rows)
        sl = pl.ds(start, rows)
        l0, l1, l2 = lse_scr[0, r, sl, :], lse_scr[1, r, sl, :], lse_scr[2, r, sl, :]
        mx = jnp.maximum(jnp.maximum(l0, l1), l2)
        w0, w1, w2 = jnp.exp2(l0 - mx), jnp.exp2(l1 - mx), jnp.exp2(l2 - mx)
        o = (w0 * o_scr[0, r, sl, :] + w1 * o_scr[1, r, sl, :] + w2 * o_scr[2, r, sl, :]) / (w0 + w1 + w2)
        sq_hi, sq_lo = _split_bf16(o * o)
        ms = (_dot(sq_hi, head_sum) + _dot(sq_lo, head_sum)) * (1.0 / HEAD_DIM_A)
        o_ref[pl.ds(start * DIL_MID + r, rows, stride=DIL_MID), :] = o * lax.rsqrt(ms + EPS) * nw
        return carry

    lax.fori_loop(0, SEQ // rows, merge, 0, unroll=2)


def _attention(q4, k4, v4, q16, k16, v16, attn_norm_w):
    hp = D_ATTN // LANES
    mid = pl.BlockSpec((None, DIL_MID, SUB_MID, LANES), lambda b, h: (b, 0, 0, h))
    mx = pl.BlockSpec((None, DIL_MAX, SUB_MAX, LANES), lambda b, h: (b, 0, 0, h))
    scr = pltpu.VMEM((3, DIL_MID, SUB_MID, LANES), F32)
    return pl.pallas_call(
        _attn_kernel,
        grid=(BATCH, hp),
        in_specs=[mid, mid, mid, mx, mx, mx, pl.BlockSpec((1, LANES), lambda b, h: (0, h))],
        out_specs=pl.BlockSpec((None, SEQ, LANES), lambda b, h: (b, 0, h)),
        out_shape=jax.ShapeDtypeStruct((BATCH, SEQ, D_ATTN), F32),
        scratch_shapes=[pltpu.VMEM((4, ATT_BLOCK, 2 * ATT_BLOCK), BF16), scr, scr],
        compiler_params=pltpu.CompilerParams(vmem_limit_bytes=VMEM_LIMIT),
        name="dilated_attn",
    )(q4, k4, v4, q16, k16, v16, attn_norm_w)


def _hgrn_kernel(qr_ref, fr_ref, ir_ref, gr_ref, lb_ref, nw_ref, o_ref,
                 qe_scr, oi_scr, delta_scr, dec_scr, st_scr):
    sup, c = HGRN_SUPER, HGRN_CHUNK
    nch = sup // c
    n_sup = SEQ // sup
    lbp = lb_ref[...]
    lmx = jnp.max(lbp, axis=0, keepdims=True)
    ex = jnp.exp(lbp - lmx)
    lb = ex[0:1] / (ex[0:1] + ex[1:2])
    nw = nw_ref[...]

    ri = lax.broadcasted_iota(I32, (sup, sup), 0)
    ci = lax.broadcasted_iota(I32, (sup, sup), 1)
    same_chunk = (ri // c) == (ci // c)
    causal = same_chunk & (ci <= ri)
    cum_op = jnp.where(causal, 1.0, 0.0).astype(BF16)

    def chunk_rows(kd, ch):
        parts = []
        if ch > 0:
            parts.append(jnp.zeros((ch * c, LANES), BF16))
        parts.append(kd[ch * c:(ch + 1) * c])
        if ch < nch - 1:
            parts.append(jnp.zeros(((nch - 1 - ch) * c, LANES), BF16))
        return jnp.concatenate(parts, axis=0)

    group = 2

    def independent(g, carry):
        ts = [g * group + i for i in range(group)]
        sls = [pl.ds(pl.multiple_of(t * sup, sup), sup) for t in ts]
        pre = []
        for sl in sls:
            f = lb + (1.0 - lb) * _sigmoid(fr_ref[sl, :])
            logf_hi, logf_lo = _split_bf16(jnp.log(f))
            pre.append((1.0 - f, _dot(cum_op, logf_hi) + _dot(cum_op, logf_lo)))
        mid = []
        for t, sl, (kk, b) in zip(ts, sls, pre):
            b_last = jnp.concatenate(
                [jnp.broadcast_to(b[(ch + 1) * c - 1:(ch + 1) * c], (c, LANES)) for ch in range(nch)],
                axis=0)
            q = qr_ref[sl, :]
            qeb = (q * _sigmoid(q) * jnp.exp(b)).astype(BF16)
            ke = (kk * jnp.exp(-b)).astype(BF16)
            kd = (kk * jnp.exp(b_last - b)).astype(BF16)
            qe_scr[sl, :] = qeb
            dec_rows = jnp.concatenate([b_last[ch * c:ch * c + 1] for ch in range(nch)], axis=0)
            dec_scr[pl.ds(pl.multiple_of(t * nch, nch), nch), :] = jnp.exp(dec_rows)
            v = ir_ref[sl, :]
            vt = v.T.astype(BF16)
            scores = _dot_nt(qeb, ke)
            for pair in range(nch // 2):
                rhs = jnp.concatenate([chunk_rows(kd, 2 * pair), chunk_rows(kd, 2 * pair + 1)], axis=1)
                d2 = _dot(vt, rhs)
                delta_scr[t * nch + 2 * pair] = d2[:, 0:LANES]
                delta_scr[t * nch + 2 * pair + 1] = d2[:, LANES:]
            mid.append((scores, v.astype(BF16)))
        for sl, (scores, vb) in zip(sls, mid):
            a = jnp.where(causal, scores, 0.0)
            oi_scr[sl, :] = _dot(a.astype(BF16), vb)
        return carry

    lax.fori_loop(0, n_sup // group, independent, 0)

    def recur(ch, st):
        st_scr[ch] = st.astype(BF16)
        return st * dec_scr[pl.ds(ch, 1), :] + delta_scr[ch]

    lax.fori_loop(0, SEQ // c, recur, jnp.zeros((LANES, LANES), F32), unroll=8)

    def finish(t, carry):
        sl = pl.ds(pl.multiple_of(t * sup, sup), sup)
        parts = [_dot_nt(qe_scr[pl.ds(pl.multiple_of(t * sup + ch * c, c), c), :], st_scr[t * nch + ch])
                 for ch in range(nch)]
        o = oi_scr[sl, :] + jnp.concatenate(parts, axis=0)
        g = gr_ref[sl, :]
        o_ref[sl, :] = (_rms(o) * nw * (g * _sigmoid(g))).astype(o_ref.dtype)
        return carry

    lax.fori_loop(0, n_sup, finish, 0, unroll=2)


def _hgrn(qr, fr, ir, gr, hgrn_lb, hgrn_norm_w):
    nh = D_REC // LANES
    n_chunks = SEQ // HGRN_CHUNK
    blk = pl.BlockSpec((None, SEQ, LANES), lambda b, h: (b, 0, h))
    return pl.pallas_call(
        _hgrn_kernel,
        grid=(BATCH, nh),
        in_specs=[blk, blk, blk, blk,
                  pl.BlockSpec((2, LANES), lambda b, h: (0, h)),
                  pl.BlockSpec((1, LANES), lambda b, h: (0, h))],
        out_specs=blk,
        out_shape=jax.ShapeDtypeStruct((BATCH, SEQ, D_REC), BF16),
        scratch_shapes=[pltpu.VMEM((SEQ, LANES), BF16),
                        pltpu.VMEM((SEQ, LANES), F32),
                        pltpu.VMEM((n_chunks, LANES, LANES), F32),
                        pltpu.VMEM((n_chunks, LANES), F32),
                        pltpu.VMEM((n_chunks, LANES, LANES), BF16)],
        compiler_params=pltpu.CompilerParams(vmem_limit_bytes=VMEM_LIMIT),
        name="hgrn2",
    )(qr, fr, ir, gr, hgrn_lb, hgrn_norm_w)


def _tile_batch(i):
    return i // (SEQ // TOK_TILE)


def _store_row_tiles(ref, val):
    rows = val.shape[0]
    for j in range(ROW_TILES):
        ref[pl.ds(j, rows, stride=ROW_TILES), :] = val[:, j * LANES:(j + 1) * LANES]


def _load_row_tiles(ref, rows):
    return jnp.concatenate(
        [ref[pl.ds(j, rows, stride=ROW_TILES), :] for j in range(ROW_TILES)], axis=1)


def _mid_kernel(ya_ref, yr_ref, x_ref, mod_ref, gpost_ref, gpre_ref, wo_ref, wr_ref, br_ref,
                x1_ref, h2_ref, idx_ref, gate_ref, rank_ref, cnt_ref, carry_ref):
    i = pl.program_id(0)

    @pl.when(i == 0)
    def _():
        carry_ref[...] = jnp.zeros_like(carry_ref)

    mod = mod_ref[0]
    gate_m, shift_f, scale_f = mod[2:3], mod[3:4], mod[4:5]
    y = _dot(ya_ref[...].astype(BF16), wo_ref[0:D_ATTN, :]) + _dot(yr_ref[...], wo_ref[D_ATTN:, :])
    x1 = x_ref[...] + gate_m * (_rms(y) * gpost_ref[...])
    x1_ref[...] = x1
    h2 = _rms(x1) * gpre_ref[...] * (1.0 + scale_f) + shift_f
    _store_row_tiles(h2_ref, h2)

    h2_hi, h2_lo = _split_bf16(h2)
    wr_hi, wr_lo = _split_bf16(wr_ref[...])
    logits = _dot(h2_hi, wr_hi) + _dot(h2_lo, wr_hi) + _dot(h2_hi, wr_lo) + br_ref[...]
    tm = logits.shape[0]
    eidx = lax.broadcasted_iota(I32, (tm, N_EXPERTS), 1)
    work = logits
    vals, idxs = [], []
    onehot = jnp.zeros((tm, N_EXPERTS), F32)
    for _ in range(TOP_K):
        m = jnp.max(work, axis=-1, keepdims=True)
        sel = jnp.min(jnp.where(work == m, eidx, N_EXPERTS), axis=-1, keepdims=True)
        hit = eidx == sel
        work = jnp.where(hit, -jnp.inf, work)
        onehot = jnp.where(hit, 1.0, onehot)
        vals.append(m)
        idxs.append(sel)
    ex = [jnp.exp(vv - vals[0]) for vv in vals]
    den = ex[0] + ex[1] + ex[2] + ex[3]

    ri = lax.broadcasted_iota(I32, (tm, tm), 0)
    ci = lax.broadcasted_iota(I32, (tm, tm), 1)
    strict_lower = jnp.where(ci < ri, 1.0, 0.0).astype(BF16)
    before = _dot(strict_lower, onehot.astype(BF16)) + carry_ref[...]
    lane4 = lax.broadcasted_iota(I32, (tm, TOP_K), 1)
    idx_o = jnp.zeros((tm, TOP_K), I32)
    gate_o = jnp.zeros((tm, TOP_K), F32)
    rank_o = jnp.zeros((tm, TOP_K), I32)
    for kk in range(TOP_K):
        rk = jnp.sum(jnp.where(eidx == idxs[kk], before, 0.0), axis=-1, keepdims=True)
        idx_o = jnp.where(lane4 == kk, idxs[kk], idx_o)
        gate_o = jnp.where(lane4 == kk, ex[kk] / den, gate_o)
        rank_o = jnp.where(lane4 == kk, rk.astype(I32), rank_o)
    idx_ref[...] = idx_o
    gate_ref[...] = gate_o
    rank_ref[...] = rank_o
    total = carry_ref[...] + jnp.sum(onehot, axis=0, keepdims=True)
    carry_ref[...] = total
    cnt_ref[...] = total.astype(I32)


def _mid(ya, yr, x2, mod3, g_post, g_pre, w_out_bf16, w_router, b_router):
    tok = lambda w: pl.BlockSpec((TOK_TILE, w), lambda i: (i, 0))
    const = lambda s: pl.BlockSpec(s, lambda i: (0,) * len(s))
    return pl.pallas_call(
        _mid_kernel,
        grid=(N_TOK // TOK_TILE,),
        in_specs=[tok(D_ATTN), tok(D_REC), tok(D_MODEL),
                  pl.BlockSpec((1, 6, D_MODEL), lambda i: (_tile_batch(i), 0, 0)),
                  const((1, D_MODEL)), const((1, D_MODEL)),
                  const((D_MODEL, D_MODEL)), const((D_MODEL, N_EXPERTS)), const((1, N_EXPERTS))],
        out_specs=[tok(D_MODEL),
                   pl.BlockSpec((TOK_TILE * ROW_TILES, LANES), lambda i: (i, 0)),
                   tok(TOP_K), tok(TOP_K), tok(TOP_K), const((1, N_EXPERTS))],
        out_shape=[jax.ShapeDtypeStruct((N_TOK, D_MODEL), F32),
                   jax.ShapeDtypeStruct((N_TOK * ROW_TILES, LANES), F32),
                   jax.ShapeDtypeStruct((N_TOK, TOP_K), I32),
                   jax.ShapeDtypeStruct((N_TOK, TOP_K), F32),
                   jax.ShapeDtypeStruct((N_TOK, TOP_K), I32),
                   jax.ShapeDtypeStruct((1, N_EXPERTS), I32)],
        scratch_shapes=[pltpu.VMEM((1, N_EXPERTS), F32)],
        compiler_params=pltpu.CompilerParams(dimension_semantics=("arbitrary",),
                                             vmem_limit_bytes=VMEM_LIMIT),
        name="outproj_router",
    )(ya, yr, x2, mod3, g_post, g_pre, w_out_bf16, w_router, b_router)


def _sc_mesh():
    return plsc.VectorSubcoreMesh(core_axis_name="c", subcore_axis_name="s")


def _sc_worker_count():
    info = plsc.get_sparse_core_info()
    return info.num_cores, info.num_cores * info.num_subcores


def _sc_dispatch(h_rows, dest_win):
    n_cores, n_workers = _sc_worker_count()
    n_win = N_TOK // SC_WINDOW
    per_worker = n_win // n_workers

    @functools.partial(
        pl.kernel, mesh=_sc_mesh(),
        out_type=jax.ShapeDtypeStruct((N_SLOTS, ROW_TILES, LANES), F32),
        scratch_types=[pltpu.VMEM((TOP_K, SC_WINDOW), I32),
                       pltpu.VMEM((SC_WINDOW, ROW_TILES, LANES), F32),
                       pltpu.SemaphoreType.DMA],
        name="sc_dispatch")
    def run(h_hbm, dest_hbm, xs_hbm, idx_v, rows_v, sem):
        wid = lax.axis_index("s") * n_cores + lax.axis_index("c")

        @pl.loop(0, per_worker)
        def _(j):
            win = wid * per_worker + j
            pltpu.sync_copy(dest_hbm.at[win], idx_v)
            pltpu.sync_copy(h_hbm.at[pl.ds(win * SC_WINDOW, SC_WINDOW)], rows_v)
            copies = [pltpu.async_copy(rows_v, xs_hbm.at[idx_v.at[kk]], sem)
                      for kk in range(TOP_K)]
            for cp in copies:
                cp.wait()

    return run(h_rows, dest_win)


def _sc_collect(y_rows, dest_win):
    n_cores, n_workers = _sc_worker_count()
    n_win = N_TOK // SC_WINDOW
    per_worker = n_win // n_workers

    @functools.partial(
        pl.kernel, mesh=_sc_mesh(),
        out_type=jax.ShapeDtypeStruct((TOP_K, N_TOK, ROW_TILES, LANES), F32),
        scratch_types=[pltpu.VMEM((TOP_K, SC_WINDOW), I32),
                       pltpu.VMEM((SC_WINDOW, ROW_TILES, LANES), F32),
                       pltpu.SemaphoreType.DMA],
        name="sc_collect")
    def run(y_hbm, dest_hbm, yg_hbm, idx_v, rows_v, sem):
        wid = lax.axis_index("s") * n_cores + lax.axis_index("c")

        @pl.loop(0, per_worker)
        def _(j):
            win = wid * per_worker + j
            pltpu.sync_copy(dest_hbm.at[win], idx_v)
            for kk in range(TOP_K):
                pltpu.async_copy(y_hbm.at[idx_v.at[kk]], rows_v, sem).wait()
                pltpu.sync_copy(rows_v, yg_hbm.at[kk, pl.ds(win * SC_WINDOW, SC_WINDOW)])

    return run(y_rows, dest_win)


def _expert_kernel(be_ref, nu_ref, nx_ref, x_ref, wgu_hbm, bgu_ref, wd_hbm, bd_ref, y_ref,
                   wgu32, wd32, wgu16, wd16, sems):
    i = pl.program_id(0)
    e = be_ref[i]
    prev = be_ref[jnp.maximum(i - 1, 0)]

    def weight_copies(ex):
        return (pltpu.make_async_copy(wgu_hbm.at[ex], wgu32, sems.at[0]),
                pltpu.make_async_copy(wd_hbm.at[ex], wd32, sems.at[1]))

    @pl.when(i == 0)
    def _():
        for cp in weight_copies(e):
            cp.start()

    @pl.when((i == 0) | (e != prev))
    def _():
        for cp in weight_copies(e):
            cp.wait()
        rows = 128

        def cast(r, carry):
            sl = pl.ds(pl.multiple_of(r * rows, rows), rows)
            wgu16[sl, :] = wgu32[sl, :].astype(BF16)
            wd16[sl, :] = wd32[sl, :].astype(BF16)
            return carry

        lax.fori_loop(0, D_MODEL // rows, cast, 0)
        nxt = nx_ref[i]

        @pl.when(nxt >= 0)
        def _():
            for cp in weight_copies(nxt):
                cp.start()

    @pl.when(i < nu_ref[0])
    def _():
        x = _load_row_tiles(x_ref, MOE_BLOCK).astype(BF16)
        bgu = bgu_ref[0]
        glu = _dot(x, wgu16[:, 0:D_FF]) + bgu[:, 0:D_FF]
        lin = _dot(x, wgu16[:, D_FF:]) + bgu[:, D_FF:]
        glu = jnp.minimum(glu, SWIGLU_LIMIT)
        lin = jnp.clip(lin, -SWIGLU_LIMIT, SWIGLU_LIMIT)
        act = glu * _sigmoid(SWIGLU_ALPHA * glu) * (lin + 1.0)
        y = _dot(act.astype(BF16), wd16[...]) + bd_ref[0]
        _store_row_tiles(y_ref, y)


def _experts(blk_e, n_used, next_e, xs2, w_gu, b_gu3, w_down, b_down3):
    row_blk = pl.BlockSpec((MOE_BLOCK * ROW_TILES, LANES),
                           lambda i, be, nu, nx: (jnp.minimum(i, nu[0] - 1), 0))
    grid_spec = pltpu.PrefetchScalarGridSpec(
        num_scalar_prefetch=3,
        grid=(N_BLOCKS,),
        in_specs=[row_blk,
                  pl.BlockSpec(memory_space=pl.ANY),
                  pl.BlockSpec((1, 1, 2 * D_FF), lambda i, be, nu, nx: (be[i], 0, 0)),
                  pl.BlockSpec(memory_space=pl.ANY),
                  pl.BlockSpec((1, 1, D_MODEL), lambda i, be, nu, nx: (be[i], 0, 0))],
        out_specs=row_blk,
        scratch_shapes=[pltpu.VMEM((D_MODEL, 2 * D_FF), F32),
                        pltpu.VMEM((D_FF, D_MODEL), F32),
                        pltpu.VMEM((D_MODEL, 2 * D_FF), BF16),
                        pltpu.VMEM((D_FF, D_MODEL), BF16),
                        pltpu.SemaphoreType.DMA((2,))],
    )
    return pl.pallas_call(
        _expert_kernel,
        grid_spec=grid_spec,
        out_shape=jax.ShapeDtypeStruct((N_SLOTS * ROW_TILES, LANES), F32),
        compiler_params=pltpu.CompilerParams(dimension_semantics=("arbitrary",),
                                             vmem_limit_bytes=VMEM_LIMIT),
        name="experts",
    )(blk_e, n_used, next_e, xs2, w_gu, b_gu3, w_down, b_down3)


def _final_kernel(yg_ref, gate_ref, x1_ref, mod_ref, gpost_ref, o_ref):
    gates = gate_ref[...]
    y = None
    for kk in range(TOP_K):
        part = _load_row_tiles(yg_ref.at[kk], TOK_TILE) * gates[:, kk:kk + 1]
        y = part if y is None else y + part
    gate_f = mod_ref[0][5:6]
    o_ref[...] = x1_ref[...] + gate_f * (_rms(y) * gpost_ref[...])


def _final(yg3, gates, x1, mod3, g_post):
    return pl.pallas_call(
        _final_kernel,
        grid=(N_TOK // TOK_TILE,),
        in_specs=[pl.BlockSpec((TOP_K, TOK_TILE * ROW_TILES, LANES), lambda i: (0, i, 0)),
                  pl.BlockSpec((TOK_TILE, TOP_K), lambda i: (i, 0)),
                  pl.BlockSpec((TOK_TILE, D_MODEL), lambda i: (i, 0)),
                  pl.BlockSpec((1, 6, D_MODEL), lambda i: (_tile_batch(i), 0, 0)),
                  pl.BlockSpec((1, D_MODEL), lambda i: (0, 0))],
        out_specs=pl.BlockSpec((TOK_TILE, D_MODEL), lambda i: (i, 0)),
        out_shape=jax.ShapeDtypeStruct((N_TOK, D_MODEL), F32),
        compiler_params=pltpu.CompilerParams(vmem_limit_bytes=VMEM_LIMIT),
        name="combine_final",
    )(yg3, gates, x1, mod3, g_post)


def _routing_tables(idx, rank, counts):
    counts = counts.reshape(N_EXPERTS)
    experts = jnp.arange(N_EXPERTS, dtype=I32)
    padded = ((counts + MOE_BLOCK - 1) // MOE_BLOCK) * MOE_BLOCK
    pend = jnp.cumsum(padded)
    pstart = pend - padded
    dest = jnp.sum(jnp.where(idx[..., None] == experts, pstart, 0), axis=-1) + rank
    dest_win = dest.reshape(N_TOK // SC_WINDOW, SC_WINDOW, TOP_K).transpose(0, 2, 1)
    n_used = (pend[-1] // MOE_BLOCK).astype(I32).reshape(1)
    blk_start = jnp.arange(N_BLOCKS, dtype=I32) * MOE_BLOCK
    blk_e = jnp.sum(blk_start[:, None] >= pend[None, :], axis=1).astype(I32)
    last_e = jnp.max(jnp.where(counts > 0, experts, 0))
    blk_e = jnp.minimum(blk_e, last_e)
    later = (experts[None, :] > experts[:, None]) & (counts[None, :] > 0)
    next_nonempty = jnp.min(jnp.where(later, experts[None, :], N_EXPERTS), axis=1)
    next_nonempty = jnp.where(next_nonempty == N_EXPERTS, -1, next_nonempty).astype(I32)
    next_e = jnp.sum(jnp.where(blk_e[:, None] == experts, next_nonempty, 0), axis=1).astype(I32)
    return dest_win, blk_e, n_used, next_e


def kernel(x, c, w_ada, b_ada, g_pre_mix, g_post_mix, w_in, attn_norm_w, hgrn_lb, hgrn_norm_w,
           w_out, g_pre_ffn, g_post_ffn, w_router, b_router, w_gu, b_gu, w_down, b_down):
    c_pad = jnp.pad(c, ((0, SUBLANES - BATCH), (0, 0)))
    mod = _ada_mod(c_pad, w_ada[0], b_ada)
    mod3 = mod[:BATCH].reshape(BATCH, 6, D_MODEL)

    x2 = x.reshape(N_TOK, D_MODEL)
    q4, k4, v4, q16, k16, v16, qr, fr, ir, gr = _inproj(x2, mod3, g_pre_mix, w_in[0].astype(BF16))
    nat = lambda t: t.reshape(BATCH, SEQ, D_REC)
    ya = _attention(q4, k4, v4, q16, k16, v16, attn_norm_w)
    yr = _hgrn(nat(qr), nat(fr), nat(ir), nat(gr), hgrn_lb, hgrn_norm_w)

    x1, h2, idx, gates, rank, counts = _mid(
        ya.reshape(N_TOK, D_ATTN), yr.reshape(N_TOK, D_REC), x2, mod3, g_post_mix, g_pre_ffn,
        w_out[0].astype(BF16), w_router[0], b_router)

    dest_win, blk_e, n_used, next_e = _routing_tables(idx, rank, counts)
    xs = _sc_dispatch(h2.reshape(N_TOK, ROW_TILES, LANES), dest_win)
    ys = _experts(blk_e, n_used, next_e, xs.reshape(N_SLOTS * ROW_TILES, LANES),
                  w_gu[0], b_gu[0].reshape(N_EXPERTS, 1, 2 * D_FF),
                  w_down[0], b_down[0].reshape(N_EXPERTS, 1, D_MODEL))
    yg = _sc_collect(ys.reshape(N_SLOTS, ROW_TILES, LANES), dest_win)
    out = _final(yg.reshape(TOP_K, N_TOK * ROW_TILES, LANES), gates, x1, mod3, g_post_ffn)
    return out.reshape(BATCH, SEQ, D_MODEL)
```

```python
import functools
import math

import jax
import jax.numpy as jnp
from jax import lax
from jax.experimental import pallas as pl
from jax.experimental.pallas import tpu as pltpu
from jax.experimental.pallas import tpu_sc as plsc

F32 = jnp.float32
BF16 = jnp.bfloat16
I32 = jnp.int32
HIGHEST = lax.Precision.HIGHEST

D_MODEL = 1024
BATCH = 4
SEQ = 4096
N_TOK = BATCH * SEQ
D_ATTN = 512
HEAD_DIM_A = 64
ATT_BLOCK = 128
DIL_MID = 4
DIL_MAX = 16
SUB_MID = SEQ // DIL_MID
SUB_MAX = SEQ // DIL_MAX
D_REC = 512
HGRN_CHUNK = 32
HGRN_SUPER = 256
N_EXPERTS = 32
TOP_K = 4
D_FF = 1024
SWIGLU_LIMIT = 7.0
SWIGLU_ALPHA = 1.702
EPS = 1e-6
NEG_BIG = -1e30
Q_SCALE = HEAD_DIM_A ** -0.5 * math.log2(math.e)

LANES = 128
SUBLANES = 8
ROW_TILES = D_MODEL // LANES

TOK_TILE = 512
MOE_BLOCK = 512
N_SLOTS = N_TOK * TOP_K + N_EXPERTS * MOE_BLOCK
N_BLOCKS = N_SLOTS // MOE_BLOCK
SC_WINDOW = 32
VMEM_LIMIT = 56 * 1024 * 1024


def _sigmoid(x):
    return 1.0 / (1.0 + jnp.exp(-x))


def _dot(a, b):
    return jnp.dot(a, b, preferred_element_type=F32)


def _dot_nt(a, b):
    return lax.dot_general(a, b, (((1,), (1,)), ((), ())), preferred_element_type=F32)


def _split_bf16(x):
    hi = x.astype(BF16)
    return hi, (x - hi.astype(F32)).astype(BF16)


def _rms(x):
    return x * lax.rsqrt(jnp.mean(x * x, axis=-1, keepdims=True) + EPS)


def _ada_kernel(c_ref, w_ref, b_ref, o_ref):
    c = c_ref[...]
    cond = c * _sigmoid(c)
    o_ref[...] = jnp.dot(cond, w_ref[...], precision=HIGHEST,
                         preferred_element_type=F32) + b_ref[...]


def _ada_mod(c_pad, w_ada, b_ada):
    n = w_ada.shape[1]
    tn = 1536
    return pl.pallas_call(
        _ada_kernel,
        grid=(n // tn,),
        in_specs=[pl.BlockSpec((SUBLANES, D_MODEL), lambda j: (0, 0)),
                  pl.BlockSpec((D_MODEL, tn), lambda j: (0, j)),
                  pl.BlockSpec((1, tn), lambda j: (0, j))],
        out_specs=pl.BlockSpec((SUBLANES, tn), lambda j: (0, j)),
        out_shape=jax.ShapeDtypeStruct((SUBLANES, n), F32),
        compiler_params=pltpu.CompilerParams(vmem_limit_bytes=VMEM_LIMIT),
        name="ada_mod",
    )(c_pad, w_ada, b_ada)


def _inproj_kernel(x_ref, mod_ref, g_ref, w_ref, q4, k4, v4, q16, k16, v16, qr, fr, ir, gr,
                   stage_nat, stage_mid):
    mod = mod_ref[0]
    shift, scale = mod[0:1], mod[1:2]
    h = _rms(x_ref[...]) * g_ref[...] * (1.0 + scale) + shift
    hb = h.astype(BF16)
    slabs = D_ATTN // LANES
    rows_mid = TOK_TILE // DIL_MID
    rows_max = TOK_TILE // DIL_MAX

    def proj(j):
        return _dot(hb, w_ref[:, j * D_ATTN:(j + 1) * D_ATTN])

    for j, (o_mid, o_max) in enumerate(((q4, q16), (k4, k16), (v4, v16))):
        r = proj(j)
        if j == 0:
            r = r * Q_SCALE
        for cs in range(slabs):
            stage_nat[cs] = r[:, cs * LANES:(cs + 1) * LANES]
        for cs in range(slabs):
            lanes = slice(cs * LANES, (cs + 1) * LANES)
            for sub in range(DIL_MID):
                piece = stage_nat[cs, pl.ds(sub, rows_mid, stride=DIL_MID), :]
                o_mid[sub, :, lanes] = piece.astype(BF16)
                stage_mid[cs, sub] = piece
            for sub in range(DIL_MAX):
                piece = stage_mid[cs, sub % DIL_MID, pl.ds(sub // DIL_MID, rows_max, stride=DIL_MID), :]
                o_max[sub, :, lanes] = piece.astype(BF16)
    for j, o_ref in enumerate((qr, fr, ir, gr)):
        o_ref[...] = proj(3 + j)


def _inproj(x2, mod3, g_pre, w_in_bf16):
    tiles_per_seq = SEQ // TOK_TILE
    rows_mid = TOK_TILE // DIL_MID
    rows_max = TOK_TILE // DIL_MAX
    mid = pl.BlockSpec((None, DIL_MID, rows_mid, D_ATTN),
                       lambda i: (i // tiles_per_seq, 0, i % tiles_per_seq, 0))
    mx = pl.BlockSpec((None, DIL_MAX, rows_max, D_ATTN),
                      lambda i: (i // tiles_per_seq, 0, i % tiles_per_seq, 0))
    nat = pl.BlockSpec((TOK_TILE, D_REC), lambda i: (i, 0))
    mid_shape = jax.ShapeDtypeStruct((BATCH, DIL_MID, SUB_MID, D_ATTN), BF16)
    mx_shape = jax.ShapeDtypeStruct((BATCH, DIL_MAX, SUB_MAX, D_ATTN), BF16)
    nat_shape = jax.ShapeDtypeStruct((N_TOK, D_REC), F32)
    return pl.pallas_call(
        _inproj_kernel,
        grid=(N_TOK // TOK_TILE,),
        in_specs=[pl.BlockSpec((TOK_TILE, D_MODEL), lambda i: (i, 0)),
                  pl.BlockSpec((1, 6, D_MODEL), lambda i: (i // tiles_per_seq, 0, 0)),
                  pl.BlockSpec((1, D_MODEL), lambda i: (0, 0)),
                  pl.BlockSpec(w_in_bf16.shape, lambda i: (0, 0))],
        out_specs=[mid, mid, mid, mx, mx, mx, nat, nat, nat, nat],
        out_shape=[mid_shape] * 3 + [mx_shape] * 3 + [nat_shape] * 4,
        scratch_shapes=[pltpu.VMEM((D_ATTN // LANES, TOK_TILE, LANES), F32),
                        pltpu.VMEM((D_ATTN // LANES, DIL_MID, rows_mid, LANES), F32)],
        compiler_params=pltpu.CompilerParams(vmem_limit_bytes=VMEM_LIMIT),
        name="inproj",
    )(x2, mod3, g_pre, w_in_bf16)


def _attn_kernel(q4_ref, k4_ref, v4_ref, q16_ref, k16_ref, v16_ref, nw_ref, o_ref,
                 bias_scr, o_scr, m_scr, l_scr):
    bw = ATT_BLOCK
    lane = lax.broadcasted_iota(I32, (bw, LANES), 1)
    head0 = lane < HEAD_DIM_A
    head_masks = (jnp.where(head0, 1.0, 0.0).astype(BF16), jnp.where(head0, 0.0, 1.0).astype(BF16))
    eye = jnp.where(lax.broadcasted_iota(I32, (bw, bw), 0) == lax.broadcasted_iota(I32, (bw, bw), 1),
                    1.0, 0.0).astype(BF16)

    jr = lax.broadcasted_iota(I32, (bw, 2 * bw), 1)
    rc = lax.broadcasted_iota(I32, (bw, 2 * bw), 0)
    jp = DIL_MID * (jr % (2 * bw // DIL_MID)) + jr // (2 * bw // DIL_MID)
    rp = DIL_MID * (rc % (bw // DIL_MID)) + rc // (bw // DIL_MID)
    for var, (j, r) in enumerate(((jr, rc), (jr, rc), (jp, rp), (jp, rp))):
        valid = (j <= r) if var % 2 else ((j >= r) & (j <= r + bw))
        bias_scr[var] = jnp.where(valid, 0.0, NEG_BIG).astype(BF16)

    def attend_group(loaded):
        scores = [_dot_nt(q * hm, k) + _dot(eye, bias)
                  for (q, k, _, bias) in loaded for hm in head_masks]
        probs = []
        for s in scores:
            m = jnp.max(s, axis=-1, keepdims=True)
            p = jnp.exp2(s - m)
            probs.append((p.astype(BF16), jnp.sum(p, axis=-1, keepdims=True), m))
        pvs = [_dot(p, loaded[n // 2][2]) for n, (p, _, _) in enumerate(probs)]
        results = []
        for u in range(len(loaded)):
            (_, l0, m0), (_, l1, m1) = probs[2 * u], probs[2 * u + 1]
            results.append((jnp.where(head0, pvs[2 * u], pvs[2 * u + 1]),
                            jnp.where(head0, jnp.broadcast_to(m0, (bw, LANES)),
                                      jnp.broadcast_to(m1, (bw, LANES))),
                            jnp.where(head0, jnp.broadcast_to(l0, (bw, LANES)),
                                      jnp.broadcast_to(l1, (bw, LANES)))))
        return results

    piece = bw // DIL_MID
    group = 4

    def group_d1(g, carry):
        loaded, starts = [], []
        for i in range(group):
            blk = g * group + i
            qs = pl.multiple_of(blk * piece, piece)
            ks = pl.multiple_of(jnp.maximum(blk - 1, 0) * piece, piece)
            q = jnp.concatenate([q4_ref[r, pl.ds(qs, piece), :] for r in range(DIL_MID)], axis=0)
            k = jnp.concatenate([k4_ref[r, pl.ds(ks, 2 * piece), :] for r in range(DIL_MID)], axis=0)
            v = jnp.concatenate([v4_ref[r, pl.ds(ks, 2 * piece), :] for r in range(DIL_MID)], axis=0)
            loaded.append((q, k, v, bias_scr[jnp.where(blk == 0, 3, 2)]))
            starts.append(qs)
        for qs, parts in zip(starts, attend_group(loaded)):
            for r in range(DIL_MID):
                for scr, val in zip((o_scr, m_scr, l_scr), parts):
                    scr[0, r, pl.ds(qs, piece), :] = val[r * piece:(r + 1) * piece]
        return carry

    def group_d4(blk, carry):
        qs = pl.multiple_of(blk * bw, bw)
        ks = pl.multiple_of(jnp.maximum(blk - 1, 0) * bw, bw)
        bias = bias_scr[jnp.where(blk == 0, 1, 0)]
        loaded = [(q4_ref[r, pl.ds(qs, bw), :], k4_ref[r, pl.ds(ks, 2 * bw), :],
                   v4_ref[r, pl.ds(ks, 2 * bw), :], bias) for r in range(DIL_MID)]
        for r, parts in enumerate(attend_group(loaded)):
            for scr, val in zip((o_scr, m_scr, l_scr), parts):
                scr[1, r, pl.ds(qs, bw), :] = val
        return carry

    def group_d16(g, carry):
        loaded, dsts = [], []
        for i in range(group):
            r, blk = g * (group // 2) + i // 2, i % 2
            loaded.append((q16_ref[r, blk * bw:(blk + 1) * bw, :], k16_ref[r], v16_ref[r],
                           bias_scr[1 - blk]))
            dsts.append((r % DIL_MID, pl.ds(blk * (bw * DIL_MID) + r // DIL_MID, bw, stride=DIL_MID)))
        for (sub, dst), parts in zip(dsts, attend_group(loaded)):
            for scr, val in zip((o_scr, m_scr, l_scr), parts):
                scr[2, sub, dst, :] = val
        return carry

    units = SEQ // bw
    lax.fori_loop(0, units // group, group_d1, 0)
    lax.fori_loop(0, units // DIL_MID, group_d4, 0)
    lax.fori_loop(0, units // group, group_d16, 0)

    rows = 256
    hi = lax.broadcasted_iota(I32, (LANES, LANES), 0) // HEAD_DIM_A
    hj = lax.broadcasted_iota(I32, (LANES, LANES), 1) // HEAD_DIM_A
    head_sum = jnp.where(hi == hj, 1.0, 0.0).astype(BF16)
    nw = nw_ref[...]

    def merge(t, carry):
        r = t // (SUB_MID // rows)
        start = pl.multiple_of((t % (SUB_MID // rows)) * rows, rows)
        sl = pl.ds(start, rows)
        ms = [m_scr[n, r, sl, :] for n in range(3)]
        mx = jnp.maximum(jnp.maximum(ms[0], ms[1]), ms[2])
        ws = [jnp.exp2(m - mx) for m in ms]
        num = ws[0] * o_scr[0, r, sl, :] + ws[1] * o_scr[1, r, sl, :] + ws[2] * o_scr[2, r, sl, :]
        den = ws[0] * l_scr[0, r, sl, :] + ws[1] * l_scr[1, r, sl, :] + ws[2] * l_scr[2, r, sl, :]
        o = num / den
        sq_hi, sq_lo = _split_bf16(o * o)
        mean_sq = (_dot(sq_hi, head_sum) + _dot(sq_lo, head_sum)) * (1.0 / HEAD_DIM_A)
        o_ref[pl.ds(start * DIL_MID + r, rows, stride=DIL_MID), :] = o * lax.rsqrt(mean_sq + EPS) * nw
        return carry

    lax.fori_loop(0, SEQ // rows, merge, 0, unroll=2)


def _attention(q4, k4, v4, q16, k16, v16, attn_norm_w):
    hp = D_ATTN // LANES
    mid = pl.BlockSpec((None, DIL_MID, SUB_MID, LANES), lambda b, h: (b, 0, 0, h))
    mx = pl.BlockSpec((None, DIL_MAX, SUB_MAX, LANES), lambda b, h: (b, 0, 0, h))
    scr = pltpu.VMEM((3, DIL_MID, SUB_MID, LANES), F32)
    return pl.pallas_call(
        _attn_kernel,
        grid=(BATCH, hp),
        in_specs=[mid, mid, mid, mx, mx, mx, pl.BlockSpec((1, LANES), lambda b, h: (0, h))],
        out_specs=pl.BlockSpec((None, SEQ, LANES), lambda b, h: (b, 0, h)),
        out_shape=jax.ShapeDtypeStruct((BATCH, SEQ, D_ATTN), F32),
        scratch_shapes=[pltpu.VMEM((4, ATT_BLOCK, 2 * ATT_BLOCK), BF16), scr, scr, scr],
        compiler_params=pltpu.CompilerParams(vmem_limit_bytes=VMEM_LIMIT),
        name="dilated_attn",
    )(q4, k4, v4, q16, k16, v16, attn_norm_w)


def _hgrn_kernel(qr_ref, fr_ref, ir_ref, gr_ref, lb_ref, nw_ref, o_ref,
                 qe_scr, oi_scr, delta_scr, dec_scr, st_scr):
    sup, c = HGRN_SUPER, HGRN_CHUNK
    nch = sup // c
    n_sup = SEQ // sup
    lbp = lb_ref[...]
    lmx = jnp.max(lbp, axis=0, keepdims=True)
    ex = jnp.exp(lbp - lmx)
    lb = ex[0:1] / (ex[0:1] + ex[1:2])
    nw = nw_ref[...]

    ri = lax.broadcasted_iota(I32, (sup, sup), 0)
    ci = lax.broadcasted_iota(I32, (sup, sup), 1)
    same_chunk = (ri // c) == (ci // c)
    causal = same_chunk & (ci <= ri)
    cum_op = jnp.where(causal, 1.0, 0.0).astype(BF16)

    def chunk_rows(kd, ch):
        parts = []
        if ch > 0:
            parts.append(jnp.zeros((ch * c, LANES), BF16))
        parts.append(kd[ch * c:(ch + 1) * c])
        if ch < nch - 1:
            parts.append(jnp.zeros(((nch - 1 - ch) * c, LANES), BF16))
        return jnp.concatenate(parts, axis=0)

    group = 2

    def independent(g, carry):
        ts = [g * group + i for i in range(group)]
        sls = [pl.ds(pl.multiple_of(t * sup, sup), sup) for t in ts]
        pre = []
        for sl in sls:
            f = lb + (1.0 - lb) * _sigmoid(fr_ref[sl, :])
            logf_hi, logf_lo = _split_bf16(jnp.log(f))
            pre.append((1.0 - f, _dot(cum_op, logf_hi) + _dot(cum_op, logf_lo)))
        mid = []
        for t, sl, (kk, b) in zip(ts, sls, pre):
            b_last = jnp.concatenate(
                [jnp.broadcast_to(b[(ch + 1) * c - 1:(ch + 1) * c], (c, LANES)) for ch in range(nch)],
                axis=0)
            q = qr_ref[sl, :]
            qeb = (q * _sigmoid(q) * jnp.exp(b)).astype(BF16)
            ke = (kk * jnp.exp(-b)).astype(BF16)
            kd = (kk * jnp.exp(b_last - b)).astype(BF16)
            qe_scr[sl, :] = qeb
            dec_rows = jnp.concatenate([b_last[ch * c:ch * c + 1] for ch in range(nch)], axis=0)
            dec_scr[pl.ds(pl.multiple_of(t * nch, nch), nch), :] = jnp.exp(dec_rows)
            v = ir_ref[sl, :]
            vt = v.T.astype(BF16)
            scores = _dot_nt(qeb, ke)
            for pair in range(nch // 2):
                rhs = jnp.concatenate([chunk_rows(kd, 2 * pair), chunk_rows(kd, 2 * pair + 1)], axis=1)
                d2 = _dot(vt, rhs)
                delta_scr[t * nch + 2 * pair] = d2[:, 0:LANES]
                delta_scr[t * nch + 2 * pair + 1] = d2[:, LANES:]
            mid.append((scores, v.astype(BF16)))
        for sl, (scores, vb) in zip(sls, mid):
            a = jnp.where(causal, scores, 0.0)
            oi_scr[sl, :] = _dot(a.astype(BF16), vb)
        return carry

    lax.fori_loop(0, n_sup // group, independent, 0)

    def recur(ch, st):
        st_scr[ch] = st.astype(BF16)
        return st * dec_scr[pl.ds(ch, 1), :] + delta_scr[ch]

    lax.fori_loop(0, SEQ // c, recur, jnp.zeros((LANES, LANES), F32), unroll=8)

    def finish(t, carry):
        sl = pl.ds(pl.multiple_of(t * sup, sup), sup)
        parts = [_dot_nt(qe_scr[pl.ds(pl.multiple_of(t * sup + ch * c, c), c), :], st_scr[t * nch + ch])
                 for ch in range(nch)]
        o = oi_scr[sl, :] + jnp.concatenate(parts, axis=0)
        g = gr_ref[sl, :]
        o_ref[sl, :] = (_rms(o) * nw * (g * _sigmoid(g))).astype(o_ref.dtype)
        return carry

    lax.fori_loop(0, n_sup, finish, 0, unroll=2)


def _hgrn(qr, fr, ir, gr, hgrn_lb, hgrn_norm_w):
    nh = D_REC // LANES
    n_chunks = SEQ // HGRN_CHUNK
    blk = pl.BlockSpec((None, SEQ, LANES), lambda b, h: (b, 0, h))
    return pl.pallas_call(
        _hgrn_kernel,
        grid=(BATCH, nh),
        in_specs=[blk, blk, blk, blk,
                  pl.BlockSpec((2, LANES), lambda b, h: (0, h)),
                  pl.BlockSpec((1, LANES), lambda b, h: (0, h))],
        out_specs=blk,
        out_shape=jax.ShapeDtypeStruct((BATCH, SEQ, D_REC), BF16),
        scratch_shapes=[pltpu.VMEM((SEQ, LANES), BF16),
                        pltpu.VMEM((SEQ, LANES), F32),
                        pltpu.VMEM((n_chunks, LANES, LANES), F32),
                        pltpu.VMEM((n_chunks, LANES), F32),
                        pltpu.VMEM((n_chunks, LANES, LANES), BF16)],
        compiler_params=pltpu.CompilerParams(vmem_limit_bytes=VMEM_LIMIT),
        name="hgrn2",
    )(qr, fr, ir, gr, hgrn_lb, hgrn_norm_w)


def _tile_batch(i):
    return i // (SEQ // TOK_TILE)


def _store_row_tiles(ref, val):
    rows = val.shape[0]
    for j in range(ROW_TILES):
        ref[pl.ds(j, rows, stride=ROW_TILES), :] = val[:, j * LANES:(j + 1) * LANES]


def _load_row_tiles(ref, rows):
    return jnp.concatenate(
        [ref[pl.ds(j, rows, stride=ROW_TILES), :] for j in range(ROW_TILES)], axis=1)


def _mid_kernel(ya_ref, yr_ref, x_ref, mod_ref, gpost_ref, gpre_ref, wo_ref, wr_ref, br_ref,
                x1_ref, h2_ref, idx_ref, gate_ref, rank_ref, cnt_ref, carry_ref):
    i = pl.program_id(0)

    @pl.when(i == 0)
    def _():
        carry_ref[...] = jnp.zeros_like(carry_ref)

    mod = mod_ref[0]
    gate_m, shift_f, scale_f = mod[2:3], mod[3:4], mod[4:5]
    y = _dot(ya_ref[...].astype(BF16), wo_ref[0:D_ATTN, :]) + _dot(yr_ref[...], wo_ref[D_ATTN:, :])
    x1 = x_ref[...] + gate_m * (_rms(y) * gpost_ref[...])
    x1_ref[...] = x1
    h2 = _rms(x1) * gpre_ref[...] * (1.0 + scale_f) + shift_f
    _store_row_tiles(h2_ref, h2)

    h2_hi, h2_lo = _split_bf16(h2)
    wr_hi, wr_lo = _split_bf16(wr_ref[...])
    logits = _dot(h2_hi, wr_hi) + _dot(h2_lo, wr_hi) + _dot(h2_hi, wr_lo) + br_ref[...]
    tm = logits.shape[0]
    pad = jnp.zeros((tm, LANES - N_EXPERTS), F32)
    work = jnp.concatenate([logits, pad], axis=1).T[0:N_EXPERTS]
    eidx = lax.broadcasted_iota(I32, (N_EXPERTS, tm), 0).astype(F32)
    vals, idxs = [], []
    onehot = jnp.zeros((N_EXPERTS, tm), F32)
    for _ in range(TOP_K):
        m = jnp.max(work, axis=0, keepdims=True)
        sel = jnp.min(jnp.where(work == m, eidx, float(N_EXPERTS)), axis=0, keepdims=True)
        hit = eidx == sel
        work = jnp.where(hit, -jnp.inf, work)
        onehot = jnp.where(hit, 1.0, onehot)
        vals.append(m)
        idxs.append(sel)
    ex = [jnp.exp(vv - vals[0]) for vv in vals]
    inv_den = 1.0 / (ex[0] + ex[1] + ex[2] + ex[3])

    ri = lax.broadcasted_iota(I32, (tm, tm), 0)
    ci = lax.broadcasted_iota(I32, (tm, tm), 1)
    strict_upper = jnp.where(ri < ci, 1.0, 0.0).astype(BF16)
    before = _dot(onehot.astype(BF16), strict_upper) + carry_ref[...]
    ranks = [jnp.sum(jnp.where(eidx == idxs[kk], before, 0.0), axis=0, keepdims=True)
             for kk in range(TOP_K)]
    idx_ref[...] = jnp.concatenate(idxs, axis=0).astype(I32)
    rank_ref[...] = jnp.concatenate(ranks, axis=0).astype(I32)
    gates_t = jnp.concatenate([e * inv_den for e in ex]
                              + [jnp.zeros((LANES - TOP_K, tm), F32)], axis=0)
    gate_ref[...] = gates_t.T[:, 0:TOP_K]
    total = carry_ref[...] + jnp.sum(onehot, axis=1, keepdims=True)
    carry_ref[...] = total
    cnt_ref[...] = total.astype(I32)


def _mid(ya, yr, x2, mod3, g_post, g_pre, w_out_bf16, w_router, b_router):
    tok = lambda w: pl.BlockSpec((TOK_TILE, w), lambda i: (i, 0))
    const = lambda s: pl.BlockSpec(s, lambda i: (0,) * len(s))
    lanes_tok = pl.BlockSpec((TOP_K, TOK_TILE), lambda i: (0, i))
    return pl.pallas_call(
        _mid_kernel,
        grid=(N_TOK // TOK_TILE,),
        in_specs=[tok(D_ATTN), tok(D_REC), tok(D_MODEL),
                  pl.BlockSpec((1, 6, D_MODEL), lambda i: (_tile_batch(i), 0, 0)),
                  const((1, D_MODEL)), const((1, D_MODEL)),
                  const((D_MODEL, D_MODEL)), const((D_MODEL, N_EXPERTS)), const((1, N_EXPERTS))],
        out_specs=[tok(D_MODEL),
                   pl.BlockSpec((TOK_TILE * ROW_TILES, LANES), lambda i: (i, 0)),
                   lanes_tok, tok(TOP_K), lanes_tok, const((N_EXPERTS, 1))],
        out_shape=[jax.ShapeDtypeStruct((N_TOK, D_MODEL), F32),
                   jax.ShapeDtypeStruct((N_TOK * ROW_TILES, LANES), F32),
                   jax.ShapeDtypeStruct((TOP_K, N_TOK), I32),
                   jax.ShapeDtypeStruct((N_TOK, TOP_K), F32),
                   jax.ShapeDtypeStruct((TOP_K, N_TOK), I32),
                   jax.ShapeDtypeStruct((N_EXPERTS, 1), I32)],
        scratch_shapes=[pltpu.VMEM((N_EXPERTS, 1), F32)],
        compiler_params=pltpu.CompilerParams(dimension_semantics=("arbitrary",),
                                             vmem_limit_bytes=VMEM_LIMIT),
        name="outproj_router",
    )(ya, yr, x2, mod3, g_post, g_pre, w_out_bf16, w_router, b_router)


def _sc_mesh():
    return plsc.VectorSubcoreMesh(core_axis_name="c", subcore_axis_name="s")


def _sc_worker_count():
    info = plsc.get_sparse_core_info()
    return info.num_cores, info.num_cores * info.num_subcores


def _sc_dispatch(h_rows, dest_win):
    n_cores, n_workers = _sc_worker_count()
    n_win = N_TOK // SC_WINDOW
    per_worker = n_win // n_workers

    @functools.partial(
        pl.kernel, mesh=_sc_mesh(),
        out_type=jax.ShapeDtypeStruct((N_SLOTS, ROW_TILES, LANES), F32),
        scratch_types=[pltpu.VMEM((TOP_K, SC_WINDOW), I32),
                       pltpu.VMEM((SC_WINDOW, ROW_TILES, LANES), F32),
                       pltpu.SemaphoreType.DMA],
        name="sc_dispatch")
    def run(h_hbm, dest_hbm, xs_hbm, idx_v, rows_v, sem):
        wid = lax.axis_index("s") * n_cores + lax.axis_index("c")

        @pl.loop(0, per_worker)
        def _(j):
            win = wid * per_worker + j
            pltpu.sync_copy(dest_hbm.at[win], idx_v)
            pltpu.sync_copy(h_hbm.at[pl.ds(win * SC_WINDOW, SC_WINDOW)], rows_v)
            copies = [pltpu.async_copy(rows_v, xs_hbm.at[idx_v.at[kk]], sem)
                      for kk in range(TOP_K)]
            for cp in copies:
                cp.wait()

    return run(h_rows, dest_win)


def _sc_collect(y_rows, dest_win):
    n_cores, n_workers = _sc_worker_count()
    n_win = dest_win.shape[0]
    per_worker = n_win // n_workers

    @functools.partial(
        pl.kernel, mesh=_sc_mesh(),
        out_type=jax.ShapeDtypeStruct((TOP_K, n_win * SC_WINDOW, ROW_TILES, LANES), F32),
        scratch_types=[pltpu.VMEM((TOP_K, SC_WINDOW), I32),
                       pltpu.VMEM((SC_WINDOW, ROW_TILES, LANES), F32),
                       pltpu.SemaphoreType.DMA],
        name="sc_collect")
    def run(y_hbm, dest_hbm, yg_hbm, idx_v, rows_v, sem):
        wid = lax.axis_index("s") * n_cores + lax.axis_index("c")

        @pl.loop(0, per_worker)
        def _(j):
            win = wid * per_worker + j
            pltpu.sync_copy(dest_hbm.at[win], idx_v)
            for kk in range(TOP_K):
                pltpu.async_copy(y_hbm.at[idx_v.at[kk]], rows_v, sem).wait()
                pltpu.sync_copy(rows_v, yg_hbm.at[kk, pl.ds(win * SC_WINDOW, SC_WINDOW)])

    return run(y_rows, dest_win)


def _expert_kernel(be_ref, nu_ref, nx_ref, x_ref, wgu_hbm, bgu_ref, wd_hbm, bd_ref, y_ref,
                   wgu32, wd32, wgu16, wd16, sems):
    i = pl.program_id(0)
    e = be_ref[i]
    prev = be_ref[jnp.maximum(i - 1, 0)]

    def weight_copies(ex):
        return (pltpu.make_async_copy(wgu_hbm.at[ex], wgu32, sems.at[0]),
                pltpu.make_async_copy(wd_hbm.at[ex], wd32, sems.at[1]))

    @pl.when(i == 0)
    def _():
        for cp in weight_copies(e):
            cp.start()

    @pl.when((i == 0) | (e != prev))
    def _():
        for cp in weight_copies(e):
            cp.wait()
        rows = 128

        def cast(r, carry):
            sl = pl.ds(pl.multiple_of(r * rows, rows), rows)
            wgu16[sl, :] = wgu32[sl, :].astype(BF16)
            wd16[sl, :] = wd32[sl, :].astype(BF16)
            return carry

        lax.fori_loop(0, D_MODEL // rows, cast, 0)
        nxt = nx_ref[i]

        @pl.when(nxt >= 0)
        def _():
            for cp in weight_copies(nxt):
                cp.start()

    @pl.when(i < nu_ref[0])
    def _():
        x = _load_row_tiles(x_ref, MOE_BLOCK).astype(BF16)
        bgu = bgu_ref[0]
        glu = _dot(x, wgu16[:, 0:D_FF]) + bgu[:, 0:D_FF]
        lin = _dot(x, wgu16[:, D_FF:]) + bgu[:, D_FF:]
        glu = jnp.minimum(glu, SWIGLU_LIMIT)
        lin = jnp.clip(lin, -SWIGLU_LIMIT, SWIGLU_LIMIT)
        act = glu * _sigmoid(SWIGLU_ALPHA * glu) * (lin + 1.0)
        y = _dot(act.astype(BF16), wd16[...]) + bd_ref[0]
        _store_row_tiles(y_ref, y)


def _experts(blk_e, n_used, next_e, xs2, w_gu, b_gu3, w_down, b_down3):
    row_blk = pl.BlockSpec((MOE_BLOCK * ROW_TILES, LANES),
                           lambda i, be, nu, nx: (jnp.minimum(i, nu[0] - 1), 0))
    grid_spec = pltpu.PrefetchScalarGridSpec(
        num_scalar_prefetch=3,
        grid=(N_BLOCKS,),
        in_specs=[row_blk,
                  pl.BlockSpec(memory_space=pl.ANY),
                  pl.BlockSpec((1, 1, 2 * D_FF), lambda i, be, nu, nx: (be[i], 0, 0)),
                  pl.BlockSpec(memory_space=pl.ANY),
                  pl.BlockSpec((1, 1, D_MODEL), lambda i, be, nu, nx: (be[i], 0, 0))],
        out_specs=row_blk,
        scratch_shapes=[pltpu.VMEM((D_MODEL, 2 * D_FF), F32),
                        pltpu.VMEM((D_FF, D_MODEL), F32),
                        pltpu.VMEM((D_MODEL, 2 * D_FF), BF16),
                        pltpu.VMEM((D_FF, D_MODEL), BF16),
                        pltpu.SemaphoreType.DMA((2,))],
    )
    return pl.pallas_call(
        _expert_kernel,
        grid_spec=grid_spec,
        out_shape=jax.ShapeDtypeStruct((N_SLOTS * ROW_TILES, LANES), F32),
        compiler_params=pltpu.CompilerParams(dimension_semantics=("arbitrary",),
                                             vmem_limit_bytes=VMEM_LIMIT),
        name="experts",
    )(blk_e, n_used, next_e, xs2, w_gu, b_gu3, w_down, b_down3)


def _final_kernel(yg_ref, gate_ref, x1_ref, mod_ref, gpost_ref, *maybe_alias_and_out):
    o_ref = maybe_alias_and_out[-1]
    gates = gate_ref[...]
    y = None
    for kk in range(TOP_K):
        part = _load_row_tiles(yg_ref.at[kk], TOK_TILE) * gates[:, kk:kk + 1]
        y = part if y is None else y + part
    gate_f = mod_ref[0][5:6]
    o_ref[...] = x1_ref[...] + gate_f * (_rms(y) * gpost_ref[...])


def _final(part, yg3, gates, x1, mod3, g_post, out_so_far):
    tiles = SEQ // TOK_TILE
    tok = lambda w: pl.BlockSpec((TOK_TILE, w), lambda i: (part * tiles + i, 0))
    in_specs = [pl.BlockSpec((TOP_K, TOK_TILE * ROW_TILES, LANES), lambda i: (0, i, 0)),
                tok(TOP_K), tok(D_MODEL),
                pl.BlockSpec((1, 6, D_MODEL), lambda i: (part, 0, 0)),
                pl.BlockSpec((1, D_MODEL), lambda i: (0, 0))]
    args = [yg3, gates, x1, mod3, g_post]
    aliases = {}
    if out_so_far is not None:
        in_specs.append(pl.BlockSpec(memory_space=pl.ANY))
        args.append(out_so_far)
        aliases = {len(args) - 1: 0}
    return pl.pallas_call(
        _final_kernel,
        grid=(tiles,),
        in_specs=in_specs,
        out_specs=tok(D_MODEL),
        out_shape=jax.ShapeDtypeStruct((N_TOK, D_MODEL), F32),
        input_output_aliases=aliases,
        compiler_params=pltpu.CompilerParams(vmem_limit_bytes=VMEM_LIMIT),
        name="combine_final",
    )(*args)


def _routing_tables(idx, rank, counts):
    counts = counts.reshape(N_EXPERTS)
    experts = jnp.arange(N_EXPERTS, dtype=I32)
    padded = ((counts + MOE_BLOCK - 1) // MOE_BLOCK) * MOE_BLOCK
    pend = jnp.cumsum(padded)
    pstart = pend - padded
    dest = jnp.sum(jnp.where(idx[..., None] == experts, pstart, 0), axis=-1) + rank
    dest_win = dest.reshape(TOP_K, N_TOK // SC_WINDOW, SC_WINDOW).transpose(1, 0, 2)
    n_used = (pend[-1] // MOE_BLOCK).astype(I32).reshape(1)
    blk_start = jnp.arange(N_BLOCKS, dtype=I32) * MOE_BLOCK
    blk_e = jnp.sum(blk_start[:, None] >= pend[None, :], axis=1).astype(I32)
    last_e = jnp.max(jnp.where(counts > 0, experts, 0))
    blk_e = jnp.minimum(blk_e, last_e)
    later = (experts[None, :] > experts[:, None]) & (counts[None, :] > 0)
    next_nonempty = jnp.min(jnp.where(later, experts[None, :], N_EXPERTS), axis=1)
    next_nonempty = jnp.where(next_nonempty == N_EXPERTS, -1, next_nonempty).astype(I32)
    next_e = jnp.sum(jnp.where(blk_e[:, None] == experts, next_nonempty, 0), axis=1).astype(I32)
    return dest_win, blk_e, n_used, next_e


def kernel(x, c, w_ada, b_ada, g_pre_mix, g_post_mix, w_in, attn_norm_w, hgrn_lb, hgrn_norm_w,
           w_out, g_pre_ffn, g_post_ffn, w_router, b_router, w_gu, b_gu, w_down, b_down):
    c_pad = jnp.pad(c, ((0, SUBLANES - BATCH), (0, 0)))
    mod = _ada_mod(c_pad, w_ada[0], b_ada)
    mod3 = mod[:BATCH].reshape(BATCH, 6, D_MODEL)

    x2 = x.reshape(N_TOK, D_MODEL)
    q4, k4, v4, q16, k16, v16, qr, fr, ir, gr = _inproj(x2, mod3, g_pre_mix, w_in[0].astype(BF16))
    nat = lambda t: t.reshape(BATCH, SEQ, D_REC)
    ya = _attention(q4, k4, v4, q16, k16, v16, attn_norm_w)
    yr = _hgrn(nat(qr), nat(fr), nat(ir), nat(gr), hgrn_lb, hgrn_norm_w)

    x1, h2, idx, gates, rank, counts = _mid(
        ya.reshape(N_TOK, D_ATTN), yr.reshape(N_TOK, D_REC), x2, mod3, g_post_mix, g_pre_ffn,
        w_out[0].astype(BF16), w_router[0], b_router)

    dest_win, blk_e, n_used, next_e = _routing_tables(idx, rank, counts)
    xs = _sc_dispatch(h2.reshape(N_TOK, ROW_TILES, LANES), dest_win)
    ys = _experts(blk_e, n_used, next_e, xs.reshape(N_SLOTS * ROW_TILES, LANES),
                  w_gu[0], b_gu[0].reshape(N_EXPERTS, 1, 2 * D_FF),
                  w_down[0], b_down[0].reshape(N_EXPERTS, 1, D_MODEL))
    ys3 = ys.reshape(N_SLOTS, ROW_TILES, LANES)
    win_per_seq = SEQ // SC_WINDOW
    out = None
    for b in range(BATCH):
        yg = _sc_collect(ys3, dest_win[b * win_per_seq:(b + 1) * win_per_seq])
        out = _final(b, yg.reshape(TOP_K, SEQ * ROW_TILES, LANES), gates, x1, mod3, g_post_ffn, out)
    return out.reshape(BATCH, SEQ, D_MODEL)
```

```python
import functools
import math

import jax
import jax.numpy as jnp
from jax import lax
from jax.experimental import pallas as pl
from jax.experimental.pallas import tpu as pltpu
from jax.experimental.pallas import tpu_sc as plsc

F32 = jnp.float32
BF16 = jnp.bfloat16
I32 = jnp.int32
U32 = jnp.uint32
HIGHEST = lax.Precision.HIGHEST

D_MODEL = 1024
BATCH = 4
SEQ = 4096
N_TOK = BATCH * SEQ
D_ATTN = 512
HEAD_DIM_A = 64
ATT_BLOCK = 128
DIL_MID = 4
DIL_MAX = 16
SUB_MID = SEQ // DIL_MID
SUB_MAX = SEQ // DIL_MAX
D_REC = 512
HGRN_CHUNK = 32
HGRN_SUPER = 256
N_EXPERTS = 32
TOP_K = 4
D_FF = 1024
SWIGLU_LIMIT = 7.0
SWIGLU_ALPHA = 1.702
EPS = 1e-6
NEG_BIG = -1e30
Q_SCALE = HEAD_DIM_A ** -0.5 * math.log2(math.e)

LANES = 128
SUBLANES = 8
ROW_TILES = D_MODEL // 2 // LANES

TOK_TILE = 512
MOE_BLOCK = 512
N_SLOTS = N_TOK * TOP_K + N_EXPERTS * MOE_BLOCK
N_BLOCKS = N_SLOTS // MOE_BLOCK
SC_WINDOW = 64
VMEM_LIMIT = 56 * 1024 * 1024


def _sigmoid(x):
    return 1.0 / (1.0 + jnp.exp(-x))


def _dot(a, b):
    return jnp.dot(a, b, preferred_element_type=F32)


def _dot_nt(a, b):
    return lax.dot_general(a, b, (((1,), (1,)), ((), ())), preferred_element_type=F32)


def _split_bf16(x):
    hi = x.astype(BF16)
    return hi, (x - hi.astype(F32)).astype(BF16)


def _rms(x):
    return x * lax.rsqrt(jnp.mean(x * x, axis=-1, keepdims=True) + EPS)


def _ada_kernel(c_ref, w_ref, b_ref, o_ref):
    c = c_ref[...]
    cond = c * _sigmoid(c)
    o_ref[...] = jnp.dot(cond, w_ref[...], precision=HIGHEST,
                         preferred_element_type=F32) + b_ref[...]


def _ada_mod(c_pad, w_ada, b_ada):
    n = w_ada.shape[1]
    tn = 1536
    return pl.pallas_call(
        _ada_kernel,
        grid=(n // tn,),
        in_specs=[pl.BlockSpec((SUBLANES, D_MODEL), lambda j: (0, 0)),
                  pl.BlockSpec((D_MODEL, tn), lambda j: (0, j)),
                  pl.BlockSpec((1, tn), lambda j: (0, j))],
        out_specs=pl.BlockSpec((SUBLANES, tn), lambda j: (0, j)),
        out_shape=jax.ShapeDtypeStruct((SUBLANES, n), F32),
        compiler_params=pltpu.CompilerParams(vmem_limit_bytes=VMEM_LIMIT),
        name="ada_mod",
    )(c_pad, w_ada, b_ada)


def _inproj_kernel(x_ref, mod_ref, g_ref, w_ref, q4, k4, v4, q16, k16, v16, qr, fr, ir, gr,
                   stage_nat, stage_mid):
    mod = mod_ref[0]
    shift, scale = mod[0:1], mod[1:2]
    h = _rms(x_ref[...]) * g_ref[...] * (1.0 + scale) + shift
    hb = h.astype(BF16)
    slabs = D_ATTN // LANES
    rows_mid = TOK_TILE // DIL_MID
    rows_max = TOK_TILE // DIL_MAX

    def proj(j):
        return _dot(hb, w_ref[:, j * D_ATTN:(j + 1) * D_ATTN])

    for j, (o_mid, o_max) in enumerate(((q4, q16), (k4, k16), (v4, v16))):
        r = proj(j)
        if j == 0:
            r = r * Q_SCALE
        for cs in range(slabs):
            stage_nat[cs] = r[:, cs * LANES:(cs + 1) * LANES]
        for cs in range(slabs):
            lanes = slice(cs * LANES, (cs + 1) * LANES)
            for sub in range(DIL_MID):
                piece = stage_nat[cs, pl.ds(sub, rows_mid, stride=DIL_MID), :]
                o_mid[sub, :, lanes] = piece.astype(BF16)
                stage_mid[cs, sub] = piece
            for sub in range(DIL_MAX):
                piece = stage_mid[cs, sub % DIL_MID, pl.ds(sub // DIL_MID, rows_max, stride=DIL_MID), :]
                o_max[sub, :, lanes] = piece.astype(BF16)
    for j, o_ref in enumerate((qr, fr, ir, gr)):
        o_ref[...] = proj(3 + j)


def _inproj(x2, mod3, g_pre, w_in_bf16):
    tiles_per_seq = SEQ // TOK_TILE
    rows_mid = TOK_TILE // DIL_MID
    rows_max = TOK_TILE // DIL_MAX
    mid = pl.BlockSpec((None, DIL_MID, rows_mid, D_ATTN),
                       lambda i: (i // tiles_per_seq, 0, i % tiles_per_seq, 0))
    mx = pl.BlockSpec((None, DIL_MAX, rows_max, D_ATTN),
                      lambda i: (i // tiles_per_seq, 0, i % tiles_per_seq, 0))
    nat = pl.BlockSpec((TOK_TILE, D_REC), lambda i: (i, 0))
    mid_shape = jax.ShapeDtypeStruct((BATCH, DIL_MID, SUB_MID, D_ATTN), BF16)
    mx_shape = jax.ShapeDtypeStruct((BATCH, DIL_MAX, SUB_MAX, D_ATTN), BF16)
    nat_shape = jax.ShapeDtypeStruct((N_TOK, D_REC), F32)
    return pl.pallas_call(
        _inproj_kernel,
        grid=(N_TOK // TOK_TILE,),
        in_specs=[pl.BlockSpec((TOK_TILE, D_MODEL), lambda i: (i, 0)),
                  pl.BlockSpec((1, 6, D_MODEL), lambda i: (i // tiles_per_seq, 0, 0)),
                  pl.BlockSpec((1, D_MODEL), lambda i: (0, 0)),
                  pl.BlockSpec(w_in_bf16.shape, lambda i: (0, 0))],
        out_specs=[mid, mid, mid, mx, mx, mx, nat, nat, nat, nat],
        out_shape=[mid_shape] * 3 + [mx_shape] * 3 + [nat_shape] * 4,
        scratch_shapes=[pltpu.VMEM((D_ATTN // LANES, TOK_TILE, LANES), F32),
                        pltpu.VMEM((D_ATTN // LANES, DIL_MID, rows_mid, LANES), F32)],
        compiler_params=pltpu.CompilerParams(vmem_limit_bytes=VMEM_LIMIT),
        name="inproj",
    )(x2, mod3, g_pre, w_in_bf16)


def _attn_kernel(q4_ref, k4_ref, v4_ref, q16_ref, k16_ref, v16_ref, nw_ref, o_ref,
                 bias_scr, o_scr, m_scr, l_scr):
    bw = ATT_BLOCK
    lane = lax.broadcasted_iota(I32, (bw, LANES), 1)
    head0 = lane < HEAD_DIM_A
    head_masks = (jnp.where(head0, 1.0, 0.0).astype(BF16), jnp.where(head0, 0.0, 1.0).astype(BF16))
    eye = jnp.where(lax.broadcasted_iota(I32, (bw, bw), 0) == lax.broadcasted_iota(I32, (bw, bw), 1),
                    1.0, 0.0).astype(BF16)

    jr = lax.broadcasted_iota(I32, (bw, 2 * bw), 1)
    rc = lax.broadcasted_iota(I32, (bw, 2 * bw), 0)
    jp = DIL_MID * (jr % (2 * bw // DIL_MID)) + jr // (2 * bw // DIL_MID)
    rp = DIL_MID * (rc % (bw // DIL_MID)) + rc // (bw // DIL_MID)
    for var, (j, r) in enumerate(((jr, rc), (jr, rc), (jp, rp), (jp, rp))):
        valid = (j <= r) if var % 2 else ((j >= r) & (j <= r + bw))
        bias_scr[var] = jnp.where(valid, 0.0, NEG_BIG).astype(BF16)

    def attend_group(loaded):
        scores = [_dot_nt(q * hm, k) + _dot(eye, bias)
                  for (q, k, _, bias) in loaded for hm in head_masks]
        probs = []
        for s in scores:
            m = jnp.max(s, axis=-1, keepdims=True)
            p = jnp.exp2(s - m)
            probs.append((p.astype(BF16), jnp.sum(p, axis=-1, keepdims=True), m))
        pvs = [_dot(p, loaded[n // 2][2]) for n, (p, _, _) in enumerate(probs)]
        results = []
        for u in range(len(loaded)):
            (_, l0, m0), (_, l1, m1) = probs[2 * u], probs[2 * u + 1]
            results.append((jnp.where(head0, pvs[2 * u], pvs[2 * u + 1]),
                            jnp.where(head0, jnp.broadcast_to(m0, (bw, LANES)),
                                      jnp.broadcast_to(m1, (bw, LANES))),
                            jnp.where(head0, jnp.broadcast_to(l0, (bw, LANES)),
                                      jnp.broadcast_to(l1, (bw, LANES)))))
        return results

    piece = bw // DIL_MID
    group = 4

    def group_d1(g, carry):
        loaded, starts = [], []
        for i in range(group):
            blk = g * group + i
            qs = pl.multiple_of(blk * piece, piece)
            ks = pl.multiple_of(jnp.maximum(blk - 1, 0) * piece, piece)
            q = jnp.concatenate([q4_ref[r, pl.ds(qs, piece), :] for r in range(DIL_MID)], axis=0)
            k = jnp.concatenate([k4_ref[r, pl.ds(ks, 2 * piece), :] for r in range(DIL_MID)], axis=0)
            v = jnp.concatenate([v4_ref[r, pl.ds(ks, 2 * piece), :] for r in range(DIL_MID)], axis=0)
            loaded.append((q, k, v, bias_scr[jnp.where(blk == 0, 3, 2)]))
            starts.append(qs)
        for qs, parts in zip(starts, attend_group(loaded)):
            for r in range(DIL_MID):
                for scr, val in zip((o_scr, m_scr, l_scr), parts):
                    scr[0, r, pl.ds(qs, piece), :] = val[r * piece:(r + 1) * piece]
        return carry

    def group_d4(blk, carry):
        qs = pl.multiple_of(blk * bw, bw)
        ks = pl.multiple_of(jnp.maximum(blk - 1, 0) * bw, bw)
        bias = bias_scr[jnp.where(blk == 0, 1, 0)]
        loaded = [(q4_ref[r, pl.ds(qs, bw), :], k4_ref[r, pl.ds(ks, 2 * bw), :],
                   v4_ref[r, pl.ds(ks, 2 * bw), :], bias) for r in range(DIL_MID)]
        for r, parts in enumerate(attend_group(loaded)):
            for scr, val in zip((o_scr, m_scr, l_scr), parts):
                scr[1, r, pl.ds(qs, bw), :] = val
        return carry

    def group_d16(g, carry):
        loaded, dsts = [], []
        for i in range(group):
            r, blk = g * (group // 2) + i // 2, i % 2
            loaded.append((q16_ref[r, blk * bw:(blk + 1) * bw, :], k16_ref[r], v16_ref[r],
                           bias_scr[1 - blk]))
            dsts.append((r % DIL_MID, pl.ds(blk * (bw * DIL_MID) + r // DIL_MID, bw, stride=DIL_MID)))
        for (sub, dst), parts in zip(dsts, attend_group(loaded)):
            for scr, val in zip((o_scr, m_scr, l_scr), parts):
                scr[2, sub, dst, :] = val
        return carry

    units = SEQ // bw
    lax.fori_loop(0, units // group, group_d1, 0)
    lax.fori_loop(0, units // DIL_MID, group_d4, 0)
    lax.fori_loop(0, units // group, group_d16, 0)

    rows = 256
    hi = lax.broadcasted_iota(I32, (LANES, LANES), 0) // HEAD_DIM_A
    hj = lax.broadcasted_iota(I32, (LANES, LANES), 1) // HEAD_DIM_A
    head_sum = jnp.where(hi == hj, 1.0, 0.0).astype(BF16)
    nw = nw_ref[...]

    def merge(t, carry):
        r = t // (SUB_MID // rows)
        start = pl.multiple_of((t % (SUB_MID // rows)) * rows, rows)
        sl = pl.ds(start, rows)
        ms = [m_scr[n, r, sl, :] for n in range(3)]
        mx = jnp.maximum(jnp.maximum(ms[0], ms[1]), ms[2])
        ws = [jnp.exp2(m - mx) for m in ms]
        num = ws[0] * o_scr[0, r, sl, :] + ws[1] * o_scr[1, r, sl, :] + ws[2] * o_scr[2, r, sl, :]
        den = ws[0] * l_scr[0, r, sl, :] + ws[1] * l_scr[1, r, sl, :] + ws[2] * l_scr[2, r, sl, :]
        o = num / den
        sq_hi, sq_lo = _split_bf16(o * o)
        mean_sq = (_dot(sq_hi, head_sum) + _dot(sq_lo, head_sum)) * (1.0 / HEAD_DIM_A)
        o_ref[pl.ds(start * DIL_MID + r, rows, stride=DIL_MID), :] = o * lax.rsqrt(mean_sq + EPS) * nw
        return carry

    lax.fori_loop(0, SEQ // rows, merge, 0, unroll=2)


def _attention(q4, k4, v4, q16, k16, v16, attn_norm_w):
    hp = D_ATTN // LANES
    mid = pl.BlockSpec((None, DIL_MID, SUB_MID, LANES), lambda b, h: (b, 0, 0, h))
    mx = pl.BlockSpec((None, DIL_MAX, SUB_MAX, LANES), lambda b, h: (b, 0, 0, h))
    scr = pltpu.VMEM((3, DIL_MID, SUB_MID, LANES), F32)
    return pl.pallas_call(
        _attn_kernel,
        grid=(BATCH, hp),
        in_specs=[mid, mid, mid, mx, mx, mx, pl.BlockSpec((1, LANES), lambda b, h: (0, h))],
        out_specs=pl.BlockSpec((None, SEQ, LANES), lambda b, h: (b, 0, h)),
        out_shape=jax.ShapeDtypeStruct((BATCH, SEQ, D_ATTN), F32),
        scratch_shapes=[pltpu.VMEM((4, ATT_BLOCK, 2 * ATT_BLOCK), BF16), scr, scr, scr],
        compiler_params=pltpu.CompilerParams(vmem_limit_bytes=VMEM_LIMIT),
        name="dilated_attn",
    )(q4, k4, v4, q16, k16, v16, attn_norm_w)


def _hgrn_kernel(qr_ref, fr_ref, ir_ref, gr_ref, lb_ref, nw_ref, o_ref,
                 qe_scr, oi_scr, delta_scr, dec_scr, st_scr):
    sup, c = HGRN_SUPER, HGRN_CHUNK
    nch = sup // c
    n_sup = SEQ // sup
    lbp = lb_ref[...]
    lmx = jnp.max(lbp, axis=0, keepdims=True)
    ex = jnp.exp(lbp - lmx)
    lb = ex[0:1] / (ex[0:1] + ex[1:2])
    nw = nw_ref[...]

    ri = lax.broadcasted_iota(I32, (sup, sup), 0)
    ci = lax.broadcasted_iota(I32, (sup, sup), 1)
    same_chunk = (ri // c) == (ci // c)
    causal = same_chunk & (ci <= ri)
    cum_op = jnp.where(causal, 1.0, 0.0).astype(BF16)

    def chunk_rows(kd, ch):
        parts = []
        if ch > 0:
            parts.append(jnp.zeros((ch * c, LANES), BF16))
        parts.append(kd[ch * c:(ch + 1) * c])
        if ch < nch - 1:
            parts.append(jnp.zeros(((nch - 1 - ch) * c, LANES), BF16))
        return jnp.concatenate(parts, axis=0)

    group = 2

    def independent(g, carry):
        ts = [g * group + i for i in range(group)]
        sls = [pl.ds(pl.multiple_of(t * sup, sup), sup) for t in ts]
        pre = []
        for sl in sls:
            f = lb + (1.0 - lb) * _sigmoid(fr_ref[sl, :])
            logf_hi, logf_lo = _split_bf16(jnp.log(f))
            pre.append((1.0 - f, _dot(cum_op, logf_hi) + _dot(cum_op, logf_lo)))
        mid = []
        for t, sl, (kk, b) in zip(ts, sls, pre):
            b_last = jnp.concatenate(
                [jnp.broadcast_to(b[(ch + 1) * c - 1:(ch + 1) * c], (c, LANES)) for ch in range(nch)],
                axis=0)
            q = qr_ref[sl, :]
            qeb = (q * _sigmoid(q) * jnp.exp(b)).astype(BF16)
            ke = (kk * jnp.exp(-b)).astype(BF16)
            kd = (kk * jnp.exp(b_last - b)).astype(BF16)
            qe_scr[sl, :] = qeb
            dec_rows = jnp.concatenate([b_last[ch * c:ch * c + 1] for ch in range(nch)], axis=0)
            dec_scr[pl.ds(pl.multiple_of(t * nch, nch), nch), :] = jnp.exp(dec_rows)
            v = ir_ref[sl, :]
            vt = v.T.astype(BF16)
            scores = _dot_nt(qeb, ke)
            for pair in range(nch // 2):
                rhs = jnp.concatenate([chunk_rows(kd, 2 * pair), chunk_rows(kd, 2 * pair + 1)], axis=1)
                d2 = _dot(vt, rhs)
                delta_scr[t * nch + 2 * pair] = d2[:, 0:LANES]
                delta_scr[t * nch + 2 * pair + 1] = d2[:, LANES:]
            mid.append((scores, v.astype(BF16)))
        for sl, (scores, vb) in zip(sls, mid):
            a = jnp.where(causal, scores, 0.0)
            oi_scr[sl, :] = _dot(a.astype(BF16), vb)
        return carry

    lax.fori_loop(0, n_sup // group, independent, 0)

    def recur(ch, st):
        st_scr[ch] = st.astype(BF16)
        return st * dec_scr[pl.ds(ch, 1), :] + delta_scr[ch]

    lax.fori_loop(0, SEQ // c, recur, jnp.zeros((LANES, LANES), F32), unroll=8)

    def finish(t, carry):
        sl = pl.ds(pl.multiple_of(t * sup, sup), sup)
        parts = [_dot_nt(qe_scr[pl.ds(pl.multiple_of(t * sup + ch * c, c), c), :], st_scr[t * nch + ch])
                 for ch in range(nch)]
        o = oi_scr[sl, :] + jnp.concatenate(parts, axis=0)
        g = gr_ref[sl, :]
        o_ref[sl, :] = (_rms(o) * nw * (g * _sigmoid(g))).astype(o_ref.dtype)
        return carry

    lax.fori_loop(0, n_sup, finish, 0, unroll=2)


def _hgrn(qr, fr, ir, gr, hgrn_lb, hgrn_norm_w):
    nh = D_REC // LANES
    n_chunks = SEQ // HGRN_CHUNK
    blk = pl.BlockSpec((None, SEQ, LANES), lambda b, h: (b, 0, h))
    return pl.pallas_call(
        _hgrn_kernel,
        grid=(BATCH, nh),
        in_specs=[blk, blk, blk, blk,
                  pl.BlockSpec((2, LANES), lambda b, h: (0, h)),
                  pl.BlockSpec((1, LANES), lambda b, h: (0, h))],
        out_specs=blk,
        out_shape=jax.ShapeDtypeStruct((BATCH, SEQ, D_REC), BF16),
        scratch_shapes=[pltpu.VMEM((SEQ, LANES), BF16),
                        pltpu.VMEM((SEQ, LANES), F32),
                        pltpu.VMEM((n_chunks, LANES, LANES), F32),
                        pltpu.VMEM((n_chunks, LANES), F32),
                        pltpu.VMEM((n_chunks, LANES, LANES), BF16)],
        compiler_params=pltpu.CompilerParams(vmem_limit_bytes=VMEM_LIMIT),
        name="hgrn2",
    )(qr, fr, ir, gr, hgrn_lb, hgrn_norm_w)


def _tile_batch(i):
    return i // (SEQ // TOK_TILE)


def _store_row_tiles(ref, val):
    rows, half = val.shape[0], D_MODEL // 2
    lo = lax.bitcast_convert_type(val[:, :half].astype(BF16).astype(F32), U32)
    hi = lax.bitcast_convert_type(val[:, half:].astype(BF16).astype(F32), U32)
    words = lax.shift_right_logical(lo, jnp.uint32(16)) | (hi & jnp.uint32(0xFFFF0000))
    for j in range(ROW_TILES):
        ref[pl.ds(j, rows, stride=ROW_TILES), :] = words[:, j * LANES:(j + 1) * LANES]


def _load_row_tiles(ref, rows):
    words = jnp.concatenate(
        [ref[pl.ds(j, rows, stride=ROW_TILES), :] for j in range(ROW_TILES)], axis=1)
    lo = lax.bitcast_convert_type(lax.shift_left(words, jnp.uint32(16)), F32)
    hi = lax.bitcast_convert_type(words & jnp.uint32(0xFFFF0000), F32)
    return jnp.concatenate([lo, hi], axis=1)


def _mid_kernel(ya_ref, yr_ref, x_ref, mod_ref, gpost_ref, gpre_ref, wo_ref, wr_ref, br_ref,
                x1_ref, h2_ref, idx_ref, gate_ref, rank_ref, cnt_ref, carry_ref):
    i = pl.program_id(0)

    @pl.when(i == 0)
    def _():
        carry_ref[...] = jnp.zeros_like(carry_ref)

    mod = mod_ref[0]
    gate_m, shift_f, scale_f = mod[2:3], mod[3:4], mod[4:5]
    y = _dot(ya_ref[...].astype(BF16), wo_ref[0:D_ATTN, :]) + _dot(yr_ref[...], wo_ref[D_ATTN:, :])
    x1 = x_ref[...] + gate_m * (_rms(y) * gpost_ref[...])
    x1_ref[...] = x1
    h2 = _rms(x1) * gpre_ref[...] * (1.0 + scale_f) + shift_f
    _store_row_tiles(h2_ref, h2)

    h2_hi, h2_lo = _split_bf16(h2)
    wr_hi, wr_lo = _split_bf16(wr_ref[...])
    logits = _dot(h2_hi, wr_hi) + _dot(h2_lo, wr_hi) + _dot(h2_hi, wr_lo) + br_ref[...]
    tm = logits.shape[0]
    pad = jnp.zeros((tm, LANES - N_EXPERTS), F32)
    work = jnp.concatenate([logits, pad], axis=1).T[0:N_EXPERTS]
    eidx = lax.broadcasted_iota(I32, (N_EXPERTS, tm), 0).astype(F32)
    vals, idxs = [], []
    onehot = jnp.zeros((N_EXPERTS, tm), F32)
    for _ in range(TOP_K):
        m = jnp.max(work, axis=0, keepdims=True)
        sel = jnp.min(jnp.where(work == m, eidx, float(N_EXPERTS)), axis=0, keepdims=True)
        hit = eidx == sel
        work = jnp.where(hit, -jnp.inf, work)
        onehot = jnp.where(hit, 1.0, onehot)
        vals.append(m)
        idxs.append(sel)
    ex = [jnp.exp(vv - vals[0]) for vv in vals]
    inv_den = 1.0 / (ex[0] + ex[1] + ex[2] + ex[3])

    ri = lax.broadcasted_iota(I32, (tm, tm), 0)
    ci = lax.broadcasted_iota(I32, (tm, tm), 1)
    strict_upper = jnp.where(ri < ci, 1.0, 0.0).astype(BF16)
    before = _dot(onehot.astype(BF16), strict_upper) + carry_ref[...]
    ranks = [jnp.sum(jnp.where(eidx == idxs[kk], before, 0.0), axis=0, keepdims=True)
             for kk in range(TOP_K)]
    idx_ref[...] = jnp.concatenate(idxs, axis=0).astype(I32)
    rank_ref[...] = jnp.concatenate(ranks, axis=0).astype(I32)
    gates_t = jnp.concatenate([e * inv_den for e in ex]
                              + [jnp.zeros((LANES - TOP_K, tm), F32)], axis=0)
    gate_ref[...] = gates_t.T[:, 0:TOP_K]
    total = carry_ref[...] + jnp.sum(onehot, axis=1, keepdims=True)
    carry_ref[...] = total
    cnt_ref[...] = total.astype(I32)


def _mid(ya, yr, x2, mod3, g_post, g_pre, w_out_bf16, w_router, b_router):
    tok = lambda w: pl.BlockSpec((TOK_TILE, w), lambda i: (i, 0))
    const = lambda s: pl.BlockSpec(s, lambda i: (0,) * len(s))
    lanes_tok = pl.BlockSpec((TOP_K, TOK_TILE), lambda i: (0, i))
    return pl.pallas_call(
        _mid_kernel,
        grid=(N_TOK // TOK_TILE,),
        in_specs=[tok(D_ATTN), tok(D_REC), tok(D_MODEL),
                  pl.BlockSpec((1, 6, D_MODEL), lambda i: (_tile_batch(i), 0, 0)),
                  const((1, D_MODEL)), const((1, D_MODEL)),
                  const((D_MODEL, D_MODEL)), const((D_MODEL, N_EXPERTS)), const((1, N_EXPERTS))],
        out_specs=[tok(D_MODEL),
                   pl.BlockSpec((TOK_TILE * ROW_TILES, LANES), lambda i: (i, 0)),
                   lanes_tok, tok(TOP_K), lanes_tok, const((N_EXPERTS, 1))],
        out_shape=[jax.ShapeDtypeStruct((N_TOK, D_MODEL), F32),
                   jax.ShapeDtypeStruct((N_TOK * ROW_TILES, LANES), U32),
                   jax.ShapeDtypeStruct((TOP_K, N_TOK), I32),
                   jax.ShapeDtypeStruct((N_TOK, TOP_K), F32),
                   jax.ShapeDtypeStruct((TOP_K, N_TOK), I32),
                   jax.ShapeDtypeStruct((N_EXPERTS, 1), I32)],
        scratch_shapes=[pltpu.VMEM((N_EXPERTS, 1), F32)],
        compiler_params=pltpu.CompilerParams(dimension_semantics=("arbitrary",),
                                             vmem_limit_bytes=VMEM_LIMIT),
        name="outproj_router",
    )(ya, yr, x2, mod3, g_post, g_pre, w_out_bf16, w_router, b_router)


def _sc_mesh():
    return plsc.VectorSubcoreMesh(core_axis_name="c", subcore_axis_name="s")


def _sc_worker_count():
    info = plsc.get_sparse_core_info()
    return info.num_cores, info.num_cores * info.num_subcores


def _sc_dispatch(h_rows, dest_win):
    n_cores, n_workers = _sc_worker_count()
    n_win = N_TOK // SC_WINDOW
    per_worker = n_win // n_workers

    @functools.partial(
        pl.kernel, mesh=_sc_mesh(),
        out_type=jax.ShapeDtypeStruct((N_SLOTS, ROW_TILES, LANES), U32),
        scratch_types=[pltpu.VMEM((TOP_K, SC_WINDOW), I32),
                       pltpu.VMEM((SC_WINDOW, ROW_TILES, LANES), U32),
                       pltpu.SemaphoreType.DMA],
        name="sc_dispatch")
    def run(h_hbm, dest_hbm, xs_hbm, idx_v, rows_v, sem):
        wid = lax.axis_index("s") * n_cores + lax.axis_index("c")

        @pl.loop(0, per_worker)
        def _(j):
            win = wid * per_worker + j
            pltpu.sync_copy(dest_hbm.at[win], idx_v)
            pltpu.sync_copy(h_hbm.at[pl.ds(win * SC_WINDOW, SC_WINDOW)], rows_v)
            copies = [pltpu.async_copy(rows_v, xs_hbm.at[idx_v.at[kk]], sem)
                      for kk in range(TOP_K)]
            for cp in copies:
                cp.wait()

    return run(h_rows, dest_win)


def _sc_collect(y_rows, dest_win):
    n_cores, n_workers = _sc_worker_count()
    n_win = dest_win.shape[0]
    per_worker = n_win // n_workers

    @functools.partial(
        pl.kernel, mesh=_sc_mesh(),
        out_type=jax.ShapeDtypeStruct((TOP_K, n_win * SC_WINDOW, ROW_TILES, LANES), U32),
        scratch_types=[pltpu.VMEM((TOP_K, SC_WINDOW), I32),
                       pltpu.VMEM((SC_WINDOW, ROW_TILES, LANES), U32),
                       pltpu.SemaphoreType.DMA],
        name="sc_collect")
    def run(y_hbm, dest_hbm, yg_hbm, idx_v, rows_v, sem):
        wid = lax.axis_index("s") * n_cores + lax.axis_index("c")

        @pl.loop(0, per_worker)
        def _(j):
            win = wid * per_worker + j
            pltpu.sync_copy(dest_hbm.at[win], idx_v)
            for kk in range(TOP_K):
                pltpu.async_copy(y_hbm.at[idx_v.at[kk]], rows_v, sem).wait()
                pltpu.sync_copy(rows_v, yg_hbm.at[kk, pl.ds(win * SC_WINDOW, SC_WINDOW)])

    return run(y_rows, dest_win)


def _expert_kernel(be_ref, nu_ref, nx_ref, x_ref, wgu_hbm, bgu_ref, wd_hbm, bd_ref, y_ref,
                   wgu32, wd32, wgu16, wd16, sems):
    i = pl.program_id(0)
    e = be_ref[i]
    prev = be_ref[jnp.maximum(i - 1, 0)]

    def weight_copies(ex):
        return (pltpu.make_async_copy(wgu_hbm.at[ex], wgu32, sems.at[0]),
                pltpu.make_async_copy(wd_hbm.at[ex], wd32, sems.at[1]))

    @pl.when(i == 0)
    def _():
        for cp in weight_copies(e):
            cp.start()

    @pl.when((i == 0) | (e != prev))
    def _():
        for cp in weight_copies(e):
            cp.wait()
        rows = 128

        def cast(r, carry):
            sl = pl.ds(pl.multiple_of(r * rows, rows), rows)
            wgu16[sl, :] = wgu32[sl, :].astype(BF16)
            wd16[sl, :] = wd32[sl, :].astype(BF16)
            return carry

        lax.fori_loop(0, D_MODEL // rows, cast, 0)
        nxt = nx_ref[i]

        @pl.when(nxt >= 0)
        def _():
            for cp in weight_copies(nxt):
                cp.start()

    @pl.when(i < nu_ref[0])
    def _():
        x = _load_row_tiles(x_ref, MOE_BLOCK).astype(BF16)
        bgu = bgu_ref[0]
        glu = _dot(x, wgu16[:, 0:D_FF]) + bgu[:, 0:D_FF]
        lin = _dot(x, wgu16[:, D_FF:]) + bgu[:, D_FF:]
        glu = jnp.minimum(glu, SWIGLU_LIMIT)
        lin = jnp.clip(lin, -SWIGLU_LIMIT, SWIGLU_LIMIT)
        act = glu * _sigmoid(SWIGLU_ALPHA * glu) * (lin + 1.0)
        y = _dot(act.astype(BF16), wd16[...]) + bd_ref[0]
        _store_row_tiles(y_ref, y)


def _experts(blk_e, n_used, next_e, xs2, w_gu, b_gu3, w_down, b_down3):
    row_blk = pl.BlockSpec((MOE_BLOCK * ROW_TILES, LANES),
                           lambda i, be, nu, nx: (jnp.minimum(i, nu[0] - 1), 0))
    grid_spec = pltpu.PrefetchScalarGridSpec(
        num_scalar_prefetch=3,
        grid=(N_BLOCKS,),
        in_specs=[row_blk,
                  pl.BlockSpec(memory_space=pl.ANY),
                  pl.BlockSpec((1, 1, 2 * D_FF), lambda i, be, nu, nx: (be[i], 0, 0)),
                  pl.BlockSpec(memory_space=pl.ANY),
                  pl.BlockSpec((1, 1, D_MODEL), lambda i, be, nu, nx: (be[i], 0, 0))],
        out_specs=row_blk,
        scratch_shapes=[pltpu.VMEM((D_MODEL, 2 * D_FF), F32),
                        pltpu.VMEM((D_FF, D_MODEL), F32),
                        pltpu.VMEM((D_MODEL, 2 * D_FF), BF16),
                        pltpu.VMEM((D_FF, D_MODEL), BF16),
                        pltpu.SemaphoreType.DMA((2,))],
    )
    return pl.pallas_call(
        _expert_kernel,
        grid_spec=grid_spec,
        out_shape=jax.ShapeDtypeStruct((N_SLOTS * ROW_TILES, LANES), U32),
        compiler_params=pltpu.CompilerParams(dimension_semantics=("arbitrary",),
                                             vmem_limit_bytes=VMEM_LIMIT),
        name="experts",
    )(blk_e, n_used, next_e, xs2, w_gu, b_gu3, w_down, b_down3)


def _final_kernel(yg_ref, gate_ref, x1_ref, mod_ref, gpost_ref, *maybe_alias_and_out):
    o_ref = maybe_alias_and_out[-1]
    gates = gate_ref[...]
    y = None
    for kk in range(TOP_K):
        part = _load_row_tiles(yg_ref.at[kk], TOK_TILE) * gates[:, kk:kk + 1]
        y = part if y is None else y + part
    gate_f = mod_ref[0][5:6]
    o_ref[...] = x1_ref[...] + gate_f * (_rms(y) * gpost_ref[...])


def _final(part, yg3, gates, x1, mod3, g_post, out_so_far):
    tiles = SEQ // TOK_TILE
    tok = lambda w: pl.BlockSpec((TOK_TILE, w), lambda i: (part * tiles + i, 0))
    in_specs = [pl.BlockSpec((TOP_K, TOK_TILE * ROW_TILES, LANES), lambda i: (0, i, 0)),
                tok(TOP_K), tok(D_MODEL),
                pl.BlockSpec((1, 6, D_MODEL), lambda i: (part, 0, 0)),
                pl.BlockSpec((1, D_MODEL), lambda i: (0, 0))]
    args = [yg3, gates, x1, mod3, g_post]
    aliases = {}
    if out_so_far is not None:
        in_specs.append(pl.BlockSpec(memory_space=pl.ANY))
        args.append(out_so_far)
        aliases = {len(args) - 1: 0}
    return pl.pallas_call(
        _final_kernel,
        grid=(tiles,),
        in_specs=in_specs,
        out_specs=tok(D_MODEL),
        out_shape=jax.ShapeDtypeStruct((N_TOK, D_MODEL), F32),
        input_output_aliases=aliases,
        compiler_params=pltpu.CompilerParams(vmem_limit_bytes=VMEM_LIMIT),
        name="combine_final",
    )(*args)


def _routing_tables(idx, rank, counts):
    counts = counts.reshape(N_EXPERTS)
    experts = jnp.arange(N_EXPERTS, dtype=I32)
    padded = ((counts + MOE_BLOCK - 1) // MOE_BLOCK) * MOE_BLOCK
    pend = jnp.cumsum(padded)
    pstart = pend - padded
    dest = jnp.sum(jnp.where(idx[..., None] == experts, pstart, 0), axis=-1) + rank
    dest_win = dest.reshape(TOP_K, N_TOK // SC_WINDOW, SC_WINDOW).transpose(1, 0, 2)
    n_used = (pend[-1] // MOE_BLOCK).astype(I32).reshape(1)
    blk_start = jnp.arange(N_BLOCKS, dtype=I32) * MOE_BLOCK
    blk_e = jnp.sum(blk_start[:, None] >= pend[None, :], axis=1).astype(I32)
    last_e = jnp.max(jnp.where(counts > 0, experts, 0))
    blk_e = jnp.minimum(blk_e, last_e)
    later = (experts[None, :] > experts[:, None]) & (counts[None, :] > 0)
    next_nonempty = jnp.min(jnp.where(later, experts[None, :], N_EXPERTS), axis=1)
    next_nonempty = jnp.where(next_nonempty == N_EXPERTS, -1, next_nonempty).astype(I32)
    next_e = jnp.sum(jnp.where(blk_e[:, None] == experts, next_nonempty, 0), axis=1).astype(I32)
    return dest_win, blk_e, n_used, next_e


def kernel(x, c, w_ada, b_ada, g_pre_mix, g_post_mix, w_in, attn_norm_w, hgrn_lb, hgrn_norm_w,
           w_out, g_pre_ffn, g_post_ffn, w_router, b_router, w_gu, b_gu, w_down, b_down):
    c_pad = jnp.pad(c, ((0, SUBLANES - BATCH), (0, 0)))
    mod = _ada_mod(c_pad, w_ada[0], b_ada)
    mod3 = mod[:BATCH].reshape(BATCH, 6, D_MODEL)

    x2 = x.reshape(N_TOK, D_MODEL)
    q4, k4, v4, q16, k16, v16, qr, fr, ir, gr = _inproj(x2, mod3, g_pre_mix, w_in[0].astype(BF16))
    nat = lambda t: t.reshape(BATCH, SEQ, D_REC)
    ya = _attention(q4, k4, v4, q16, k16, v16, attn_norm_w)
    yr = _hgrn(nat(qr), nat(fr), nat(ir), nat(gr), hgrn_lb, hgrn_norm_w)

    x1, h2, idx, gates, rank, counts = _mid(
        ya.reshape(N_TOK, D_ATTN), yr.reshape(N_TOK, D_REC), x2, mod3, g_post_mix, g_pre_ffn,
        w_out[0].astype(BF16), w_router[0], b_router)

    dest_win, blk_e, n_used, next_e = _routing_tables(idx, rank, counts)
    xs = _sc_dispatch(h2.reshape(N_TOK, ROW_TILES, LANES), dest_win)
    ys = _experts(blk_e, n_used, next_e, xs.reshape(N_SLOTS * ROW_TILES, LANES),
                  w_gu[0], b_gu[0].reshape(N_EXPERTS, 1, 2 * D_FF),
                  w_down[0], b_down[0].reshape(N_EXPERTS, 1, D_MODEL))
    ys3 = ys.reshape(N_SLOTS, ROW_TILES, LANES)
    win_per_seq = SEQ // SC_WINDOW
    out = None
    for b in range(BATCH):
        yg = _sc_collect(ys3, dest_win[b * win_per_seq:(b + 1) * win_per_seq])
        out = _final(b, yg.reshape(TOP_K, SEQ * ROW_TILES, LANES), gates, x1, mod3, g_post_ffn, out)
    return out.reshape(BATCH, SEQ, D_MODEL)
```

```python
import functools
import math

import jax
import jax.numpy as jnp
from jax import lax
from jax.experimental import pallas as pl
from jax.experimental.pallas import tpu as pltpu
from jax.experimental.pallas import tpu_sc as plsc

F32 = jnp.float32
BF16 = jnp.bfloat16
I32 = jnp.int32
U32 = jnp.uint32
HIGHEST = lax.Precision.HIGHEST

D_MODEL = 1024
BATCH = 4
SEQ = 4096
N_TOK = BATCH * SEQ
D_ATTN = 512
HEAD_DIM_A = 64
ATT_BLOCK = 128
DIL_MID = 4
DIL_MAX = 16
SUB_MID = SEQ // DIL_MID
SUB_MAX = SEQ // DIL_MAX
D_REC = 512
HGRN_CHUNK = 32
HGRN_SUPER = 256
N_EXPERTS = 32
TOP_K = 4
D_FF = 1024
SWIGLU_LIMIT = 7.0
SWIGLU_ALPHA = 1.702
EPS = 1e-6
NEG_BIG = -1e30
Q_SCALE = HEAD_DIM_A ** -0.5 * math.log2(math.e)

LANES = 128
SUBLANES = 8
ROW_TILES = D_MODEL // 2 // LANES

TOK_TILE = 512
MOE_BLOCK = 512
N_SLOTS = N_TOK * TOP_K + N_EXPERTS * MOE_BLOCK
N_BLOCKS = N_SLOTS // MOE_BLOCK
SC_WINDOW = 64
VMEM_LIMIT = 56 * 1024 * 1024


def _sigmoid(x):
    return 1.0 / (1.0 + jnp.exp(-x))


def _dot(a, b):
    return jnp.dot(a, b, preferred_element_type=F32)


def _dot_nt(a, b):
    return lax.dot_general(a, b, (((1,), (1,)), ((), ())), preferred_element_type=F32)


def _split_bf16(x):
    hi = x.astype(BF16)
    return hi, (x - hi.astype(F32)).astype(BF16)


def _rms(x):
    return x * lax.rsqrt(jnp.mean(x * x, axis=-1, keepdims=True) + EPS)


def _ada_kernel(c_ref, w_ref, b_ref, o_ref):
    c = c_ref[...]
    cond = c * _sigmoid(c)
    o_ref[...] = jnp.dot(cond, w_ref[...], precision=HIGHEST,
                         preferred_element_type=F32) + b_ref[...]


def _ada_mod(c_pad, w_ada, b_ada):
    n = w_ada.shape[1]
    tn = 1536
    return pl.pallas_call(
        _ada_kernel,
        grid=(n // tn,),
        in_specs=[pl.BlockSpec((SUBLANES, D_MODEL), lambda j: (0, 0)),
                  pl.BlockSpec((D_MODEL, tn), lambda j: (0, j)),
                  pl.BlockSpec((1, tn), lambda j: (0, j))],
        out_specs=pl.BlockSpec((SUBLANES, tn), lambda j: (0, j)),
        out_shape=jax.ShapeDtypeStruct((SUBLANES, n), F32),
        compiler_params=pltpu.CompilerParams(vmem_limit_bytes=VMEM_LIMIT),
        name="ada_mod",
    )(c_pad, w_ada, b_ada)


def _inproj_kernel(x_ref, mod_ref, g_ref, w_ref, q4, k4, v4, q16, k16, v16, qr, fr, ir, gr,
                   stage_nat, stage_mid):
    mod = mod_ref[0]
    shift, scale = mod[0:1], mod[1:2]
    h = _rms(x_ref[...]) * g_ref[...] * (1.0 + scale) + shift
    hb = h.astype(BF16)
    slabs = D_ATTN // LANES
    rows_mid = TOK_TILE // DIL_MID
    rows_max = TOK_TILE // DIL_MAX

    def proj(j):
        return _dot(hb, w_ref[:, j * D_ATTN:(j + 1) * D_ATTN])

    for j, (o_mid, o_max) in enumerate(((q4, q16), (k4, k16), (v4, v16))):
        r = proj(j)
        if j == 0:
            r = r * Q_SCALE
        for cs in range(slabs):
            stage_nat[cs] = r[:, cs * LANES:(cs + 1) * LANES]
        for cs in range(slabs):
            lanes = slice(cs * LANES, (cs + 1) * LANES)
            for sub in range(DIL_MID):
                piece = stage_nat[cs, pl.ds(sub, rows_mid, stride=DIL_MID), :]
                o_mid[sub, :, lanes] = piece.astype(BF16)
                stage_mid[cs, sub] = piece
            for sub in range(DIL_MAX):
                piece = stage_mid[cs, sub % DIL_MID, pl.ds(sub // DIL_MID, rows_max, stride=DIL_MID), :]
                o_max[sub, :, lanes] = piece.astype(BF16)
    for j, o_ref in enumerate((qr, fr, ir, gr)):
        o_ref[...] = proj(3 + j)


def _inproj(x2, mod3, g_pre, w_in_bf16):
    tiles_per_seq = SEQ // TOK_TILE
    rows_mid = TOK_TILE // DIL_MID
    rows_max = TOK_TILE // DIL_MAX
    mid = pl.BlockSpec((None, DIL_MID, rows_mid, D_ATTN),
                       lambda i: (i // tiles_per_seq, 0, i % tiles_per_seq, 0))
    mx = pl.BlockSpec((None, DIL_MAX, rows_max, D_ATTN),
                      lambda i: (i // tiles_per_seq, 0, i % tiles_per_seq, 0))
    nat = pl.BlockSpec((TOK_TILE, D_REC), lambda i: (i, 0))
    mid_shape = jax.ShapeDtypeStruct((BATCH, DIL_MID, SUB_MID, D_ATTN), BF16)
    mx_shape = jax.ShapeDtypeStruct((BATCH, DIL_MAX, SUB_MAX, D_ATTN), BF16)
    nat_shape = jax.ShapeDtypeStruct((N_TOK, D_REC), F32)
    return pl.pallas_call(
        _inproj_kernel,
        grid=(N_TOK // TOK_TILE,),
        in_specs=[pl.BlockSpec((TOK_TILE, D_MODEL), lambda i: (i, 0)),
                  pl.BlockSpec((1, 6, D_MODEL), lambda i: (i // tiles_per_seq, 0, 0)),
                  pl.BlockSpec((1, D_MODEL), lambda i: (0, 0)),
                  pl.BlockSpec(w_in_bf16.shape, lambda i: (0, 0))],
        out_specs=[mid, mid, mid, mx, mx, mx, nat, nat, nat, nat],
        out_shape=[mid_shape] * 3 + [mx_shape] * 3 + [nat_shape] * 4,
        scratch_shapes=[pltpu.VMEM((D_ATTN // LANES, TOK_TILE, LANES), F32),
                        pltpu.VMEM((D_ATTN // LANES, DIL_MID, rows_mid, LANES), F32)],
        compiler_params=pltpu.CompilerParams(vmem_limit_bytes=VMEM_LIMIT),
        name="inproj",
    )(x2, mod3, g_pre, w_in_bf16)


def _attn_kernel(q4_ref, k4_ref, v4_ref, q16_ref, k16_ref, v16_ref, nw_ref, o_ref,
                 sel_scr, keep_scr, o_scr, m_scr, l_scr):
    bw = ATT_BLOCK
    lane = lax.broadcasted_iota(I32, (bw, LANES), 1)
    head0 = lane < HEAD_DIM_A
    head_masks = (jnp.where(head0, 1.0, 0.0).astype(BF16), jnp.where(head0, 0.0, 1.0).astype(BF16))

    rr = lax.broadcasted_iota(I32, (bw, bw), 0)
    cc = lax.broadcasted_iota(I32, (bw, bw), 1)
    piece = bw // DIL_MID
    rp = DIL_MID * (rr % piece) + rr // piece
    cp = DIL_MID * (cc % piece) + cc // piece
    for var, (r, c) in enumerate(((rr, cc), (rr, cc), (rp, cp), (rp, cp))):
        first = var % 2 == 1
        left = (c < r) if first else (c >= r)
        right = jnp.zeros_like(left) if first else jnp.logical_not(left)
        sel_scr[var] = jnp.where(left, 1.0, 0.0)
        keep_scr[var, 0] = jnp.where(left, 1.0, 0.0).astype(BF16)
        keep_scr[var, 1] = jnp.where(right, 1.0, 0.0).astype(BF16)

    def attend_group(loaded):
        scores = [_dot_nt(q * hm, k) for (q, k, _, _, _) in loaded for hm in head_masks]
        probs = []
        for n, s in enumerate(scores):
            _, _, _, var, right_bias = loaded[n // 2]
            folded = jnp.where(sel_scr[var] > 0.5, s[:, 0:bw], s[:, bw:] + right_bias)
            m = jnp.max(folded, axis=-1, keepdims=True)
            p = jnp.exp2(folded - m)
            pb = p.astype(BF16)
            spread = jnp.concatenate([pb * keep_scr[var, 0], pb * keep_scr[var, 1]], axis=1)
            probs.append((spread, jnp.sum(p, axis=-1, keepdims=True), m))
        pvs = [_dot(p, loaded[n // 2][2]) for n, (p, _, _) in enumerate(probs)]
        results = []
        for u in range(len(loaded)):
            (_, l0, m0), (_, l1, m1) = probs[2 * u], probs[2 * u + 1]
            results.append((jnp.where(head0, pvs[2 * u], pvs[2 * u + 1]),
                            jnp.where(head0, jnp.broadcast_to(m0, (bw, LANES)),
                                      jnp.broadcast_to(m1, (bw, LANES))),
                            jnp.where(head0, jnp.broadcast_to(l0, (bw, LANES)),
                                      jnp.broadcast_to(l1, (bw, LANES)))))
        return results

    group = 8

    def first_block_bias(blk):
        return jnp.where(blk == 0, NEG_BIG, 0.0)

    def group_d1(g, carry):
        loaded, starts = [], []
        for i in range(group):
            blk = g * group + i
            qs = pl.multiple_of(blk * piece, piece)
            ks = pl.multiple_of(jnp.maximum(blk - 1, 0) * piece, piece)
            q = jnp.concatenate([q4_ref[r, pl.ds(qs, piece), :] for r in range(DIL_MID)], axis=0)
            k = jnp.concatenate([k4_ref[r, pl.ds(ks + half * piece, piece), :]
                                 for half in range(2) for r in range(DIL_MID)], axis=0)
            v = jnp.concatenate([v4_ref[r, pl.ds(ks + half * piece, piece), :]
                                 for half in range(2) for r in range(DIL_MID)], axis=0)
            loaded.append((q, k, v, jnp.where(blk == 0, 3, 2), first_block_bias(blk)))
            starts.append(qs)
        for qs, parts in zip(starts, attend_group(loaded)):
            for r in range(DIL_MID):
                for scr, val in zip((o_scr, m_scr, l_scr), parts):
                    scr[0, r, pl.ds(qs, piece), :] = val[r * piece:(r + 1) * piece]
        return carry

    def group_d4(g, carry):
        loaded, dsts = [], []
        for i in range(group):
            r, blk = i % DIL_MID, g * (group // DIL_MID) + i // DIL_MID
            qs = pl.multiple_of(blk * bw, bw)
            ks = pl.multiple_of(jnp.maximum(blk - 1, 0) * bw, bw)
            loaded.append((q4_ref[r, pl.ds(qs, bw), :], k4_ref[r, pl.ds(ks, 2 * bw), :],
                           v4_ref[r, pl.ds(ks, 2 * bw), :], jnp.where(blk == 0, 1, 0),
                           first_block_bias(blk)))
            dsts.append((r, qs))
        for (r, qs), parts in zip(dsts, attend_group(loaded)):
            for scr, val in zip((o_scr, m_scr, l_scr), parts):
                scr[1, r, pl.ds(qs, bw), :] = val
        return carry

    def group_d16(g, carry):
        loaded, dsts = [], []
        for i in range(group):
            r, blk = g * (group // 2) + i // 2, i % 2
            loaded.append((q16_ref[r, blk * bw:(blk + 1) * bw, :], k16_ref[r], v16_ref[r],
                           1 - blk, NEG_BIG if blk == 0 else 0.0))
            dsts.append((r % DIL_MID, pl.ds(blk * (bw * DIL_MID) + r // DIL_MID, bw, stride=DIL_MID)))
        for (sub, dst), parts in zip(dsts, attend_group(loaded)):
            for scr, val in zip((o_scr, m_scr, l_scr), parts):
                scr[2, sub, dst, :] = val
        return carry

    units = SEQ // bw
    lax.fori_loop(0, units // group, group_d1, 0)
    lax.fori_loop(0, units // group, group_d4, 0)
    lax.fori_loop(0, units // group, group_d16, 0)

    rows = 256
    hi = lax.broadcasted_iota(I32, (LANES, LANES), 0) // HEAD_DIM_A
    hj = lax.broadcasted_iota(I32, (LANES, LANES), 1) // HEAD_DIM_A
    head_sum = jnp.where(hi == hj, 1.0, 0.0).astype(BF16)
    nw = nw_ref[...]

    def head_sums(x):
        x_hi, x_lo = _split_bf16(x)
        return _dot(x_hi, head_sum) + _dot(x_lo, head_sum)

    def merge(t, carry):
        r = t // (SUB_MID // rows)
        start = pl.multiple_of((t % (SUB_MID // rows)) * rows, rows)
        sl = pl.ds(start, rows)
        own_score = head_sums(q4_ref[r, sl, :].astype(F32) * k4_ref[r, sl, :].astype(F32))
        ms = [m_scr[n, r, sl, :] for n in range(3)]
        mx = jnp.maximum(jnp.maximum(jnp.maximum(ms[0], ms[1]), ms[2]), own_score)
        ws = [jnp.exp2(m - mx) for m in ms]
        w_own = float(len(ms)) * jnp.exp2(own_score - mx)
        num = (ws[0] * o_scr[0, r, sl, :] + ws[1] * o_scr[1, r, sl, :] + ws[2] * o_scr[2, r, sl, :]
               + w_own * v4_ref[r, sl, :].astype(F32))
        den = ws[0] * l_scr[0, r, sl, :] + ws[1] * l_scr[1, r, sl, :] + ws[2] * l_scr[2, r, sl, :] + w_own
        o = num / den
        mean_sq = head_sums(o * o) * (1.0 / HEAD_DIM_A)
        o_ref[pl.ds(start * DIL_MID + r, rows, stride=DIL_MID), :] = o * lax.rsqrt(mean_sq + EPS) * nw
        return carry

    lax.fori_loop(0, SEQ // rows, merge, 0, unroll=2)


def _attention(q4, k4, v4, q16, k16, v16, attn_norm_w):
    hp = D_ATTN // LANES
    mid = pl.BlockSpec((None, DIL_MID, SUB_MID, LANES), lambda b, h: (b, 0, 0, h))
    mx = pl.BlockSpec((None, DIL_MAX, SUB_MAX, LANES), lambda b, h: (b, 0, 0, h))
    scr = pltpu.VMEM((3, DIL_MID, SUB_MID, LANES), F32)
    return pl.pallas_call(
        _attn_kernel,
        grid=(BATCH, hp),
        in_specs=[mid, mid, mid, mx, mx, mx, pl.BlockSpec((1, LANES), lambda b, h: (0, h))],
        out_specs=pl.BlockSpec((None, SEQ, LANES), lambda b, h: (b, 0, h)),
        out_shape=jax.ShapeDtypeStruct((BATCH, SEQ, D_ATTN), F32),
        scratch_shapes=[pltpu.VMEM((4, ATT_BLOCK, ATT_BLOCK), F32),
                        pltpu.VMEM((4, 2, ATT_BLOCK, ATT_BLOCK), BF16), scr, scr, scr],
        compiler_params=pltpu.CompilerParams(vmem_limit_bytes=VMEM_LIMIT),
        name="dilated_attn",
    )(q4, k4, v4, q16, k16, v16, attn_norm_w)


def _hgrn_kernel(qr_ref, fr_ref, ir_ref, gr_ref, lb_ref, nw_ref, o_ref,
                 qe_scr, oi_scr, delta_scr, dec_scr, st_scr):
    sup, c = HGRN_SUPER, HGRN_CHUNK
    nch = sup // c
    n_sup = SEQ // sup
    lbp = lb_ref[...]
    lmx = jnp.max(lbp, axis=0, keepdims=True)
    ex = jnp.exp(lbp - lmx)
    lb = ex[0:1] / (ex[0:1] + ex[1:2])
    nw = nw_ref[...]

    ri = lax.broadcasted_iota(I32, (sup, sup), 0)
    ci = lax.broadcasted_iota(I32, (sup, sup), 1)
    same_chunk = (ri // c) == (ci // c)
    causal = same_chunk & (ci <= ri)
    cum_op = jnp.where(causal, 1.0, 0.0).astype(BF16)

    def chunk_rows(kd, ch):
        parts = []
        if ch > 0:
            parts.append(jnp.zeros((ch * c, LANES), BF16))
        parts.append(kd[ch * c:(ch + 1) * c])
        if ch < nch - 1:
            parts.append(jnp.zeros(((nch - 1 - ch) * c, LANES), BF16))
        return jnp.concatenate(parts, axis=0)

    group = 2

    def independent(g, carry):
        ts = [g * group + i for i in range(group)]
        sls = [pl.ds(pl.multiple_of(t * sup, sup), sup) for t in ts]
        pre = []
        for sl in sls:
            f = lb + (1.0 - lb) * _sigmoid(fr_ref[sl, :])
            logf_hi, logf_lo = _split_bf16(jnp.log(f))
            pre.append((1.0 - f, _dot(cum_op, logf_hi) + _dot(cum_op, logf_lo)))
        mid = []
        for t, sl, (kk, b) in zip(ts, sls, pre):
            b_last = jnp.concatenate(
                [jnp.broadcast_to(b[(ch + 1) * c - 1:(ch + 1) * c], (c, LANES)) for ch in range(nch)],
                axis=0)
            q = qr_ref[sl, :]
            qeb = (q * _sigmoid(q) * jnp.exp(b)).astype(BF16)
            ke = (kk * jnp.exp(-b)).astype(BF16)
            kd = (kk * jnp.exp(b_last - b)).astype(BF16)
            qe_scr[sl, :] = qeb
            dec_rows = jnp.concatenate([b_last[ch * c:ch * c + 1] for ch in range(nch)], axis=0)
            dec_scr[pl.ds(pl.multiple_of(t * nch, nch), nch), :] = jnp.exp(dec_rows)
            v = ir_ref[sl, :]
            vt = v.T.astype(BF16)
            scores = _dot_nt(qeb, ke)
            for pair in range(nch // 2):
                rhs = jnp.concatenate([chunk_rows(kd, 2 * pair), chunk_rows(kd, 2 * pair + 1)], axis=1)
                d2 = _dot(vt, rhs)
                delta_scr[t * nch + 2 * pair] = d2[:, 0:LANES]
                delta_scr[t * nch + 2 * pair + 1] = d2[:, LANES:]
            mid.append((scores, v.astype(BF16)))
        for sl, (scores, vb) in zip(sls, mid):
            a = jnp.where(causal, scores, 0.0)
            oi_scr[sl, :] = _dot(a.astype(BF16), vb)
        return carry

    lax.fori_loop(0, n_sup // group, independent, 0)

    def recur(ch, st):
        st_scr[ch] = st.astype(BF16)
        return st * dec_scr[pl.ds(ch, 1), :] + delta_scr[ch]

    lax.fori_loop(0, SEQ // c, recur, jnp.zeros((LANES, LANES), F32), unroll=8)

    def finish(t, carry):
        sl = pl.ds(pl.multiple_of(t * sup, sup), sup)
        parts = [_dot_nt(qe_scr[pl.ds(pl.multiple_of(t * sup + ch * c, c), c), :], st_scr[t * nch + ch])
                 for ch in range(nch)]
        o = oi_scr[sl, :] + jnp.concatenate(parts, axis=0)
        g = gr_ref[sl, :]
        o_ref[sl, :] = (_rms(o) * nw * (g * _sigmoid(g))).astype(o_ref.dtype)
        return carry

    lax.fori_loop(0, n_sup, finish, 0, unroll=4)


def _hgrn(qr, fr, ir, gr, hgrn_lb, hgrn_norm_w):
    nh = D_REC // LANES
    n_chunks = SEQ // HGRN_CHUNK
    blk = pl.BlockSpec((None, SEQ, LANES), lambda b, h: (b, 0, h))
    return pl.pallas_call(
        _hgrn_kernel,
        grid=(BATCH, nh),
        in_specs=[blk, blk, blk, blk,
                  pl.BlockSpec((2, LANES), lambda b, h: (0, h)),
                  pl.BlockSpec((1, LANES), lambda b, h: (0, h))],
        out_specs=blk,
        out_shape=jax.ShapeDtypeStruct((BATCH, SEQ, D_REC), BF16),
        scratch_shapes=[pltpu.VMEM((SEQ, LANES), BF16),
                        pltpu.VMEM((SEQ, LANES), F32),
                        pltpu.VMEM((n_chunks, LANES, LANES), F32),
                        pltpu.VMEM((n_chunks, LANES), F32),
                        pltpu.VMEM((n_chunks, LANES, LANES), BF16)],
        compiler_params=pltpu.CompilerParams(vmem_limit_bytes=VMEM_LIMIT),
        name="hgrn2",
    )(qr, fr, ir, gr, hgrn_lb, hgrn_norm_w)


def _tile_batch(i):
    return i // (SEQ // TOK_TILE)


def _store_row_tiles(ref, val):
    rows, half = val.shape[0], D_MODEL // 2
    lo = lax.bitcast_convert_type(val[:, :half].astype(BF16).astype(F32), U32)
    hi = lax.bitcast_convert_type(val[:, half:].astype(BF16).astype(F32), U32)
    words = lax.shift_right_logical(lo, jnp.uint32(16)) | (hi & jnp.uint32(0xFFFF0000))
    for j in range(ROW_TILES):
        ref[pl.ds(j, rows, stride=ROW_TILES), :] = words[:, j * LANES:(j + 1) * LANES]


def _load_row_tiles(ref, rows):
    words = jnp.concatenate(
        [ref[pl.ds(j, rows, stride=ROW_TILES), :] for j in range(ROW_TILES)], axis=1)
    lo = lax.bitcast_convert_type(lax.shift_left(words, jnp.uint32(16)), F32)
    hi = lax.bitcast_convert_type(words & jnp.uint32(0xFFFF0000), F32)
    return jnp.concatenate([lo, hi], axis=1)


def _mid_kernel(ya_ref, yr_ref, x_ref, mod_ref, gpost_ref, gpre_ref, wo_ref, wr_ref, br_ref,
                x1_ref, h2_ref, idx_ref, gate_ref, rank_ref, cnt_ref, carry_ref):
    i = pl.program_id(0)

    @pl.when(i == 0)
    def _():
        carry_ref[...] = jnp.zeros_like(carry_ref)

    mod = mod_ref[0]
    gate_m, shift_f, scale_f = mod[2:3], mod[3:4], mod[4:5]
    y = _dot(ya_ref[...].astype(BF16), wo_ref[0:D_ATTN, :]) + _dot(yr_ref[...], wo_ref[D_ATTN:, :])
    x1 = x_ref[...] + gate_m * (_rms(y) * gpost_ref[...])
    x1_ref[...] = x1
    h2 = _rms(x1) * gpre_ref[...] * (1.0 + scale_f) + shift_f
    _store_row_tiles(h2_ref, h2)

    h2_hi, h2_lo = _split_bf16(h2)
    wr_hi, wr_lo = _split_bf16(wr_ref[...])
    logits = _dot(h2_hi, wr_hi) + _dot(h2_lo, wr_hi) + _dot(h2_hi, wr_lo) + br_ref[...]
    tm = logits.shape[0]
    pad = jnp.zeros((tm, LANES - N_EXPERTS), F32)
    work = jnp.concatenate([logits, pad], axis=1).T[0:N_EXPERTS]
    eidx = lax.broadcasted_iota(I32, (N_EXPERTS, tm), 0).astype(F32)
    vals, idxs = [], []
    onehot = jnp.zeros((N_EXPERTS, tm), F32)
    for _ in range(TOP_K):
        m = jnp.max(work, axis=0, keepdims=True)
        sel = jnp.min(jnp.where(work == m, eidx, float(N_EXPERTS)), axis=0, keepdims=True)
        hit = eidx == sel
        work = jnp.where(hit, -jnp.inf, work)
        onehot = jnp.where(hit, 1.0, onehot)
        vals.append(m)
        idxs.append(sel)
    ex = [jnp.exp(vv - vals[0]) for vv in vals]
    inv_den = 1.0 / (ex[0] + ex[1] + ex[2] + ex[3])

    ri = lax.broadcasted_iota(I32, (tm, tm), 0)
    ci = lax.broadcasted_iota(I32, (tm, tm), 1)
    strict_upper = jnp.where(ri < ci, 1.0, 0.0).astype(BF16)
    before = _dot(onehot.astype(BF16), strict_upper) + carry_ref[...]
    ranks = [jnp.sum(jnp.where(eidx == idxs[kk], before, 0.0), axis=0, keepdims=True)
             for kk in range(TOP_K)]
    idx_ref[...] = jnp.concatenate(idxs, axis=0).astype(I32)
    rank_ref[...] = jnp.concatenate(ranks, axis=0).astype(I32)
    gates_t = jnp.concatenate([e * inv_den for e in ex]
                              + [jnp.zeros((LANES - TOP_K, tm), F32)], axis=0)
    gate_ref[...] = gates_t.T[:, 0:TOP_K]
    total = carry_ref[...] + jnp.sum(onehot, axis=1, keepdims=True)
    carry_ref[...] = total
    cnt_ref[...] = total.astype(I32)


def _mid(ya, yr, x2, mod3, g_post, g_pre, w_out_bf16, w_router, b_router):
    tok = lambda w: pl.BlockSpec((TOK_TILE, w), lambda i: (i, 0))
    const = lambda s: pl.BlockSpec(s, lambda i: (0,) * len(s))
    lanes_tok = pl.BlockSpec((TOP_K, TOK_TILE), lambda i: (0, i))
    return pl.pallas_call(
        _mid_kernel,
        grid=(N_TOK // TOK_TILE,),
        in_specs=[tok(D_ATTN), tok(D_REC), tok(D_MODEL),
                  pl.BlockSpec((1, 6, D_MODEL), lambda i: (_tile_batch(i), 0, 0)),
                  const((1, D_MODEL)), const((1, D_MODEL)),
                  const((D_MODEL, D_MODEL)), const((D_MODEL, N_EXPERTS)), const((1, N_EXPERTS))],
        out_specs=[tok(D_MODEL),
                   pl.BlockSpec((TOK_TILE * ROW_TILES, LANES), lambda i: (i, 0)),
                   lanes_tok, tok(TOP_K), lanes_tok, const((N_EXPERTS, 1))],
        out_shape=[jax.ShapeDtypeStruct((N_TOK, D_MODEL), F32),
                   jax.ShapeDtypeStruct((N_TOK * ROW_TILES, LANES), U32),
                   jax.ShapeDtypeStruct((TOP_K, N_TOK), I32),
                   jax.ShapeDtypeStruct((N_TOK, TOP_K), F32),
                   jax.ShapeDtypeStruct((TOP_K, N_TOK), I32),
                   jax.ShapeDtypeStruct((N_EXPERTS, 1), I32)],
        scratch_shapes=[pltpu.VMEM((N_EXPERTS, 1), F32)],
        compiler_params=pltpu.CompilerParams(dimension_semantics=("arbitrary",),
                                             vmem_limit_bytes=VMEM_LIMIT),
        name="outproj_router",
    )(ya, yr, x2, mod3, g_post, g_pre, w_out_bf16, w_router, b_router)


def _sc_mesh():
    return plsc.VectorSubcoreMesh(core_axis_name="c", subcore_axis_name="s")


def _sc_worker_count():
    info = plsc.get_sparse_core_info()
    return info.num_cores, info.num_cores * info.num_subcores


def _sc_dispatch(h_rows, dest_win):
    n_cores, n_workers = _sc_worker_count()
    n_win = N_TOK // SC_WINDOW
    per_worker = n_win // n_workers

    @functools.partial(
        pl.kernel, mesh=_sc_mesh(),
        out_type=jax.ShapeDtypeStruct((N_SLOTS, ROW_TILES, LANES), U32),
        scratch_types=[pltpu.VMEM((TOP_K, SC_WINDOW), I32),
                       pltpu.VMEM((SC_WINDOW, ROW_TILES, LANES), U32),
                       pltpu.SemaphoreType.DMA],
        name="sc_dispatch")
    def run(h_hbm, dest_hbm, xs_hbm, idx_v, rows_v, sem):
        wid = lax.axis_index("s") * n_cores + lax.axis_index("c")

        @pl.loop(0, per_worker)
        def _(j):
            win = wid * per_worker + j
            pltpu.sync_copy(dest_hbm.at[win], idx_v)
            pltpu.sync_copy(h_hbm.at[pl.ds(win * SC_WINDOW, SC_WINDOW)], rows_v)
            copies = [pltpu.async_copy(rows_v, xs_hbm.at[idx_v.at[kk]], sem)
                      for kk in range(TOP_K)]
            for cp in copies:
                cp.wait()

    return run(h_rows, dest_win)


def _sc_collect(y_rows, dest_win):
    n_cores, n_workers = _sc_worker_count()
    n_win = dest_win.shape[0]
    per_worker = n_win // n_workers

    @functools.partial(
        pl.kernel, mesh=_sc_mesh(),
        out_type=jax.ShapeDtypeStruct((TOP_K, n_win * SC_WINDOW, ROW_TILES, LANES), U32),
        scratch_types=[pltpu.VMEM((TOP_K, SC_WINDOW), I32),
                       pltpu.VMEM((SC_WINDOW, ROW_TILES, LANES), U32),
                       pltpu.SemaphoreType.DMA],
        name="sc_collect")
    def run(y_hbm, dest_hbm, yg_hbm, idx_v, rows_v, sem):
        wid = lax.axis_index("s") * n_cores + lax.axis_index("c")

        @pl.loop(0, per_worker)
        def _(j):
            win = wid * per_worker + j
            pltpu.sync_copy(dest_hbm.at[win], idx_v)
            for kk in range(TOP_K):
                pltpu.async_copy(y_hbm.at[idx_v.at[kk]], rows_v, sem).wait()
                pltpu.sync_copy(rows_v, yg_hbm.at[kk, pl.ds(win * SC_WINDOW, SC_WINDOW)])

    return run(y_rows, dest_win)


def _expert_kernel(be_ref, nu_ref, nx_ref, x_ref, wgu_hbm, bgu_ref, wd_hbm, bd_ref, y_ref,
                   wgu32, wd32, wgu16, wd16, sems):
    i = pl.program_id(0)
    e = be_ref[i]
    prev = be_ref[jnp.maximum(i - 1, 0)]

    def weight_copies(ex):
        return (pltpu.make_async_copy(wgu_hbm.at[ex], wgu32, sems.at[0]),
                pltpu.make_async_copy(wd_hbm.at[ex], wd32, sems.at[1]))

    @pl.when(i == 0)
    def _():
        for cp in weight_copies(e):
            cp.start()

    @pl.when((i == 0) | (e != prev))
    def _():
        for cp in weight_copies(e):
            cp.wait()
        rows = 128

        def cast(r, carry):
            sl = pl.ds(pl.multiple_of(r * rows, rows), rows)
            wgu16[sl, :] = wgu32[sl, :].astype(BF16)
            wd16[sl, :] = wd32[sl, :].astype(BF16)
            return carry

        lax.fori_loop(0, D_MODEL // rows, cast, 0)
        nxt = nx_ref[i]

        @pl.when(nxt >= 0)
        def _():
            for cp in weight_copies(nxt):
                cp.start()

    @pl.when(i < nu_ref[0])
    def _():
        x = _load_row_tiles(x_ref, MOE_BLOCK).astype(BF16)
        bgu = bgu_ref[0]
        glu = _dot(x, wgu16[:, 0:D_FF]) + bgu[:, 0:D_FF]
        lin = _dot(x, wgu16[:, D_FF:]) + bgu[:, D_FF:]
        glu = jnp.minimum(glu, SWIGLU_LIMIT)
        lin = jnp.clip(lin, -SWIGLU_LIMIT, SWIGLU_LIMIT)
        act = glu * _sigmoid(SWIGLU_ALPHA * glu) * (lin + 1.0)
        y = _dot(act.astype(BF16), wd16[...]) + bd_ref[0]
        _store_row_tiles(y_ref, y)


def _experts(blk_e, n_used, next_e, xs2, w_gu, b_gu3, w_down, b_down3):
    row_blk = pl.BlockSpec((MOE_BLOCK * ROW_TILES, LANES),
                           lambda i, be, nu, nx: (jnp.minimum(i, nu[0] - 1), 0))
    grid_spec = pltpu.PrefetchScalarGridSpec(
        num_scalar_prefetch=3,
        grid=(N_BLOCKS,),
        in_specs=[row_blk,
                  pl.BlockSpec(memory_space=pl.ANY),
                  pl.BlockSpec((1, 1, 2 * D_FF), lambda i, be, nu, nx: (be[i], 0, 0)),
                  pl.BlockSpec(memory_space=pl.ANY),
                  pl.BlockSpec((1, 1, D_MODEL), lambda i, be, nu, nx: (be[i], 0, 0))],
        out_specs=row_blk,
        scratch_shapes=[pltpu.VMEM((D_MODEL, 2 * D_FF), F32),
                        pltpu.VMEM((D_FF, D_MODEL), F32),
                        pltpu.VMEM((D_MODEL, 2 * D_FF), BF16),
                        pltpu.VMEM((D_FF, D_MODEL), BF16),
                        pltpu.SemaphoreType.DMA((2,))],
    )
    return pl.pallas_call(
        _expert_kernel,
        grid_spec=grid_spec,
        out_shape=jax.ShapeDtypeStruct((N_SLOTS * ROW_TILES, LANES), U32),
        compiler_params=pltpu.CompilerParams(dimension_semantics=("arbitrary",),
                                             vmem_limit_bytes=VMEM_LIMIT),
        name="experts",
    )(blk_e, n_used, next_e, xs2, w_gu, b_gu3, w_down, b_down3)


def _final_kernel(yg_ref, gate_ref, x1_ref, mod_ref, gpost_ref, *maybe_alias_and_out):
    o_ref = maybe_alias_and_out[-1]
    gates = gate_ref[...]
    y = None
    for kk in range(TOP_K):
        part = _load_row_tiles(yg_ref.at[kk], TOK_TILE) * gates[:, kk:kk + 1]
        y = part if y is None else y + part
    gate_f = mod_ref[0][5:6]
    o_ref[...] = x1_ref[...] + gate_f * (_rms(y) * gpost_ref[...])


def _final(part, yg3, gates, x1, mod3, g_post, out_so_far):
    tiles = SEQ // TOK_TILE
    tok = lambda w: pl.BlockSpec((TOK_TILE, w), lambda i: (part * tiles + i, 0))
    in_specs = [pl.BlockSpec((TOP_K, TOK_TILE * ROW_TILES, LANES), lambda i: (0, i, 0)),
                tok(TOP_K), tok(D_MODEL),
                pl.BlockSpec((1, 6, D_MODEL), lambda i: (part, 0, 0)),
                pl.BlockSpec((1, D_MODEL), lambda i: (0, 0))]
    args = [yg3, gates, x1, mod3, g_post]
    aliases = {}
    if out_so_far is not None:
        in_specs.append(pl.BlockSpec(memory_space=pl.ANY))
        args.append(out_so_far)
        aliases = {len(args) - 1: 0}
    return pl.pallas_call(
        _final_kernel,
        grid=(tiles,),
        in_specs=in_specs,
        out_specs=tok(D_MODEL),
        out_shape=jax.ShapeDtypeStruct((N_TOK, D_MODEL), F32),
        input_output_aliases=aliases,
        compiler_params=pltpu.CompilerParams(vmem_limit_bytes=VMEM_LIMIT),
        name="combine_final",
    )(*args)


def _routing_tables(idx, rank, counts):
    counts = counts.reshape(N_EXPERTS)
    experts = jnp.arange(N_EXPERTS, dtype=I32)
    padded = ((counts + MOE_BLOCK - 1) // MOE_BLOCK) * MOE_BLOCK
    pend = jnp.cumsum(padded)
    pstart = pend - padded
    dest = jnp.sum(jnp.where(idx[..., None] == experts, pstart, 0), axis=-1) + rank
    dest_win = dest.reshape(TOP_K, N_TOK // SC_WINDOW, SC_WINDOW).transpose(1, 0, 2)
    n_used = (pend[-1] // MOE_BLOCK).astype(I32).reshape(1)
    blk_start = jnp.arange(N_BLOCKS, dtype=I32) * MOE_BLOCK
    blk_e = jnp.sum(blk_start[:, None] >= pend[None, :], axis=1).astype(I32)
    last_e = jnp.max(jnp.where(counts > 0, experts, 0))
    blk_e = jnp.minimum(blk_e, last_e)
    later = (experts[None, :] > experts[:, None]) & (counts[None, :] > 0)
    next_nonempty = jnp.min(jnp.where(later, experts[None, :], N_EXPERTS), axis=1)
    next_nonempty = jnp.where(next_nonempty == N_EXPERTS, -1, next_nonempty).astype(I32)
    next_e = jnp.sum(jnp.where(blk_e[:, None] == experts, next_nonempty, 0), axis=1).astype(I32)
    return dest_win, blk_e, n_used, next_e


def kernel(x, c, w_ada, b_ada, g_pre_mix, g_post_mix, w_in, attn_norm_w, hgrn_lb, hgrn_norm_w,
           w_out, g_pre_ffn, g_post_ffn, w_router, b_router, w_gu, b_gu, w_down, b_down):
    c_pad = jnp.pad(c, ((0, SUBLANES - BATCH), (0, 0)))
    mod = _ada_mod(c_pad, w_ada[0], b_ada)
    mod3 = mod[:BATCH].reshape(BATCH, 6, D_MODEL)

    x2 = x.reshape(N_TOK, D_MODEL)
    q4, k4, v4, q16, k16, v16, qr, fr, ir, gr = _inproj(x2, mod3, g_pre_mix, w_in[0].astype(BF16))
    nat = lambda t: t.reshape(BATCH, SEQ, D_REC)
    ya = _attention(q4, k4, v4, q16, k16, v16, attn_norm_w)
    yr = _hgrn(nat(qr), nat(fr), nat(ir), nat(gr), hgrn_lb, hgrn_norm_w)

    x1, h2, idx, gates, rank, counts = _mid(
        ya.reshape(N_TOK, D_ATTN), yr.reshape(N_TOK, D_REC), x2, mod3, g_post_mix, g_pre_ffn,
        w_out[0].astype(BF16), w_router[0], b_router)

    dest_win, blk_e, n_used, next_e = _routing_tables(idx, rank, counts)
    xs = _sc_dispatch(h2.reshape(N_TOK, ROW_TILES, LANES), dest_win)
    ys = _experts(blk_e, n_used, next_e, xs.reshape(N_SLOTS * ROW_TILES, LANES),
                  w_gu[0], b_gu[0].reshape(N_EXPERTS, 1, 2 * D_FF),
                  w_down[0], b_down[0].reshape(N_EXPERTS, 1, D_MODEL))
    ys3 = ys.reshape(N_SLOTS, ROW_TILES, LANES)
    win_per_seq = SEQ // SC_WINDOW
    out = None
    for b in range(BATCH):
        yg = _sc_collect(ys3, dest_win[b * win_per_seq:(b + 1) * win_per_seq])
        out = _final(b, yg.reshape(TOP_K, SEQ * ROW_TILES, LANES), gates, x1, mod3, g_post_ffn, out)
    return out.reshape(BATCH, SEQ, D_MODEL)
```

```python
import functools
import math

import jax
import jax.numpy as jnp
from jax import lax
from jax.experimental import pallas as pl
from jax.experimental.pallas import tpu as pltpu
from jax.experimental.pallas import tpu_sc as plsc

F32 = jnp.float32
BF16 = jnp.bfloat16
I32 = jnp.int32
U32 = jnp.uint32
HIGHEST = lax.Precision.HIGHEST

D_MODEL = 1024
BATCH = 4
SEQ = 4096
N_TOK = BATCH * SEQ
D_ATTN = 512
HEAD_DIM_A = 64
ATT_BLOCK = 128
DIL_MID = 4
DIL_MAX = 16
SUB_MID = SEQ // DIL_MID
SUB_MAX = SEQ // DIL_MAX
D_REC = 512
HGRN_CHUNK = 32
HGRN_SUPER = 256
N_EXPERTS = 32
TOP_K = 4
D_FF = 1024
SWIGLU_LIMIT = 7.0
SWIGLU_ALPHA = 1.702
EPS = 1e-6
NEG_BIG = -1e30
Q_SCALE = HEAD_DIM_A ** -0.5 * math.log2(math.e)

LANES = 128
SUBLANES = 8
ROW_TILES = D_MODEL // 2 // LANES

TOK_TILE = 512
MOE_BLOCK = 512
N_SLOTS = N_TOK * TOP_K + N_EXPERTS * MOE_BLOCK
N_BLOCKS = N_SLOTS // MOE_BLOCK
SC_WINDOW = 64
VMEM_LIMIT = 56 * 1024 * 1024


def _sigmoid(x):
    return 1.0 / (1.0 + jnp.exp(-x))


def _dot(a, b):
    return jnp.dot(a, b, preferred_element_type=F32)


def _dot_nt(a, b):
    return lax.dot_general(a, b, (((1,), (1,)), ((), ())), preferred_element_type=F32)


def _split_bf16(x):
    hi = x.astype(BF16)
    return hi, (x - hi.astype(F32)).astype(BF16)


def _rms(x):
    return x * lax.rsqrt(jnp.mean(x * x, axis=-1, keepdims=True) + EPS)


def _ada_kernel(c_ref, w_ref, b_ref, o_ref):
    c = c_ref[...]
    cond = c * _sigmoid(c)
    o_ref[...] = jnp.dot(cond, w_ref[...], precision=HIGHEST,
                         preferred_element_type=F32) + b_ref[...]


def _ada_mod(c_pad, w_ada, b_ada):
    n = w_ada.shape[1]
    tn = 1536
    return pl.pallas_call(
        _ada_kernel,
        grid=(n // tn,),
        in_specs=[pl.BlockSpec((SUBLANES, D_MODEL), lambda j: (0, 0)),
                  pl.BlockSpec((D_MODEL, tn), lambda j: (0, j)),
                  pl.BlockSpec((1, tn), lambda j: (0, j))],
        out_specs=pl.BlockSpec((SUBLANES, tn), lambda j: (0, j)),
        out_shape=jax.ShapeDtypeStruct((SUBLANES, n), F32),
        compiler_params=pltpu.CompilerParams(vmem_limit_bytes=VMEM_LIMIT),
        name="ada_mod",
    )(c_pad, w_ada, b_ada)


def _inproj_kernel(x_ref, mod_ref, g_ref, w_ref, q4, k4, v4, q16, k16, v16, qr, fr, ir, gr,
                   stage_nat, stage_mid):
    mod = mod_ref[0]
    shift, scale = mod[0:1], mod[1:2]
    h = _rms(x_ref[...]) * g_ref[...] * (1.0 + scale) + shift
    hb = h.astype(BF16)
    slabs = D_ATTN // LANES
    rows_mid = TOK_TILE // DIL_MID
    rows_max = TOK_TILE // DIL_MAX

    def proj(j):
        return _dot(hb, w_ref[:, j * D_ATTN:(j + 1) * D_ATTN])

    for j, (o_mid, o_max) in enumerate(((q4, q16), (k4, k16), (v4, v16))):
        r = proj(j)
        if j == 0:
            r = r * Q_SCALE
        for cs in range(slabs):
            stage_nat[cs] = r[:, cs * LANES:(cs + 1) * LANES]
        for cs in range(slabs):
            lanes = slice(cs * LANES, (cs + 1) * LANES)
            for sub in range(DIL_MID):
                piece = stage_nat[cs, pl.ds(sub, rows_mid, stride=DIL_MID), :]
                o_mid[sub, :, lanes] = piece.astype(BF16)
                stage_mid[cs, sub] = piece
            for sub in range(DIL_MAX):
                piece = stage_mid[cs, sub % DIL_MID, pl.ds(sub // DIL_MID, rows_max, stride=DIL_MID), :]
                o_max[sub, :, lanes] = piece.astype(BF16)
    for j, o_ref in enumerate((qr, fr, ir, gr)):
        o_ref[...] = proj(3 + j)


def _inproj(x2, mod3, g_pre, w_in_bf16):
    tiles_per_seq = SEQ // TOK_TILE
    rows_mid = TOK_TILE // DIL_MID
    rows_max = TOK_TILE // DIL_MAX
    mid = pl.BlockSpec((None, DIL_MID, rows_mid, D_ATTN),
                       lambda i: (i // tiles_per_seq, 0, i % tiles_per_seq, 0))
    mx = pl.BlockSpec((None, DIL_MAX, rows_max, D_ATTN),
                      lambda i: (i // tiles_per_seq, 0, i % tiles_per_seq, 0))
    nat = pl.BlockSpec((TOK_TILE, D_REC), lambda i: (i, 0))
    mid_shape = jax.ShapeDtypeStruct((BATCH, DIL_MID, SUB_MID, D_ATTN), BF16)
    mx_shape = jax.ShapeDtypeStruct((BATCH, DIL_MAX, SUB_MAX, D_ATTN), BF16)
    nat_shape = jax.ShapeDtypeStruct((N_TOK, D_REC), F32)
    return pl.pallas_call(
        _inproj_kernel,
        grid=(N_TOK // TOK_TILE,),
        in_specs=[pl.BlockSpec((TOK_TILE, D_MODEL), lambda i: (i, 0)),
                  pl.BlockSpec((1, 6, D_MODEL), lambda i: (i // tiles_per_seq, 0, 0)),
                  pl.BlockSpec((1, D_MODEL), lambda i: (0, 0)),
                  pl.BlockSpec(w_in_bf16.shape, lambda i: (0, 0))],
        out_specs=[mid, mid, mid, mx, mx, mx, nat, nat, nat, nat],
        out_shape=[mid_shape] * 3 + [mx_shape] * 3 + [nat_shape] * 4,
        scratch_shapes=[pltpu.VMEM((D_ATTN // LANES, TOK_TILE, LANES), F32),
                        pltpu.VMEM((D_ATTN // LANES, DIL_MID, rows_mid, LANES), F32)],
        compiler_params=pltpu.CompilerParams(vmem_limit_bytes=VMEM_LIMIT),
        name="inproj",
    )(x2, mod3, g_pre, w_in_bf16)


def _attn_kernel(q4_ref, k4_ref, v4_ref, q16_ref, k16_ref, v16_ref, nw_ref, o_ref,
                 sel_scr, keep_scr, o_scr, m_scr, l_scr):
    bw = ATT_BLOCK
    lane = lax.broadcasted_iota(I32, (bw, LANES), 1)
    head0 = lane < HEAD_DIM_A
    head_masks = (jnp.where(head0, 1.0, 0.0).astype(BF16), jnp.where(head0, 0.0, 1.0).astype(BF16))

    rr = lax.broadcasted_iota(I32, (bw, bw), 0)
    cc = lax.broadcasted_iota(I32, (bw, bw), 1)
    piece = bw // DIL_MID
    rp = DIL_MID * (rr % piece) + rr // piece
    cp = DIL_MID * (cc % piece) + cc // piece
    for var, (r, c) in enumerate(((rr, cc), (rr, cc), (rp, cp), (rp, cp))):
        first = var % 2 == 1
        left = (c < r) if first else (c >= r)
        right = jnp.zeros_like(left) if first else jnp.logical_not(left)
        sel_scr[var] = jnp.where(left, 1.0, 0.0)
        keep_scr[var, 0] = jnp.where(left, 1.0, 0.0).astype(BF16)
        keep_scr[var, 1] = jnp.where(right, 1.0, 0.0).astype(BF16)

    def attend_group(loaded):
        scores = [_dot_nt(q * hm, k) for (q, k, _, _, _) in loaded for hm in head_masks]
        probs = []
        for n, s in enumerate(scores):
            _, _, _, var, right_bias = loaded[n // 2]
            folded = jnp.where(sel_scr[var] > 0.5, s[:, 0:bw], s[:, bw:] + right_bias)
            m = jnp.max(folded, axis=-1, keepdims=True)
            p = jnp.exp2(folded - m)
            pb = p.astype(BF16)
            spread = jnp.concatenate([pb * keep_scr[var, 0], pb * keep_scr[var, 1]], axis=1)
            probs.append((spread, jnp.sum(p, axis=-1, keepdims=True), m))
        pvs = [_dot(p, loaded[n // 2][2]) for n, (p, _, _) in enumerate(probs)]
        results = []
        for u in range(len(loaded)):
            (_, l0, m0), (_, l1, m1) = probs[2 * u], probs[2 * u + 1]
            results.append((jnp.where(head0, pvs[2 * u], pvs[2 * u + 1]),
                            jnp.where(head0, jnp.broadcast_to(m0, (bw, LANES)),
                                      jnp.broadcast_to(m1, (bw, LANES))),
                            jnp.where(head0, jnp.broadcast_to(l0, (bw, LANES)),
                                      jnp.broadcast_to(l1, (bw, LANES)))))
        return results

    group = 8

    def first_block_bias(blk):
        return jnp.where(blk == 0, NEG_BIG, 0.0)

    def group_d1(g, carry):
        loaded, starts = [], []
        for i in range(group):
            blk = g * group + i
            qs = pl.multiple_of(blk * piece, piece)
            ks = pl.multiple_of(jnp.maximum(blk - 1, 0) * piece, piece)
            q = jnp.concatenate([q4_ref[r, pl.ds(qs, piece), :] for r in range(DIL_MID)], axis=0)
            k = jnp.concatenate([k4_ref[r, pl.ds(ks + half * piece, piece), :]
                                 for half in range(2) for r in range(DIL_MID)], axis=0)
            v = jnp.concatenate([v4_ref[r, pl.ds(ks + half * piece, piece), :]
                                 for half in range(2) for r in range(DIL_MID)], axis=0)
            loaded.append((q, k, v, jnp.where(blk == 0, 3, 2), first_block_bias(blk)))
            starts.append(qs)
        for qs, parts in zip(starts, attend_group(loaded)):
            for r in range(DIL_MID):
                for scr, val in zip((o_scr, m_scr, l_scr), parts):
                    scr[0, r, pl.ds(qs, piece), :] = val[r * piece:(r + 1) * piece]
        return carry

    def group_d4(g, carry):
        loaded, dsts = [], []
        for i in range(group):
            r, blk = i % DIL_MID, g * (group // DIL_MID) + i // DIL_MID
            qs = pl.multiple_of(blk * bw, bw)
            ks = pl.multiple_of(jnp.maximum(blk - 1, 0) * bw, bw)
            loaded.append((q4_ref[r, pl.ds(qs, bw), :], k4_ref[r, pl.ds(ks, 2 * bw), :],
                           v4_ref[r, pl.ds(ks, 2 * bw), :], jnp.where(blk == 0, 1, 0),
                           first_block_bias(blk)))
            dsts.append((r, qs))
        for (r, qs), parts in zip(dsts, attend_group(loaded)):
            for scr, val in zip((o_scr, m_scr, l_scr), parts):
                scr[1, r, pl.ds(qs, bw), :] = val
        return carry

    def group_d16(g, carry):
        loaded, dsts = [], []
        for i in range(group):
            r, blk = g * (group // 2) + i // 2, i % 2
            loaded.append((q16_ref[r, blk * bw:(blk + 1) * bw, :], k16_ref[r], v16_ref[r],
                           1 - blk, NEG_BIG if blk == 0 else 0.0))
            dsts.append((r % DIL_MID, pl.ds(blk * (bw * DIL_MID) + r // DIL_MID, bw, stride=DIL_MID)))
        for (sub, dst), parts in zip(dsts, attend_group(loaded)):
            for scr, val in zip((o_scr, m_scr, l_scr), parts):
                scr[2, sub, dst, :] = val
        return carry

    units = SEQ // bw
    lax.fori_loop(0, units // group, group_d1, 0)
    lax.fori_loop(0, units // group, group_d4, 0)
    lax.fori_loop(0, units // group, group_d16, 0)

    rows = 256
    hi = lax.broadcasted_iota(I32, (LANES, LANES), 0) // HEAD_DIM_A
    hj = lax.broadcasted_iota(I32, (LANES, LANES), 1) // HEAD_DIM_A
    head_sum = jnp.where(hi == hj, 1.0, 0.0).astype(BF16)
    nw = nw_ref[...]

    def head_sums(x):
        x_hi, x_lo = _split_bf16(x)
        return _dot(x_hi, head_sum) + _dot(x_lo, head_sum)

    def merge(t, carry):
        r = t // (SUB_MID // rows)
        start = pl.multiple_of((t % (SUB_MID // rows)) * rows, rows)
        sl = pl.ds(start, rows)
        own_score = head_sums(q4_ref[r, sl, :].astype(F32) * k4_ref[r, sl, :].astype(F32))
        ms = [m_scr[n, r, sl, :] for n in range(3)]
        mx = jnp.maximum(jnp.maximum(jnp.maximum(ms[0], ms[1]), ms[2]), own_score)
        ws = [jnp.exp2(m - mx) for m in ms]
        w_own = float(len(ms)) * jnp.exp2(own_score - mx)
        num = (ws[0] * o_scr[0, r, sl, :] + ws[1] * o_scr[1, r, sl, :] + ws[2] * o_scr[2, r, sl, :]
               + w_own * v4_ref[r, sl, :].astype(F32))
        den = ws[0] * l_scr[0, r, sl, :] + ws[1] * l_scr[1, r, sl, :] + ws[2] * l_scr[2, r, sl, :] + w_own
        o = num / den
        mean_sq = head_sums(o * o) * (1.0 / HEAD_DIM_A)
        o_ref[pl.ds(start * DIL_MID + r, rows, stride=DIL_MID), :] = o * lax.rsqrt(mean_sq + EPS) * nw
        return carry

    lax.fori_loop(0, SEQ // rows, merge, 0, unroll=2)


def _attention(q4, k4, v4, q16, k16, v16, attn_norm_w):
    hp = D_ATTN // LANES
    mid = pl.BlockSpec((None, DIL_MID, SUB_MID, LANES), lambda b, h: (b, 0, 0, h))
    mx = pl.BlockSpec((None, DIL_MAX, SUB_MAX, LANES), lambda b, h: (b, 0, 0, h))
    scr = pltpu.VMEM((3, DIL_MID, SUB_MID, LANES), F32)
    return pl.pallas_call(
        _attn_kernel,
        grid=(BATCH, hp),
        in_specs=[mid, mid, mid, mx, mx, mx, pl.BlockSpec((1, LANES), lambda b, h: (0, h))],
        out_specs=pl.BlockSpec((None, SEQ, LANES), lambda b, h: (b, 0, h)),
        out_shape=jax.ShapeDtypeStruct((BATCH, SEQ, D_ATTN), F32),
        scratch_shapes=[pltpu.VMEM((4, ATT_BLOCK, ATT_BLOCK), F32),
                        pltpu.VMEM((4, 2, ATT_BLOCK, ATT_BLOCK), BF16), scr, scr, scr],
        compiler_params=pltpu.CompilerParams(vmem_limit_bytes=VMEM_LIMIT),
        name="dilated_attn",
    )(q4, k4, v4, q16, k16, v16, attn_norm_w)


def _hgrn_kernel(qr_ref, fr_ref, ir_ref, gr_ref, lb_ref, nw_ref, o_ref,
                 qe_scr, oi_scr, delta_scr, dec_scr, st_scr):
    sup, c = HGRN_SUPER, HGRN_CHUNK
    nch = sup // c
    n_sup = SEQ // sup
    lbp = lb_ref[...]
    lmx = jnp.max(lbp, axis=0, keepdims=True)
    ex = jnp.exp(lbp - lmx)
    lb = ex[0:1] / (ex[0:1] + ex[1:2])
    nw = nw_ref[...]

    ri = lax.broadcasted_iota(I32, (sup, sup), 0)
    ci = lax.broadcasted_iota(I32, (sup, sup), 1)
    same_chunk = (ri // c) == (ci // c)
    causal = same_chunk & (ci <= ri)
    cum_op = jnp.where(causal, 1.0, 0.0).astype(BF16)

    def chunk_rows(kd, ch):
        parts = []
        if ch > 0:
            parts.append(jnp.zeros((ch * c, LANES), BF16))
        parts.append(kd[ch * c:(ch + 1) * c])
        if ch < nch - 1:
            parts.append(jnp.zeros(((nch - 1 - ch) * c, LANES), BF16))
        return jnp.concatenate(parts, axis=0)

    group = 2

    def independent(g, carry):
        ts = [g * group + i for i in range(group)]
        sls = [pl.ds(pl.multiple_of(t * sup, sup), sup) for t in ts]
        pre = []
        for sl in sls:
            f = lb + (1.0 - lb) * _sigmoid(fr_ref[sl, :])
            logf_hi, logf_lo = _split_bf16(jnp.log(f))
            pre.append((1.0 - f, _dot(cum_op, logf_hi) + _dot(cum_op, logf_lo)))
        mid = []
        for t, sl, (kk, b) in zip(ts, sls, pre):
            b_last = jnp.concatenate(
                [jnp.broadcast_to(b[(ch + 1) * c - 1:(ch + 1) * c], (c, LANES)) for ch in range(nch)],
                axis=0)
            q = qr_ref[sl, :]
            qeb = (q * _sigmoid(q) * jnp.exp(b)).astype(BF16)
            ke = (kk * jnp.exp(-b)).astype(BF16)
            kd = (kk * jnp.exp(b_last - b)).astype(BF16)
            qe_scr[sl, :] = qeb
            dec_rows = jnp.concatenate([b_last[ch * c:ch * c + 1] for ch in range(nch)], axis=0)
            dec_scr[pl.ds(pl.multiple_of(t * nch, nch), nch), :] = jnp.exp(dec_rows)
            v = ir_ref[sl, :]
            vt = v.T.astype(BF16)
            scores = _dot_nt(qeb, ke)
            for pair in range(nch // 2):
                rhs = jnp.concatenate([chunk_rows(kd, 2 * pair), chunk_rows(kd, 2 * pair + 1)], axis=1)
                d2 = _dot(vt, rhs)
                delta_scr[t * nch + 2 * pair] = d2[:, 0:LANES]
                delta_scr[t * nch + 2 * pair + 1] = d2[:, LANES:]
            mid.append((scores, v.astype(BF16)))
        for sl, (scores, vb) in zip(sls, mid):
            a = jnp.where(causal, scores, 0.0)
            oi_scr[sl, :] = _dot(a.astype(BF16), vb)
        return carry

    lax.fori_loop(0, n_sup // group, independent, 0)

    def recur(ch, st):
        st_scr[ch] = st.astype(BF16)
        return st * dec_scr[pl.ds(ch, 1), :] + delta_scr[ch]

    lax.fori_loop(0, SEQ // c, recur, jnp.zeros((LANES, LANES), F32), unroll=8)

    def finish(t, carry):
        sl = pl.ds(pl.multiple_of(t * sup, sup), sup)
        parts = [_dot_nt(qe_scr[pl.ds(pl.multiple_of(t * sup + ch * c, c), c), :], st_scr[t * nch + ch])
                 for ch in range(nch)]
        o = oi_scr[sl, :] + jnp.concatenate(parts, axis=0)
        g = gr_ref[sl, :]
        o_ref[sl, :] = (_rms(o) * nw * (g * _sigmoid(g))).astype(o_ref.dtype)
        return carry

    lax.fori_loop(0, n_sup, finish, 0, unroll=4)


def _hgrn(qr, fr, ir, gr, hgrn_lb, hgrn_norm_w):
    nh = D_REC // LANES
    n_chunks = SEQ // HGRN_CHUNK
    blk = pl.BlockSpec((None, SEQ, LANES), lambda b, h: (b, 0, h))
    return pl.pallas_call(
        _hgrn_kernel,
        grid=(BATCH, nh),
        in_specs=[blk, blk, blk, blk,
                  pl.BlockSpec((2, LANES), lambda b, h: (0, h)),
                  pl.BlockSpec((1, LANES), lambda b, h: (0, h))],
        out_specs=blk,
        out_shape=jax.ShapeDtypeStruct((BATCH, SEQ, D_REC), BF16),
        scratch_shapes=[pltpu.VMEM((SEQ, LANES), BF16),
                        pltpu.VMEM((SEQ, LANES), F32),
                        pltpu.VMEM((n_chunks, LANES, LANES), F32),
                        pltpu.VMEM((n_chunks, LANES), F32),
                        pltpu.VMEM((n_chunks, LANES, LANES), BF16)],
        compiler_params=pltpu.CompilerParams(vmem_limit_bytes=VMEM_LIMIT),
        name="hgrn2",
    )(qr, fr, ir, gr, hgrn_lb, hgrn_norm_w)


def _tile_batch(i):
    return i // (SEQ // TOK_TILE)


def _store_row_tiles(ref, val):
    rows, half = val.shape[0], D_MODEL // 2
    lo = lax.bitcast_convert_type(val[:, :half].astype(BF16).astype(F32), U32)
    hi = lax.bitcast_convert_type(val[:, half:].astype(BF16).astype(F32), U32)
    words = lax.shift_right_logical(lo, jnp.uint32(16)) | (hi & jnp.uint32(0xFFFF0000))
    for j in range(ROW_TILES):
        ref[pl.ds(j, rows, stride=ROW_TILES), :] = words[:, j * LANES:(j + 1) * LANES]


def _load_row_tiles(ref, rows):
    words = jnp.concatenate(
        [ref[pl.ds(j, rows, stride=ROW_TILES), :] for j in range(ROW_TILES)], axis=1)
    lo = lax.bitcast_convert_type(lax.shift_left(words, jnp.uint32(16)), F32)
    hi = lax.bitcast_convert_type(words & jnp.uint32(0xFFFF0000), F32)
    return jnp.concatenate([lo, hi], axis=1)


def _mid_kernel(ya_ref, yr_ref, x_ref, mod_ref, gpost_ref, gpre_ref, wo_ref, wr_ref, br_ref,
                x1_ref, h2_ref, idx_ref, gate_ref, rank_ref, cnt_ref, carry_ref):
    i = pl.program_id(0)

    @pl.when(i == 0)
    def _():
        carry_ref[...] = jnp.zeros_like(carry_ref)

    mod = mod_ref[0]
    gate_m, shift_f, scale_f = mod[2:3], mod[3:4], mod[4:5]
    y = _dot(ya_ref[...].astype(BF16), wo_ref[0:D_ATTN, :]) + _dot(yr_ref[...], wo_ref[D_ATTN:, :])
    x1 = x_ref[...] + gate_m * (_rms(y) * gpost_ref[...])
    x1_ref[...] = x1
    h2 = _rms(x1) * gpre_ref[...] * (1.0 + scale_f) + shift_f
    _store_row_tiles(h2_ref, h2)

    h2_hi, h2_lo = _split_bf16(h2)
    wr_hi, wr_lo = _split_bf16(wr_ref[...])
    logits = _dot(h2_hi, wr_hi) + _dot(h2_lo, wr_hi) + _dot(h2_hi, wr_lo) + br_ref[...]
    tm = logits.shape[0]
    pad = jnp.zeros((tm, LANES - N_EXPERTS), F32)
    work = jnp.concatenate([logits, pad], axis=1).T[0:N_EXPERTS]
    eidx = lax.broadcasted_iota(I32, (N_EXPERTS, tm), 0).astype(F32)
    vals, idxs = [], []
    onehot = jnp.zeros((N_EXPERTS, tm), F32)
    for _ in range(TOP_K):
        m = jnp.max(work, axis=0, keepdims=True)
        sel = jnp.min(jnp.where(work == m, eidx, float(N_EXPERTS)), axis=0, keepdims=True)
        hit = eidx == sel
        work = jnp.where(hit, -jnp.inf, work)
        onehot = jnp.where(hit, 1.0, onehot)
        vals.append(m)
        idxs.append(sel)
    ex = [jnp.exp(vv - vals[0]) for vv in vals]
    inv_den = 1.0 / (ex[0] + ex[1] + ex[2] + ex[3])

    ri = lax.broadcasted_iota(I32, (tm, tm), 0)
    ci = lax.broadcasted_iota(I32, (tm, tm), 1)
    strict_upper = jnp.where(ri < ci, 1.0, 0.0).astype(BF16)
    before = _dot(onehot.astype(BF16), strict_upper) + carry_ref[...]
    ranks = [jnp.sum(jnp.where(eidx == idxs[kk], before, 0.0), axis=0, keepdims=True)
             for kk in range(TOP_K)]
    idx_ref[...] = jnp.concatenate(idxs, axis=0).astype(I32)
    rank_ref[...] = jnp.concatenate(ranks, axis=0).astype(I32)
    gates_t = jnp.concatenate([e * inv_den for e in ex]
                              + [jnp.zeros((LANES - TOP_K, tm), F32)], axis=0)
    gate_ref[...] = gates_t.T[:, 0:TOP_K]
    total = carry_ref[...] + jnp.sum(onehot, axis=1, keepdims=True)
    carry_ref[...] = total
    cnt_ref[...] = total.astype(I32)


def _mid(ya, yr, x2, mod3, g_post, g_pre, w_out_bf16, w_router, b_router):
    tok = lambda w: pl.BlockSpec((TOK_TILE, w), lambda i: (i, 0))
    const = lambda s: pl.BlockSpec(s, lambda i: (0,) * len(s))
    lanes_tok = pl.BlockSpec((TOP_K, TOK_TILE), lambda i: (0, i))
    return pl.pallas_call(
        _mid_kernel,
        grid=(N_TOK // TOK_TILE,),
        in_specs=[tok(D_ATTN), tok(D_REC), tok(D_MODEL),
                  pl.BlockSpec((1, 6, D_MODEL), lambda i: (_tile_batch(i), 0, 0)),
                  const((1, D_MODEL)), const((1, D_MODEL)),
                  const((D_MODEL, D_MODEL)), const((D_MODEL, N_EXPERTS)), const((1, N_EXPERTS))],
        out_specs=[tok(D_MODEL),
                   pl.BlockSpec((TOK_TILE * ROW_TILES, LANES), lambda i: (i, 0)),
                   lanes_tok, tok(TOP_K), lanes_tok, const((N_EXPERTS, 1))],
        out_shape=[jax.ShapeDtypeStruct((N_TOK, D_MODEL), F32),
                   jax.ShapeDtypeStruct((N_TOK * ROW_TILES, LANES), U32),
                   jax.ShapeDtypeStruct((TOP_K, N_TOK), I32),
                   jax.ShapeDtypeStruct((N_TOK, TOP_K), F32),
                   jax.ShapeDtypeStruct((TOP_K, N_TOK), I32),
                   jax.ShapeDtypeStruct((N_EXPERTS, 1), I32)],
        scratch_shapes=[pltpu.VMEM((N_EXPERTS, 1), F32)],
        compiler_params=pltpu.CompilerParams(dimension_semantics=("arbitrary",),
                                             vmem_limit_bytes=VMEM_LIMIT),
        name="outproj_router",
    )(ya, yr, x2, mod3, g_post, g_pre, w_out_bf16, w_router, b_router)


def _sc_mesh():
    return plsc.VectorSubcoreMesh(core_axis_name="c", subcore_axis_name="s")


def _sc_worker_count():
    info = plsc.get_sparse_core_info()
    return info.num_cores, info.num_cores * info.num_subcores


def _sc_dispatch(h_rows, dest_win):
    n_cores, n_workers = _sc_worker_count()
    n_win = N_TOK // SC_WINDOW
    per_worker = n_win // n_workers

    @functools.partial(
        pl.kernel, mesh=_sc_mesh(),
        out_type=jax.ShapeDtypeStruct((N_SLOTS, ROW_TILES, LANES), U32),
        scratch_types=[pltpu.VMEM((TOP_K, SC_WINDOW), I32),
                       pltpu.VMEM((SC_WINDOW, ROW_TILES, LANES), U32),
                       pltpu.SemaphoreType.DMA],
        name="sc_dispatch")
    def run(h_hbm, dest_hbm, xs_hbm, idx_v, rows_v, sem):
        wid = lax.axis_index("s") * n_cores + lax.axis_index("c")

        @pl.loop(0, per_worker)
        def _(j):
            win = wid * per_worker + j
            pltpu.sync_copy(dest_hbm.at[win], idx_v)
            pltpu.sync_copy(h_hbm.at[pl.ds(win * SC_WINDOW, SC_WINDOW)], rows_v)
            copies = [pltpu.async_copy(rows_v, xs_hbm.at[idx_v.at[kk]], sem)
                      for kk in range(TOP_K)]
            for cp in copies:
                cp.wait()

    return run(h_rows, dest_win)


def _sc_collect(y_rows, dest_win):
    n_cores, n_workers = _sc_worker_count()
    n_win = dest_win.shape[0]
    per_worker = n_win // n_workers

    @functools.partial(
        pl.kernel, mesh=_sc_mesh(),
        out_type=jax.ShapeDtypeStruct((TOP_K, n_win * SC_WINDOW, ROW_TILES, LANES), U32),
        scratch_types=[pltpu.VMEM((TOP_K, SC_WINDOW), I32),
                       pltpu.VMEM((SC_WINDOW, ROW_TILES, LANES), U32),
                       pltpu.SemaphoreType.DMA],
        name="sc_collect")
    def run(y_hbm, dest_hbm, yg_hbm, idx_v, rows_v, sem):
        wid = lax.axis_index("s") * n_cores + lax.axis_index("c")

        @pl.loop(0, per_worker)
        def _(j):
            win = wid * per_worker + j
            pltpu.sync_copy(dest_hbm.at[win], idx_v)
            for kk in range(TOP_K):
                pltpu.async_copy(y_hbm.at[idx_v.at[kk]], rows_v, sem).wait()
                pltpu.sync_copy(rows_v, yg_hbm.at[kk, pl.ds(win * SC_WINDOW, SC_WINDOW)])

    return run(y_rows, dest_win)


def _expert_kernel(be_ref, nu_ref, nx_ref, nv_ref, x_ref, wgu_hbm, bgu_ref, wd_hbm, bd_ref, y_ref,
                   wgu32, wd32, wgu16, wd16, sems):
    i = pl.program_id(0)
    e = be_ref[i]
    prev = be_ref[jnp.maximum(i - 1, 0)]

    def weight_copies(ex):
        return (pltpu.make_async_copy(wgu_hbm.at[ex], wgu32, sems.at[0]),
                pltpu.make_async_copy(wd_hbm.at[ex], wd32, sems.at[1]))

    @pl.when(i == 0)
    def _():
        for cp in weight_copies(e):
            cp.start()

    @pl.when((i == 0) | (e != prev))
    def _():
        for cp in weight_copies(e):
            cp.wait()
        rows = 128

        def cast(r, carry):
            sl = pl.ds(pl.multiple_of(r * rows, rows), rows)
            wgu16[sl, :] = wgu32[sl, :].astype(BF16)
            wd16[sl, :] = wd32[sl, :].astype(BF16)
            return carry

        lax.fori_loop(0, D_MODEL // rows, cast, 0)
        nxt = nx_ref[i]

        @pl.when(nxt >= 0)
        def _():
            for cp in weight_copies(nxt):
                cp.start()

    def run_rows(rows):
        x = _load_row_tiles(x_ref, rows).astype(BF16)
        bgu = bgu_ref[0]
        glu = _dot(x, wgu16[:, 0:D_FF]) + bgu[:, 0:D_FF]
        lin = _dot(x, wgu16[:, D_FF:]) + bgu[:, D_FF:]
        glu = jnp.minimum(glu, SWIGLU_LIMIT)
        lin = jnp.clip(lin, -SWIGLU_LIMIT, SWIGLU_LIMIT)
        act = glu * _sigmoid(SWIGLU_ALPHA * glu) * (lin + 1.0)
        y = _dot(act.astype(BF16), wd16[...]) + bd_ref[0]
        _store_row_tiles(y_ref, y)

    valid = nv_ref[i]
    pl.when(valid > MOE_BLOCK // 2)(lambda: run_rows(MOE_BLOCK))
    pl.when((valid > 0) & (valid <= MOE_BLOCK // 2))(lambda: run_rows(MOE_BLOCK // 2))


def _experts(blk_e, n_used, next_e, blk_valid, xs2, w_gu, b_gu3, w_down, b_down3):
    row_blk = pl.BlockSpec((MOE_BLOCK * ROW_TILES, LANES),
                           lambda i, be, nu, nx, nv: (jnp.minimum(i, nu[0] - 1), 0))
    grid_spec = pltpu.PrefetchScalarGridSpec(
        num_scalar_prefetch=4,
        grid=(N_BLOCKS,),
        in_specs=[row_blk,
                  pl.BlockSpec(memory_space=pl.ANY),
                  pl.BlockSpec((1, 1, 2 * D_FF), lambda i, be, nu, nx, nv: (be[i], 0, 0)),
                  pl.BlockSpec(memory_space=pl.ANY),
                  pl.BlockSpec((1, 1, D_MODEL), lambda i, be, nu, nx, nv: (be[i], 0, 0))],
        out_specs=row_blk,
        scratch_shapes=[pltpu.VMEM((D_MODEL, 2 * D_FF), F32),
                        pltpu.VMEM((D_FF, D_MODEL), F32),
                        pltpu.VMEM((D_MODEL, 2 * D_FF), BF16),
                        pltpu.VMEM((D_FF, D_MODEL), BF16),
                        pltpu.SemaphoreType.DMA((2,))],
    )
    return pl.pallas_call(
        _expert_kernel,
        grid_spec=grid_spec,
        out_shape=jax.ShapeDtypeStruct((N_SLOTS * ROW_TILES, LANES), U32),
        compiler_params=pltpu.CompilerParams(dimension_semantics=("arbitrary",),
                                             vmem_limit_bytes=VMEM_LIMIT),
        name="experts",
    )(blk_e, n_used, next_e, blk_valid, xs2, w_gu, b_gu3, w_down, b_down3)


def _final_kernel(yg_ref, gate_ref, x1_ref, mod_ref, gpost_ref, *maybe_alias_and_out):
    o_ref = maybe_alias_and_out[-1]
    gates = gate_ref[...]
    y = None
    for kk in range(TOP_K):
        part = _load_row_tiles(yg_ref.at[kk], TOK_TILE) * gates[:, kk:kk + 1]
        y = part if y is None else y + part
    gate_f = mod_ref[0][5:6]
    o_ref[...] = x1_ref[...] + gate_f * (_rms(y) * gpost_ref[...])


def _final(part, yg3, gates, x1, mod3, g_post, out_so_far):
    tiles = SEQ // TOK_TILE
    tok = lambda w: pl.BlockSpec((TOK_TILE, w), lambda i: (part * tiles + i, 0))
    in_specs = [pl.BlockSpec((TOP_K, TOK_TILE * ROW_TILES, LANES), lambda i: (0, i, 0)),
                tok(TOP_K), tok(D_MODEL),
                pl.BlockSpec((1, 6, D_MODEL), lambda i: (part, 0, 0)),
                pl.BlockSpec((1, D_MODEL), lambda i: (0, 0))]
    args = [yg3, gates, x1, mod3, g_post]
    aliases = {}
    if out_so_far is not None:
        in_specs.append(pl.BlockSpec(memory_space=pl.ANY))
        args.append(out_so_far)
        aliases = {len(args) - 1: 0}
    return pl.pallas_call(
        _final_kernel,
        grid=(tiles,),
        in_specs=in_specs,
        out_specs=tok(D_MODEL),
        out_shape=jax.ShapeDtypeStruct((N_TOK, D_MODEL), F32),
        input_output_aliases=aliases,
        compiler_params=pltpu.CompilerParams(vmem_limit_bytes=VMEM_LIMIT),
        name="combine_final",
    )(*args)


def _routing_tables(idx, rank, counts):
    counts = counts.reshape(N_EXPERTS)
    experts = jnp.arange(N_EXPERTS, dtype=I32)
    padded = ((counts + MOE_BLOCK - 1) // MOE_BLOCK) * MOE_BLOCK
    pend = jnp.cumsum(padded)
    pstart = pend - padded
    dest = jnp.sum(jnp.where(idx[..., None] == experts, pstart, 0), axis=-1) + rank
    dest_win = dest.reshape(TOP_K, N_TOK // SC_WINDOW, SC_WINDOW).transpose(1, 0, 2)
    n_used = (pend[-1] // MOE_BLOCK).astype(I32).reshape(1)
    blk_start = jnp.arange(N_BLOCKS, dtype=I32) * MOE_BLOCK
    blk_e = jnp.sum(blk_start[:, None] >= pend[None, :], axis=1).astype(I32)
    last_e = jnp.max(jnp.where(counts > 0, experts, 0))
    blk_e = jnp.minimum(blk_e, last_e)
    later = (experts[None, :] > experts[:, None]) & (counts[None, :] > 0)
    next_nonempty = jnp.min(jnp.where(later, experts[None, :], N_EXPERTS), axis=1)
    next_nonempty = jnp.where(next_nonempty == N_EXPERTS, -1, next_nonempty).astype(I32)
    of_block = lambda table: jnp.sum(jnp.where(blk_e[:, None] == experts, table, 0), axis=1).astype(I32)
    next_e = of_block(next_nonempty)
    blk_valid = jnp.clip(of_block(counts) - (blk_start - of_block(pstart)), 0, MOE_BLOCK)
    blk_valid = jnp.where(blk_start < pend[-1], blk_valid, 0).astype(I32)
    return dest_win, blk_e, n_used, next_e, blk_valid


def kernel(x, c, w_ada, b_ada, g_pre_mix, g_post_mix, w_in, attn_norm_w, hgrn_lb, hgrn_norm_w,
           w_out, g_pre_ffn, g_post_ffn, w_router, b_router, w_gu, b_gu, w_down, b_down):
    c_pad = jnp.pad(c, ((0, SUBLANES - BATCH), (0, 0)))
    mod = _ada_mod(c_pad, w_ada[0], b_ada)
    mod3 = mod[:BATCH].reshape(BATCH, 6, D_MODEL)

    x2 = x.reshape(N_TOK, D_MODEL)
    q4, k4, v4, q16, k16, v16, qr, fr, ir, gr = _inproj(x2, mod3, g_pre_mix, w_in[0].astype(BF16))
    nat = lambda t: t.reshape(BATCH, SEQ, D_REC)
    ya = _attention(q4, k4, v4, q16, k16, v16, attn_norm_w)
    yr = _hgrn(nat(qr), nat(fr), nat(ir), nat(gr), hgrn_lb, hgrn_norm_w)

    x1, h2, idx, gates, rank, counts = _mid(
        ya.reshape(N_TOK, D_ATTN), yr.reshape(N_TOK, D_REC), x2, mod3, g_post_mix, g_pre_ffn,
        w_out[0].astype(BF16), w_router[0], b_router)

    dest_win, blk_e, n_used, next_e, blk_valid = _routing_tables(idx, rank, counts)
    xs = _sc_dispatch(h2.reshape(N_TOK, ROW_TILES, LANES), dest_win)
    ys = _experts(blk_e, n_used, next_e, blk_valid, xs.reshape(N_SLOTS * ROW_TILES, LANES),
                  w_gu[0], b_gu[0].reshape(N_EXPERTS, 1, 2 * D_FF),
                  w_down[0], b_down[0].reshape(N_EXPERTS, 1, D_MODEL))
    ys3 = ys.reshape(N_SLOTS, ROW_TILES, LANES)
    win_per_seq = SEQ // SC_WINDOW
    out = None
    for b in range(BATCH):
        yg = _sc_collect(ys3, dest_win[b * win_per_seq:(b + 1) * win_per_seq])
        out = _final(b, yg.reshape(TOP_K, SEQ * ROW_TILES, LANES), gates, x1, mod3, g_post_ffn, out)
    return out.reshape(BATCH, SEQ, D_MODEL)
```

```python
import functools
import math

import jax
import jax.numpy as jnp
from jax import lax
from jax.experimental import pallas as pl
from jax.experimental.pallas import tpu as pltpu
from jax.experimental.pallas import tpu_sc as plsc

F32 = jnp.float32
BF16 = jnp.bfloat16
I32 = jnp.int32
U32 = jnp.uint32
HIGHEST = lax.Precision.HIGHEST

D_MODEL = 1024
BATCH = 4
SEQ = 4096
N_TOK = BATCH * SEQ
D_ATTN = 512
HEAD_DIM_A = 64
ATT_BLOCK = 128
DIL_MID = 4
DIL_MAX = 16
SUB_MID = SEQ // DIL_MID
SUB_MAX = SEQ // DIL_MAX
D_REC = 512
HGRN_CHUNK = 32
HGRN_SUPER = 256
N_EXPERTS = 32
TOP_K = 4
D_FF = 1024
SWIGLU_LIMIT = 7.0
SWIGLU_ALPHA = 1.702
EPS = 1e-6
NEG_BIG = -1e30
Q_SCALE = HEAD_DIM_A ** -0.5 * math.log2(math.e)

LANES = 128
SUBLANES = 8
ROW_TILES = D_MODEL // 2 // LANES

TOK_TILE = 512
MOE_BLOCK = 512
N_SLOTS = N_TOK * TOP_K + N_EXPERTS * MOE_BLOCK
N_BLOCKS = N_SLOTS // MOE_BLOCK
SC_WINDOW = 64
VMEM_LIMIT = 56 * 1024 * 1024


def _sigmoid(x):
    return 1.0 / (1.0 + jnp.exp(-x))


def _dot(a, b):
    return jnp.dot(a, b, preferred_element_type=F32)


def _dot_nt(a, b):
    return lax.dot_general(a, b, (((1,), (1,)), ((), ())), preferred_element_type=F32)


def _split_bf16(x):
    hi = x.astype(BF16)
    return hi, (x - hi.astype(F32)).astype(BF16)


def _rms(x):
    return x * lax.rsqrt(jnp.mean(x * x, axis=-1, keepdims=True) + EPS)


def _ada_kernel(c_ref, w_ref, b_ref, o_ref):
    c = c_ref[...]
    cond = c * _sigmoid(c)
    o_ref[...] = jnp.dot(cond, w_ref[...], precision=HIGHEST,
                         preferred_element_type=F32) + b_ref[...]


def _ada_mod(c_pad, w_ada, b_ada):
    n = w_ada.shape[1]
    tn = 1536
    return pl.pallas_call(
        _ada_kernel,
        grid=(n // tn,),
        in_specs=[pl.BlockSpec((SUBLANES, D_MODEL), lambda j: (0, 0)),
                  pl.BlockSpec((D_MODEL, tn), lambda j: (0, j)),
                  pl.BlockSpec((1, tn), lambda j: (0, j))],
        out_specs=pl.BlockSpec((SUBLANES, tn), lambda j: (0, j)),
        out_shape=jax.ShapeDtypeStruct((SUBLANES, n), F32),
        compiler_params=pltpu.CompilerParams(vmem_limit_bytes=VMEM_LIMIT),
        name="ada_mod",
    )(c_pad, w_ada, b_ada)


def _inproj_kernel(x_ref, mod_ref, g_ref, w_ref, q4, k4, v4, q16, k16, v16, qr, fr, ir, gr,
                   stage_nat, stage_mid):
    mod = mod_ref[0]
    shift, scale = mod[0:1], mod[1:2]
    h = _rms(x_ref[...]) * g_ref[...] * (1.0 + scale) + shift
    hb = h.astype(BF16)
    slabs = D_ATTN // LANES
    rows_mid = TOK_TILE // DIL_MID
    rows_max = TOK_TILE // DIL_MAX

    def proj(j):
        return _dot(hb, w_ref[:, j * D_ATTN:(j + 1) * D_ATTN])

    for j, (o_mid, o_max) in enumerate(((q4, q16), (k4, k16), (v4, v16))):
        r = proj(j)
        if j == 0:
            r = r * Q_SCALE
        for cs in range(slabs):
            stage_nat[cs] = r[:, cs * LANES:(cs + 1) * LANES]
        for cs in range(slabs):
            lanes = slice(cs * LANES, (cs + 1) * LANES)
            for sub in range(DIL_MID):
                piece = stage_nat[cs, pl.ds(sub, rows_mid, stride=DIL_MID), :]
                o_mid[sub, :, lanes] = piece.astype(BF16)
                stage_mid[cs, sub] = piece
            for sub in range(DIL_MAX):
                piece = stage_mid[cs, sub % DIL_MID, pl.ds(sub // DIL_MID, rows_max, stride=DIL_MID), :]
                o_max[sub, :, lanes] = piece.astype(BF16)
    for j, o_ref in enumerate((qr, fr, ir, gr)):
        o_ref[...] = proj(3 + j)


def _inproj(x2, mod3, g_pre, w_in_bf16):
    tiles_per_seq = SEQ // TOK_TILE
    rows_mid = TOK_TILE // DIL_MID
    rows_max = TOK_TILE // DIL_MAX
    mid = pl.BlockSpec((None, DIL_MID, rows_mid, D_ATTN),
                       lambda i: (i // tiles_per_seq, 0, i % tiles_per_seq, 0))
    mx = pl.BlockSpec((None, DIL_MAX, rows_max, D_ATTN),
                      lambda i: (i // tiles_per_seq, 0, i % tiles_per_seq, 0))
    nat = pl.BlockSpec((TOK_TILE, D_REC), lambda i: (i, 0))
    mid_shape = jax.ShapeDtypeStruct((BATCH, DIL_MID, SUB_MID, D_ATTN), BF16)
    mx_shape = jax.ShapeDtypeStruct((BATCH, DIL_MAX, SUB_MAX, D_ATTN), BF16)
    nat_shape = jax.ShapeDtypeStruct((N_TOK, D_REC), F32)
    return pl.pallas_call(
        _inproj_kernel,
        grid=(N_TOK // TOK_TILE,),
        in_specs=[pl.BlockSpec((TOK_TILE, D_MODEL), lambda i: (i, 0)),
                  pl.BlockSpec((1, 6, D_MODEL), lambda i: (i // tiles_per_seq, 0, 0)),
                  pl.BlockSpec((1, D_MODEL), lambda i: (0, 0)),
                  pl.BlockSpec(w_in_bf16.shape, lambda i: (0, 0))],
        out_specs=[mid, mid, mid, mx, mx, mx, nat, nat, nat, nat],
        out_shape=[mid_shape] * 3 + [mx_shape] * 3 + [nat_shape] * 4,
        scratch_shapes=[pltpu.VMEM((D_ATTN // LANES, TOK_TILE, LANES), F32),
                        pltpu.VMEM((D_ATTN // LANES, DIL_MID, rows_mid, LANES), F32)],
        compiler_params=pltpu.CompilerParams(vmem_limit_bytes=VMEM_LIMIT),
        name="inproj",
    )(x2, mod3, g_pre, w_in_bf16)


def _attn_kernel(q4_ref, k4_ref, v4_ref, q16_ref, k16_ref, v16_ref, nw_ref, o_ref,
                 sel_scr, keep_scr, o_scr, m_scr, l_scr):
    bw = ATT_BLOCK
    lane = lax.broadcasted_iota(I32, (bw, LANES), 1)
    head0 = lane < HEAD_DIM_A
    head_masks = (jnp.where(head0, 1.0, 0.0).astype(BF16), jnp.where(head0, 0.0, 1.0).astype(BF16))

    rr = lax.broadcasted_iota(I32, (bw, bw), 0)
    cc = lax.broadcasted_iota(I32, (bw, bw), 1)
    piece = bw // DIL_MID
    rp = DIL_MID * (rr % piece) + rr // piece
    cp = DIL_MID * (cc % piece) + cc // piece
    for var, (r, c) in enumerate(((rr, cc), (rr, cc), (rp, cp), (rp, cp))):
        first = var % 2 == 1
        left = (c < r) if first else (c >= r)
        right = jnp.zeros_like(left) if first else jnp.logical_not(left)
        sel_scr[var] = jnp.where(left, 1.0, 0.0)
        keep_scr[var, 0] = jnp.where(left, 1.0, 0.0).astype(BF16)
        keep_scr[var, 1] = jnp.where(right, 1.0, 0.0).astype(BF16)

    def attend_group(loaded):
        scores = [_dot_nt(q * hm, k) for (q, k, _, _, _) in loaded for hm in head_masks]
        probs = []
        for n, s in enumerate(scores):
            _, _, _, var, right_bias = loaded[n // 2]
            folded = jnp.where(sel_scr[var] > 0.5, s[:, 0:bw], s[:, bw:] + right_bias)
            m = jnp.max(folded, axis=-1, keepdims=True)
            p = jnp.exp2(folded - m)
            pb = p.astype(BF16)
            spread = jnp.concatenate([pb * keep_scr[var, 0], pb * keep_scr[var, 1]], axis=1)
            probs.append((spread, jnp.sum(p, axis=-1, keepdims=True), m))
        pvs = [_dot(p, loaded[n // 2][2]) for n, (p, _, _) in enumerate(probs)]
        results = []
        for u in range(len(loaded)):
            (_, l0, m0), (_, l1, m1) = probs[2 * u], probs[2 * u + 1]
            results.append((jnp.where(head0, pvs[2 * u], pvs[2 * u + 1]),
                            jnp.where(head0, jnp.broadcast_to(m0, (bw, LANES)),
                                      jnp.broadcast_to(m1, (bw, LANES))),
                            jnp.where(head0, jnp.broadcast_to(l0, (bw, LANES)),
                                      jnp.broadcast_to(l1, (bw, LANES)))))
        return results

    group = 8

    def first_block_bias(blk):
        return jnp.where(blk == 0, NEG_BIG, 0.0)

    def group_d1(g, carry):
        loaded, starts = [], []
        for i in range(group):
            blk = g * group + i
            qs = pl.multiple_of(blk * piece, piece)
            ks = pl.multiple_of(jnp.maximum(blk - 1, 0) * piece, piece)
            q = jnp.concatenate([q4_ref[r, pl.ds(qs, piece), :] for r in range(DIL_MID)], axis=0)
            k = jnp.concatenate([k4_ref[r, pl.ds(ks + half * piece, piece), :]
                                 for half in range(2) for r in range(DIL_MID)], axis=0)
            v = jnp.concatenate([v4_ref[r, pl.ds(ks + half * piece, piece), :]
                                 for half in range(2) for r in range(DIL_MID)], axis=0)
            loaded.append((q, k, v, jnp.where(blk == 0, 3, 2), first_block_bias(blk)))
            starts.append(qs)
        for qs, parts in zip(starts, attend_group(loaded)):
            for r in range(DIL_MID):
                for scr, val in zip((o_scr, m_scr, l_scr), parts):
                    scr[0, r, pl.ds(qs, piece), :] = val[r * piece:(r + 1) * piece]
        return carry

    def group_d4(g, carry):
        loaded, dsts = [], []
        for i in range(group):
            r, blk = i % DIL_MID, g * (group // DIL_MID) + i // DIL_MID
            qs = pl.multiple_of(blk * bw, bw)
            ks = pl.multiple_of(jnp.maximum(blk - 1, 0) * bw, bw)
            loaded.append((q4_ref[r, pl.ds(qs, bw), :], k4_ref[r, pl.ds(ks, 2 * bw), :],
                           v4_ref[r, pl.ds(ks, 2 * bw), :], jnp.where(blk == 0, 1, 0),
                           first_block_bias(blk)))
            dsts.append((r, qs))
        for (r, qs), parts in zip(dsts, attend_group(loaded)):
            for scr, val in zip((o_scr, m_scr, l_scr), parts):
                scr[1, r, pl.ds(qs, bw), :] = val
        return carry

    def group_d16(g, carry):
        loaded, dsts = [], []
        for i in range(group):
            r, blk = g * (group // 2) + i // 2, i % 2
            loaded.append((q16_ref[r, blk * bw:(blk + 1) * bw, :], k16_ref[r], v16_ref[r],
                           1 - blk, NEG_BIG if blk == 0 else 0.0))
            dsts.append((r % DIL_MID, pl.ds(blk * (bw * DIL_MID) + r // DIL_MID, bw, stride=DIL_MID)))
        for (sub, dst), parts in zip(dsts, attend_group(loaded)):
            for scr, val in zip((o_scr, m_scr, l_scr), parts):
                scr[2, sub, dst, :] = val
        return carry

    units = SEQ // bw
    lax.fori_loop(0, units // group, group_d1, 0)
    lax.fori_loop(0, units // group, group_d4, 0)
    lax.fori_loop(0, units // group, group_d16, 0)

    rows = 256
    hi = lax.broadcasted_iota(I32, (LANES, LANES), 0) // HEAD_DIM_A
    hj = lax.broadcasted_iota(I32, (LANES, LANES), 1) // HEAD_DIM_A
    head_sum = jnp.where(hi == hj, 1.0, 0.0).astype(BF16)
    nw = nw_ref[...]

    def head_sums(x):
        x_hi, x_lo = _split_bf16(x)
        return _dot(x_hi, head_sum) + _dot(x_lo, head_sum)

    def merge(t, carry):
        r = t // (SUB_MID // rows)
        start = pl.multiple_of((t % (SUB_MID // rows)) * rows, rows)
        sl = pl.ds(start, rows)
        own_score = head_sums(q4_ref[r, sl, :].astype(F32) * k4_ref[r, sl, :].astype(F32))
        ms = [m_scr[n, r, sl, :] for n in range(3)]
        mx = jnp.maximum(jnp.maximum(jnp.maximum(ms[0], ms[1]), ms[2]), own_score)
        ws = [jnp.exp2(m - mx) for m in ms]
        w_own = float(len(ms)) * jnp.exp2(own_score - mx)
        num = (ws[0] * o_scr[0, r, sl, :] + ws[1] * o_scr[1, r, sl, :] + ws[2] * o_scr[2, r, sl, :]
               + w_own * v4_ref[r, sl, :].astype(F32))
        den = ws[0] * l_scr[0, r, sl, :] + ws[1] * l_scr[1, r, sl, :] + ws[2] * l_scr[2, r, sl, :] + w_own
        o = num / den
        mean_sq = head_sums(o * o) * (1.0 / HEAD_DIM_A)
        o_ref[pl.ds(start * DIL_MID + r, rows, stride=DIL_MID), :] = o * lax.rsqrt(mean_sq + EPS) * nw
        return carry

    lax.fori_loop(0, SEQ // rows, merge, 0, unroll=2)


def _attention(q4, k4, v4, q16, k16, v16, attn_norm_w):
    hp = D_ATTN // LANES
    mid = pl.BlockSpec((None, DIL_MID, SUB_MID, LANES), lambda b, h: (b, 0, 0, h))
    mx = pl.BlockSpec((None, DIL_MAX, SUB_MAX, LANES), lambda b, h: (b, 0, 0, h))
    scr = pltpu.VMEM((3, DIL_MID, SUB_MID, LANES), F32)
    return pl.pallas_call(
        _attn_kernel,
        grid=(BATCH, hp),
        in_specs=[mid, mid, mid, mx, mx, mx, pl.BlockSpec((1, LANES), lambda b, h: (0, h))],
        out_specs=pl.BlockSpec((None, SEQ, LANES), lambda b, h: (b, 0, h)),
        out_shape=jax.ShapeDtypeStruct((BATCH, SEQ, D_ATTN), F32),
        scratch_shapes=[pltpu.VMEM((4, ATT_BLOCK, ATT_BLOCK), F32),
                        pltpu.VMEM((4, 2, ATT_BLOCK, ATT_BLOCK), BF16), scr, scr, scr],
        compiler_params=pltpu.CompilerParams(vmem_limit_bytes=VMEM_LIMIT),
        name="dilated_attn",
    )(q4, k4, v4, q16, k16, v16, attn_norm_w)


def _hgrn_kernel(qr_ref, fr_ref, ir_ref, gr_ref, lb_ref, nw_ref, o_ref,
                 qe_scr, oi_scr, delta_scr, dec_scr, st_scr):
    sup, c = HGRN_SUPER, HGRN_CHUNK
    nch = sup // c
    n_sup = SEQ // sup
    lbp = lb_ref[...]
    lmx = jnp.max(lbp, axis=0, keepdims=True)
    ex = jnp.exp(lbp - lmx)
    lb = ex[0:1] / (ex[0:1] + ex[1:2])
    nw = nw_ref[...]

    ri = lax.broadcasted_iota(I32, (sup, sup), 0)
    ci = lax.broadcasted_iota(I32, (sup, sup), 1)
    same_chunk = (ri // c) == (ci // c)
    causal = same_chunk & (ci <= ri)
    cum_op = jnp.where(causal, 1.0, 0.0).astype(BF16)

    def chunk_rows(kd, ch):
        parts = []
        if ch > 0:
            parts.append(jnp.zeros((ch * c, LANES), BF16))
        parts.append(kd[ch * c:(ch + 1) * c])
        if ch < nch - 1:
            parts.append(jnp.zeros(((nch - 1 - ch) * c, LANES), BF16))
        return jnp.concatenate(parts, axis=0)

    group = 2

    def independent(g, carry):
        ts = [g * group + i for i in range(group)]
        sls = [pl.ds(pl.multiple_of(t * sup, sup), sup) for t in ts]
        pre = []
        for sl in sls:
            f = lb + (1.0 - lb) * _sigmoid(fr_ref[sl, :])
            logf_hi, logf_lo = _split_bf16(jnp.log(f))
            pre.append((1.0 - f, _dot(cum_op, logf_hi) + _dot(cum_op, logf_lo)))
        mid = []
        for t, sl, (kk, b) in zip(ts, sls, pre):
            b_last = jnp.concatenate(
                [jnp.broadcast_to(b[(ch + 1) * c - 1:(ch + 1) * c], (c, LANES)) for ch in range(nch)],
                axis=0)
            q = qr_ref[sl, :]
            qeb = (q * _sigmoid(q) * jnp.exp(b)).astype(BF16)
            ke = (kk * jnp.exp(-b)).astype(BF16)
            kd = (kk * jnp.exp(b_last - b)).astype(BF16)
            qe_scr[sl, :] = qeb
            dec_rows = jnp.concatenate([b_last[ch * c:ch * c + 1] for ch in range(nch)], axis=0)
            dec_scr[pl.ds(pl.multiple_of(t * nch, nch), nch), :] = jnp.exp(dec_rows)
            v = ir_ref[sl, :]
            vt = v.T.astype(BF16)
            scores = _dot_nt(qeb, ke)
            for pair in range(nch // 2):
                rhs = jnp.concatenate([chunk_rows(kd, 2 * pair), chunk_rows(kd, 2 * pair + 1)], axis=1)
                d2 = _dot(vt, rhs)
                delta_scr[t * nch + 2 * pair] = d2[:, 0:LANES]
                delta_scr[t * nch + 2 * pair + 1] = d2[:, LANES:]
            mid.append((scores, v.astype(BF16)))
        for sl, (scores, vb) in zip(sls, mid):
            a = jnp.where(causal, scores, 0.0)
            oi_scr[sl, :] = _dot(a.astype(BF16), vb)
        return carry

    lax.fori_loop(0, n_sup // group, independent, 0)

    def recur(ch, st):
        st_scr[ch] = st.astype(BF16)
        return st * dec_scr[pl.ds(ch, 1), :] + delta_scr[ch]

    lax.fori_loop(0, SEQ // c, recur, jnp.zeros((LANES, LANES), F32), unroll=8)

    def finish(t, carry):
        sl = pl.ds(pl.multiple_of(t * sup, sup), sup)
        parts = [_dot_nt(qe_scr[pl.ds(pl.multiple_of(t * sup + ch * c, c), c), :], st_scr[t * nch + ch])
                 for ch in range(nch)]
        o = oi_scr[sl, :] + jnp.concatenate(parts, axis=0)
        g = gr_ref[sl, :]
        o_ref[sl, :] = (_rms(o) * nw * (g * _sigmoid(g))).astype(o_ref.dtype)
        return carry

    lax.fori_loop(0, n_sup, finish, 0, unroll=4)


def _hgrn(qr, fr, ir, gr, hgrn_lb, hgrn_norm_w):
    nh = D_REC // LANES
    n_chunks = SEQ // HGRN_CHUNK
    blk = pl.BlockSpec((None, SEQ, LANES), lambda b, h: (b, 0, h))
    return pl.pallas_call(
        _hgrn_kernel,
        grid=(BATCH, nh),
        in_specs=[blk, blk, blk, blk,
                  pl.BlockSpec((2, LANES), lambda b, h: (0, h)),
                  pl.BlockSpec((1, LANES), lambda b, h: (0, h))],
        out_specs=blk,
        out_shape=jax.ShapeDtypeStruct((BATCH, SEQ, D_REC), BF16),
        scratch_shapes=[pltpu.VMEM((SEQ, LANES), BF16),
                        pltpu.VMEM((SEQ, LANES), F32),
                        pltpu.VMEM((n_chunks, LANES, LANES), F32),
                        pltpu.VMEM((n_chunks, LANES), F32),
                        pltpu.VMEM((n_chunks, LANES, LANES), BF16)],
        compiler_params=pltpu.CompilerParams(vmem_limit_bytes=VMEM_LIMIT),
        name="hgrn2",
    )(qr, fr, ir, gr, hgrn_lb, hgrn_norm_w)


def _tile_batch(i):
    return i // (SEQ // TOK_TILE)


def _store_row_tiles(ref, val):
    rows, half = val.shape[0], D_MODEL // 2
    lo = lax.bitcast_convert_type(val[:, :half].astype(BF16).astype(F32), U32)
    hi = lax.bitcast_convert_type(val[:, half:].astype(BF16).astype(F32), U32)
    words = lax.shift_right_logical(lo, jnp.uint32(16)) | (hi & jnp.uint32(0xFFFF0000))
    for j in range(ROW_TILES):
        ref[pl.ds(j, rows, stride=ROW_TILES), :] = words[:, j * LANES:(j + 1) * LANES]


def _load_row_tiles(ref, rows):
    words = jnp.concatenate(
        [ref[pl.ds(j, rows, stride=ROW_TILES), :] for j in range(ROW_TILES)], axis=1)
    lo = lax.bitcast_convert_type(lax.shift_left(words, jnp.uint32(16)), F32)
    hi = lax.bitcast_convert_type(words & jnp.uint32(0xFFFF0000), F32)
    return jnp.concatenate([lo, hi], axis=1)


def _mid_kernel(ya_ref, yr_ref, x_ref, mod_ref, gpost_ref, gpre_ref, wo_ref, wr_ref, br_ref,
                x1_ref, h2_ref, idx_ref, gate_ref, rank_ref, cnt_ref, carry_ref):
    i = pl.program_id(0)

    @pl.when(i == 0)
    def _():
        carry_ref[...] = jnp.zeros_like(carry_ref)

    mod = mod_ref[0]
    gate_m, shift_f, scale_f = mod[2:3], mod[3:4], mod[4:5]
    y = _dot(ya_ref[...].astype(BF16), wo_ref[0:D_ATTN, :]) + _dot(yr_ref[...], wo_ref[D_ATTN:, :])
    x1 = x_ref[...] + gate_m * (_rms(y) * gpost_ref[...])
    x1_ref[...] = x1
    h2 = _rms(x1) * gpre_ref[...] * (1.0 + scale_f) + shift_f
    _store_row_tiles(h2_ref, h2)

    tm = h2.shape[0]
    h2_hi, h2_lo = _split_bf16(h2)
    wr_hi, wr_lo = _split_bf16(wr_ref[...])
    parts = jnp.concatenate([_dot(h2_hi, jnp.concatenate([wr_hi, wr_lo], axis=1)),
                             _dot(h2_lo, wr_hi),
                             jnp.broadcast_to(br_ref[...], (tm, N_EXPERTS))], axis=1).T
    work = sum(parts[g * N_EXPERTS:(g + 1) * N_EXPERTS] for g in range(LANES // N_EXPERTS))
    eidx = lax.broadcasted_iota(I32, (N_EXPERTS, tm), 0).astype(F32)
    vals, idxs = [], []
    onehot = jnp.zeros((N_EXPERTS, tm), F32)
    for _ in range(TOP_K):
        m = jnp.max(work, axis=0, keepdims=True)
        sel = jnp.min(jnp.where(work == m, eidx, float(N_EXPERTS)), axis=0, keepdims=True)
        hit = eidx == sel
        work = jnp.where(hit, -jnp.inf, work)
        onehot = jnp.where(hit, 1.0, onehot)
        vals.append(m)
        idxs.append(sel)
    ex = [jnp.exp(vv - vals[0]) for vv in vals]
    inv_den = 1.0 / (ex[0] + ex[1] + ex[2] + ex[3])

    ri = lax.broadcasted_iota(I32, (tm, tm), 0)
    ci = lax.broadcasted_iota(I32, (tm, tm), 1)
    strict_upper = jnp.where(ri < ci, 1.0, 0.0).astype(BF16)
    before = _dot(onehot.astype(BF16), strict_upper) + carry_ref[...]
    ranks = [jnp.sum(jnp.where(eidx == idxs[kk], before, 0.0), axis=0, keepdims=True)
             for kk in range(TOP_K)]
    idx_ref[...] = jnp.concatenate(idxs, axis=0).astype(I32)
    rank_ref[...] = jnp.concatenate(ranks, axis=0).astype(I32)
    gates_t = jnp.concatenate([e * inv_den for e in ex]
                              + [jnp.zeros((LANES - TOP_K, tm), F32)], axis=0)
    gate_ref[...] = gates_t.T[:, 0:TOP_K]
    total = carry_ref[...] + jnp.sum(onehot, axis=1, keepdims=True)
    carry_ref[...] = total
    cnt_ref[...] = total.astype(I32)


def _mid(ya, yr, x2, mod3, g_post, g_pre, w_out_bf16, w_router, b_router):
    tok = lambda w: pl.BlockSpec((TOK_TILE, w), lambda i: (i, 0))
    const = lambda s: pl.BlockSpec(s, lambda i: (0,) * len(s))
    lanes_tok = pl.BlockSpec((TOP_K, TOK_TILE), lambda i: (0, i))
    return pl.pallas_call(
        _mid_kernel,
        grid=(N_TOK // TOK_TILE,),
        in_specs=[tok(D_ATTN), tok(D_REC), tok(D_MODEL),
                  pl.BlockSpec((1, 6, D_MODEL), lambda i: (_tile_batch(i), 0, 0)),
                  const((1, D_MODEL)), const((1, D_MODEL)),
                  const((D_MODEL, D_MODEL)), const((D_MODEL, N_EXPERTS)), const((1, N_EXPERTS))],
        out_specs=[tok(D_MODEL),
                   pl.BlockSpec((TOK_TILE * ROW_TILES, LANES), lambda i: (i, 0)),
                   lanes_tok, tok(TOP_K), lanes_tok, const((N_EXPERTS, 1))],
        out_shape=[jax.ShapeDtypeStruct((N_TOK, D_MODEL), F32),
                   jax.ShapeDtypeStruct((N_TOK * ROW_TILES, LANES), U32),
                   jax.ShapeDtypeStruct((TOP_K, N_TOK), I32),
                   jax.ShapeDtypeStruct((N_TOK, TOP_K), F32),
                   jax.ShapeDtypeStruct((TOP_K, N_TOK), I32),
                   jax.ShapeDtypeStruct((N_EXPERTS, 1), I32)],
        scratch_shapes=[pltpu.VMEM((N_EXPERTS, 1), F32)],
        compiler_params=pltpu.CompilerParams(dimension_semantics=("arbitrary",),
                                             vmem_limit_bytes=VMEM_LIMIT),
        name="outproj_router",
    )(ya, yr, x2, mod3, g_post, g_pre, w_out_bf16, w_router, b_router)


def _sc_mesh():
    return plsc.VectorSubcoreMesh(core_axis_name="c", subcore_axis_name="s")


def _sc_worker_count():
    info = plsc.get_sparse_core_info()
    return info.num_cores, info.num_cores * info.num_subcores


def _sc_dispatch(h_rows, dest_win):
    n_cores, n_workers = _sc_worker_count()
    n_win = N_TOK // SC_WINDOW
    per_worker = n_win // n_workers

    @functools.partial(
        pl.kernel, mesh=_sc_mesh(),
        out_type=jax.ShapeDtypeStruct((N_SLOTS, ROW_TILES, LANES), U32),
        scratch_types=[pltpu.VMEM((TOP_K, SC_WINDOW), I32),
                       pltpu.VMEM((SC_WINDOW, ROW_TILES, LANES), U32),
                       pltpu.SemaphoreType.DMA],
        name="sc_dispatch")
    def run(h_hbm, dest_hbm, xs_hbm, idx_v, rows_v, sem):
        wid = lax.axis_index("s") * n_cores + lax.axis_index("c")

        @pl.loop(0, per_worker)
        def _(j):
            win = wid * per_worker + j
            pltpu.sync_copy(dest_hbm.at[win], idx_v)
            pltpu.sync_copy(h_hbm.at[pl.ds(win * SC_WINDOW, SC_WINDOW)], rows_v)
            copies = [pltpu.async_copy(rows_v, xs_hbm.at[idx_v.at[kk]], sem)
                      for kk in range(TOP_K)]
            for cp in copies:
                cp.wait()

    return run(h_rows, dest_win)


def _sc_collect(y_rows, dest_win):
    n_cores, n_workers = _sc_worker_count()
    n_win = dest_win.shape[0]
    per_worker = n_win // n_workers

    @functools.partial(
        pl.kernel, mesh=_sc_mesh(),
        out_type=jax.ShapeDtypeStruct((TOP_K, n_win * SC_WINDOW, ROW_TILES, LANES), U32),
        scratch_types=[pltpu.VMEM((TOP_K, SC_WINDOW), I32),
                       pltpu.VMEM((SC_WINDOW, ROW_TILES, LANES), U32),
                       pltpu.SemaphoreType.DMA],
        name="sc_collect")
    def run(y_hbm, dest_hbm, yg_hbm, idx_v, rows_v, sem):
        wid = lax.axis_index("s") * n_cores + lax.axis_index("c")

        @pl.loop(0, per_worker)
        def _(j):
            win = wid * per_worker + j
            pltpu.sync_copy(dest_hbm.at[win], idx_v)
            for kk in range(TOP_K):
                pltpu.async_copy(y_hbm.at[idx_v.at[kk]], rows_v, sem).wait()
                pltpu.sync_copy(rows_v, yg_hbm.at[kk, pl.ds(win * SC_WINDOW, SC_WINDOW)])

    return run(y_rows, dest_win)


def _expert_kernel(be_ref, nu_ref, nx_ref, nv_ref, x_ref, wgu_hbm, bgu_ref, wd_hbm, bd_ref, y_ref,
                   wgu32, wd32, wgu16, wd16, sems):
    i = pl.program_id(0)
    e = be_ref[i]
    prev = be_ref[jnp.maximum(i - 1, 0)]

    def weight_copies(ex):
        return (pltpu.make_async_copy(wgu_hbm.at[ex], wgu32, sems.at[0]),
                pltpu.make_async_copy(wd_hbm.at[ex], wd32, sems.at[1]))

    @pl.when(i == 0)
    def _():
        for cp in weight_copies(e):
            cp.start()

    @pl.when((i == 0) | (e != prev))
    def _():
        for cp in weight_copies(e):
            cp.wait()
        rows = 128

        def cast(r, carry):
            sl = pl.ds(pl.multiple_of(r * rows, rows), rows)
            wgu16[sl, :] = wgu32[sl, :].astype(BF16)
            wd16[sl, :] = wd32[sl, :].astype(BF16)
            return carry

        lax.fori_loop(0, D_MODEL // rows, cast, 0)
        nxt = nx_ref[i]

        @pl.when(nxt >= 0)
        def _():
            for cp in weight_copies(nxt):
                cp.start()

    def run_rows(rows):
        x = _load_row_tiles(x_ref, rows).astype(BF16)
        bgu = bgu_ref[0]
        glu = _dot(x, wgu16[:, 0:D_FF]) + bgu[:, 0:D_FF]
        lin = _dot(x, wgu16[:, D_FF:]) + bgu[:, D_FF:]
        glu = jnp.minimum(glu, SWIGLU_LIMIT)
        lin = jnp.clip(lin, -SWIGLU_LIMIT, SWIGLU_LIMIT)
        act = glu * _sigmoid(SWIGLU_ALPHA * glu) * (lin + 1.0)
        y = _dot(act.astype(BF16), wd16[...]) + bd_ref[0]
        _store_row_tiles(y_ref, y)

    valid = nv_ref[i]
    quarter = MOE_BLOCK // 4
    for rows in range(quarter, MOE_BLOCK + 1, quarter):
        pl.when((valid > rows - quarter) & (valid <= rows))(functools.partial(run_rows, rows))


def _experts(blk_e, n_used, next_e, blk_valid, xs2, w_gu, b_gu3, w_down, b_down3):
    row_blk = pl.BlockSpec((MOE_BLOCK * ROW_TILES, LANES),
                           lambda i, be, nu, nx, nv: (jnp.minimum(i, nu[0] - 1), 0))
    grid_spec = pltpu.PrefetchScalarGridSpec(
        num_scalar_prefetch=4,
        grid=(N_BLOCKS,),
        in_specs=[row_blk,
                  pl.BlockSpec(memory_space=pl.ANY),
                  pl.BlockSpec((1, 1, 2 * D_FF), lambda i, be, nu, nx, nv: (be[i], 0, 0)),
                  pl.BlockSpec(memory_space=pl.ANY),
                  pl.BlockSpec((1, 1, D_MODEL), lambda i, be, nu, nx, nv: (be[i], 0, 0))],
        out_specs=row_blk,
        scratch_shapes=[pltpu.VMEM((D_MODEL, 2 * D_FF), F32),
                        pltpu.VMEM((D_FF, D_MODEL), F32),
                        pltpu.VMEM((D_MODEL, 2 * D_FF), BF16),
                        pltpu.VMEM((D_FF, D_MODEL), BF16),
                        pltpu.SemaphoreType.DMA((2,))],
    )
    return pl.pallas_call(
        _expert_kernel,
        grid_spec=grid_spec,
        out_shape=jax.ShapeDtypeStruct((N_SLOTS * ROW_TILES, LANES), U32),
        compiler_params=pltpu.CompilerParams(dimension_semantics=("arbitrary",),
                                             vmem_limit_bytes=VMEM_LIMIT),
        name="experts",
    )(blk_e, n_used, next_e, blk_valid, xs2, w_gu, b_gu3, w_down, b_down3)


def _final_kernel(yg_ref, gate_ref, x1_ref, mod_ref, gpost_ref, *maybe_alias_and_out):
    o_ref = maybe_alias_and_out[-1]
    gates = gate_ref[...]
    y = None
    for kk in range(TOP_K):
        part = _load_row_tiles(yg_ref.at[kk], TOK_TILE) * gates[:, kk:kk + 1]
        y = part if y is None else y + part
    gate_f = mod_ref[0][5:6]
    o_ref[...] = x1_ref[...] + gate_f * (_rms(y) * gpost_ref[...])


def _final(part, yg3, gates, x1, mod3, g_post, out_so_far):
    tiles = SEQ // TOK_TILE
    tok = lambda w: pl.BlockSpec((TOK_TILE, w), lambda i: (part * tiles + i, 0))
    in_specs = [pl.BlockSpec((TOP_K, TOK_TILE * ROW_TILES, LANES), lambda i: (0, i, 0)),
                tok(TOP_K), tok(D_MODEL),
                pl.BlockSpec((1, 6, D_MODEL), lambda i: (part, 0, 0)),
                pl.BlockSpec((1, D_MODEL), lambda i: (0, 0))]
    args = [yg3, gates, x1, mod3, g_post]
    aliases = {}
    if out_so_far is not None:
        in_specs.append(pl.BlockSpec(memory_space=pl.ANY))
        args.append(out_so_far)
        aliases = {len(args) - 1: 0}
    return pl.pallas_call(
        _final_kernel,
        grid=(tiles,),
        in_specs=in_specs,
        out_specs=tok(D_MODEL),
        out_shape=jax.ShapeDtypeStruct((N_TOK, D_MODEL), F32),
        input_output_aliases=aliases,
        compiler_params=pltpu.CompilerParams(vmem_limit_bytes=VMEM_LIMIT),
        name="combine_final",
    )(*args)


def _routing_tables(idx, rank, counts):
    counts = counts.reshape(N_EXPERTS)
    experts = jnp.arange(N_EXPERTS, dtype=I32)
    padded = ((counts + MOE_BLOCK - 1) // MOE_BLOCK) * MOE_BLOCK
    pend = jnp.cumsum(padded)
    pstart = pend - padded
    dest = jnp.sum(jnp.where(idx[..., None] == experts, pstart, 0), axis=-1) + rank
    dest_win = dest.reshape(TOP_K, N_TOK // SC_WINDOW, SC_WINDOW).transpose(1, 0, 2)
    n_used = (pend[-1] // MOE_BLOCK).astype(I32).reshape(1)
    blk_start = jnp.arange(N_BLOCKS, dtype=I32) * MOE_BLOCK
    blk_e = jnp.sum(blk_start[:, None] >= pend[None, :], axis=1).astype(I32)
    last_e = jnp.max(jnp.where(counts > 0, experts, 0))
    blk_e = jnp.minimum(blk_e, last_e)
    later = (experts[None, :] > experts[:, None]) & (counts[None, :] > 0)
    next_nonempty = jnp.min(jnp.where(later, experts[None, :], N_EXPERTS), axis=1)
    next_nonempty = jnp.where(next_nonempty == N_EXPERTS, -1, next_nonempty).astype(I32)
    of_block = lambda table: jnp.sum(jnp.where(blk_e[:, None] == experts, table, 0), axis=1).astype(I32)
    next_e = of_block(next_nonempty)
    blk_valid = jnp.clip(of_block(counts) - (blk_start - of_block(pstart)), 0, MOE_BLOCK)
    blk_valid = jnp.where(blk_start < pend[-1], blk_valid, 0).astype(I32)
    return dest_win, blk_e, n_used, next_e, blk_valid


def kernel(x, c, w_ada, b_ada, g_pre_mix, g_post_mix, w_in, attn_norm_w, hgrn_lb, hgrn_norm_w,
           w_out, g_pre_ffn, g_post_ffn, w_router, b_router, w_gu, b_gu, w_down, b_down):
    c_pad = jnp.pad(c, ((0, SUBLANES - BATCH), (0, 0)))
    mod = _ada_mod(c_pad, w_ada[0], b_ada)
    mod3 = mod[:BATCH].reshape(BATCH, 6, D_MODEL)

    x2 = x.reshape(N_TOK, D_MODEL)
    q4, k4, v4, q16, k16, v16, qr, fr, ir, gr = _inproj(x2, mod3, g_pre_mix, w_in[0].astype(BF16))
    nat = lambda t: t.reshape(BATCH, SEQ, D_REC)
    ya = _attention(q4, k4, v4, q16, k16, v16, attn_norm_w)
    yr = _hgrn(nat(qr), nat(fr), nat(ir), nat(gr), hgrn_lb, hgrn_norm_w)

    x1, h2, idx, gates, rank, counts = _mid(
        ya.reshape(N_TOK, D_ATTN), yr.reshape(N_TOK, D_REC), x2, mod3, g_post_mix, g_pre_ffn,
        w_out[0].astype(BF16), w_router[0], b_router)

    dest_win, blk_e, n_used, next_e, blk_valid = _routing_tables(idx, rank, counts)
    xs = _sc_dispatch(h2.reshape(N_TOK, ROW_TILES, LANES), dest_win)
    ys = _experts(blk_e, n_used, next_e, blk_valid, xs.reshape(N_SLOTS * ROW_TILES, LANES),
                  w_gu[0], b_gu[0].reshape(N_EXPERTS, 1, 2 * D_FF),
                  w_down[0], b_down[0].reshape(N_EXPERTS, 1, D_MODEL))
    ys3 = ys.reshape(N_SLOTS, ROW_TILES, LANES)
    win_per_seq = SEQ // SC_WINDOW
    out = None
    for b in range(BATCH):
        yg = _sc_collect(ys3, dest_win[b * win_per_seq:(b + 1) * win_per_seq])
        out = _final(b, yg.reshape(TOP_K, SEQ * ROW_TILES, LANES), gates, x1, mod3, g_post_ffn, out)
    return out.reshape(BATCH, SEQ, D_MODEL)
```

```python
import functools
import math

import jax
import jax.numpy as jnp
from jax import lax
from jax.experimental import pallas as pl
from jax.experimental.pallas import tpu as pltpu
from jax.experimental.pallas import tpu_sc as plsc

F32 = jnp.float32
BF16 = jnp.bfloat16
I32 = jnp.int32
U32 = jnp.uint32
HIGHEST = lax.Precision.HIGHEST

D_MODEL = 1024
BATCH = 4
SEQ = 4096
N_TOK = BATCH * SEQ
D_ATTN = 512
HEAD_DIM_A = 64
ATT_BLOCK = 128
DIL_MID = 4
DIL_MAX = 16
SUB_MID = SEQ // DIL_MID
SUB_MAX = SEQ // DIL_MAX
D_REC = 512
HGRN_CHUNK = 32
HGRN_SUPER = 256
N_EXPERTS = 32
TOP_K = 4
D_FF = 1024
SWIGLU_LIMIT = 7.0
SWIGLU_ALPHA = 1.702
EPS = 1e-6
NEG_BIG = -1e30
Q_SCALE = HEAD_DIM_A ** -0.5 * math.log2(math.e)

LANES = 128
SUBLANES = 8
ROW_TILES = D_MODEL // 2 // LANES

TOK_TILE = 512
MOE_BLOCK = 512
N_SLOTS = N_TOK * TOP_K + N_EXPERTS * MOE_BLOCK
N_BLOCKS = N_SLOTS // MOE_BLOCK
SC_WINDOW = 64
VMEM_LIMIT = 56 * 1024 * 1024


def _sigmoid(x):
    return 1.0 / (1.0 + jnp.exp(-x))


def _dot(a, b):
    return jnp.dot(a, b, preferred_element_type=F32)


def _dot_nt(a, b):
    return lax.dot_general(a, b, (((1,), (1,)), ((), ())), preferred_element_type=F32)


def _split_bf16(x):
    hi = x.astype(BF16)
    return hi, (x - hi.astype(F32)).astype(BF16)


def _rms(x):
    return x * lax.rsqrt(jnp.mean(x * x, axis=-1, keepdims=True) + EPS)


def _ada_kernel(c_ref, w_ref, b_ref, o_ref):
    c = c_ref[...]
    cond = c * _sigmoid(c)
    o_ref[...] = jnp.dot(cond, w_ref[...], precision=HIGHEST,
                         preferred_element_type=F32) + b_ref[...]


def _ada_mod(c_pad, w_ada, b_ada):
    n = w_ada.shape[1]
    tn = 1536
    return pl.pallas_call(
        _ada_kernel,
        grid=(n // tn,),
        in_specs=[pl.BlockSpec((SUBLANES, D_MODEL), lambda j: (0, 0)),
                  pl.BlockSpec((D_MODEL, tn), lambda j: (0, j)),
                  pl.BlockSpec((1, tn), lambda j: (0, j))],
        out_specs=pl.BlockSpec((SUBLANES, tn), lambda j: (0, j)),
        out_shape=jax.ShapeDtypeStruct((SUBLANES, n), F32),
        compiler_params=pltpu.CompilerParams(vmem_limit_bytes=VMEM_LIMIT),
        name="ada_mod",
    )(c_pad, w_ada, b_ada)


def _inproj_kernel(x_ref, mod_ref, g_ref, w_ref, q4, k4, v4, q16, k16, v16, qr, fr, ir, gr,
                   stage_nat, stage_mid):
    mod = mod_ref[0]
    shift, scale = mod[0:1], mod[1:2]
    h = _rms(x_ref[...]) * g_ref[...] * (1.0 + scale) + shift
    hb = h.astype(BF16)
    slabs = D_ATTN // LANES
    rows_mid = TOK_TILE // DIL_MID
    rows_max = TOK_TILE // DIL_MAX

    def proj(j):
        return _dot(hb, w_ref[:, j * D_ATTN:(j + 1) * D_ATTN])

    for j, (o_mid, o_max) in enumerate(((q4, q16), (k4, k16), (v4, v16))):
        r = proj(j)
        if j == 0:
            r = r * Q_SCALE
        for cs in range(slabs):
            stage_nat[cs] = r[:, cs * LANES:(cs + 1) * LANES]
        for cs in range(slabs):
            lanes = slice(cs * LANES, (cs + 1) * LANES)
            for sub in range(DIL_MID):
                piece = stage_nat[cs, pl.ds(sub, rows_mid, stride=DIL_MID), :]
                o_mid[sub, :, lanes] = piece.astype(BF16)
                stage_mid[cs, sub] = piece
            for sub in range(DIL_MAX):
                piece = stage_mid[cs, sub % DIL_MID, pl.ds(sub // DIL_MID, rows_max, stride=DIL_MID), :]
                o_max[sub, :, lanes] = piece.astype(BF16)
    for j, o_ref in enumerate((qr, fr, ir, gr)):
        o_ref[...] = proj(3 + j)


def _inproj(x2, mod3, g_pre, w_in_bf16):
    tiles_per_seq = SEQ // TOK_TILE
    rows_mid = TOK_TILE // DIL_MID
    rows_max = TOK_TILE // DIL_MAX
    mid = pl.BlockSpec((None, DIL_MID, rows_mid, D_ATTN),
                       lambda i: (i // tiles_per_seq, 0, i % tiles_per_seq, 0))
    mx = pl.BlockSpec((None, DIL_MAX, rows_max, D_ATTN),
                      lambda i: (i // tiles_per_seq, 0, i % tiles_per_seq, 0))
    nat = pl.BlockSpec((TOK_TILE, D_REC), lambda i: (i, 0))
    mid_shape = jax.ShapeDtypeStruct((BATCH, DIL_MID, SUB_MID, D_ATTN), BF16)
    mx_shape = jax.ShapeDtypeStruct((BATCH, DIL_MAX, SUB_MAX, D_ATTN), BF16)
    nat_shape = jax.ShapeDtypeStruct((N_TOK, D_REC), F32)
    return pl.pallas_call(
        _inproj_kernel,
        grid=(N_TOK // TOK_TILE,),
        in_specs=[pl.BlockSpec((TOK_TILE, D_MODEL), lambda i: (i, 0)),
                  pl.BlockSpec((1, 6, D_MODEL), lambda i: (i // tiles_per_seq, 0, 0)),
                  pl.BlockSpec((1, D_MODEL), lambda i: (0, 0)),
                  pl.BlockSpec(w_in_bf16.shape, lambda i: (0, 0))],
        out_specs=[mid, mid, mid, mx, mx, mx, nat, nat, nat, nat],
        out_shape=[mid_shape] * 3 + [mx_shape] * 3 + [nat_shape] * 4,
        scratch_shapes=[pltpu.VMEM((D_ATTN // LANES, TOK_TILE, LANES), F32),
                        pltpu.VMEM((D_ATTN // LANES, DIL_MID, rows_mid, LANES), F32)],
        compiler_params=pltpu.CompilerParams(vmem_limit_bytes=VMEM_LIMIT),
        name="inproj",
    )(x2, mod3, g_pre, w_in_bf16)


def _attn_kernel(q4_ref, k4_ref, v4_ref, q16_ref, k16_ref, v16_ref, nw_ref, o_ref,
                 sel_scr, keep_scr, o_scr, m_scr, l_scr):
    bw = ATT_BLOCK
    lane = lax.broadcasted_iota(I32, (bw, LANES), 1)
    head0 = lane < HEAD_DIM_A
    head_masks = (jnp.where(head0, 1.0, 0.0).astype(BF16), jnp.where(head0, 0.0, 1.0).astype(BF16))

    rr = lax.broadcasted_iota(I32, (bw, bw), 0)
    cc = lax.broadcasted_iota(I32, (bw, bw), 1)
    piece = bw // DIL_MID
    rp = DIL_MID * (rr % piece) + rr // piece
    cp = DIL_MID * (cc % piece) + cc // piece
    for var, (r, c) in enumerate(((rr, cc), (rr, cc), (rp, cp), (rp, cp))):
        first = var % 2 == 1
        left = (c < r) if first else (c >= r)
        right = jnp.zeros_like(left) if first else jnp.logical_not(left)
        sel_scr[var] = jnp.where(left, 1.0, 0.0)
        keep_scr[var, 0] = jnp.where(left, 1.0, 0.0).astype(BF16)
        keep_scr[var, 1] = jnp.where(right, 1.0, 0.0).astype(BF16)

    def score_matmuls(loaded):
        return [_dot_nt(q * hm, k) for (q, k, _, _, _) in loaded for hm in head_masks]

    def finish_group(scores, loaded):
        probs = []
        for n, s in enumerate(scores):
            _, _, _, var, right_bias = loaded[n // 2]
            folded = jnp.where(sel_scr[var] > 0.5, s[:, 0:bw], s[:, bw:] + right_bias)
            m = jnp.max(folded, axis=-1, keepdims=True)
            p = jnp.exp2(folded - m)
            pb = p.astype(BF16)
            spread = jnp.concatenate([pb * keep_scr[var, 0], pb * keep_scr[var, 1]], axis=1)
            probs.append((spread, jnp.sum(p, axis=-1, keepdims=True), m))
        pvs = [_dot(p, loaded[n // 2][2]) for n, (p, _, _) in enumerate(probs)]
        results = []
        for u in range(len(loaded)):
            (_, l0, m0), (_, l1, m1) = probs[2 * u], probs[2 * u + 1]
            results.append((jnp.where(head0, pvs[2 * u], pvs[2 * u + 1]),
                            jnp.where(head0, jnp.broadcast_to(m0, (bw, LANES)),
                                      jnp.broadcast_to(m1, (bw, LANES))),
                            jnp.where(head0, jnp.broadcast_to(l0, (bw, LANES)),
                                      jnp.broadcast_to(l1, (bw, LANES)))))
        return results

    group = 8

    def first_block_bias(blk):
        return jnp.where(blk == 0, NEG_BIG, 0.0)

    def load_d1(g):
        loaded, starts = [], []
        for i in range(group):
            blk = g * group + i
            qs = pl.multiple_of(blk * piece, piece)
            ks = pl.multiple_of(jnp.maximum(blk - 1, 0) * piece, piece)
            q = jnp.concatenate([q4_ref[r, pl.ds(qs, piece), :] for r in range(DIL_MID)], axis=0)
            k = jnp.concatenate([k4_ref[r, pl.ds(ks + half * piece, piece), :]
                                 for half in range(2) for r in range(DIL_MID)], axis=0)
            v = jnp.concatenate([v4_ref[r, pl.ds(ks + half * piece, piece), :]
                                 for half in range(2) for r in range(DIL_MID)], axis=0)
            loaded.append((q, k, v, jnp.where(blk == 0, 3, 2), first_block_bias(blk)))
            starts.append(qs)

        def store(results):
            for qs, parts in zip(starts, results):
                for r in range(DIL_MID):
                    for scr, val in zip((o_scr, m_scr, l_scr), parts):
                        scr[0, r, pl.ds(qs, piece), :] = val[r * piece:(r + 1) * piece]

        return loaded, store

    def load_d4(g):
        loaded, dsts = [], []
        for i in range(group):
            r, blk = i % DIL_MID, g * (group // DIL_MID) + i // DIL_MID
            qs = pl.multiple_of(blk * bw, bw)
            ks = pl.multiple_of(jnp.maximum(blk - 1, 0) * bw, bw)
            loaded.append((q4_ref[r, pl.ds(qs, bw), :], k4_ref[r, pl.ds(ks, 2 * bw), :],
                           v4_ref[r, pl.ds(ks, 2 * bw), :], jnp.where(blk == 0, 1, 0),
                           first_block_bias(blk)))
            dsts.append((r, qs))

        def store(results):
            for (r, qs), parts in zip(dsts, results):
                for scr, val in zip((o_scr, m_scr, l_scr), parts):
                    scr[1, r, pl.ds(qs, bw), :] = val

        return loaded, store

    def load_d16(g):
        loaded, dsts = [], []
        for i in range(group):
            r, blk = g * (group // 2) + i // 2, i % 2
            loaded.append((q16_ref[r, blk * bw:(blk + 1) * bw, :], k16_ref[r], v16_ref[r],
                           1 - blk, NEG_BIG if blk == 0 else 0.0))
            dsts.append((r % DIL_MID, pl.ds(blk * (bw * DIL_MID) + r // DIL_MID, bw, stride=DIL_MID)))

        def store(results):
            for (sub, dst), parts in zip(dsts, results):
                for scr, val in zip((o_scr, m_scr, l_scr), parts):
                    scr[2, sub, dst, :] = val

        return loaded, store

    def run_group(load):
        def step(g, carry):
            loaded, store = load(g)
            store(finish_group(score_matmuls(loaded), loaded))
            return carry
        return step

    n_groups = SEQ // bw // group
    for load in (load_d1, load_d4, load_d16):
        lax.fori_loop(0, n_groups, run_group(load), 0)

    rows = 256
    hi = lax.broadcasted_iota(I32, (LANES, LANES), 0) // HEAD_DIM_A
    hj = lax.broadcasted_iota(I32, (LANES, LANES), 1) // HEAD_DIM_A
    head_sum = jnp.where(hi == hj, 1.0, 0.0).astype(BF16)
    nw = nw_ref[...]

    def head_sums(x):
        x_hi, x_lo = _split_bf16(x)
        return _dot(x_hi, head_sum) + _dot(x_lo, head_sum)

    def merge(t, carry):
        r = t // (SUB_MID // rows)
        start = pl.multiple_of((t % (SUB_MID // rows)) * rows, rows)
        sl = pl.ds(start, rows)
        own_score = head_sums(q4_ref[r, sl, :].astype(F32) * k4_ref[r, sl, :].astype(F32))
        ms = [m_scr[n, r, sl, :] for n in range(3)]
        mx = jnp.maximum(jnp.maximum(jnp.maximum(ms[0], ms[1]), ms[2]), own_score)
        ws = [jnp.exp2(m - mx) for m in ms]
        w_own = float(len(ms)) * jnp.exp2(own_score - mx)
        num = (ws[0] * o_scr[0, r, sl, :] + ws[1] * o_scr[1, r, sl, :] + ws[2] * o_scr[2, r, sl, :]
               + w_own * v4_ref[r, sl, :].astype(F32))
        den = ws[0] * l_scr[0, r, sl, :] + ws[1] * l_scr[1, r, sl, :] + ws[2] * l_scr[2, r, sl, :] + w_own
        o = num / den
        mean_sq = head_sums(o * o) * (1.0 / HEAD_DIM_A)
        o_ref[pl.ds(start * DIL_MID + r, rows, stride=DIL_MID), :] = o * lax.rsqrt(mean_sq + EPS) * nw
        return carry

    lax.fori_loop(0, SEQ // rows, merge, 0, unroll=8)


def _attention(q4, k4, v4, q16, k16, v16, attn_norm_w):
    hp = D_ATTN // LANES
    mid = pl.BlockSpec((None, DIL_MID, SUB_MID, LANES), lambda b, h: (b, 0, 0, h))
    mx = pl.BlockSpec((None, DIL_MAX, SUB_MAX, LANES), lambda b, h: (b, 0, 0, h))
    scr = pltpu.VMEM((3, DIL_MID, SUB_MID, LANES), F32)
    return pl.pallas_call(
        _attn_kernel,
        grid=(BATCH, hp),
        in_specs=[mid, mid, mid, mx, mx, mx, pl.BlockSpec((1, LANES), lambda b, h: (0, h))],
        out_specs=pl.BlockSpec((None, SEQ, LANES), lambda b, h: (b, 0, h)),
        out_shape=jax.ShapeDtypeStruct((BATCH, SEQ, D_ATTN), F32),
        scratch_shapes=[pltpu.VMEM((4, ATT_BLOCK, ATT_BLOCK), F32),
                        pltpu.VMEM((4, 2, ATT_BLOCK, ATT_BLOCK), BF16), scr, scr, scr],
        compiler_params=pltpu.CompilerParams(vmem_limit_bytes=VMEM_LIMIT),
        name="dilated_attn",
    )(q4, k4, v4, q16, k16, v16, attn_norm_w)


def _hgrn_kernel(qr_ref, fr_ref, ir_ref, gr_ref, lb_ref, nw_ref, o_ref,
                 qe_scr, oi_scr, delta_scr, dec_scr, st_scr):
    sup, c = HGRN_SUPER, HGRN_CHUNK
    nch = sup // c
    n_sup = SEQ // sup
    lbp = lb_ref[...]
    lmx = jnp.max(lbp, axis=0, keepdims=True)
    ex = jnp.exp(lbp - lmx)
    lb = ex[0:1] / (ex[0:1] + ex[1:2])
    nw = nw_ref[...]

    ri = lax.broadcasted_iota(I32, (sup, sup), 0)
    ci = lax.broadcasted_iota(I32, (sup, sup), 1)
    same_chunk = (ri // c) == (ci // c)
    causal = same_chunk & (ci <= ri)
    cum_op = jnp.where(causal, 1.0, 0.0).astype(BF16)

    def chunk_rows(kd, ch):
        parts = []
        if ch > 0:
            parts.append(jnp.zeros((ch * c, LANES), BF16))
        parts.append(kd[ch * c:(ch + 1) * c])
        if ch < nch - 1:
            parts.append(jnp.zeros(((nch - 1 - ch) * c, LANES), BF16))
        return jnp.concatenate(parts, axis=0)

    group = 4

    def independent(g, carry):
        ts = [g * group + i for i in range(group)]
        sls = [pl.ds(pl.multiple_of(t * sup, sup), sup) for t in ts]
        pre = []
        for sl in sls:
            f = lb + (1.0 - lb) * _sigmoid(fr_ref[sl, :])
            logf_hi, logf_lo = _split_bf16(jnp.log(f))
            pre.append((1.0 - f, _dot(cum_op, logf_hi) + _dot(cum_op, logf_lo)))
        mid = []
        for t, sl, (kk, b) in zip(ts, sls, pre):
            b_last = jnp.concatenate(
                [jnp.broadcast_to(b[(ch + 1) * c - 1:(ch + 1) * c], (c, LANES)) for ch in range(nch)],
                axis=0)
            q = qr_ref[sl, :]
            qeb = (q * _sigmoid(q) * jnp.exp(b)).astype(BF16)
            ke = (kk * jnp.exp(-b)).astype(BF16)
            kd = (kk * jnp.exp(b_last - b)).astype(BF16)
            qe_scr[sl, :] = qeb
            dec_rows = jnp.concatenate([b_last[ch * c:ch * c + 1] for ch in range(nch)], axis=0)
            dec_scr[pl.ds(pl.multiple_of(t * nch, nch), nch), :] = jnp.exp(dec_rows)
            v = ir_ref[sl, :]
            vt = v.T.astype(BF16)
            scores = _dot_nt(qeb, ke)
            for pair in range(nch // 2):
                rhs = jnp.concatenate([chunk_rows(kd, 2 * pair), chunk_rows(kd, 2 * pair + 1)], axis=1)
                d2 = _dot(vt, rhs)
                delta_scr[t * nch + 2 * pair] = d2[:, 0:LANES]
                delta_scr[t * nch + 2 * pair + 1] = d2[:, LANES:]
            mid.append((scores, v.astype(BF16)))
        for sl, (scores, vb) in zip(sls, mid):
            a = jnp.where(causal, scores, 0.0)
            oi_scr[sl, :] = _dot(a.astype(BF16), vb)
        return carry

    lax.fori_loop(0, n_sup // group, independent, 0)

    def recur(ch, st):
        st_scr[ch] = st.astype(BF16)
        return st * dec_scr[pl.ds(ch, 1), :] + delta_scr[ch]

    lax.fori_loop(0, SEQ // c, recur, jnp.zeros((LANES, LANES), F32), unroll=8)

    def finish(t, carry):
        sl = pl.ds(pl.multiple_of(t * sup, sup), sup)
        parts = [_dot_nt(qe_scr[pl.ds(pl.multiple_of(t * sup + ch * c, c), c), :], st_scr[t * nch + ch])
                 for ch in range(nch)]
        o = oi_scr[sl, :] + jnp.concatenate(parts, axis=0)
        g = gr_ref[sl, :]
        o_ref[sl, :] = (_rms(o) * nw * (g * _sigmoid(g))).astype(o_ref.dtype)
        return carry

    lax.fori_loop(0, n_sup, finish, 0, unroll=8)


def _hgrn(qr, fr, ir, gr, hgrn_lb, hgrn_norm_w):
    nh = D_REC // LANES
    n_chunks = SEQ // HGRN_CHUNK
    blk = pl.BlockSpec((None, SEQ, LANES), lambda b, h: (b, 0, h))
    return pl.pallas_call(
        _hgrn_kernel,
        grid=(BATCH, nh),
        in_specs=[blk, blk, blk, blk,
                  pl.BlockSpec((2, LANES), lambda b, h: (0, h)),
                  pl.BlockSpec((1, LANES), lambda b, h: (0, h))],
        out_specs=blk,
        out_shape=jax.ShapeDtypeStruct((BATCH, SEQ, D_REC), BF16),
        scratch_shapes=[pltpu.VMEM((SEQ, LANES), BF16),
                        pltpu.VMEM((SEQ, LANES), F32),
                        pltpu.VMEM((n_chunks, LANES, LANES), F32),
                        pltpu.VMEM((n_chunks, LANES), F32),
                        pltpu.VMEM((n_chunks, LANES, LANES), BF16)],
        compiler_params=pltpu.CompilerParams(vmem_limit_bytes=VMEM_LIMIT),
        name="hgrn2",
    )(qr, fr, ir, gr, hgrn_lb, hgrn_norm_w)


def _tile_batch(i):
    return i // (SEQ // TOK_TILE)


def _store_row_tiles(ref, val):
    rows, half = val.shape[0], D_MODEL // 2
    lo = lax.bitcast_convert_type(val[:, :half].astype(BF16).astype(F32), U32)
    hi = lax.bitcast_convert_type(val[:, half:].astype(BF16).astype(F32), U32)
    words = lax.shift_right_logical(lo, jnp.uint32(16)) | (hi & jnp.uint32(0xFFFF0000))
    for j in range(ROW_TILES):
        ref[pl.ds(j, rows, stride=ROW_TILES), :] = words[:, j * LANES:(j + 1) * LANES]


def _load_row_tiles(ref, rows):
    words = jnp.concatenate(
        [ref[pl.ds(j, rows, stride=ROW_TILES), :] for j in range(ROW_TILES)], axis=1)
    lo = lax.bitcast_convert_type(lax.shift_left(words, jnp.uint32(16)), F32)
    hi = lax.bitcast_convert_type(words & jnp.uint32(0xFFFF0000), F32)
    return jnp.concatenate([lo, hi], axis=1)


def _mid_kernel(ya_ref, yr_ref, x_ref, mod_ref, gpost_ref, gpre_ref, wo_ref, wr_ref, br_ref,
                x1_ref, h2_ref, idx_ref, gate_ref, rank_ref, cnt_ref, carry_ref):
    i = pl.program_id(0)

    @pl.when(i == 0)
    def _():
        carry_ref[...] = jnp.zeros_like(carry_ref)

    mod = mod_ref[0]
    gate_m, shift_f, scale_f = mod[2:3], mod[3:4], mod[4:5]
    y = _dot(ya_ref[...].astype(BF16), wo_ref[0:D_ATTN, :]) + _dot(yr_ref[...], wo_ref[D_ATTN:, :])
    x1 = x_ref[...] + gate_m * (_rms(y) * gpost_ref[...])
    x1_ref[...] = x1
    h2 = _rms(x1) * gpre_ref[...] * (1.0 + scale_f) + shift_f
    _store_row_tiles(h2_ref, h2)

    tm = h2.shape[0]
    h2_hi, h2_lo = _split_bf16(h2)
    wr_hi, wr_lo = _split_bf16(wr_ref[...])
    parts = jnp.concatenate([_dot(h2_hi, jnp.concatenate([wr_hi, wr_lo], axis=1)),
                             _dot(h2_lo, wr_hi),
                             jnp.broadcast_to(br_ref[...], (tm, N_EXPERTS))], axis=1).T
    work = sum(parts[g * N_EXPERTS:(g + 1) * N_EXPERTS] for g in range(LANES // N_EXPERTS))
    eidx = lax.broadcasted_iota(I32, (N_EXPERTS, tm), 0).astype(F32)
    vals, idxs = [], []
    onehot = jnp.zeros((N_EXPERTS, tm), F32)
    for _ in range(TOP_K):
        m = jnp.max(work, axis=0, keepdims=True)
        sel = jnp.min(jnp.where(work == m, eidx, float(N_EXPERTS)), axis=0, keepdims=True)
        hit = eidx == sel
        work = jnp.where(hit, -jnp.inf, work)
        onehot = jnp.where(hit, 1.0, onehot)
        vals.append(m)
        idxs.append(sel)
    ex = [jnp.exp(vv - vals[0]) for vv in vals]
    inv_den = 1.0 / (ex[0] + ex[1] + ex[2] + ex[3])

    ri = lax.broadcasted_iota(I32, (tm, tm), 0)
    ci = lax.broadcasted_iota(I32, (tm, tm), 1)
    strict_upper = jnp.where(ri < ci, 1.0, 0.0).astype(BF16)
    before = _dot(onehot.astype(BF16), strict_upper) + carry_ref[...]
    ranks = [jnp.sum(jnp.where(eidx == idxs[kk], before, 0.0), axis=0, keepdims=True)
             for kk in range(TOP_K)]
    idx_ref[...] = jnp.concatenate(idxs, axis=0).astype(I32)
    rank_ref[...] = jnp.concatenate(ranks, axis=0).astype(I32)
    gates_t = jnp.concatenate([e * inv_den for e in ex]
                              + [jnp.zeros((LANES - TOP_K, tm), F32)], axis=0)
    gate_ref[...] = gates_t.T[:, 0:TOP_K]
    total = carry_ref[...] + jnp.sum(onehot, axis=1, keepdims=True)
    carry_ref[...] = total
    cnt_ref[...] = total.astype(I32)


def _mid(ya, yr, x2, mod3, g_post, g_pre, w_out_bf16, w_router, b_router):
    tok = lambda w: pl.BlockSpec((TOK_TILE, w), lambda i: (i, 0))
    const = lambda s: pl.BlockSpec(s, lambda i: (0,) * len(s))
    lanes_tok = pl.BlockSpec((TOP_K, TOK_TILE), lambda i: (0, i))
    return pl.pallas_call(
        _mid_kernel,
        grid=(N_TOK // TOK_TILE,),
        in_specs=[tok(D_ATTN), tok(D_REC), tok(D_MODEL),
                  pl.BlockSpec((1, 6, D_MODEL), lambda i: (_tile_batch(i), 0, 0)),
                  const((1, D_MODEL)), const((1, D_MODEL)),
                  const((D_MODEL, D_MODEL)), const((D_MODEL, N_EXPERTS)), const((1, N_EXPERTS))],
        out_specs=[tok(D_MODEL),
                   pl.BlockSpec((TOK_TILE * ROW_TILES, LANES), lambda i: (i, 0)),
                   lanes_tok, tok(TOP_K), lanes_tok, const((N_EXPERTS, 1))],
        out_shape=[jax.ShapeDtypeStruct((N_TOK, D_MODEL), F32),
                   jax.ShapeDtypeStruct((N_TOK * ROW_TILES, LANES), U32),
                   jax.ShapeDtypeStruct((TOP_K, N_TOK), I32),
                   jax.ShapeDtypeStruct((N_TOK, TOP_K), F32),
                   jax.ShapeDtypeStruct((TOP_K, N_TOK), I32),
                   jax.ShapeDtypeStruct((N_EXPERTS, 1), I32)],
        scratch_shapes=[pltpu.VMEM((N_EXPERTS, 1), F32)],
        compiler_params=pltpu.CompilerParams(dimension_semantics=("arbitrary",),
                                             vmem_limit_bytes=VMEM_LIMIT),
        name="outproj_router",
    )(ya, yr, x2, mod3, g_post, g_pre, w_out_bf16, w_router, b_router)


def _sc_mesh():
    return plsc.VectorSubcoreMesh(core_axis_name="c", subcore_axis_name="s")


def _sc_worker_count():
    info = plsc.get_sparse_core_info()
    return info.num_cores, info.num_cores * info.num_subcores


def _sc_dispatch(h_rows, dest_win):
    n_cores, n_workers = _sc_worker_count()
    n_win = N_TOK // SC_WINDOW
    per_worker = n_win // n_workers

    @functools.partial(
        pl.kernel, mesh=_sc_mesh(),
        out_type=jax.ShapeDtypeStruct((N_SLOTS, ROW_TILES, LANES), U32),
        scratch_types=[pltpu.VMEM((TOP_K, SC_WINDOW), I32),
                       pltpu.VMEM((SC_WINDOW, ROW_TILES, LANES), U32),
                       pltpu.SemaphoreType.DMA],
        name="sc_dispatch")
    def run(h_hbm, dest_hbm, xs_hbm, idx_v, rows_v, sem):
        wid = lax.axis_index("s") * n_cores + lax.axis_index("c")

        @pl.loop(0, per_worker)
        def _(j):
            win = wid * per_worker + j
            pltpu.sync_copy(dest_hbm.at[win], idx_v)
            pltpu.sync_copy(h_hbm.at[pl.ds(win * SC_WINDOW, SC_WINDOW)], rows_v)
            copies = [pltpu.async_copy(rows_v, xs_hbm.at[idx_v.at[kk]], sem)
                      for kk in range(TOP_K)]
            for cp in copies:
                cp.wait()

    return run(h_rows, dest_win)


def _sc_collect(y_rows, dest_win):
    n_cores, n_workers = _sc_worker_count()
    n_win = dest_win.shape[0]
    per_worker = n_win // n_workers

    @functools.partial(
        pl.kernel, mesh=_sc_mesh(),
        out_type=jax.ShapeDtypeStruct((TOP_K, n_win * SC_WINDOW, ROW_TILES, LANES), U32),
        scratch_types=[pltpu.VMEM((TOP_K, SC_WINDOW), I32),
                       pltpu.VMEM((SC_WINDOW, ROW_TILES, LANES), U32),
                       pltpu.SemaphoreType.DMA],
        name="sc_collect")
    def run(y_hbm, dest_hbm, yg_hbm, idx_v, rows_v, sem):
        wid = lax.axis_index("s") * n_cores + lax.axis_index("c")

        @pl.loop(0, per_worker)
        def _(j):
            win = wid * per_worker + j
            pltpu.sync_copy(dest_hbm.at[win], idx_v)
            for kk in range(TOP_K):
                pltpu.async_copy(y_hbm.at[idx_v.at[kk]], rows_v, sem).wait()
                pltpu.sync_copy(rows_v, yg_hbm.at[kk, pl.ds(win * SC_WINDOW, SC_WINDOW)])

    return run(y_rows, dest_win)


def _expert_kernel(be_ref, nu_ref, nx_ref, nv_ref, x_ref, wgu_hbm, bgu_ref, wd_hbm, bd_ref, y_ref,
                   wgu32, wd32, wgu16, wd16, sems):
    i = pl.program_id(0)
    e = be_ref[i]
    prev = be_ref[jnp.maximum(i - 1, 0)]

    def weight_copies(ex):
        return (pltpu.make_async_copy(wgu_hbm.at[ex], wgu32, sems.at[0]),
                pltpu.make_async_copy(wd_hbm.at[ex], wd32, sems.at[1]))

    @pl.when(i == 0)
    def _():
        for cp in weight_copies(e):
            cp.start()

    @pl.when((i == 0) | (e != prev))
    def _():
        for cp in weight_copies(e):
            cp.wait()
        rows = 128

        def cast(r, carry):
            sl = pl.ds(pl.multiple_of(r * rows, rows), rows)
            wgu16[sl, :] = wgu32[sl, :].astype(BF16)
            wd16[sl, :] = wd32[sl, :].astype(BF16)
            return carry

        lax.fori_loop(0, D_MODEL // rows, cast, 0)
        nxt = nx_ref[i]

        @pl.when(nxt >= 0)
        def _():
            for cp in weight_copies(nxt):
                cp.start()

    def run_rows(rows):
        x = _load_row_tiles(x_ref, rows).astype(BF16)
        bgu = bgu_ref[0]
        glu = _dot(x, wgu16[:, 0:D_FF]) + bgu[:, 0:D_FF]
        lin = _dot(x, wgu16[:, D_FF:]) + bgu[:, D_FF:]
        glu = jnp.minimum(glu, SWIGLU_LIMIT)
        lin = jnp.clip(lin, -SWIGLU_LIMIT, SWIGLU_LIMIT)
        act = glu * _sigmoid(SWIGLU_ALPHA * glu) * (lin + 1.0)
        y = _dot(act.astype(BF16), wd16[...]) + bd_ref[0]
        _store_row_tiles(y_ref, y)

    valid = nv_ref[i]
    quarter = MOE_BLOCK // 4
    for rows in range(quarter, MOE_BLOCK + 1, quarter):
        pl.when((valid > rows - quarter) & (valid <= rows))(functools.partial(run_rows, rows))


def _experts(blk_e, n_used, next_e, blk_valid, xs2, w_gu, b_gu3, w_down, b_down3):
    row_blk = pl.BlockSpec((MOE_BLOCK * ROW_TILES, LANES),
                           lambda i, be, nu, nx, nv: (jnp.minimum(i, nu[0] - 1), 0))
    grid_spec = pltpu.PrefetchScalarGridSpec(
        num_scalar_prefetch=4,
        grid=(N_BLOCKS,),
        in_specs=[row_blk,
                  pl.BlockSpec(memory_space=pl.ANY),
                  pl.BlockSpec((1, 1, 2 * D_FF), lambda i, be, nu, nx, nv: (be[i], 0, 0)),
                  pl.BlockSpec(memory_space=pl.ANY),
                  pl.BlockSpec((1, 1, D_MODEL), lambda i, be, nu, nx, nv: (be[i], 0, 0))],
        out_specs=row_blk,
        scratch_shapes=[pltpu.VMEM((D_MODEL, 2 * D_FF), F32),
                        pltpu.VMEM((D_FF, D_MODEL), F32),
                        pltpu.VMEM((D_MODEL, 2 * D_FF), BF16),
                        pltpu.VMEM((D_FF, D_MODEL), BF16),
                        pltpu.SemaphoreType.DMA((2,))],
    )
    return pl.pallas_call(
        _expert_kernel,
        grid_spec=grid_spec,
        out_shape=jax.ShapeDtypeStruct((N_SLOTS * ROW_TILES, LANES), U32),
        compiler_params=pltpu.CompilerParams(dimension_semantics=("arbitrary",),
                                             vmem_limit_bytes=VMEM_LIMIT),
        name="experts",
    )(blk_e, n_used, next_e, blk_valid, xs2, w_gu, b_gu3, w_down, b_down3)


def _final_kernel(yg_ref, gate_ref, x1_ref, mod_ref, gpost_ref, *maybe_alias_and_out):
    o_ref = maybe_alias_and_out[-1]
    gates = gate_ref[...]
    y = None
    for kk in range(TOP_K):
        part = _load_row_tiles(yg_ref.at[kk], TOK_TILE) * gates[:, kk:kk + 1]
        y = part if y is None else y + part
    gate_f = mod_ref[0][5:6]
    o_ref[...] = x1_ref[...] + gate_f * (_rms(y) * gpost_ref[...])


def _final(part, yg3, gates, x1, mod3, g_post, out_so_far):
    tiles = SEQ // TOK_TILE
    tok = lambda w: pl.BlockSpec((TOK_TILE, w), lambda i: (part * tiles + i, 0))
    in_specs = [pl.BlockSpec((TOP_K, TOK_TILE * ROW_TILES, LANES), lambda i: (0, i, 0)),
                tok(TOP_K), tok(D_MODEL),
                pl.BlockSpec((1, 6, D_MODEL), lambda i: (part, 0, 0)),
                pl.BlockSpec((1, D_MODEL), lambda i: (0, 0))]
    args = [yg3, gates, x1, mod3, g_post]
    aliases = {}
    if out_so_far is not None:
        in_specs.append(pl.BlockSpec(memory_space=pl.ANY))
        args.append(out_so_far)
        aliases = {len(args) - 1: 0}
    return pl.pallas_call(
        _final_kernel,
        grid=(tiles,),
        in_specs=in_specs,
        out_specs=tok(D_MODEL),
        out_shape=jax.ShapeDtypeStruct((N_TOK, D_MODEL), F32),
        input_output_aliases=aliases,
        compiler_params=pltpu.CompilerParams(vmem_limit_bytes=VMEM_LIMIT),
        name="combine_final",
    )(*args)


def _routing_tables(idx, rank, counts):
    counts = counts.reshape(N_EXPERTS)
    experts = jnp.arange(N_EXPERTS, dtype=I32)
    padded = ((counts + MOE_BLOCK - 1) // MOE_BLOCK) * MOE_BLOCK
    pend = jnp.cumsum(padded)
    pstart = pend - padded
    dest = jnp.sum(jnp.where(idx[..., None] == experts, pstart, 0), axis=-1) + rank
    dest_win = dest.reshape(TOP_K, N_TOK // SC_WINDOW, SC_WINDOW).transpose(1, 0, 2)
    n_used = (pend[-1] // MOE_BLOCK).astype(I32).reshape(1)
    blk_start = jnp.arange(N_BLOCKS, dtype=I32) * MOE_BLOCK
    blk_e = jnp.sum(blk_start[:, None] >= pend[None, :], axis=1).astype(I32)
    last_e = jnp.max(jnp.where(counts > 0, experts, 0))
    blk_e = jnp.minimum(blk_e, last_e)
    later = (experts[None, :] > experts[:, None]) & (counts[None, :] > 0)
    next_nonempty = jnp.min(jnp.where(later, experts[None, :], N_EXPERTS), axis=1)
    next_nonempty = jnp.where(next_nonempty == N_EXPERTS, -1, next_nonempty).astype(I32)
    of_block = lambda table: jnp.sum(jnp.where(blk_e[:, None] == experts, table, 0), axis=1).astype(I32)
    next_e = of_block(next_nonempty)
    blk_valid = jnp.clip(of_block(counts) - (blk_start - of_block(pstart)), 0, MOE_BLOCK)
    blk_valid = jnp.where(blk_start < pend[-1], blk_valid, 0).astype(I32)
    return dest_win, blk_e, n_used, next_e, blk_valid


def kernel(x, c, w_ada, b_ada, g_pre_mix, g_post_mix, w_in, attn_norm_w, hgrn_lb, hgrn_norm_w,
           w_out, g_pre_ffn, g_post_ffn, w_router, b_router, w_gu, b_gu, w_down, b_down):
    c_pad = jnp.pad(c, ((0, SUBLANES - BATCH), (0, 0)))
    mod = _ada_mod(c_pad, w_ada[0], b_ada)
    mod3 = mod[:BATCH].reshape(BATCH, 6, D_MODEL)

    x2 = x.reshape(N_TOK, D_MODEL)
    q4, k4, v4, q16, k16, v16, qr, fr, ir, gr = _inproj(x2, mod3, g_pre_mix, w_in[0].astype(BF16))
    nat = lambda t: t.reshape(BATCH, SEQ, D_REC)
    ya = _attention(q4, k4, v4, q16, k16, v16, attn_norm_w)
    yr = _hgrn(nat(qr), nat(fr), nat(ir), nat(gr), hgrn_lb, hgrn_norm_w)

    x1, h2, idx, gates, rank, counts = _mid(
        ya.reshape(N_TOK, D_ATTN), yr.reshape(N_TOK, D_REC), x2, mod3, g_post_mix, g_pre_ffn,
        w_out[0].astype(BF16), w_router[0], b_router)

    dest_win, blk_e, n_used, next_e, blk_valid = _routing_tables(idx, rank, counts)
    xs = _sc_dispatch(h2.reshape(N_TOK, ROW_TILES, LANES), dest_win)
    ys = _experts(blk_e, n_used, next_e, blk_valid, xs.reshape(N_SLOTS * ROW_TILES, LANES),
                  w_gu[0], b_gu[0].reshape(N_EXPERTS, 1, 2 * D_FF),
                  w_down[0], b_down[0].reshape(N_EXPERTS, 1, D_MODEL))
    ys3 = ys.reshape(N_SLOTS, ROW_TILES, LANES)
    win_per_seq = SEQ // SC_WINDOW
    out = None
    for b in range(BATCH):
        yg = _sc_collect(ys3, dest_win[b * win_per_seq:(b + 1) * win_per_seq])
        out = _final(b, yg.reshape(TOP_K, SEQ * ROW_TILES, LANES), gates, x1, mod3, g_post_ffn, out)
    return out.reshape(BATCH, SEQ, D_MODEL)
```

```python
import functools
import math

import jax
import jax.numpy as jnp
from jax import lax
from jax.experimental import pallas as pl
from jax.experimental.pallas import tpu as pltpu
from jax.experimental.pallas import tpu_sc as plsc

F32 = jnp.float32
BF16 = jnp.bfloat16
I32 = jnp.int32
U32 = jnp.uint32

D_MODEL = 1024
BATCH = 4
SEQ = 4096
N_TOK = BATCH * SEQ
D_ATTN = 512
HEAD_DIM_A = 64
ATT_BLOCK = 128
DIL_MID = 4
DIL_MAX = 16
SUB_MID = SEQ // DIL_MID
SUB_MAX = SEQ // DIL_MAX
D_REC = 512
HGRN_CHUNK = 32
HGRN_SUPER = 256
N_EXPERTS = 32
TOP_K = 4
D_FF = 1024
SWIGLU_LIMIT = 7.0
SWIGLU_ALPHA = 1.702
EPS = 1e-6
NEG_BIG = -1e30
Q_SCALE = HEAD_DIM_A ** -0.5 * math.log2(math.e)

LANES = 128
SUBLANES = 8
ROW_TILES = D_MODEL // 2 // LANES

TOK_TILE = 512
MOE_BLOCK = 512
N_SLOTS = N_TOK * TOP_K + N_EXPERTS * MOE_BLOCK
N_BLOCKS = N_SLOTS // MOE_BLOCK
SC_WINDOW = 64
VMEM_LIMIT = 56 * 1024 * 1024


def _sigmoid(x):
    return 1.0 / (1.0 + jnp.exp(-x))


def _dot(a, b):
    return jnp.dot(a, b, preferred_element_type=F32)


def _dot_nt(a, b):
    return lax.dot_general(a, b, (((1,), (1,)), ((), ())), preferred_element_type=F32)


def _split_bf16(x):
    hi = x.astype(BF16)
    return hi, (x - hi.astype(F32)).astype(BF16)


def _rms(x):
    return x * lax.rsqrt(jnp.mean(x * x, axis=-1, keepdims=True) + EPS)


def _ada_kernel(c_ref, w_ref, b_ref, o_ref):
    c = c_ref[...]
    cond = c * _sigmoid(c)
    c_hi, c_lo = _split_bf16(cond)
    w_hi, w_lo = _split_bf16(w_ref[...])
    o_ref[...] = _dot(c_hi, w_hi) + _dot(c_lo, w_hi) + _dot(c_hi, w_lo) + b_ref[...]


def _ada_mod(c_pad, w_ada, b_ada):
    n = w_ada.shape[1]
    tn = 1536
    return pl.pallas_call(
        _ada_kernel,
        grid=(n // tn,),
        in_specs=[pl.BlockSpec((SUBLANES, D_MODEL), lambda j: (0, 0)),
                  pl.BlockSpec((D_MODEL, tn), lambda j: (0, j)),
                  pl.BlockSpec((1, tn), lambda j: (0, j))],
        out_specs=pl.BlockSpec((SUBLANES, tn), lambda j: (0, j)),
        out_shape=jax.ShapeDtypeStruct((SUBLANES, n), F32),
        compiler_params=pltpu.CompilerParams(vmem_limit_bytes=VMEM_LIMIT),
        name="ada_mod",
    )(c_pad, w_ada, b_ada)


def _inproj_kernel(x_ref, mod_ref, g_ref, w_ref, q4, k4, v4, q16, k16, v16, qr, fr, ir, gr,
                   stage_nat, stage_mid):
    mod = mod_ref[0]
    shift, scale = mod[0:1], mod[1:2]
    h = _rms(x_ref[...]) * (g_ref[...] * (1.0 + scale)) + shift
    hb = h.astype(BF16)
    slabs = D_ATTN // LANES
    rows_mid = TOK_TILE // DIL_MID
    rows_max = TOK_TILE // DIL_MAX

    def proj(j):
        return _dot(hb, w_ref[:, j * D_ATTN:(j + 1) * D_ATTN])

    for j, (o_mid, o_max) in enumerate(((q4, q16), (k4, k16), (v4, v16))):
        r = proj(j)
        if j == 0:
            r = r * Q_SCALE
        for cs in range(slabs):
            stage_nat[cs] = r[:, cs * LANES:(cs + 1) * LANES]
        for cs in range(slabs):
            lanes = slice(cs * LANES, (cs + 1) * LANES)
            for sub in range(DIL_MID):
                piece = stage_nat[cs, pl.ds(sub, rows_mid, stride=DIL_MID), :]
                o_mid[sub, :, lanes] = piece.astype(BF16)
                stage_mid[cs, sub] = piece
            for sub in range(DIL_MAX):
                piece = stage_mid[cs, sub % DIL_MID, pl.ds(sub // DIL_MID, rows_max, stride=DIL_MID), :]
                o_max[sub, :, lanes] = piece.astype(BF16)
    for j, o_ref in enumerate((qr, fr, ir, gr)):
        o_ref[...] = proj(3 + j)


def _inproj(x2, mod3, g_pre, w_in_bf16):
    tiles_per_seq = SEQ // TOK_TILE
    rows_mid = TOK_TILE // DIL_MID
    rows_max = TOK_TILE // DIL_MAX
    mid = pl.BlockSpec((None, DIL_MID, rows_mid, D_ATTN),
                       lambda i: (i // tiles_per_seq, 0, i % tiles_per_seq, 0))
    mx = pl.BlockSpec((None, DIL_MAX, rows_max, D_ATTN),
                      lambda i: (i // tiles_per_seq, 0, i % tiles_per_seq, 0))
    nat = pl.BlockSpec((TOK_TILE, D_REC), lambda i: (i, 0))
    mid_shape = jax.ShapeDtypeStruct((BATCH, DIL_MID, SUB_MID, D_ATTN), BF16)
    mx_shape = jax.ShapeDtypeStruct((BATCH, DIL_MAX, SUB_MAX, D_ATTN), BF16)
    nat_shape = jax.ShapeDtypeStruct((N_TOK, D_REC), F32)
    return pl.pallas_call(
        _inproj_kernel,
        grid=(N_TOK // TOK_TILE,),
        in_specs=[pl.BlockSpec((TOK_TILE, D_MODEL), lambda i: (i, 0)),
                  pl.BlockSpec((1, 6, D_MODEL), lambda i: (i // tiles_per_seq, 0, 0)),
                  pl.BlockSpec((1, D_MODEL), lambda i: (0, 0)),
                  pl.BlockSpec(w_in_bf16.shape, lambda i: (0, 0))],
        out_specs=[mid, mid, mid, mx, mx, mx, nat, nat, nat, nat],
        out_shape=[mid_shape] * 3 + [mx_shape] * 3 + [nat_shape] * 4,
        scratch_shapes=[pltpu.VMEM((D_ATTN // LANES, TOK_TILE, LANES), F32),
                        pltpu.VMEM((D_ATTN // LANES, DIL_MID, rows_mid, LANES), F32)],
        compiler_params=pltpu.CompilerParams(vmem_limit_bytes=VMEM_LIMIT),
        name="inproj",
    )(x2, mod3, g_pre, w_in_bf16)


def _attn_kernel(q4_ref, k4_ref, v4_ref, q16_ref, k16_ref, v16_ref, nw_ref, o_ref,
                 sel_scr, keep_scr, o_scr, m_scr, l_scr):
    bw = ATT_BLOCK
    lane = lax.broadcasted_iota(I32, (bw, LANES), 1)
    head0 = lane < HEAD_DIM_A
    head_masks = (jnp.where(head0, 1.0, 0.0).astype(BF16), jnp.where(head0, 0.0, 1.0).astype(BF16))

    rr = lax.broadcasted_iota(I32, (bw, bw), 0)
    cc = lax.broadcasted_iota(I32, (bw, bw), 1)
    piece = bw // DIL_MID
    rp = DIL_MID * (rr % piece) + rr // piece
    cp = DIL_MID * (cc % piece) + cc // piece
    for var, (r, c) in enumerate(((rr, cc), (rr, cc), (rp, cp), (rp, cp))):
        first = var % 2 == 1
        left = (c < r) if first else (c >= r)
        right = jnp.zeros_like(left) if first else jnp.logical_not(left)
        sel_scr[var] = jnp.where(left, 1.0, 0.0)
        keep_scr[var, 0] = jnp.where(left, 1.0, 0.0).astype(BF16)
        keep_scr[var, 1] = jnp.where(right, 1.0, 0.0).astype(BF16)

    def score_matmuls(loaded):
        return [_dot_nt(q * hm, k) for (q, k, _, _, _) in loaded for hm in head_masks]

    def finish_group(scores, loaded):
        probs = []
        for n, s in enumerate(scores):
            _, _, _, var, right_bias = loaded[n // 2]
            folded = jnp.where(sel_scr[var] > 0.5, s[:, 0:bw], s[:, bw:] + right_bias)
            m = jnp.max(folded, axis=-1, keepdims=True)
            p = jnp.exp2(folded - m)
            pb = p.astype(BF16)
            spread = jnp.concatenate([pb * keep_scr[var, 0], pb * keep_scr[var, 1]], axis=1)
            probs.append((spread, jnp.sum(p, axis=-1, keepdims=True), m))
        pvs = [_dot(p, loaded[n // 2][2]) for n, (p, _, _) in enumerate(probs)]
        results = []
        for u in range(len(loaded)):
            (_, l0, m0), (_, l1, m1) = probs[2 * u], probs[2 * u + 1]
            results.append((jnp.where(head0, pvs[2 * u], pvs[2 * u + 1]),
                            jnp.where(head0, jnp.broadcast_to(m0, (bw, LANES)),
                                      jnp.broadcast_to(m1, (bw, LANES))),
                            jnp.where(head0, jnp.broadcast_to(l0, (bw, LANES)),
                                      jnp.broadcast_to(l1, (bw, LANES)))))
        return results

    group = 8

    def first_block_bias(blk):
        return jnp.where(blk == 0, NEG_BIG, 0.0)

    def load_d1(g):
        loaded, starts = [], []
        for i in range(group):
            blk = g * group + i
            qs = pl.multiple_of(blk * piece, piece)
            ks = pl.multiple_of(jnp.maximum(blk - 1, 0) * piece, piece)
            q = jnp.concatenate([q4_ref[r, pl.ds(qs, piece), :] for r in range(DIL_MID)], axis=0)
            k = jnp.concatenate([k4_ref[r, pl.ds(ks + half * piece, piece), :]
                                 for half in range(2) for r in range(DIL_MID)], axis=0)
            v = jnp.concatenate([v4_ref[r, pl.ds(ks + half * piece, piece), :]
                                 for half in range(2) for r in range(DIL_MID)], axis=0)
            loaded.append((q, k, v, jnp.where(blk == 0, 3, 2), first_block_bias(blk)))
            starts.append(qs)

        def store(results):
            for qs, parts in zip(starts, results):
                for r in range(DIL_MID):
                    for scr, val in zip((o_scr, m_scr, l_scr), parts):
                        scr[0, r, pl.ds(qs, piece), :] = val[r * piece:(r + 1) * piece]

        return loaded, store

    def load_d4(g):
        loaded, dsts = [], []
        for i in range(group):
            r, blk = i % DIL_MID, g * (group // DIL_MID) + i // DIL_MID
            qs = pl.multiple_of(blk * bw, bw)
            ks = pl.multiple_of(jnp.maximum(blk - 1, 0) * bw, bw)
            loaded.append((q4_ref[r, pl.ds(qs, bw), :], k4_ref[r, pl.ds(ks, 2 * bw), :],
                           v4_ref[r, pl.ds(ks, 2 * bw), :], jnp.where(blk == 0, 1, 0),
                           first_block_bias(blk)))
            dsts.append((r, qs))

        def store(results):
            for (r, qs), parts in zip(dsts, results):
                for scr, val in zip((o_scr, m_scr, l_scr), parts):
                    scr[1, r, pl.ds(qs, bw), :] = val

        return loaded, store

    def load_d16(g):
        loaded, dsts = [], []
        for i in range(group):
            r, blk = g * (group // 2) + i // 2, i % 2
            loaded.append((q16_ref[r, blk * bw:(blk + 1) * bw, :], k16_ref[r], v16_ref[r],
                           1 - blk, NEG_BIG if blk == 0 else 0.0))
            dsts.append((r % DIL_MID, pl.ds(blk * (bw * DIL_MID) + r // DIL_MID, bw, stride=DIL_MID)))

        def store(results):
            for (sub, dst), parts in zip(dsts, results):
                for scr, val in zip((o_scr, m_scr, l_scr), parts):
                    scr[2, sub, dst, :] = val

        return loaded, store

    def run_group(load):
        def step(g, carry):
            loaded, store = load(g)
            store(finish_group(score_matmuls(loaded), loaded))
            return carry
        return step

    n_groups = SEQ // bw // group
    for load in (load_d1, load_d4, load_d16):
        lax.fori_loop(0, n_groups, run_group(load), 0)

    rows = 256
    hi = lax.broadcasted_iota(I32, (LANES, LANES), 0) // HEAD_DIM_A
    hj = lax.broadcasted_iota(I32, (LANES, LANES), 1) // HEAD_DIM_A
    head_sum = jnp.where(hi == hj, 1.0, 0.0).astype(BF16)
    nw = nw_ref[...]

    def head_sums(x):
        x_hi, x_lo = _split_bf16(x)
        return _dot(x_hi, head_sum) + _dot(x_lo, head_sum)

    def merge(t, carry):
        r = t // (SUB_MID // rows)
        start = pl.multiple_of((t % (SUB_MID // rows)) * rows, rows)
        sl = pl.ds(start, rows)
        own_score = head_sums(q4_ref[r, sl, :].astype(F32) * k4_ref[r, sl, :].astype(F32))
        ms = [m_scr[n, r, sl, :] for n in range(3)]
        mx = jnp.maximum(jnp.maximum(jnp.maximum(ms[0], ms[1]), ms[2]), own_score)
        ws = [jnp.exp2(m - mx) for m in ms]
        w_own = float(len(ms)) * jnp.exp2(own_score - mx)
        num = (ws[0] * o_scr[0, r, sl, :] + ws[1] * o_scr[1, r, sl, :] + ws[2] * o_scr[2, r, sl, :]
               + w_own * v4_ref[r, sl, :].astype(F32))
        den = ws[0] * l_scr[0, r, sl, :] + ws[1] * l_scr[1, r, sl, :] + ws[2] * l_scr[2, r, sl, :] + w_own
        o = num / den
        mean_sq = head_sums(o * o) * (1.0 / HEAD_DIM_A)
        o_ref[pl.ds(start * DIL_MID + r, rows, stride=DIL_MID), :] = o * lax.rsqrt(mean_sq + EPS) * nw
        return carry

    lax.fori_loop(0, SEQ // rows, merge, 0, unroll=8)


def _attention(q4, k4, v4, q16, k16, v16, attn_norm_w):
    hp = D_ATTN // LANES
    mid = pl.BlockSpec((None, DIL_MID, SUB_MID, LANES), lambda b, h: (b, 0, 0, h))
    mx = pl.BlockSpec((None, DIL_MAX, SUB_MAX, LANES), lambda b, h: (b, 0, 0, h))
    scr = pltpu.VMEM((3, DIL_MID, SUB_MID, LANES), F32)
    return pl.pallas_call(
        _attn_kernel,
        grid=(BATCH, hp),
        in_specs=[mid, mid, mid, mx, mx, mx, pl.BlockSpec((1, LANES), lambda b, h: (0, h))],
        out_specs=pl.BlockSpec((None, SEQ, LANES), lambda b, h: (b, 0, h)),
        out_shape=jax.ShapeDtypeStruct((BATCH, SEQ, D_ATTN), F32),
        scratch_shapes=[pltpu.VMEM((4, ATT_BLOCK, ATT_BLOCK), F32),
                        pltpu.VMEM((4, 2, ATT_BLOCK, ATT_BLOCK), BF16), scr, scr, scr],
        compiler_params=pltpu.CompilerParams(vmem_limit_bytes=VMEM_LIMIT),
        name="dilated_attn",
    )(q4, k4, v4, q16, k16, v16, attn_norm_w)


def _hgrn_kernel(qr_ref, fr_ref, ir_ref, gr_ref, lb_ref, nw_ref, o_ref,
                 qe_scr, oi_scr, delta_scr, dec_scr, st_scr):
    sup, c = HGRN_SUPER, HGRN_CHUNK
    nch = sup // c
    n_sup = SEQ // sup
    lbp = lb_ref[...]
    lmx = jnp.max(lbp, axis=0, keepdims=True)
    ex = jnp.exp(lbp - lmx)
    lb = ex[0:1] / (ex[0:1] + ex[1:2])
    nw = nw_ref[...]

    ri = lax.broadcasted_iota(I32, (sup, sup), 0)
    ci = lax.broadcasted_iota(I32, (sup, sup), 1)
    same_chunk = (ri // c) == (ci // c)
    causal = same_chunk & (ci <= ri)
    cum_op = jnp.where(causal, 1.0, 0.0).astype(BF16)

    def chunk_rows(kd, ch):
        parts = []
        if ch > 0:
            parts.append(jnp.zeros((ch * c, LANES), BF16))
        parts.append(kd[ch * c:(ch + 1) * c])
        if ch < nch - 1:
            parts.append(jnp.zeros(((nch - 1 - ch) * c, LANES), BF16))
        return jnp.concatenate(parts, axis=0)

    group = 4

    def independent(g, carry):
        ts = [g * group + i for i in range(group)]
        sls = [pl.ds(pl.multiple_of(t * sup, sup), sup) for t in ts]
        pre = []
        for sl in sls:
            f = lb + (1.0 - lb) * _sigmoid(fr_ref[sl, :])
            logf_hi, logf_lo = _split_bf16(jnp.log(f))
            pre.append((1.0 - f, _dot(cum_op, logf_hi) + _dot(cum_op, logf_lo)))
        mid = []
        for t, sl, (kk, b) in zip(ts, sls, pre):
            b_last = jnp.concatenate(
                [jnp.broadcast_to(b[(ch + 1) * c - 1:(ch + 1) * c], (c, LANES)) for ch in range(nch)],
                axis=0)
            q = qr_ref[sl, :]
            qeb = (q * _sigmoid(q) * jnp.exp(b)).astype(BF16)
            ke = (kk * jnp.exp(-b)).astype(BF16)
            kd = (kk * jnp.exp(b_last - b)).astype(BF16)
            qe_scr[sl, :] = qeb
            dec_rows = jnp.concatenate([b_last[ch * c:ch * c + 1] for ch in range(nch)], axis=0)
            dec_scr[pl.ds(pl.multiple_of(t * nch, nch), nch), :] = jnp.exp(dec_rows)
            v = ir_ref[sl, :]
            vt = v.T.astype(BF16)
            scores = _dot_nt(qeb, ke)
            for pair in range(nch // 2):
                rhs = jnp.concatenate([chunk_rows(kd, 2 * pair), chunk_rows(kd, 2 * pair + 1)], axis=1)
                d2 = _dot(vt, rhs)
                delta_scr[t * nch + 2 * pair] = d2[:, 0:LANES]
                delta_scr[t * nch + 2 * pair + 1] = d2[:, LANES:]
            mid.append((scores, v.astype(BF16)))
        for sl, (scores, vb) in zip(sls, mid):
            a = jnp.where(causal, scores, 0.0)
            oi_scr[sl, :] = _dot(a.astype(BF16), vb)
        return carry

    lax.fori_loop(0, n_sup // group, independent, 0)

    def recur(ch, st):
        st_scr[ch] = st.astype(BF16)
        return st * dec_scr[pl.ds(ch, 1), :] + delta_scr[ch]

    lax.fori_loop(0, SEQ // c, recur, jnp.zeros((LANES, LANES), F32), unroll=8)

    def finish(t, carry):
        sl = pl.ds(pl.multiple_of(t * sup, sup), sup)
        parts = [_dot_nt(qe_scr[pl.ds(pl.multiple_of(t * sup + ch * c, c), c), :], st_scr[t * nch + ch])
                 for ch in range(nch)]
        o = oi_scr[sl, :] + jnp.concatenate(parts, axis=0)
        g = gr_ref[sl, :]
        o_ref[sl, :] = (_rms(o) * nw * (g * _sigmoid(g))).astype(o_ref.dtype)
        return carry

    lax.fori_loop(0, n_sup, finish, 0, unroll=8)


def _hgrn(qr, fr, ir, gr, hgrn_lb, hgrn_norm_w):
    nh = D_REC // LANES
    n_chunks = SEQ // HGRN_CHUNK
    blk = pl.BlockSpec((None, SEQ, LANES), lambda b, h: (b, 0, h))
    return pl.pallas_call(
        _hgrn_kernel,
        grid=(BATCH, nh),
        in_specs=[blk, blk, blk, blk,
                  pl.BlockSpec((2, LANES), lambda b, h: (0, h)),
                  pl.BlockSpec((1, LANES), lambda b, h: (0, h))],
        out_specs=blk,
        out_shape=jax.ShapeDtypeStruct((BATCH, SEQ, D_REC), BF16),
        scratch_shapes=[pltpu.VMEM((SEQ, LANES), BF16),
                        pltpu.VMEM((SEQ, LANES), F32),
                        pltpu.VMEM((n_chunks, LANES, LANES), F32),
                        pltpu.VMEM((n_chunks, LANES), F32),
                        pltpu.VMEM((n_chunks, LANES, LANES), BF16)],
        compiler_params=pltpu.CompilerParams(vmem_limit_bytes=VMEM_LIMIT),
        name="hgrn2",
    )(qr, fr, ir, gr, hgrn_lb, hgrn_norm_w)


def _tile_batch(i):
    return i // (SEQ // TOK_TILE)


def _store_row_tiles(ref, val):
    rows, half = val.shape[0], D_MODEL // 2
    lo = lax.bitcast_convert_type(val[:, :half], U32)
    hi = lax.bitcast_convert_type(val[:, half:], U32)
    words = lax.shift_right_logical(lo, jnp.uint32(16)) | (hi & jnp.uint32(0xFFFF0000))
    for j in range(ROW_TILES):
        ref[pl.ds(j, rows, stride=ROW_TILES), :] = words[:, j * LANES:(j + 1) * LANES]


def _load_row_tiles(ref, rows):
    words = jnp.concatenate(
        [ref[pl.ds(j, rows, stride=ROW_TILES), :] for j in range(ROW_TILES)], axis=1)
    lo = lax.bitcast_convert_type(lax.shift_left(words, jnp.uint32(16)), F32)
    hi = lax.bitcast_convert_type(words & jnp.uint32(0xFFFF0000), F32)
    return jnp.concatenate([lo, hi], axis=1)


def _mid_kernel(ya_ref, yr_ref, x_ref, mod_ref, gpost_ref, gpre_ref, wo_ref, wr_ref, br_ref,
                x1_ref, h2_ref, idx_ref, gate_ref, rank_ref, cnt_ref, carry_ref):
    i = pl.program_id(0)

    @pl.when(i == 0)
    def _():
        carry_ref[...] = jnp.zeros_like(carry_ref)

    mod = mod_ref[0]
    gate_m, shift_f, scale_f = mod[2:3], mod[3:4], mod[4:5]
    y = _dot(ya_ref[...].astype(BF16), wo_ref[0:D_ATTN, :]) + _dot(yr_ref[...], wo_ref[D_ATTN:, :])
    x1 = x_ref[...] + _rms(y) * (gate_m * gpost_ref[...])
    x1_ref[...] = x1
    h2 = _rms(x1) * (gpre_ref[...] * (1.0 + scale_f)) + shift_f
    h2_hi = h2.astype(BF16)
    h2_rounded = h2_hi.astype(F32)
    _store_row_tiles(h2_ref, h2_rounded)

    tm = h2.shape[0]
    h2_lo = (h2 - h2_rounded).astype(BF16)
    wr_hi, wr_lo = _split_bf16(wr_ref[...])
    parts = jnp.concatenate([_dot(h2_hi, jnp.concatenate([wr_hi, wr_lo], axis=1)),
                             _dot(h2_lo, wr_hi),
                             jnp.broadcast_to(br_ref[...], (tm, N_EXPERTS))], axis=1).T
    work = sum(parts[g * N_EXPERTS:(g + 1) * N_EXPERTS] for g in range(LANES // N_EXPERTS))
    eidx = lax.broadcasted_iota(I32, (N_EXPERTS, tm), 0).astype(F32)
    vals, idxs = [], []
    onehot = jnp.zeros((N_EXPERTS, tm), F32)
    for _ in range(TOP_K):
        m = jnp.max(work, axis=0, keepdims=True)
        sel = jnp.min(jnp.where(work == m, eidx, float(N_EXPERTS)), axis=0, keepdims=True)
        hit = eidx == sel
        work = jnp.where(hit, -jnp.inf, work)
        onehot = jnp.where(hit, 1.0, onehot)
        vals.append(m)
        idxs.append(sel)
    ex = [jnp.exp(vv - vals[0]) for vv in vals]
    inv_den = 1.0 / (ex[0] + ex[1] + ex[2] + ex[3])

    ri = lax.broadcasted_iota(I32, (tm, tm), 0)
    ci = lax.broadcasted_iota(I32, (tm, tm), 1)
    strict_upper = jnp.where(ri < ci, 1.0, 0.0).astype(BF16)
    before = _dot(onehot.astype(BF16), strict_upper) + carry_ref[...]
    ranks = [jnp.sum(jnp.where(eidx == idxs[kk], before, 0.0), axis=0, keepdims=True)
             for kk in range(TOP_K)]
    idx_ref[...] = jnp.concatenate(idxs, axis=0).astype(I32)
    rank_ref[...] = jnp.concatenate(ranks, axis=0).astype(I32)
    gates_t = jnp.concatenate([e * inv_den for e in ex]
                              + [jnp.zeros((LANES - TOP_K, tm), F32)], axis=0)
    gate_ref[...] = gates_t.T[:, 0:TOP_K]
    total = carry_ref[...] + jnp.sum(onehot, axis=1, keepdims=True)
    carry_ref[...] = total
    cnt_ref[...] = total.astype(I32)


def _mid(ya, yr, x2, mod3, g_post, g_pre, w_out_bf16, w_router, b_router):
    tok = lambda w: pl.BlockSpec((TOK_TILE, w), lambda i: (i, 0))
    const = lambda s: pl.BlockSpec(s, lambda i: (0,) * len(s))
    lanes_tok = pl.BlockSpec((TOP_K, TOK_TILE), lambda i: (0, i))
    return pl.pallas_call(
        _mid_kernel,
        grid=(N_TOK // TOK_TILE,),
        in_specs=[tok(D_ATTN), tok(D_REC), tok(D_MODEL),
                  pl.BlockSpec((1, 6, D_MODEL), lambda i: (_tile_batch(i), 0, 0)),
                  const((1, D_MODEL)), const((1, D_MODEL)),
                  const((D_MODEL, D_MODEL)), const((D_MODEL, N_EXPERTS)), const((1, N_EXPERTS))],
        out_specs=[tok(D_MODEL),
                   pl.BlockSpec((TOK_TILE * ROW_TILES, LANES), lambda i: (i, 0)),
                   lanes_tok, tok(TOP_K), lanes_tok, const((N_EXPERTS, 1))],
        out_shape=[jax.ShapeDtypeStruct((N_TOK, D_MODEL), F32),
                   jax.ShapeDtypeStruct((N_TOK * ROW_TILES, LANES), U32),
                   jax.ShapeDtypeStruct((TOP_K, N_TOK), I32),
                   jax.ShapeDtypeStruct((N_TOK, TOP_K), F32),
                   jax.ShapeDtypeStruct((TOP_K, N_TOK), I32),
                   jax.ShapeDtypeStruct((N_EXPERTS, 1), I32)],
        scratch_shapes=[pltpu.VMEM((N_EXPERTS, 1), F32)],
        compiler_params=pltpu.CompilerParams(dimension_semantics=("arbitrary",),
                                             vmem_limit_bytes=VMEM_LIMIT),
        name="outproj_router",
    )(ya, yr, x2, mod3, g_post, g_pre, w_out_bf16, w_router, b_router)


def _sc_mesh():
    return plsc.VectorSubcoreMesh(core_axis_name="c", subcore_axis_name="s")


def _sc_worker_count():
    info = plsc.get_sparse_core_info()
    return info.num_cores, info.num_cores * info.num_subcores


def _sc_dispatch(h_rows, dest_win):
    n_cores, n_workers = _sc_worker_count()
    n_win = N_TOK // SC_WINDOW
    per_worker = n_win // n_workers

    @functools.partial(
        pl.kernel, mesh=_sc_mesh(),
        out_type=jax.ShapeDtypeStruct((N_SLOTS, ROW_TILES, LANES), U32),
        scratch_types=[pltpu.VMEM((TOP_K, SC_WINDOW), I32),
                       pltpu.VMEM((SC_WINDOW, ROW_TILES, LANES), U32),
                       pltpu.SemaphoreType.DMA],
        name="sc_dispatch")
    def run(h_hbm, dest_hbm, xs_hbm, idx_v, rows_v, sem):
        wid = lax.axis_index("s") * n_cores + lax.axis_index("c")

        @pl.loop(0, per_worker)
        def _(j):
            win = wid * per_worker + j
            pltpu.sync_copy(dest_hbm.at[win], idx_v)
            pltpu.sync_copy(h_hbm.at[pl.ds(win * SC_WINDOW, SC_WINDOW)], rows_v)
            copies = [pltpu.async_copy(rows_v, xs_hbm.at[idx_v.at[kk]], sem)
                      for kk in range(TOP_K)]
            for cp in copies:
                cp.wait()

    return run(h_rows, dest_win)


def _sc_collect(y_rows, dest_win):
    n_cores, n_workers = _sc_worker_count()
    n_win = dest_win.shape[0]
    per_worker = n_win // n_workers

    @functools.partial(
        pl.kernel, mesh=_sc_mesh(),
        out_type=jax.ShapeDtypeStruct((TOP_K, n_win * SC_WINDOW, ROW_TILES, LANES), U32),
        scratch_types=[pltpu.VMEM((TOP_K, SC_WINDOW), I32),
                       pltpu.VMEM((SC_WINDOW, ROW_TILES, LANES), U32),
                       pltpu.SemaphoreType.DMA],
        name="sc_collect")
    def run(y_hbm, dest_hbm, yg_hbm, idx_v, rows_v, sem):
        wid = lax.axis_index("s") * n_cores + lax.axis_index("c")

        @pl.loop(0, per_worker)
        def _(j):
            win = wid * per_worker + j
            pltpu.sync_copy(dest_hbm.at[win], idx_v)
            for kk in range(TOP_K):
                pltpu.async_copy(y_hbm.at[idx_v.at[kk]], rows_v, sem).wait()
                pltpu.sync_copy(rows_v, yg_hbm.at[kk, pl.ds(win * SC_WINDOW, SC_WINDOW)])

    return run(y_rows, dest_win)


def _expert_kernel(be_ref, nu_ref, nx_ref, nv_ref, x_ref, wgu_hbm, bgu_ref, wd_hbm, bd_ref, y_ref,
                   wgu32, wd32, wgu16, wd16, sems):
    i = pl.program_id(0)
    e = be_ref[i]
    prev = be_ref[jnp.maximum(i - 1, 0)]

    def weight_copies(ex):
        return (pltpu.make_async_copy(wgu_hbm.at[ex], wgu32, sems.at[0]),
                pltpu.make_async_copy(wd_hbm.at[ex], wd32, sems.at[1]))

    @pl.when(i == 0)
    def _():
        for cp in weight_copies(e):
            cp.start()

    @pl.when((i == 0) | (e != prev))
    def _():
        for cp in weight_copies(e):
            cp.wait()
        rows = 128

        def cast(r, carry):
            sl = pl.ds(pl.multiple_of(r * rows, rows), rows)
            wgu16[sl, :] = wgu32[sl, :].astype(BF16)
            wd16[sl, :] = wd32[sl, :].astype(BF16)
            return carry

        lax.fori_loop(0, D_MODEL // rows, cast, 0)
        nxt = nx_ref[i]

        @pl.when(nxt >= 0)
        def _():
            for cp in weight_copies(nxt):
                cp.start()

    def run_rows(rows):
        x = _load_row_tiles(x_ref, rows).astype(BF16)
        bgu = bgu_ref[0]
        glu = _dot(x, wgu16[:, 0:D_FF]) + bgu[:, 0:D_FF]
        lin = _dot(x, wgu16[:, D_FF:]) + bgu[:, D_FF:]
        glu = jnp.minimum(glu, SWIGLU_LIMIT)
        lin = jnp.clip(lin, -SWIGLU_LIMIT, SWIGLU_LIMIT)
        act = glu * _sigmoid(SWIGLU_ALPHA * glu) * (lin + 1.0)
        y = _dot(act.astype(BF16), wd16[...]) + bd_ref[0]
        _store_row_tiles(y_ref, y.astype(BF16).astype(F32))

    valid = nv_ref[i]
    quarter = MOE_BLOCK // 4
    for rows in range(quarter, MOE_BLOCK + 1, quarter):
        pl.when((valid > rows - quarter) & (valid <= rows))(functools.partial(run_rows, rows))


def _experts(blk_e, n_used, next_e, blk_valid, xs2, w_gu, b_gu3, w_down, b_down3):
    row_blk = pl.BlockSpec((MOE_BLOCK * ROW_TILES, LANES),
                           lambda i, be, nu, nx, nv: (jnp.minimum(i, nu[0] - 1), 0))
    grid_spec = pltpu.PrefetchScalarGridSpec(
        num_scalar_prefetch=4,
        grid=(N_BLOCKS,),
        in_specs=[row_blk,
                  pl.BlockSpec(memory_space=pl.ANY),
                  pl.BlockSpec((1, 1, 2 * D_FF), lambda i, be, nu, nx, nv: (be[i], 0, 0)),
                  pl.BlockSpec(memory_space=pl.ANY),
                  pl.BlockSpec((1, 1, D_MODEL), lambda i, be, nu, nx, nv: (be[i], 0, 0))],
        out_specs=row_blk,
        scratch_shapes=[pltpu.VMEM((D_MODEL, 2 * D_FF), F32),
                        pltpu.VMEM((D_FF, D_MODEL), F32),
                        pltpu.VMEM((D_MODEL, 2 * D_FF), BF16),
                        pltpu.VMEM((D_FF, D_MODEL), BF16),
                        pltpu.SemaphoreType.DMA((2,))],
    )
    return pl.pallas_call(
        _expert_kernel,
        grid_spec=grid_spec,
        out_shape=jax.ShapeDtypeStruct((N_SLOTS * ROW_TILES, LANES), U32),
        compiler_params=pltpu.CompilerParams(dimension_semantics=("arbitrary",),
                                             vmem_limit_bytes=VMEM_LIMIT),
        name="experts",
    )(blk_e, n_used, next_e, blk_valid, xs2, w_gu, b_gu3, w_down, b_down3)


def _final_kernel(yg_ref, gate_ref, x1_ref, mod_ref, gpost_ref, *maybe_alias_and_out):
    o_ref = maybe_alias_and_out[-1]
    gates = gate_ref[...]
    y = None
    for kk in range(TOP_K):
        part = _load_row_tiles(yg_ref.at[kk], TOK_TILE) * gates[:, kk:kk + 1]
        y = part if y is None else y + part
    gate_f = mod_ref[0][5:6]
    o_ref[...] = x1_ref[...] + _rms(y) * (gate_f * gpost_ref[...])


def _final(part, yg3, gates, x1, mod3, g_post, out_so_far):
    tiles = SEQ // TOK_TILE
    tok = lambda w: pl.BlockSpec((TOK_TILE, w), lambda i: (part * tiles + i, 0))
    in_specs = [pl.BlockSpec((TOP_K, TOK_TILE * ROW_TILES, LANES), lambda i: (0, i, 0)),
                tok(TOP_K), tok(D_MODEL),
                pl.BlockSpec((1, 6, D_MODEL), lambda i: (part, 0, 0)),
                pl.BlockSpec((1, D_MODEL), lambda i: (0, 0))]
    args = [yg3, gates, x1, mod3, g_post]
    aliases = {}
    if out_so_far is not None:
        in_specs.append(pl.BlockSpec(memory_space=pl.ANY))
        args.append(out_so_far)
        aliases = {len(args) - 1: 0}
    return pl.pallas_call(
        _final_kernel,
        grid=(tiles,),
        in_specs=in_specs,
        out_specs=tok(D_MODEL),
        out_shape=jax.ShapeDtypeStruct((N_TOK, D_MODEL), F32),
        input_output_aliases=aliases,
        compiler_params=pltpu.CompilerParams(vmem_limit_bytes=VMEM_LIMIT),
        name="combine_final",
    )(*args)


def _routing_tables(idx, rank, counts):
    counts = counts.reshape(N_EXPERTS)
    experts = jnp.arange(N_EXPERTS, dtype=I32)
    padded = ((counts + MOE_BLOCK - 1) // MOE_BLOCK) * MOE_BLOCK
    pend = jnp.cumsum(padded)
    pstart = pend - padded
    dest = jnp.sum(jnp.where(idx[..., None] == experts, pstart, 0), axis=-1) + rank
    dest_win = dest.reshape(TOP_K, N_TOK // SC_WINDOW, SC_WINDOW).transpose(1, 0, 2)
    n_used = (pend[-1] // MOE_BLOCK).astype(I32).reshape(1)
    blk_start = jnp.arange(N_BLOCKS, dtype=I32) * MOE_BLOCK
    blk_e = jnp.sum(blk_start[:, None] >= pend[None, :], axis=1).astype(I32)
    last_e = jnp.max(jnp.where(counts > 0, experts, 0))
    blk_e = jnp.minimum(blk_e, last_e)
    later = (experts[None, :] > experts[:, None]) & (counts[None, :] > 0)
    next_nonempty = jnp.min(jnp.where(later, experts[None, :], N_EXPERTS), axis=1)
    next_nonempty = jnp.where(next_nonempty == N_EXPERTS, -1, next_nonempty).astype(I32)
    of_block = lambda table: jnp.sum(jnp.where(blk_e[:, None] == experts, table, 0), axis=1).astype(I32)
    next_e = of_block(next_nonempty)
    blk_valid = jnp.clip(of_block(counts) - (blk_start - of_block(pstart)), 0, MOE_BLOCK)
    blk_valid = jnp.where(blk_start < pend[-1], blk_valid, 0).astype(I32)
    return dest_win, blk_e, n_used, next_e, blk_valid


def kernel(x, c, w_ada, b_ada, g_pre_mix, g_post_mix, w_in, attn_norm_w, hgrn_lb, hgrn_norm_w,
           w_out, g_pre_ffn, g_post_ffn, w_router, b_router, w_gu, b_gu, w_down, b_down):
    c_pad = jnp.pad(c, ((0, SUBLANES - BATCH), (0, 0)))
    mod = _ada_mod(c_pad, w_ada[0], b_ada)
    mod3 = mod[:BATCH].reshape(BATCH, 6, D_MODEL)

    x2 = x.reshape(N_TOK, D_MODEL)
    q4, k4, v4, q16, k16, v16, qr, fr, ir, gr = _inproj(x2, mod3, g_pre_mix, w_in[0].astype(BF16))
    nat = lambda t: t.reshape(BATCH, SEQ, D_REC)
    ya = _attention(q4, k4, v4, q16, k16, v16, attn_norm_w)
    yr = _hgrn(nat(qr), nat(fr), nat(ir), nat(gr), hgrn_lb, hgrn_norm_w)

    x1, h2, idx, gates, rank, counts = _mid(
        ya.reshape(N_TOK, D_ATTN), yr.reshape(N_TOK, D_REC), x2, mod3, g_post_mix, g_pre_ffn,
        w_out[0].astype(BF16), w_router[0], b_router)

    dest_win, blk_e, n_used, next_e, blk_valid = _routing_tables(idx, rank, counts)
    xs = _sc_dispatch(h2.reshape(N_TOK, ROW_TILES, LANES), dest_win)
    ys = _experts(blk_e, n_used, next_e, blk_valid, xs.reshape(N_SLOTS * ROW_TILES, LANES),
                  w_gu[0], b_gu[0].reshape(N_EXPERTS, 1, 2 * D_FF),
                  w_down[0], b_down[0].reshape(N_EXPERTS, 1, D_MODEL))
    ys3 = ys.reshape(N_SLOTS, ROW_TILES, LANES)
    win_per_seq = SEQ // SC_WINDOW
    out = None
    for b in range(BATCH):
        yg = _sc_collect(ys3, dest_win[b * win_per_seq:(b + 1) * win_per_seq])
        out = _final(b, yg.reshape(TOP_K, SEQ * ROW_TILES, LANES), gates, x1, mod3, g_post_ffn, out)
    return out.reshape(BATCH, SEQ, D_MODEL)
```

```python
import functools
import math

import jax
import jax.numpy as jnp
from jax import lax
from jax.experimental import pallas as pl
from jax.experimental.pallas import tpu as pltpu
from jax.experimental.pallas import tpu_sc as plsc

F32 = jnp.float32
BF16 = jnp.bfloat16
I32 = jnp.int32
U32 = jnp.uint32

D_MODEL = 1024
BATCH = 4
SEQ = 4096
N_TOK = BATCH * SEQ
D_ATTN = 512
HEAD_DIM_A = 64
ATT_BLOCK = 128
DIL_MID = 4
DIL_MAX = 16
SUB_MID = SEQ // DIL_MID
SUB_MAX = SEQ // DIL_MAX
D_REC = 512
HGRN_CHUNK = 32
HGRN_SUPER = 256
N_EXPERTS = 32
TOP_K = 4
D_FF = 1024
SWIGLU_LIMIT = 7.0
SWIGLU_ALPHA = 1.702
EPS = 1e-6
NEG_BIG = -1e30
Q_SCALE = HEAD_DIM_A ** -0.5 * math.log2(math.e)

LANES = 128
SUBLANES = 8
ROW_TILES = D_MODEL // 2 // LANES

TOK_TILE = 512
MOE_BLOCK = 1024
MOE_ROWS_STEP = 128
N_SLOTS = N_TOK * TOP_K + N_EXPERTS * MOE_BLOCK
N_BLOCKS = N_SLOTS // MOE_BLOCK
SC_WINDOW = 64
VMEM_LIMIT = 56 * 1024 * 1024


def _sigmoid(x):
    return 1.0 / (1.0 + jnp.exp(-x))


def _dot(a, b):
    return jnp.dot(a, b, preferred_element_type=F32)


def _dot_nt(a, b):
    return lax.dot_general(a, b, (((1,), (1,)), ((), ())), preferred_element_type=F32)


def _split_bf16(x):
    hi = x.astype(BF16)
    return hi, (x - hi.astype(F32)).astype(BF16)


def _rms(x):
    return x * lax.rsqrt(jnp.mean(x * x, axis=-1, keepdims=True) + EPS)


def _ada_kernel(c_ref, w_ref, b_ref, o_ref):
    c = c_ref[...]
    cond = c * _sigmoid(c)
    c_hi, c_lo = _split_bf16(cond)
    w_hi, w_lo = _split_bf16(w_ref[...])
    o_ref[...] = _dot(c_hi, w_hi) + _dot(c_lo, w_hi) + _dot(c_hi, w_lo) + b_ref[...]


def _ada_mod(c_pad, w_ada, b_ada):
    n = w_ada.shape[1]
    tn = 1536
    return pl.pallas_call(
        _ada_kernel,
        grid=(n // tn,),
        in_specs=[pl.BlockSpec((SUBLANES, D_MODEL), lambda j: (0, 0)),
                  pl.BlockSpec((D_MODEL, tn), lambda j: (0, j)),
                  pl.BlockSpec((1, tn), lambda j: (0, j))],
        out_specs=pl.BlockSpec((SUBLANES, tn), lambda j: (0, j)),
        out_shape=jax.ShapeDtypeStruct((SUBLANES, n), F32),
        compiler_params=pltpu.CompilerParams(vmem_limit_bytes=VMEM_LIMIT),
        name="ada_mod",
    )(c_pad, w_ada, b_ada)


def _inproj_kernel(x_ref, mod_ref, g_ref, w_ref, q4, k4, v4, q16, k16, v16, qr, fr, ir, gr,
                   stage_nat, stage_mid):
    mod = mod_ref[0]
    shift, scale = mod[0:1], mod[1:2]
    h = _rms(x_ref[...]) * (g_ref[...] * (1.0 + scale)) + shift
    hb = h.astype(BF16)
    slabs = D_ATTN // LANES
    rows_mid = TOK_TILE // DIL_MID
    rows_max = TOK_TILE // DIL_MAX

    def proj(j):
        return _dot(hb, w_ref[:, j * D_ATTN:(j + 1) * D_ATTN])

    for j, (o_mid, o_max) in enumerate(((q4, q16), (k4, k16), (v4, v16))):
        r = proj(j)
        if j == 0:
            r = r * Q_SCALE
        for cs in range(slabs):
            stage_nat[cs] = r[:, cs * LANES:(cs + 1) * LANES]
        for cs in range(slabs):
            lanes = slice(cs * LANES, (cs + 1) * LANES)
            for sub in range(DIL_MID):
                piece = stage_nat[cs, pl.ds(sub, rows_mid, stride=DIL_MID), :]
                o_mid[sub, :, lanes] = piece.astype(BF16)
                stage_mid[cs, sub] = piece
            for sub in range(DIL_MAX):
                piece = stage_mid[cs, sub % DIL_MID, pl.ds(sub // DIL_MID, rows_max, stride=DIL_MID), :]
                o_max[sub, :, lanes] = piece.astype(BF16)
    for j, o_ref in enumerate((qr, fr, ir, gr)):
        o_ref[...] = proj(3 + j)


def _inproj(x2, mod3, g_pre, w_in_bf16):
    tiles_per_seq = SEQ // TOK_TILE
    rows_mid = TOK_TILE // DIL_MID
    rows_max = TOK_TILE // DIL_MAX
    mid = pl.BlockSpec((None, DIL_MID, rows_mid, D_ATTN),
                       lambda i: (i // tiles_per_seq, 0, i % tiles_per_seq, 0))
    mx = pl.BlockSpec((None, DIL_MAX, rows_max, D_ATTN),
                      lambda i: (i // tiles_per_seq, 0, i % tiles_per_seq, 0))
    nat = pl.BlockSpec((TOK_TILE, D_REC), lambda i: (i, 0))
    mid_shape = jax.ShapeDtypeStruct((BATCH, DIL_MID, SUB_MID, D_ATTN), BF16)
    mx_shape = jax.ShapeDtypeStruct((BATCH, DIL_MAX, SUB_MAX, D_ATTN), BF16)
    nat_shape = jax.ShapeDtypeStruct((N_TOK, D_REC), F32)
    return pl.pallas_call(
        _inproj_kernel,
        grid=(N_TOK // TOK_TILE,),
        in_specs=[pl.BlockSpec((TOK_TILE, D_MODEL), lambda i: (i, 0)),
                  pl.BlockSpec((1, 6, D_MODEL), lambda i: (i // tiles_per_seq, 0, 0)),
                  pl.BlockSpec((1, D_MODEL), lambda i: (0, 0)),
                  pl.BlockSpec(w_in_bf16.shape, lambda i: (0, 0))],
        out_specs=[mid, mid, mid, mx, mx, mx, nat, nat, nat, nat],
        out_shape=[mid_shape] * 3 + [mx_shape] * 3 + [nat_shape] * 4,
        scratch_shapes=[pltpu.VMEM((D_ATTN // LANES, TOK_TILE, LANES), F32),
                        pltpu.VMEM((D_ATTN // LANES, DIL_MID, rows_mid, LANES), F32)],
        compiler_params=pltpu.CompilerParams(vmem_limit_bytes=VMEM_LIMIT),
        name="inproj",
    )(x2, mod3, g_pre, w_in_bf16)


def _attn_kernel(q4_ref, k4_ref, v4_ref, q16_ref, k16_ref, v16_ref, nw_ref, o_ref,
                 sel_scr, keep_scr, o_scr, m_scr, l_scr):
    bw = ATT_BLOCK
    lane = lax.broadcasted_iota(I32, (bw, LANES), 1)
    head0 = lane < HEAD_DIM_A
    head_masks = (jnp.where(head0, 1.0, 0.0).astype(BF16), jnp.where(head0, 0.0, 1.0).astype(BF16))

    rr = lax.broadcasted_iota(I32, (bw, bw), 0)
    cc = lax.broadcasted_iota(I32, (bw, bw), 1)
    piece = bw // DIL_MID
    rp = DIL_MID * (rr % piece) + rr // piece
    cp = DIL_MID * (cc % piece) + cc // piece
    for var, (r, c) in enumerate(((rr, cc), (rr, cc), (rp, cp), (rp, cp))):
        first = var % 2 == 1
        left = (c < r) if first else (c >= r)
        right = jnp.zeros_like(left) if first else jnp.logical_not(left)
        sel_scr[var] = jnp.where(left, 1.0, 0.0)
        keep_scr[var, 0] = jnp.where(left, 1.0, 0.0).astype(BF16)
        keep_scr[var, 1] = jnp.where(right, 1.0, 0.0).astype(BF16)

    def score_matmuls(loaded):
        return [_dot_nt(q * hm, k) for (q, k, _, _, _) in loaded for hm in head_masks]

    def finish_group(scores, loaded):
        probs = []
        for n, s in enumerate(scores):
            _, _, _, var, right_bias = loaded[n // 2]
            folded = jnp.where(sel_scr[var] > 0.5, s[:, 0:bw], s[:, bw:] + right_bias)
            m = jnp.max(folded, axis=-1, keepdims=True)
            p = jnp.exp2(folded - m)
            pb = p.astype(BF16)
            spread = jnp.concatenate([pb * keep_scr[var, 0], pb * keep_scr[var, 1]], axis=1)
            probs.append((spread, jnp.sum(p, axis=-1, keepdims=True), m))
        pvs = [_dot(p, loaded[n // 2][2]) for n, (p, _, _) in enumerate(probs)]
        results = []
        for u in range(len(loaded)):
            (_, l0, m0), (_, l1, m1) = probs[2 * u], probs[2 * u + 1]
            results.append((jnp.where(head0, pvs[2 * u], pvs[2 * u + 1]),
                            jnp.where(head0, jnp.broadcast_to(m0, (bw, LANES)),
                                      jnp.broadcast_to(m1, (bw, LANES))),
                            jnp.where(head0, jnp.broadcast_to(l0, (bw, LANES)),
                                      jnp.broadcast_to(l1, (bw, LANES)))))
        return results

    group = 8

    def first_block_bias(blk):
        return jnp.where(blk == 0, NEG_BIG, 0.0)

    def load_d1(g):
        loaded, starts = [], []
        for i in range(group):
            blk = g * group + i
            qs = pl.multiple_of(blk * piece, piece)
            ks = pl.multiple_of(jnp.maximum(blk - 1, 0) * piece, piece)
            q = jnp.concatenate([q4_ref[r, pl.ds(qs, piece), :] for r in range(DIL_MID)], axis=0)
            k = jnp.concatenate([k4_ref[r, pl.ds(ks + half * piece, piece), :]
                                 for half in range(2) for r in range(DIL_MID)], axis=0)
            v = jnp.concatenate([v4_ref[r, pl.ds(ks + half * piece, piece), :]
                                 for half in range(2) for r in range(DIL_MID)], axis=0)
            loaded.append((q, k, v, jnp.where(blk == 0, 3, 2), first_block_bias(blk)))
            starts.append(qs)

        def store(results):
            for qs, parts in zip(starts, results):
                for r in range(DIL_MID):
                    for scr, val in zip((o_scr, m_scr, l_scr), parts):
                        scr[0, r, pl.ds(qs, piece), :] = val[r * piece:(r + 1) * piece]

        return loaded, store

    def load_d4(g):
        loaded, dsts = [], []
        for i in range(group):
            r, blk = i % DIL_MID, g * (group // DIL_MID) + i // DIL_MID
            qs = pl.multiple_of(blk * bw, bw)
            ks = pl.multiple_of(jnp.maximum(blk - 1, 0) * bw, bw)
            loaded.append((q4_ref[r, pl.ds(qs, bw), :], k4_ref[r, pl.ds(ks, 2 * bw), :],
                           v4_ref[r, pl.ds(ks, 2 * bw), :], jnp.where(blk == 0, 1, 0),
                           first_block_bias(blk)))
            dsts.append((r, qs))

        def store(results):
            for (r, qs), parts in zip(dsts, results):
                for scr, val in zip((o_scr, m_scr, l_scr), parts):
                    scr[1, r, pl.ds(qs, bw), :] = val

        return loaded, store

    def load_d16(g):
        loaded, dsts = [], []
        for i in range(group):
            r, blk = g * (group // 2) + i // 2, i % 2
            loaded.append((q16_ref[r, blk * bw:(blk + 1) * bw, :], k16_ref[r], v16_ref[r],
                           1 - blk, NEG_BIG if blk == 0 else 0.0))
            dsts.append((r % DIL_MID, pl.ds(blk * (bw * DIL_MID) + r // DIL_MID, bw, stride=DIL_MID)))

        def store(results):
            for (sub, dst), parts in zip(dsts, results):
                for scr, val in zip((o_scr, m_scr, l_scr), parts):
                    scr[2, sub, dst, :] = val

        return loaded, store

    def run_group(load):
        def step(g, carry):
            loaded, store = load(g)
            store(finish_group(score_matmuls(loaded), loaded))
            return carry
        return step

    n_groups = SEQ // bw // group
    for load in (load_d1, load_d4, load_d16):
        lax.fori_loop(0, n_groups, run_group(load), 0)

    rows = 256
    hi = lax.broadcasted_iota(I32, (LANES, LANES), 0) // HEAD_DIM_A
    hj = lax.broadcasted_iota(I32, (LANES, LANES), 1) // HEAD_DIM_A
    head_sum = jnp.where(hi == hj, 1.0, 0.0).astype(BF16)
    nw = nw_ref[...]

    def head_sums(x):
        x_hi, x_lo = _split_bf16(x)
        return _dot(x_hi, head_sum) + _dot(x_lo, head_sum)

    def merge(t, carry):
        r = t // (SUB_MID // rows)
        start = pl.multiple_of((t % (SUB_MID // rows)) * rows, rows)
        sl = pl.ds(start, rows)
        own_score = head_sums(q4_ref[r, sl, :].astype(F32) * k4_ref[r, sl, :].astype(F32))
        ms = [m_scr[n, r, sl, :] for n in range(3)]
        mx = jnp.maximum(jnp.maximum(jnp.maximum(ms[0], ms[1]), ms[2]), own_score)
        ws = [jnp.exp2(m - mx) for m in ms]
        w_own = float(len(ms)) * jnp.exp2(own_score - mx)
        num = (ws[0] * o_scr[0, r, sl, :] + ws[1] * o_scr[1, r, sl, :] + ws[2] * o_scr[2, r, sl, :]
               + w_own * v4_ref[r, sl, :].astype(F32))
        den = ws[0] * l_scr[0, r, sl, :] + ws[1] * l_scr[1, r, sl, :] + ws[2] * l_scr[2, r, sl, :] + w_own
        o = num / den
        mean_sq = head_sums(o * o) * (1.0 / HEAD_DIM_A)
        o_ref[pl.ds(start * DIL_MID + r, rows, stride=DIL_MID), :] = o * lax.rsqrt(mean_sq + EPS) * nw
        return carry

    lax.fori_loop(0, SEQ // rows, merge, 0, unroll=8)


def _attention(q4, k4, v4, q16, k16, v16, attn_norm_w):
    hp = D_ATTN // LANES
    mid = pl.BlockSpec((None, DIL_MID, SUB_MID, LANES), lambda b, h: (b, 0, 0, h))
    mx = pl.BlockSpec((None, DIL_MAX, SUB_MAX, LANES), lambda b, h: (b, 0, 0, h))
    scr = pltpu.VMEM((3, DIL_MID, SUB_MID, LANES), F32)
    return pl.pallas_call(
        _attn_kernel,
        grid=(BATCH, hp),
        in_specs=[mid, mid, mid, mx, mx, mx, pl.BlockSpec((1, LANES), lambda b, h: (0, h))],
        out_specs=pl.BlockSpec((None, SEQ, LANES), lambda b, h: (b, 0, h)),
        out_shape=jax.ShapeDtypeStruct((BATCH, SEQ, D_ATTN), F32),
        scratch_shapes=[pltpu.VMEM((4, ATT_BLOCK, ATT_BLOCK), F32),
                        pltpu.VMEM((4, 2, ATT_BLOCK, ATT_BLOCK), BF16), scr, scr, scr],
        compiler_params=pltpu.CompilerParams(vmem_limit_bytes=VMEM_LIMIT),
        name="dilated_attn",
    )(q4, k4, v4, q16, k16, v16, attn_norm_w)


def _hgrn_kernel(qr_ref, fr_ref, ir_ref, gr_ref, lb_ref, nw_ref, o_ref,
                 qe_scr, oi_scr, delta_scr, dec_scr, st_scr):
    sup, c = HGRN_SUPER, HGRN_CHUNK
    nch = sup // c
    n_sup = SEQ // sup
    lbp = lb_ref[...]
    lmx = jnp.max(lbp, axis=0, keepdims=True)
    ex = jnp.exp(lbp - lmx)
    lb = ex[0:1] / (ex[0:1] + ex[1:2])
    nw = nw_ref[...]

    ri = lax.broadcasted_iota(I32, (sup, sup), 0)
    ci = lax.broadcasted_iota(I32, (sup, sup), 1)
    same_chunk = (ri // c) == (ci // c)
    causal = same_chunk & (ci <= ri)
    cum_op = jnp.where(causal, 1.0, 0.0).astype(BF16)

    def chunk_rows(kd, ch):
        parts = []
        if ch > 0:
            parts.append(jnp.zeros((ch * c, LANES), BF16))
        parts.append(kd[ch * c:(ch + 1) * c])
        if ch < nch - 1:
            parts.append(jnp.zeros(((nch - 1 - ch) * c, LANES), BF16))
        return jnp.concatenate(parts, axis=0)

    group = 4

    def independent(g, carry):
        ts = [g * group + i for i in range(group)]
        sls = [pl.ds(pl.multiple_of(t * sup, sup), sup) for t in ts]
        pre = []
        for sl in sls:
            f = lb + (1.0 - lb) * _sigmoid(fr_ref[sl, :])
            logf_hi, logf_lo = _split_bf16(jnp.log(f))
            pre.append((1.0 - f, _dot(cum_op, logf_hi) + _dot(cum_op, logf_lo)))
        mid = []
        for t, sl, (kk, b) in zip(ts, sls, pre):
            b_last = jnp.concatenate(
                [jnp.broadcast_to(b[(ch + 1) * c - 1:(ch + 1) * c], (c, LANES)) for ch in range(nch)],
                axis=0)
            q = qr_ref[sl, :]
            qeb = (q * _sigmoid(q) * jnp.exp(b)).astype(BF16)
            ke = (kk * jnp.exp(-b)).astype(BF16)
            kd = (kk * jnp.exp(b_last - b)).astype(BF16)
            qe_scr[sl, :] = qeb
            dec_rows = jnp.concatenate([b_last[ch * c:ch * c + 1] for ch in range(nch)], axis=0)
            dec_scr[pl.ds(pl.multiple_of(t * nch, nch), nch), :] = jnp.exp(dec_rows)
            v = ir_ref[sl, :]
            vt = v.T.astype(BF16)
            scores = _dot_nt(qeb, ke)
            for pair in range(nch // 2):
                rhs = jnp.concatenate([chunk_rows(kd, 2 * pair), chunk_rows(kd, 2 * pair + 1)], axis=1)
                d2 = _dot(vt, rhs)
                delta_scr[t * nch + 2 * pair] = d2[:, 0:LANES]
                delta_scr[t * nch + 2 * pair + 1] = d2[:, LANES:]
            mid.append((scores, v.astype(BF16)))
        for sl, (scores, vb) in zip(sls, mid):
            a = jnp.where(causal, scores, 0.0)
            oi_scr[sl, :] = _dot(a.astype(BF16), vb)
        return carry

    lax.fori_loop(0, n_sup // group, independent, 0)

    def recur(ch, st):
        st_scr[ch] = st.astype(BF16)
        return st * dec_scr[pl.ds(ch, 1), :] + delta_scr[ch]

    lax.fori_loop(0, SEQ // c, recur, jnp.zeros((LANES, LANES), F32), unroll=8)

    def finish(t, carry):
        sl = pl.ds(pl.multiple_of(t * sup, sup), sup)
        parts = [_dot_nt(qe_scr[pl.ds(pl.multiple_of(t * sup + ch * c, c), c), :], st_scr[t * nch + ch])
                 for ch in range(nch)]
        o = oi_scr[sl, :] + jnp.concatenate(parts, axis=0)
        g = gr_ref[sl, :]
        o_ref[sl, :] = (_rms(o) * nw * (g * _sigmoid(g))).astype(o_ref.dtype)
        return carry

    lax.fori_loop(0, n_sup, finish, 0, unroll=8)


def _hgrn(qr, fr, ir, gr, hgrn_lb, hgrn_norm_w):
    nh = D_REC // LANES
    n_chunks = SEQ // HGRN_CHUNK
    blk = pl.BlockSpec((None, SEQ, LANES), lambda b, h: (b, 0, h))
    return pl.pallas_call(
        _hgrn_kernel,
        grid=(BATCH, nh),
        in_specs=[blk, blk, blk, blk,
                  pl.BlockSpec((2, LANES), lambda b, h: (0, h)),
                  pl.BlockSpec((1, LANES), lambda b, h: (0, h))],
        out_specs=blk,
        out_shape=jax.ShapeDtypeStruct((BATCH, SEQ, D_REC), BF16),
        scratch_shapes=[pltpu.VMEM((SEQ, LANES), BF16),
                        pltpu.VMEM((SEQ, LANES), F32),
                        pltpu.VMEM((n_chunks, LANES, LANES), F32),
                        pltpu.VMEM((n_chunks, LANES), F32),
                        pltpu.VMEM((n_chunks, LANES, LANES), BF16)],
        compiler_params=pltpu.CompilerParams(vmem_limit_bytes=VMEM_LIMIT),
        name="hgrn2",
    )(qr, fr, ir, gr, hgrn_lb, hgrn_norm_w)


def _tile_batch(i):
    return i // (SEQ // TOK_TILE)


def _store_row_tiles(ref, val):
    rows, half = val.shape[0], D_MODEL // 2
    lo = lax.bitcast_convert_type(val[:, :half], U32)
    hi = lax.bitcast_convert_type(val[:, half:], U32)
    words = lax.shift_right_logical(lo, jnp.uint32(16)) | (hi & jnp.uint32(0xFFFF0000))
    for j in range(ROW_TILES):
        ref[pl.ds(j, rows, stride=ROW_TILES), :] = words[:, j * LANES:(j + 1) * LANES]


def _load_row_tiles(ref, rows):
    words = jnp.concatenate(
        [ref[pl.ds(j, rows, stride=ROW_TILES), :] for j in range(ROW_TILES)], axis=1)
    lo = lax.bitcast_convert_type(lax.shift_left(words, jnp.uint32(16)), F32)
    hi = lax.bitcast_convert_type(words & jnp.uint32(0xFFFF0000), F32)
    return jnp.concatenate([lo, hi], axis=1)


def _mid_kernel(ya_ref, yr_ref, x_ref, mod_ref, gpost_ref, gpre_ref, wo_ref, wr_ref, br_ref,
                x1_ref, h2_ref, idx_ref, gate_ref, rank_ref, cnt_ref, carry_ref):
    i = pl.program_id(0)

    @pl.when(i == 0)
    def _():
        carry_ref[...] = jnp.zeros_like(carry_ref)

    mod = mod_ref[0]
    gate_m, shift_f, scale_f = mod[2:3], mod[3:4], mod[4:5]
    y = _dot(ya_ref[...].astype(BF16), wo_ref[0:D_ATTN, :]) + _dot(yr_ref[...], wo_ref[D_ATTN:, :])
    x1 = x_ref[...] + _rms(y) * (gate_m * gpost_ref[...])
    x1_ref[...] = x1
    h2 = _rms(x1) * (gpre_ref[...] * (1.0 + scale_f)) + shift_f
    h2_hi = h2.astype(BF16)
    h2_rounded = h2_hi.astype(F32)
    _store_row_tiles(h2_ref, h2_rounded)

    tm = h2.shape[0]
    h2_lo = (h2 - h2_rounded).astype(BF16)
    wr_hi, wr_lo = _split_bf16(wr_ref[...])
    parts = jnp.concatenate([_dot(h2_hi, jnp.concatenate([wr_hi, wr_lo], axis=1)),
                             _dot(h2_lo, wr_hi),
                             jnp.broadcast_to(br_ref[...], (tm, N_EXPERTS))], axis=1).T
    work = sum(parts[g * N_EXPERTS:(g + 1) * N_EXPERTS] for g in range(LANES // N_EXPERTS))
    eidx = lax.broadcasted_iota(I32, (N_EXPERTS, tm), 0).astype(F32)
    vals, idxs = [], []
    onehot = jnp.zeros((N_EXPERTS, tm), F32)
    for _ in range(TOP_K):
        m = jnp.max(work, axis=0, keepdims=True)
        sel = jnp.min(jnp.where(work == m, eidx, float(N_EXPERTS)), axis=0, keepdims=True)
        hit = eidx == sel
        work = jnp.where(hit, -jnp.inf, work)
        onehot = jnp.where(hit, 1.0, onehot)
        vals.append(m)
        idxs.append(sel)
    ex = [jnp.exp(vv - vals[0]) for vv in vals]
    inv_den = 1.0 / (ex[0] + ex[1] + ex[2] + ex[3])

    ri = lax.broadcasted_iota(I32, (tm, tm), 0)
    ci = lax.broadcasted_iota(I32, (tm, tm), 1)
    strict_upper = jnp.where(ri < ci, 1.0, 0.0).astype(BF16)
    before = _dot(onehot.astype(BF16), strict_upper) + carry_ref[...]
    ranks = [jnp.sum(jnp.where(eidx == idxs[kk], before, 0.0), axis=0, keepdims=True)
             for kk in range(TOP_K)]
    idx_ref[...] = jnp.concatenate(idxs, axis=0).astype(I32)
    rank_ref[...] = jnp.concatenate(ranks, axis=0).astype(I32)
    gates_t = jnp.concatenate([e * inv_den for e in ex]
                              + [jnp.zeros((LANES - TOP_K, tm), F32)], axis=0)
    gate_ref[...] = gates_t.T[:, 0:TOP_K]
    total = carry_ref[...] + jnp.sum(onehot, axis=1, keepdims=True)
    carry_ref[...] = total
    cnt_ref[...] = total.astype(I32)


def _mid(ya, yr, x2, mod3, g_post, g_pre, w_out_bf16, w_router, b_router):
    tok = lambda w: pl.BlockSpec((TOK_TILE, w), lambda i: (i, 0))
    const = lambda s: pl.BlockSpec(s, lambda i: (0,) * len(s))
    lanes_tok = pl.BlockSpec((TOP_K, TOK_TILE), lambda i: (0, i))
    return pl.pallas_call(
        _mid_kernel,
        grid=(N_TOK // TOK_TILE,),
        in_specs=[tok(D_ATTN), tok(D_REC), tok(D_MODEL),
                  pl.BlockSpec((1, 6, D_MODEL), lambda i: (_tile_batch(i), 0, 0)),
                  const((1, D_MODEL)), const((1, D_MODEL)),
                  const((D_MODEL, D_MODEL)), const((D_MODEL, N_EXPERTS)), const((1, N_EXPERTS))],
        out_specs=[tok(D_MODEL),
                   pl.BlockSpec((TOK_TILE * ROW_TILES, LANES), lambda i: (i, 0)),
                   lanes_tok, tok(TOP_K), lanes_tok, const((N_EXPERTS, 1))],
        out_shape=[jax.ShapeDtypeStruct((N_TOK, D_MODEL), F32),
                   jax.ShapeDtypeStruct((N_TOK * ROW_TILES, LANES), U32),
                   jax.ShapeDtypeStruct((TOP_K, N_TOK), I32),
                   jax.ShapeDtypeStruct((N_TOK, TOP_K), F32),
                   jax.ShapeDtypeStruct((TOP_K, N_TOK), I32),
                   jax.ShapeDtypeStruct((N_EXPERTS, 1), I32)],
        scratch_shapes=[pltpu.VMEM((N_EXPERTS, 1), F32)],
        compiler_params=pltpu.CompilerParams(dimension_semantics=("arbitrary",),
                                             vmem_limit_bytes=VMEM_LIMIT),
        name="outproj_router",
    )(ya, yr, x2, mod3, g_post, g_pre, w_out_bf16, w_router, b_router)


def _sc_mesh():
    return plsc.VectorSubcoreMesh(core_axis_name="c", subcore_axis_name="s")


def _sc_worker_count():
    info = plsc.get_sparse_core_info()
    return info.num_cores, info.num_cores * info.num_subcores


def _sc_dispatch(h_rows, dest_win):
    n_cores, n_workers = _sc_worker_count()
    n_win = N_TOK // SC_WINDOW
    per_worker = n_win // n_workers

    @functools.partial(
        pl.kernel, mesh=_sc_mesh(),
        out_type=jax.ShapeDtypeStruct((N_SLOTS, ROW_TILES, LANES), U32),
        scratch_types=[pltpu.VMEM((TOP_K, SC_WINDOW), I32),
                       pltpu.VMEM((SC_WINDOW, ROW_TILES, LANES), U32),
                       pltpu.SemaphoreType.DMA],
        name="sc_dispatch")
    def run(h_hbm, dest_hbm, xs_hbm, idx_v, rows_v, sem):
        wid = lax.axis_index("s") * n_cores + lax.axis_index("c")

        @pl.loop(0, per_worker)
        def _(j):
            win = wid * per_worker + j
            pltpu.sync_copy(dest_hbm.at[win], idx_v)
            pltpu.sync_copy(h_hbm.at[pl.ds(win * SC_WINDOW, SC_WINDOW)], rows_v)
            copies = [pltpu.async_copy(rows_v, xs_hbm.at[idx_v.at[kk]], sem)
                      for kk in range(TOP_K)]
            for cp in copies:
                cp.wait()

    return run(h_rows, dest_win)


def _sc_collect(y_rows, dest_win):
    n_cores, n_workers = _sc_worker_count()
    n_win = dest_win.shape[0]
    per_worker = n_win // n_workers

    @functools.partial(
        pl.kernel, mesh=_sc_mesh(),
        out_type=jax.ShapeDtypeStruct((TOP_K, n_win * SC_WINDOW, ROW_TILES, LANES), U32),
        scratch_types=[pltpu.VMEM((TOP_K, SC_WINDOW), I32),
                       pltpu.VMEM((SC_WINDOW, ROW_TILES, LANES), U32),
                       pltpu.SemaphoreType.DMA],
        name="sc_collect")
    def run(y_hbm, dest_hbm, yg_hbm, idx_v, rows_v, sem):
        wid = lax.axis_index("s") * n_cores + lax.axis_index("c")

        @pl.loop(0, per_worker)
        def _(j):
            win = wid * per_worker + j
            pltpu.sync_copy(dest_hbm.at[win], idx_v)
            for kk in range(TOP_K):
                pltpu.async_copy(y_hbm.at[idx_v.at[kk]], rows_v, sem).wait()
                pltpu.sync_copy(rows_v, yg_hbm.at[kk, pl.ds(win * SC_WINDOW, SC_WINDOW)])

    return run(y_rows, dest_win)


def _expert_kernel(be_ref, nu_ref, nx_ref, nv_ref, x_ref, wgu_hbm, bgu_ref, wd_hbm, bd_ref, y_ref,
                   wgu32, wd32, wgu16, wd16, sems):
    i = pl.program_id(0)
    e = be_ref[i]
    prev = be_ref[jnp.maximum(i - 1, 0)]

    def weight_copies(ex):
        return (pltpu.make_async_copy(wgu_hbm.at[ex], wgu32, sems.at[0]),
                pltpu.make_async_copy(wd_hbm.at[ex], wd32, sems.at[1]))

    @pl.when(i == 0)
    def _():
        for cp in weight_copies(e):
            cp.start()

    @pl.when((i == 0) | (e != prev))
    def _():
        for cp in weight_copies(e):
            cp.wait()
        rows = 128

        def cast(r, carry):
            sl = pl.ds(pl.multiple_of(r * rows, rows), rows)
            wgu16[sl, :] = wgu32[sl, :].astype(BF16)
            wd16[sl, :] = wd32[sl, :].astype(BF16)
            return carry

        lax.fori_loop(0, D_MODEL // rows, cast, 0)
        nxt = nx_ref[i]

        @pl.when(nxt >= 0)
        def _():
            for cp in weight_copies(nxt):
                cp.start()

    def run_rows(rows):
        x = _load_row_tiles(x_ref, rows).astype(BF16)
        bgu = bgu_ref[0]
        glu = _dot(x, wgu16[:, 0:D_FF]) + bgu[:, 0:D_FF]
        lin = _dot(x, wgu16[:, D_FF:]) + bgu[:, D_FF:]
        glu = jnp.minimum(glu, SWIGLU_LIMIT)
        lin = jnp.clip(lin, -SWIGLU_LIMIT, SWIGLU_LIMIT)
        act = glu * _sigmoid(SWIGLU_ALPHA * glu) * (lin + 1.0)
        y = _dot(act.astype(BF16), wd16[...]) + bd_ref[0]
        _store_row_tiles(y_ref, y.astype(BF16).astype(F32))

    valid = nv_ref[i]
    for rows in range(MOE_ROWS_STEP, MOE_BLOCK + 1, MOE_ROWS_STEP):
        pl.when((valid > rows - MOE_ROWS_STEP) & (valid <= rows))(functools.partial(run_rows, rows))


def _experts(blk_e, n_used, next_e, blk_valid, xs2, w_gu, b_gu3, w_down, b_down3):
    row_blk = pl.BlockSpec((MOE_BLOCK * ROW_TILES, LANES),
                           lambda i, be, nu, nx, nv: (jnp.minimum(i, nu[0] - 1), 0))
    grid_spec = pltpu.PrefetchScalarGridSpec(
        num_scalar_prefetch=4,
        grid=(N_BLOCKS,),
        in_specs=[row_blk,
                  pl.BlockSpec(memory_space=pl.ANY),
                  pl.BlockSpec((1, 1, 2 * D_FF), lambda i, be, nu, nx, nv: (be[i], 0, 0)),
                  pl.BlockSpec(memory_space=pl.ANY),
                  pl.BlockSpec((1, 1, D_MODEL), lambda i, be, nu, nx, nv: (be[i], 0, 0))],
        out_specs=row_blk,
        scratch_shapes=[pltpu.VMEM((D_MODEL, 2 * D_FF), F32),
                        pltpu.VMEM((D_FF, D_MODEL), F32),
                        pltpu.VMEM((D_MODEL, 2 * D_FF), BF16),
                        pltpu.VMEM((D_FF, D_MODEL), BF16),
                        pltpu.SemaphoreType.DMA((2,))],
    )
    return pl.pallas_call(
        _expert_kernel,
        grid_spec=grid_spec,
        out_shape=jax.ShapeDtypeStruct((N_SLOTS * ROW_TILES, LANES), U32),
        compiler_params=pltpu.CompilerParams(dimension_semantics=("arbitrary",),
                                             vmem_limit_bytes=VMEM_LIMIT),
        name="experts",
    )(blk_e, n_used, next_e, blk_valid, xs2, w_gu, b_gu3, w_down, b_down3)


def _final_kernel(yg_ref, gate_ref, x1_ref, mod_ref, gpost_ref, *maybe_alias_and_out):
    o_ref = maybe_alias_and_out[-1]
    gates = gate_ref[...]
    y = None
    for kk in range(TOP_K):
        part = _load_row_tiles(yg_ref.at[kk], TOK_TILE) * gates[:, kk:kk + 1]
        y = part if y is None else y + part
    gate_f = mod_ref[0][5:6]
    o_ref[...] = x1_ref[...] + _rms(y) * (gate_f * gpost_ref[...])


def _final(part, yg3, gates, x1, mod3, g_post, out_so_far):
    tiles = SEQ // TOK_TILE
    tok = lambda w: pl.BlockSpec((TOK_TILE, w), lambda i: (part * tiles + i, 0))
    in_specs = [pl.BlockSpec((TOP_K, TOK_TILE * ROW_TILES, LANES), lambda i: (0, i, 0)),
                tok(TOP_K), tok(D_MODEL),
                pl.BlockSpec((1, 6, D_MODEL), lambda i: (part, 0, 0)),
                pl.BlockSpec((1, D_MODEL), lambda i: (0, 0))]
    args = [yg3, gates, x1, mod3, g_post]
    aliases = {}
    if out_so_far is not None:
        in_specs.append(pl.BlockSpec(memory_space=pl.ANY))
        args.append(out_so_far)
        aliases = {len(args) - 1: 0}
    return pl.pallas_call(
        _final_kernel,
        grid=(tiles,),
        in_specs=in_specs,
        out_specs=tok(D_MODEL),
        out_shape=jax.ShapeDtypeStruct((N_TOK, D_MODEL), F32),
        input_output_aliases=aliases,
        compiler_params=pltpu.CompilerParams(vmem_limit_bytes=VMEM_LIMIT),
        name="combine_final",
    )(*args)


def _routing_tables(idx, rank, counts):
    counts = counts.reshape(N_EXPERTS)
    experts = jnp.arange(N_EXPERTS, dtype=I32)
    padded = ((counts + MOE_BLOCK - 1) // MOE_BLOCK) * MOE_BLOCK
    pend = jnp.cumsum(padded)
    pstart = pend - padded
    dest = jnp.sum(jnp.where(idx[..., None] == experts, pstart, 0), axis=-1) + rank
    dest_win = dest.reshape(TOP_K, N_TOK // SC_WINDOW, SC_WINDOW).transpose(1, 0, 2)
    n_used = (pend[-1] // MOE_BLOCK).astype(I32).reshape(1)
    blk_start = jnp.arange(N_BLOCKS, dtype=I32) * MOE_BLOCK
    blk_e = jnp.sum(blk_start[:, None] >= pend[None, :], axis=1).astype(I32)
    last_e = jnp.max(jnp.where(counts > 0, experts, 0))
    blk_e = jnp.minimum(blk_e, last_e)
    later = (experts[None, :] > experts[:, None]) & (counts[None, :] > 0)
    next_nonempty = jnp.min(jnp.where(later, experts[None, :], N_EXPERTS), axis=1)
    next_nonempty = jnp.where(next_nonempty == N_EXPERTS, -1, next_nonempty).astype(I32)
    of_block = lambda table: jnp.sum(jnp.where(blk_e[:, None] == experts, table, 0), axis=1).astype(I32)
    next_e = of_block(next_nonempty)
    blk_valid = jnp.clip(of_block(counts) - (blk_start - of_block(pstart)), 0, MOE_BLOCK)
    blk_valid = jnp.where(blk_start < pend[-1], blk_valid, 0).astype(I32)
    return dest_win, blk_e, n_used, next_e, blk_valid


def kernel(x, c, w_ada, b_ada, g_pre_mix, g_post_mix, w_in, attn_norm_w, hgrn_lb, hgrn_norm_w,
           w_out, g_pre_ffn, g_post_ffn, w_router, b_router, w_gu, b_gu, w_down, b_down):
    c_pad = jnp.pad(c, ((0, SUBLANES - BATCH), (0, 0)))
    mod = _ada_mod(c_pad, w_ada[0], b_ada)
    mod3 = mod[:BATCH].reshape(BATCH, 6, D_MODEL)

    x2 = x.reshape(N_TOK, D_MODEL)
    q4, k4, v4, q16, k16, v16, qr, fr, ir, gr = _inproj(x2, mod3, g_pre_mix, w_in[0].astype(BF16))
    nat = lambda t: t.reshape(BATCH, SEQ, D_REC)
    ya = _attention(q4, k4, v4, q16, k16, v16, attn_norm_w)
    yr = _hgrn(nat(qr), nat(fr), nat(ir), nat(gr), hgrn_lb, hgrn_norm_w)

    x1, h2, idx, gates, rank, counts = _mid(
        ya.reshape(N_TOK, D_ATTN), yr.reshape(N_TOK, D_REC), x2, mod3, g_post_mix, g_pre_ffn,
        w_out[0].astype(BF16), w_router[0], b_router)

    dest_win, blk_e, n_used, next_e, blk_valid = _routing_tables(idx, rank, counts)
    xs = _sc_dispatch(h2.reshape(N_TOK, ROW_TILES, LANES), dest_win)
    ys = _experts(blk_e, n_used, next_e, blk_valid, xs.reshape(N_SLOTS * ROW_TILES, LANES),
                  w_gu[0], b_gu[0].reshape(N_EXPERTS, 1, 2 * D_FF),
                  w_down[0], b_down[0].reshape(N_EXPERTS, 1, D_MODEL))
    ys3 = ys.reshape(N_SLOTS, ROW_TILES, LANES)
    win_per_seq = SEQ // SC_WINDOW
    out = None
    for b in range(BATCH):
        yg = _sc_collect(ys3, dest_win[b * win_per_seq:(b + 1) * win_per_seq])
        out = _final(b, yg.reshape(TOP_K, SEQ * ROW_TILES, LANES), gates, x1, mod3, g_post_ffn, out)
    return out.reshape(BATCH, SEQ, D_MODEL)
```

```python
import functools
import math

import jax
import jax.numpy as jnp
from jax import lax
from jax.experimental import pallas as pl
from jax.experimental.pallas import tpu as pltpu
from jax.experimental.pallas import tpu_sc as plsc

F32 = jnp.float32
BF16 = jnp.bfloat16
I32 = jnp.int32
U32 = jnp.uint32

D_MODEL = 1024
BATCH = 4
SEQ = 4096
N_TOK = BATCH * SEQ
D_ATTN = 512
HEAD_DIM_A = 64
ATT_BLOCK = 128
DIL_MID = 4
DIL_MAX = 16
SUB_MID = SEQ // DIL_MID
SUB_MAX = SEQ // DIL_MAX
D_REC = 512
HGRN_CHUNK = 32
HGRN_SUPER = 256
N_EXPERTS = 32
TOP_K = 4
D_FF = 1024
SWIGLU_LIMIT = 7.0
SWIGLU_ALPHA = 1.702
EPS = 1e-6
NEG_BIG = -1e30
Q_SCALE = HEAD_DIM_A ** -0.5 * math.log2(math.e)

LANES = 128
SUBLANES = 8
ROW_TILES = D_MODEL // 2 // LANES

TOK_TILE = 512
MOE_BLOCK = 1024
MOE_ROWS_STEP = 128
N_SLOTS = N_TOK * TOP_K + N_EXPERTS * MOE_BLOCK
N_BLOCKS = N_SLOTS // MOE_BLOCK
SC_WINDOW = 64
VMEM_LIMIT = 56 * 1024 * 1024


def _sigmoid(x):
    return 1.0 / (1.0 + jnp.exp(-x))


def _dot(a, b):
    return jnp.dot(a, b, preferred_element_type=F32)


def _dot_nt(a, b):
    return lax.dot_general(a, b, (((1,), (1,)), ((), ())), preferred_element_type=F32)


def _split_bf16(x):
    hi = x.astype(BF16)
    return hi, (x - hi.astype(F32)).astype(BF16)


def _rms(x):
    return x * lax.rsqrt(jnp.mean(x * x, axis=-1, keepdims=True) + EPS)


def _ada_kernel(c_ref, w_ref, b_ref, o_ref):
    c = c_ref[...]
    cond = c * _sigmoid(c)
    c_hi, c_lo = _split_bf16(cond)
    w_hi, w_lo = _split_bf16(w_ref[...])
    o_ref[...] = _dot(c_hi, w_hi) + _dot(c_lo, w_hi) + _dot(c_hi, w_lo) + b_ref[...]


def _ada_mod(c_pad, w_ada, b_ada):
    n = w_ada.shape[1]
    tn = 1536
    return pl.pallas_call(
        _ada_kernel,
        grid=(n // tn,),
        in_specs=[pl.BlockSpec((SUBLANES, D_MODEL), lambda j: (0, 0)),
                  pl.BlockSpec((D_MODEL, tn), lambda j: (0, j)),
                  pl.BlockSpec((1, tn), lambda j: (0, j))],
        out_specs=pl.BlockSpec((SUBLANES, tn), lambda j: (0, j)),
        out_shape=jax.ShapeDtypeStruct((SUBLANES, n), F32),
        compiler_params=pltpu.CompilerParams(vmem_limit_bytes=VMEM_LIMIT),
        name="ada_mod",
    )(c_pad, w_ada, b_ada)


def _inproj_kernel(x_ref, mod_ref, g_ref, w_ref, q4, k4, v4, q16, k16, v16, qr, fr, ir, gr,
                   stage_nat, stage_mid):
    mod = mod_ref[0]
    shift, scale = mod[0:1], mod[1:2]
    h = _rms(x_ref[...]) * (g_ref[...] * (1.0 + scale)) + shift
    hb = h.astype(BF16)
    slabs = D_ATTN // LANES
    rows_mid = TOK_TILE // DIL_MID
    rows_max = TOK_TILE // DIL_MAX

    def proj(j):
        return _dot(hb, w_ref[:, j * D_ATTN:(j + 1) * D_ATTN])

    for j, (o_mid, o_max) in enumerate(((q4, q16), (k4, k16), (v4, v16))):
        r = proj(j)
        if j == 0:
            r = r * Q_SCALE
        for cs in range(slabs):
            stage_nat[cs] = r[:, cs * LANES:(cs + 1) * LANES]
        for cs in range(slabs):
            lanes = slice(cs * LANES, (cs + 1) * LANES)
            for sub in range(DIL_MID):
                piece = stage_nat[cs, pl.ds(sub, rows_mid, stride=DIL_MID), :]
                o_mid[sub, :, lanes] = piece.astype(BF16)
                stage_mid[cs, sub] = piece
            for sub in range(DIL_MAX):
                piece = stage_mid[cs, sub % DIL_MID, pl.ds(sub // DIL_MID, rows_max, stride=DIL_MID), :]
                o_max[sub, :, lanes] = piece.astype(BF16)
    for j, o_ref in enumerate((qr, fr, ir, gr)):
        o_ref[...] = proj(3 + j)


def _inproj(x2, mod3, g_pre, w_in_bf16):
    tiles_per_seq = SEQ // TOK_TILE
    rows_mid = TOK_TILE // DIL_MID
    rows_max = TOK_TILE // DIL_MAX
    mid = pl.BlockSpec((None, DIL_MID, rows_mid, D_ATTN),
                       lambda i: (i // tiles_per_seq, 0, i % tiles_per_seq, 0))
    mx = pl.BlockSpec((None, DIL_MAX, rows_max, D_ATTN),
                      lambda i: (i // tiles_per_seq, 0, i % tiles_per_seq, 0))
    nat = pl.BlockSpec((TOK_TILE, D_REC), lambda i: (i, 0))
    mid_shape = jax.ShapeDtypeStruct((BATCH, DIL_MID, SUB_MID, D_ATTN), BF16)
    mx_shape = jax.ShapeDtypeStruct((BATCH, DIL_MAX, SUB_MAX, D_ATTN), BF16)
    nat_shape = jax.ShapeDtypeStruct((N_TOK, D_REC), F32)
    return pl.pallas_call(
        _inproj_kernel,
        grid=(N_TOK // TOK_TILE,),
        in_specs=[pl.BlockSpec((TOK_TILE, D_MODEL), lambda i: (i, 0)),
                  pl.BlockSpec((1, 6, D_MODEL), lambda i: (i // tiles_per_seq, 0, 0)),
                  pl.BlockSpec((1, D_MODEL), lambda i: (0, 0)),
                  pl.BlockSpec(w_in_bf16.shape, lambda i: (0, 0))],
        out_specs=[mid, mid, mid, mx, mx, mx, nat, nat, nat, nat],
        out_shape=[mid_shape] * 3 + [mx_shape] * 3 + [nat_shape] * 4,
        scratch_shapes=[pltpu.VMEM((D_ATTN // LANES, TOK_TILE, LANES), F32),
                        pltpu.VMEM((D_ATTN // LANES, DIL_MID, rows_mid, LANES), F32)],
        compiler_params=pltpu.CompilerParams(vmem_limit_bytes=VMEM_LIMIT),
        name="inproj",
    )(x2, mod3, g_pre, w_in_bf16)


def _attn_kernel(q4_ref, k4_ref, v4_ref, q16_ref, k16_ref, v16_ref, nw_ref, o_ref,
                 sel_scr, keep_scr, o_scr, m_scr, l_scr):
    bw = ATT_BLOCK
    lane = lax.broadcasted_iota(I32, (bw, LANES), 1)
    head0 = lane < HEAD_DIM_A
    head_masks = (jnp.where(head0, 1.0, 0.0).astype(BF16), jnp.where(head0, 0.0, 1.0).astype(BF16))

    rr = lax.broadcasted_iota(I32, (bw, bw), 0)
    cc = lax.broadcasted_iota(I32, (bw, bw), 1)
    piece = bw // DIL_MID
    rp = DIL_MID * (rr % piece) + rr // piece
    cp = DIL_MID * (cc % piece) + cc // piece
    for var, (r, c) in enumerate(((rr, cc), (rr, cc), (rp, cp), (rp, cp))):
        first = var % 2 == 1
        left = (c < r) if first else (c >= r)
        right = jnp.zeros_like(left) if first else jnp.logical_not(left)
        sel_scr[var] = jnp.where(left, 1.0, 0.0)
        keep_scr[var, 0] = jnp.where(left, 1.0, 0.0).astype(BF16)
        keep_scr[var, 1] = jnp.where(right, 1.0, 0.0).astype(BF16)

    def score_matmuls(loaded):
        return [_dot_nt(q * hm, k) for (q, k, _, _, _) in loaded for hm in head_masks]

    def finish_group(scores, loaded):
        probs = []
        for n, s in enumerate(scores):
            _, _, _, var, right_bias = loaded[n // 2]
            folded = jnp.where(sel_scr[var] > 0.5, s[:, 0:bw], s[:, bw:] + right_bias)
            m = jnp.max(folded, axis=-1, keepdims=True)
            p = jnp.exp2(folded - m)
            pb = p.astype(BF16)
            spread = jnp.concatenate([pb * keep_scr[var, 0], pb * keep_scr[var, 1]], axis=1)
            probs.append((spread, jnp.sum(p, axis=-1, keepdims=True), m))
        pvs = [_dot(p, loaded[n // 2][2]) for n, (p, _, _) in enumerate(probs)]
        results = []
        for u in range(len(loaded)):
            (_, l0, m0), (_, l1, m1) = probs[2 * u], probs[2 * u + 1]
            results.append((jnp.where(head0, pvs[2 * u], pvs[2 * u + 1]),
                            jnp.where(head0, jnp.broadcast_to(m0, (bw, LANES)),
                                      jnp.broadcast_to(m1, (bw, LANES))),
                            jnp.where(head0, jnp.broadcast_to(l0, (bw, LANES)),
                                      jnp.broadcast_to(l1, (bw, LANES)))))
        return results

    group = 8

    def first_block_bias(blk):
        return jnp.where(blk == 0, NEG_BIG, 0.0)

    def load_d1(g):
        loaded, starts = [], []
        for i in range(group):
            blk = g * group + i
            qs = pl.multiple_of(blk * piece, piece)
            ks = pl.multiple_of(jnp.maximum(blk - 1, 0) * piece, piece)
            q = jnp.concatenate([q4_ref[r, pl.ds(qs, piece), :] for r in range(DIL_MID)], axis=0)
            k = jnp.concatenate([k4_ref[r, pl.ds(ks + half * piece, piece), :]
                                 for half in range(2) for r in range(DIL_MID)], axis=0)
            v = jnp.concatenate([v4_ref[r, pl.ds(ks + half * piece, piece), :]
                                 for half in range(2) for r in range(DIL_MID)], axis=0)
            loaded.append((q, k, v, jnp.where(blk == 0, 3, 2), first_block_bias(blk)))
            starts.append(qs)

        def store(results):
            for qs, parts in zip(starts, results):
                for r in range(DIL_MID):
                    for scr, val in zip((o_scr, m_scr, l_scr), parts):
                        scr[0, r, pl.ds(qs, piece), :] = val[r * piece:(r + 1) * piece]

        return loaded, store

    def load_d4(g):
        loaded, dsts = [], []
        for i in range(group):
            r, blk = i % DIL_MID, g * (group // DIL_MID) + i // DIL_MID
            qs = pl.multiple_of(blk * bw, bw)
            ks = pl.multiple_of(jnp.maximum(blk - 1, 0) * bw, bw)
            loaded.append((q4_ref[r, pl.ds(qs, bw), :], k4_ref[r, pl.ds(ks, 2 * bw), :],
                           v4_ref[r, pl.ds(ks, 2 * bw), :], jnp.where(blk == 0, 1, 0),
                           first_block_bias(blk)))
            dsts.append((r, qs))

        def store(results):
            for (r, qs), parts in zip(dsts, results):
                for scr, val in zip((o_scr, m_scr, l_scr), parts):
                    scr[1, r, pl.ds(qs, bw), :] = val

        return loaded, store

    def load_d16(g):
        loaded, dsts = [], []
        for i in range(group):
            r, blk = g * (group // 2) + i // 2, i % 2
            loaded.append((q16_ref[r, blk * bw:(blk + 1) * bw, :], k16_ref[r], v16_ref[r],
                           1 - blk, NEG_BIG if blk == 0 else 0.0))
            dsts.append((r % DIL_MID, pl.ds(blk * (bw * DIL_MID) + r // DIL_MID, bw, stride=DIL_MID)))

        def store(results):
            for (sub, dst), parts in zip(dsts, results):
                for scr, val in zip((o_scr, m_scr, l_scr), parts):
                    scr[2, sub, dst, :] = val

        return loaded, store

    def run_group(load):
        def step(g, carry):
            loaded, store = load(g)
            store(finish_group(score_matmuls(loaded), loaded))
            return carry
        return step

    n_groups = SEQ // bw // group
    for load in (load_d1, load_d4, load_d16):
        lax.fori_loop(0, n_groups, run_group(load), 0)

    rows = 256
    hi = lax.broadcasted_iota(I32, (LANES, LANES), 0) // HEAD_DIM_A
    hj = lax.broadcasted_iota(I32, (LANES, LANES), 1) // HEAD_DIM_A
    head_sum = jnp.where(hi == hj, 1.0, 0.0).astype(BF16)
    nw = nw_ref[...]

    def head_sums(x):
        x_hi, x_lo = _split_bf16(x)
        return _dot(x_hi, head_sum) + _dot(x_lo, head_sum)

    def merge(t, carry):
        r = t // (SUB_MID // rows)
        start = pl.multiple_of((t % (SUB_MID // rows)) * rows, rows)
        sl = pl.ds(start, rows)
        own_score = head_sums(q4_ref[r, sl, :].astype(F32) * k4_ref[r, sl, :].astype(F32))
        ms = [m_scr[n, r, sl, :] for n in range(3)]
        mx = jnp.maximum(jnp.maximum(jnp.maximum(ms[0], ms[1]), ms[2]), own_score)
        ws = [jnp.exp2(m - mx) for m in ms]
        w_own = float(len(ms)) * jnp.exp2(own_score - mx)
        num = (ws[0] * o_scr[0, r, sl, :] + ws[1] * o_scr[1, r, sl, :] + ws[2] * o_scr[2, r, sl, :]
               + w_own * v4_ref[r, sl, :].astype(F32))
        den = ws[0] * l_scr[0, r, sl, :] + ws[1] * l_scr[1, r, sl, :] + ws[2] * l_scr[2, r, sl, :] + w_own
        o = num / den
        mean_sq = head_sums(o * o) * (1.0 / HEAD_DIM_A)
        o_ref[pl.ds(start * DIL_MID + r, rows, stride=DIL_MID), :] = o * lax.rsqrt(mean_sq + EPS) * nw
        return carry

    lax.fori_loop(0, SEQ // rows, merge, 0, unroll=8)


def _attention(q4, k4, v4, q16, k16, v16, attn_norm_w):
    hp = D_ATTN // LANES
    mid = pl.BlockSpec((None, DIL_MID, SUB_MID, LANES), lambda b, h: (b, 0, 0, h))
    mx = pl.BlockSpec((None, DIL_MAX, SUB_MAX, LANES), lambda b, h: (b, 0, 0, h))
    scr = pltpu.VMEM((3, DIL_MID, SUB_MID, LANES), F32)
    return pl.pallas_call(
        _attn_kernel,
        grid=(BATCH, hp),
        in_specs=[mid, mid, mid, mx, mx, mx, pl.BlockSpec((1, LANES), lambda b, h: (0, h))],
        out_specs=pl.BlockSpec((None, SEQ, LANES), lambda b, h: (b, 0, h)),
        out_shape=jax.ShapeDtypeStruct((BATCH, SEQ, D_ATTN), F32),
        scratch_shapes=[pltpu.VMEM((4, ATT_BLOCK, ATT_BLOCK), F32),
                        pltpu.VMEM((4, 2, ATT_BLOCK, ATT_BLOCK), BF16), scr, scr, scr],
        compiler_params=pltpu.CompilerParams(vmem_limit_bytes=VMEM_LIMIT),
        name="dilated_attn",
    )(q4, k4, v4, q16, k16, v16, attn_norm_w)


def _hgrn_kernel(qr_ref, fr_ref, ir_ref, gr_ref, lb_ref, nw_ref, o_ref,
                 qe_scr, oi_scr, delta_scr, dec_scr, st_scr):
    sup, c = HGRN_SUPER, HGRN_CHUNK
    nch = sup // c
    n_sup = SEQ // sup
    lbp = lb_ref[...]
    lmx = jnp.max(lbp, axis=0, keepdims=True)
    ex = jnp.exp(lbp - lmx)
    lb = ex[0:1] / (ex[0:1] + ex[1:2])
    nw = nw_ref[...]

    ri = lax.broadcasted_iota(I32, (sup, sup), 0)
    ci = lax.broadcasted_iota(I32, (sup, sup), 1)
    same_chunk = (ri // c) == (ci // c)
    causal = same_chunk & (ci <= ri)
    cum_op = jnp.where(causal, 1.0, 0.0).astype(BF16)

    def chunk_rows(kd, ch):
        parts = []
        if ch > 0:
            parts.append(jnp.zeros((ch * c, LANES), BF16))
        parts.append(kd[ch * c:(ch + 1) * c])
        if ch < nch - 1:
            parts.append(jnp.zeros(((nch - 1 - ch) * c, LANES), BF16))
        return jnp.concatenate(parts, axis=0)

    group = 4

    def independent(g, carry):
        ts = [g * group + i for i in range(group)]
        sls = [pl.ds(pl.multiple_of(t * sup, sup), sup) for t in ts]
        pre = []
        for sl in sls:
            f = lb + (1.0 - lb) * _sigmoid(fr_ref[sl, :])
            logf_hi, logf_lo = _split_bf16(jnp.log(f))
            pre.append((1.0 - f, _dot(cum_op, logf_hi) + _dot(cum_op, logf_lo)))
        mid = []
        for t, sl, (kk, b) in zip(ts, sls, pre):
            b_last = jnp.concatenate(
                [jnp.broadcast_to(b[(ch + 1) * c - 1:(ch + 1) * c], (c, LANES)) for ch in range(nch)],
                axis=0)
            q = qr_ref[sl, :]
            qeb = (q * _sigmoid(q) * jnp.exp(b)).astype(BF16)
            ke = (kk * jnp.exp(-b)).astype(BF16)
            kd = (kk * jnp.exp(b_last - b)).astype(BF16)
            qe_scr[sl, :] = qeb
            dec_rows = jnp.concatenate([b_last[ch * c:ch * c + 1] for ch in range(nch)], axis=0)
            dec_scr[pl.ds(pl.multiple_of(t * nch, nch), nch), :] = jnp.exp(dec_rows)
            v = ir_ref[sl, :]
            vt = v.T.astype(BF16)
            scores = _dot_nt(qeb, ke)
            for pair in range(nch // 2):
                rhs = jnp.concatenate([chunk_rows(kd, 2 * pair), chunk_rows(kd, 2 * pair + 1)], axis=1)
                d2 = _dot(vt, rhs)
                delta_scr[t * nch + 2 * pair] = d2[:, 0:LANES]
                delta_scr[t * nch + 2 * pair + 1] = d2[:, LANES:]
            mid.append((scores, v.astype(BF16)))
        for sl, (scores, vb) in zip(sls, mid):
            a = jnp.where(causal, scores, 0.0)
            oi_scr[sl, :] = _dot(a.astype(BF16), vb)
        return carry

    lax.fori_loop(0, n_sup // group, independent, 0)

    def recur(ch, st):
        st_scr[ch] = st.astype(BF16)
        return st * dec_scr[pl.ds(ch, 1), :] + delta_scr[ch]

    lax.fori_loop(0, SEQ // c, recur, jnp.zeros((LANES, LANES), F32), unroll=8)

    def finish(t, carry):
        sl = pl.ds(pl.multiple_of(t * sup, sup), sup)
        parts = [_dot_nt(qe_scr[pl.ds(pl.multiple_of(t * sup + ch * c, c), c), :], st_scr[t * nch + ch])
                 for ch in range(nch)]
        o = oi_scr[sl, :] + jnp.concatenate(parts, axis=0)
        g = gr_ref[sl, :]
        o_ref[sl, :] = (_rms(o) * nw * (g * _sigmoid(g))).astype(o_ref.dtype)
        return carry

    lax.fori_loop(0, n_sup, finish, 0, unroll=8)


def _hgrn(qr, fr, ir, gr, hgrn_lb, hgrn_norm_w):
    nh = D_REC // LANES
    n_chunks = SEQ // HGRN_CHUNK
    blk = pl.BlockSpec((None, SEQ, LANES), lambda b, h: (b, 0, h))
    return pl.pallas_call(
        _hgrn_kernel,
        grid=(BATCH, nh),
        in_specs=[blk, blk, blk, blk,
                  pl.BlockSpec((2, LANES), lambda b, h: (0, h)),
                  pl.BlockSpec((1, LANES), lambda b, h: (0, h))],
        out_specs=blk,
        out_shape=jax.ShapeDtypeStruct((BATCH, SEQ, D_REC), BF16),
        scratch_shapes=[pltpu.VMEM((SEQ, LANES), BF16),
                        pltpu.VMEM((SEQ, LANES), F32),
                        pltpu.VMEM((n_chunks, LANES, LANES), F32),
                        pltpu.VMEM((n_chunks, LANES), F32),
                        pltpu.VMEM((n_chunks, LANES, LANES), BF16)],
        compiler_params=pltpu.CompilerParams(vmem_limit_bytes=VMEM_LIMIT),
        name="hgrn2",
    )(qr, fr, ir, gr, hgrn_lb, hgrn_norm_w)


def _tile_batch(i):
    return i // (SEQ // TOK_TILE)


def _store_row_tiles(ref, val):
    rows, half = val.shape[0], D_MODEL // 2
    lo = lax.bitcast_convert_type(val[:, :half], U32)
    hi = lax.bitcast_convert_type(val[:, half:], U32)
    words = lax.shift_right_logical(lo, jnp.uint32(16)) | (hi & jnp.uint32(0xFFFF0000))
    for j in range(ROW_TILES):
        ref[pl.ds(j, rows, stride=ROW_TILES), :] = words[:, j * LANES:(j + 1) * LANES]


def _load_row_tiles(ref, rows):
    words = jnp.concatenate(
        [ref[pl.ds(j, rows, stride=ROW_TILES), :] for j in range(ROW_TILES)], axis=1)
    lo = lax.bitcast_convert_type(lax.shift_left(words, jnp.uint32(16)), F32)
    hi = lax.bitcast_convert_type(words & jnp.uint32(0xFFFF0000), F32)
    return jnp.concatenate([lo, hi], axis=1)


def _mid_kernel(ya_ref, yr_ref, x_ref, mod_ref, gpost_ref, gpre_ref, wo_ref, wr_ref, br_ref,
                x1_ref, h2_ref, idx_ref, gate_ref, rank_ref, cnt_ref, carry_ref):
    i = pl.program_id(0)

    @pl.when(i == 0)
    def _():
        carry_ref[...] = jnp.zeros_like(carry_ref)

    mod = mod_ref[0]
    gate_m, shift_f, scale_f = mod[2:3], mod[3:4], mod[4:5]
    y = _dot(ya_ref[...].astype(BF16), wo_ref[0:D_ATTN, :]) + _dot(yr_ref[...], wo_ref[D_ATTN:, :])
    x1 = x_ref[...] + _rms(y) * (gate_m * gpost_ref[...])
    x1_ref[...] = x1
    h2 = _rms(x1) * (gpre_ref[...] * (1.0 + scale_f)) + shift_f
    h2_hi = h2.astype(BF16)
    h2_rounded = h2_hi.astype(F32)
    _store_row_tiles(h2_ref, h2_rounded)

    tm = h2.shape[0]
    h2_lo = (h2 - h2_rounded).astype(BF16)
    wr_hi, wr_lo = _split_bf16(wr_ref[...])
    parts = jnp.concatenate([_dot(h2_hi, jnp.concatenate([wr_hi, wr_lo], axis=1)),
                             _dot(h2_lo, wr_hi),
                             jnp.broadcast_to(br_ref[...], (tm, N_EXPERTS))], axis=1).T
    work = sum(parts[g * N_EXPERTS:(g + 1) * N_EXPERTS] for g in range(LANES // N_EXPERTS))
    eidx = lax.broadcasted_iota(I32, (N_EXPERTS, tm), 0).astype(F32)
    vals, idxs = [], []
    onehot = jnp.zeros((N_EXPERTS, tm), F32)
    for _ in range(TOP_K):
        m = jnp.max(work, axis=0, keepdims=True)
        sel = jnp.min(jnp.where(work == m, eidx, float(N_EXPERTS)), axis=0, keepdims=True)
        hit = eidx == sel
        work = jnp.where(hit, -jnp.inf, work)
        onehot = jnp.where(hit, 1.0, onehot)
        vals.append(m)
        idxs.append(sel)
    ex = [jnp.exp(vv - vals[0]) for vv in vals]
    inv_den = 1.0 / (ex[0] + ex[1] + ex[2] + ex[3])

    ri = lax.broadcasted_iota(I32, (tm, tm), 0)
    ci = lax.broadcasted_iota(I32, (tm, tm), 1)
    strict_upper = jnp.where(ri < ci, 1.0, 0.0).astype(BF16)
    before = _dot(onehot.astype(BF16), strict_upper) + carry_ref[...]
    ranks = [jnp.sum(jnp.where(eidx == idxs[kk], before, 0.0), axis=0, keepdims=True)
             for kk in range(TOP_K)]
    idx_ref[...] = jnp.concatenate(idxs, axis=0).astype(I32)
    rank_ref[...] = jnp.concatenate(ranks, axis=0).astype(I32)
    gates_t = jnp.concatenate([e * inv_den for e in ex]
                              + [jnp.zeros((LANES - TOP_K, tm), F32)], axis=0)
    gate_ref[...] = gates_t.T[:, 0:TOP_K]
    total = carry_ref[...] + jnp.sum(onehot, axis=1, keepdims=True)
    carry_ref[...] = total
    cnt_ref[...] = total.astype(I32)


def _mid(ya, yr, x2, mod3, g_post, g_pre, w_out_bf16, w_router, b_router):
    tok = lambda w: pl.BlockSpec((TOK_TILE, w), lambda i: (i, 0))
    const = lambda s: pl.BlockSpec(s, lambda i: (0,) * len(s))
    lanes_tok = pl.BlockSpec((TOP_K, TOK_TILE), lambda i: (0, i))
    return pl.pallas_call(
        _mid_kernel,
        grid=(N_TOK // TOK_TILE,),
        in_specs=[tok(D_ATTN), tok(D_REC), tok(D_MODEL),
                  pl.BlockSpec((1, 6, D_MODEL), lambda i: (_tile_batch(i), 0, 0)),
                  const((1, D_MODEL)), const((1, D_MODEL)),
                  const((D_MODEL, D_MODEL)), const((D_MODEL, N_EXPERTS)), const((1, N_EXPERTS))],
        out_specs=[tok(D_MODEL),
                   pl.BlockSpec((TOK_TILE * ROW_TILES, LANES), lambda i: (i, 0)),
                   lanes_tok, tok(TOP_K), lanes_tok, const((N_EXPERTS, 1))],
        out_shape=[jax.ShapeDtypeStruct((N_TOK, D_MODEL), F32),
                   jax.ShapeDtypeStruct((N_TOK * ROW_TILES, LANES), U32),
                   jax.ShapeDtypeStruct((TOP_K, N_TOK), I32),
                   jax.ShapeDtypeStruct((N_TOK, TOP_K), F32),
                   jax.ShapeDtypeStruct((TOP_K, N_TOK), I32),
                   jax.ShapeDtypeStruct((N_EXPERTS, 1), I32)],
        scratch_shapes=[pltpu.VMEM((N_EXPERTS, 1), F32)],
        compiler_params=pltpu.CompilerParams(dimension_semantics=("arbitrary",),
                                             vmem_limit_bytes=VMEM_LIMIT),
        name="outproj_router",
    )(ya, yr, x2, mod3, g_post, g_pre, w_out_bf16, w_router, b_router)


def _sc_mesh():
    return plsc.VectorSubcoreMesh(core_axis_name="c", subcore_axis_name="s")


def _sc_worker_count():
    info = plsc.get_sparse_core_info()
    return info.num_cores, info.num_cores * info.num_subcores


def _sc_dispatch(h_rows, dest_win):
    n_cores, n_workers = _sc_worker_count()
    n_win = N_TOK // SC_WINDOW
    per_worker = n_win // n_workers

    @functools.partial(
        pl.kernel, mesh=_sc_mesh(),
        out_type=jax.ShapeDtypeStruct((N_SLOTS, ROW_TILES, LANES), U32),
        scratch_types=[pltpu.VMEM((TOP_K, SC_WINDOW), I32),
                       pltpu.VMEM((SC_WINDOW, ROW_TILES, LANES), U32),
                       pltpu.SemaphoreType.DMA],
        name="sc_dispatch")
    def run(h_hbm, dest_hbm, xs_hbm, idx_v, rows_v, sem):
        wid = lax.axis_index("s") * n_cores + lax.axis_index("c")

        @pl.loop(0, per_worker)
        def _(j):
            win = wid * per_worker + j
            pltpu.sync_copy(dest_hbm.at[win], idx_v)
            pltpu.sync_copy(h_hbm.at[pl.ds(win * SC_WINDOW, SC_WINDOW)], rows_v)
            copies = [pltpu.async_copy(rows_v, xs_hbm.at[idx_v.at[kk]], sem)
                      for kk in range(TOP_K)]
            for cp in copies:
                cp.wait()

    return run(h_rows, dest_win)


def _sc_collect(y_rows, dest_win):
    n_cores, n_workers = _sc_worker_count()
    n_win = dest_win.shape[0]
    per_worker = n_win // n_workers

    @functools.partial(
        pl.kernel, mesh=_sc_mesh(),
        out_type=jax.ShapeDtypeStruct((TOP_K, n_win * SC_WINDOW, ROW_TILES, LANES), U32),
        scratch_types=[pltpu.VMEM((TOP_K, SC_WINDOW), I32),
                       pltpu.VMEM((SC_WINDOW, ROW_TILES, LANES), U32),
                       pltpu.SemaphoreType.DMA],
        name="sc_collect")
    def run(y_hbm, dest_hbm, yg_hbm, idx_v, rows_v, sem):
        wid = lax.axis_index("s") * n_cores + lax.axis_index("c")

        @pl.loop(0, per_worker)
        def _(j):
            win = wid * per_worker + j
            pltpu.sync_copy(dest_hbm.at[win], idx_v)
            for kk in range(TOP_K):
                pltpu.async_copy(y_hbm.at[idx_v.at[kk]], rows_v, sem).wait()
                pltpu.sync_copy(rows_v, yg_hbm.at[kk, pl.ds(win * SC_WINDOW, SC_WINDOW)])

    return run(y_rows, dest_win)


def _expert_kernel(be_ref, nu_ref, nx_ref, nv_ref, x_ref, wgu_hbm, bgu_ref, wd_hbm, bd_ref, y_ref,
                   wgu32, wd32, wgu16, wd16, sems):
    i = pl.program_id(0)
    e = be_ref[i]
    prev = be_ref[jnp.maximum(i - 1, 0)]

    def weight_copies(ex):
        return (pltpu.make_async_copy(wgu_hbm.at[ex], wgu32, sems.at[0]),
                pltpu.make_async_copy(wd_hbm.at[ex], wd32, sems.at[1]))

    @pl.when(i == 0)
    def _():
        for cp in weight_copies(e):
            cp.start()

    @pl.when((i == 0) | (e != prev))
    def _():
        for cp in weight_copies(e):
            cp.wait()
        rows = 128

        def cast(r, carry):
            sl = pl.ds(pl.multiple_of(r * rows, rows), rows)
            wgu16[sl, :] = wgu32[sl, :].astype(BF16)
            wd16[sl, :] = wd32[sl, :].astype(BF16)
            return carry

        lax.fori_loop(0, D_MODEL // rows, cast, 0)
        nxt = nx_ref[i]

        @pl.when(nxt >= 0)
        def _():
            for cp in weight_copies(nxt):
                cp.start()

    def run_rows(rows):
        x = _load_row_tiles(x_ref, rows).astype(BF16)
        bgu = bgu_ref[0]
        glu = _dot(x, wgu16[:, 0:D_FF]) + bgu[:, 0:D_FF]
        lin = _dot(x, wgu16[:, D_FF:]) + bgu[:, D_FF:]
        glu = jnp.minimum(glu, SWIGLU_LIMIT)
        lin = jnp.clip(lin, -SWIGLU_LIMIT, SWIGLU_LIMIT)
        act = glu * _sigmoid(SWIGLU_ALPHA * glu) * (lin + 1.0)
        y = _dot(act.astype(BF16), wd16[...]) + bd_ref[0]
        _store_row_tiles(y_ref, y.astype(BF16).astype(F32))

    valid = nv_ref[i]
    for rows in range(MOE_ROWS_STEP, MOE_BLOCK + 1, MOE_ROWS_STEP):
        pl.when((valid > rows - MOE_ROWS_STEP) & (valid <= rows))(functools.partial(run_rows, rows))


def _experts(blk_e, n_used, next_e, blk_valid, xs2, w_gu, b_gu3, w_down, b_down3):
    row_blk = pl.BlockSpec((MOE_BLOCK * ROW_TILES, LANES),
                           lambda i, be, nu, nx, nv: (jnp.minimum(i, nu[0] - 1), 0))
    grid_spec = pltpu.PrefetchScalarGridSpec(
        num_scalar_prefetch=4,
        grid=(N_BLOCKS,),
        in_specs=[row_blk,
                  pl.BlockSpec(memory_space=pl.ANY),
                  pl.BlockSpec((1, 1, 2 * D_FF), lambda i, be, nu, nx, nv: (be[i], 0, 0)),
                  pl.BlockSpec(memory_space=pl.ANY),
                  pl.BlockSpec((1, 1, D_MODEL), lambda i, be, nu, nx, nv: (be[i], 0, 0))],
        out_specs=row_blk,
        scratch_shapes=[pltpu.VMEM((D_MODEL, 2 * D_FF), F32),
                        pltpu.VMEM((D_FF, D_MODEL), F32),
                        pltpu.VMEM((D_MODEL, 2 * D_FF), BF16),
                        pltpu.VMEM((D_FF, D_MODEL), BF16),
                        pltpu.SemaphoreType.DMA((2,))],
    )
    return pl.pallas_call(
        _expert_kernel,
        grid_spec=grid_spec,
        out_shape=jax.ShapeDtypeStruct((N_SLOTS * ROW_TILES, LANES), U32),
        compiler_params=pltpu.CompilerParams(dimension_semantics=("arbitrary",),
                                             vmem_limit_bytes=VMEM_LIMIT),
        name="experts",
    )(blk_e, n_used, next_e, blk_valid, xs2, w_gu, b_gu3, w_down, b_down3)


def _final_kernel(yg_ref, gate_ref, x1_ref, mod_ref, gpost_ref, *maybe_alias_and_out):
    o_ref = maybe_alias_and_out[-1]
    gates = gate_ref[...]
    y = None
    for kk in range(TOP_K):
        part = _load_row_tiles(yg_ref.at[kk], TOK_TILE) * gates[:, kk:kk + 1]
        y = part if y is None else y + part
    gate_f = mod_ref[0][5:6]
    o_ref[...] = x1_ref[...] + _rms(y) * (gate_f * gpost_ref[...])


def _final(part, yg3, gates, x1, mod3, g_post, out_so_far):
    tiles = SEQ // TOK_TILE
    tok = lambda w: pl.BlockSpec((TOK_TILE, w), lambda i: (part * tiles + i, 0))
    in_specs = [pl.BlockSpec((TOP_K, TOK_TILE * ROW_TILES, LANES), lambda i: (0, i, 0)),
                tok(TOP_K), tok(D_MODEL),
                pl.BlockSpec((1, 6, D_MODEL), lambda i: (part, 0, 0)),
                pl.BlockSpec((1, D_MODEL), lambda i: (0, 0))]
    args = [yg3, gates, x1, mod3, g_post]
    aliases = {}
    if out_so_far is not None:
        in_specs.append(pl.BlockSpec(memory_space=pl.ANY))
        args.append(out_so_far)
        aliases = {len(args) - 1: 0}
    return pl.pallas_call(
        _final_kernel,
        grid=(tiles,),
        in_specs=in_specs,
        out_specs=tok(D_MODEL),
        out_shape=jax.ShapeDtypeStruct((N_TOK, D_MODEL), F32),
        input_output_aliases=aliases,
        compiler_params=pltpu.CompilerParams(vmem_limit_bytes=VMEM_LIMIT),
        name="combine_final",
    )(*args)


def _slot_kernel(idx_ref, rank_ref, start_ref, dest_ref):
    tn = idx_ref.shape[1]
    experts = lax.broadcasted_iota(I32, (N_EXPERTS, tn), 0)
    starts = start_ref[...].astype(F32)
    rows = [jnp.sum(jnp.where(experts == idx_ref[kk:kk + 1, :], starts, 0.0), axis=0, keepdims=True)
            for kk in range(TOP_K)]
    dest_ref[...] = jnp.concatenate(rows, axis=0).astype(I32) + rank_ref[...]


def _slots(idx, rank, pstart):
    tn = 2048
    lanes_tok = pl.BlockSpec((TOP_K, tn), lambda i: (0, i))
    return pl.pallas_call(
        _slot_kernel,
        grid=(N_TOK // tn,),
        in_specs=[lanes_tok, lanes_tok, pl.BlockSpec((N_EXPERTS, 1), lambda i: (0, 0))],
        out_specs=lanes_tok,
        out_shape=jax.ShapeDtypeStruct((TOP_K, N_TOK), I32),
        name="moe_slots",
    )(idx, rank, pstart.reshape(N_EXPERTS, 1))


def _routing_tables(idx, rank, counts):
    counts = counts.reshape(N_EXPERTS)
    experts = jnp.arange(N_EXPERTS, dtype=I32)
    padded = ((counts + MOE_BLOCK - 1) // MOE_BLOCK) * MOE_BLOCK
    pend = jnp.cumsum(padded)
    pstart = pend - padded
    dest = _slots(idx, rank, pstart)
    dest_win = dest.reshape(TOP_K, N_TOK // SC_WINDOW, SC_WINDOW).transpose(1, 0, 2)
    n_used = (pend[-1] // MOE_BLOCK).astype(I32).reshape(1)
    blk_start = jnp.arange(N_BLOCKS, dtype=I32) * MOE_BLOCK
    blk_e = jnp.sum(blk_start[:, None] >= pend[None, :], axis=1).astype(I32)
    last_e = jnp.max(jnp.where(counts > 0, experts, 0))
    blk_e = jnp.minimum(blk_e, last_e)
    later = (experts[None, :] > experts[:, None]) & (counts[None, :] > 0)
    next_nonempty = jnp.min(jnp.where(later, experts[None, :], N_EXPERTS), axis=1)
    next_nonempty = jnp.where(next_nonempty == N_EXPERTS, -1, next_nonempty).astype(I32)
    of_block = lambda table: jnp.sum(jnp.where(blk_e[:, None] == experts, table, 0), axis=1).astype(I32)
    next_e = of_block(next_nonempty)
    blk_valid = jnp.clip(of_block(counts) - (blk_start - of_block(pstart)), 0, MOE_BLOCK)
    blk_valid = jnp.where(blk_start < pend[-1], blk_valid, 0).astype(I32)
    return dest_win, blk_e, n_used, next_e, blk_valid


def kernel(x, c, w_ada, b_ada, g_pre_mix, g_post_mix, w_in, attn_norm_w, hgrn_lb, hgrn_norm_w,
           w_out, g_pre_ffn, g_post_ffn, w_router, b_router, w_gu, b_gu, w_down, b_down):
    c_pad = jnp.pad(c, ((0, SUBLANES - BATCH), (0, 0)))
    mod = _ada_mod(c_pad, w_ada[0], b_ada)
    mod3 = mod[:BATCH].reshape(BATCH, 6, D_MODEL)

    x2 = x.reshape(N_TOK, D_MODEL)
    q4, k4, v4, q16, k16, v16, qr, fr, ir, gr = _inproj(x2, mod3, g_pre_mix, w_in[0].astype(BF16))
    nat = lambda t: t.reshape(BATCH, SEQ, D_REC)
    ya = _attention(q4, k4, v4, q16, k16, v16, attn_norm_w)
    yr = _hgrn(nat(qr), nat(fr), nat(ir), nat(gr), hgrn_lb, hgrn_norm_w)

    x1, h2, idx, gates, rank, counts = _mid(
        ya.reshape(N_TOK, D_ATTN), yr.reshape(N_TOK, D_REC), x2, mod3, g_post_mix, g_pre_ffn,
        w_out[0].astype(BF16), w_router[0], b_router)

    dest_win, blk_e, n_used, next_e, blk_valid = _routing_tables(idx, rank, counts)
    xs = _sc_dispatch(h2.reshape(N_TOK, ROW_TILES, LANES), dest_win)
    ys = _experts(blk_e, n_used, next_e, blk_valid, xs.reshape(N_SLOTS * ROW_TILES, LANES),
                  w_gu[0], b_gu[0].reshape(N_EXPERTS, 1, 2 * D_FF),
                  w_down[0], b_down[0].reshape(N_EXPERTS, 1, D_MODEL))
    ys3 = ys.reshape(N_SLOTS, ROW_TILES, LANES)
    win_per_seq = SEQ // SC_WINDOW
    out = None
    for b in range(BATCH):
        yg = _sc_collect(ys3, dest_win[b * win_per_seq:(b + 1) * win_per_seq])
        out = _final(b, yg.reshape(TOP_K, SEQ * ROW_TILES, LANES), gates, x1, mod3, g_post_ffn, out)
    return out.reshape(BATCH, SEQ, D_MODEL)
```

```python
import functools
import math

import jax
import jax.numpy as jnp
from jax import lax
from jax.experimental import pallas as pl
from jax.experimental.pallas import tpu as pltpu
from jax.experimental.pallas import tpu_sc as plsc

F32 = jnp.float32
BF16 = jnp.bfloat16
I32 = jnp.int32
U32 = jnp.uint32

D_MODEL = 1024
BATCH = 4
SEQ = 4096
N_TOK = BATCH * SEQ
D_ATTN = 512
HEAD_DIM_A = 64
ATT_BLOCK = 128
DIL_MID = 4
DIL_MAX = 16
SUB_MID = SEQ // DIL_MID
SUB_MAX = SEQ // DIL_MAX
D_REC = 512
HGRN_CHUNK = 32
HGRN_SUPER = 256
N_EXPERTS = 32
TOP_K = 4
D_FF = 1024
SWIGLU_LIMIT = 7.0
SWIGLU_ALPHA = 1.702
EPS = 1e-6
NEG_BIG = -1e30
Q_SCALE = HEAD_DIM_A ** -0.5 * math.log2(math.e)

LANES = 128
SUBLANES = 8
ROW_TILES = D_MODEL // 2 // LANES

TOK_TILE = 512
MOE_BLOCK = 1024
MOE_ROWS_STEP = 128
N_SLOTS = N_TOK * TOP_K + N_EXPERTS * MOE_BLOCK
N_BLOCKS = N_SLOTS // MOE_BLOCK
SC_WINDOW = 128
VMEM_LIMIT = 56 * 1024 * 1024


def _sigmoid(x):
    return 1.0 / (1.0 + jnp.exp(-x))


def _dot(a, b):
    return jnp.dot(a, b, preferred_element_type=F32)


def _dot_nt(a, b):
    return lax.dot_general(a, b, (((1,), (1,)), ((), ())), preferred_element_type=F32)


def _split_bf16(x):
    hi = x.astype(BF16)
    return hi, (x - hi.astype(F32)).astype(BF16)


def _rms(x):
    return x * lax.rsqrt(jnp.mean(x * x, axis=-1, keepdims=True) + EPS)


def _ada_kernel(c_ref, w_ref, b_ref, o_ref):
    c = c_ref[...]
    cond = c * _sigmoid(c)
    c_hi, c_lo = _split_bf16(cond)
    w_hi, w_lo = _split_bf16(w_ref[...])
    o_ref[...] = _dot(c_hi, w_hi) + _dot(c_lo, w_hi) + _dot(c_hi, w_lo) + b_ref[...]


def _ada_mod(c_pad, w_ada, b_ada):
    n = w_ada.shape[1]
    tn = 1536
    return pl.pallas_call(
        _ada_kernel,
        grid=(n // tn,),
        in_specs=[pl.BlockSpec((SUBLANES, D_MODEL), lambda j: (0, 0)),
                  pl.BlockSpec((D_MODEL, tn), lambda j: (0, j)),
                  pl.BlockSpec((1, tn), lambda j: (0, j))],
        out_specs=pl.BlockSpec((SUBLANES, tn), lambda j: (0, j)),
        out_shape=jax.ShapeDtypeStruct((SUBLANES, n), F32),
        compiler_params=pltpu.CompilerParams(vmem_limit_bytes=VMEM_LIMIT),
        name="ada_mod",
    )(c_pad, w_ada, b_ada)


def _cast_weights_once(w_ref, w16):
    rows = 128

    @pl.when(pl.program_id(0) == 0)
    def _():
        def cast(r, carry):
            sl = pl.ds(pl.multiple_of(r * rows, rows), rows)
            w16[sl, :] = w_ref[sl, :].astype(BF16)
            return carry

        lax.fori_loop(0, w_ref.shape[0] // rows, cast, 0)


def _resident(shape):
    return pl.BlockSpec(shape, lambda i: (0,) * len(shape), pipeline_mode=pl.Buffered(1))


def _inproj_kernel(x_ref, mod_ref, g_ref, w_ref, q4, k4, v4, q16, k16, v16, qr, fr, ir, gr,
                   w16, stage_nat, stage_mid):
    _cast_weights_once(w_ref, w16)
    mod = mod_ref[0]
    shift, scale = mod[0:1], mod[1:2]
    h = _rms(x_ref[...]) * (g_ref[...] * (1.0 + scale)) + shift
    hb = h.astype(BF16)
    slabs = D_ATTN // LANES
    rows_mid = TOK_TILE // DIL_MID
    rows_max = TOK_TILE // DIL_MAX

    def proj(j):
        return _dot(hb, w16[:, j * D_ATTN:(j + 1) * D_ATTN])

    for j, (o_mid, o_max) in enumerate(((q4, q16), (k4, k16), (v4, v16))):
        r = proj(j)
        if j == 0:
            r = r * Q_SCALE
        for cs in range(slabs):
            stage_nat[cs] = r[:, cs * LANES:(cs + 1) * LANES]
        for cs in range(slabs):
            lanes = slice(cs * LANES, (cs + 1) * LANES)
            for sub in range(DIL_MID):
                piece = stage_nat[cs, pl.ds(sub, rows_mid, stride=DIL_MID), :]
                o_mid[sub, :, lanes] = piece.astype(BF16)
                stage_mid[cs, sub] = piece
            for sub in range(DIL_MAX):
                piece = stage_mid[cs, sub % DIL_MID, pl.ds(sub // DIL_MID, rows_max, stride=DIL_MID), :]
                o_max[sub, :, lanes] = piece.astype(BF16)
    for j, o_ref in enumerate((qr, fr, ir, gr)):
        o_ref[...] = proj(3 + j)


def _inproj(x2, mod3, g_pre, w_in):
    tiles_per_seq = SEQ // TOK_TILE
    rows_mid = TOK_TILE // DIL_MID
    rows_max = TOK_TILE // DIL_MAX
    mid = pl.BlockSpec((None, DIL_MID, rows_mid, D_ATTN),
                       lambda i: (i // tiles_per_seq, 0, i % tiles_per_seq, 0))
    mx = pl.BlockSpec((None, DIL_MAX, rows_max, D_ATTN),
                      lambda i: (i // tiles_per_seq, 0, i % tiles_per_seq, 0))
    nat = pl.BlockSpec((TOK_TILE, D_REC), lambda i: (i, 0))
    mid_shape = jax.ShapeDtypeStruct((BATCH, DIL_MID, SUB_MID, D_ATTN), BF16)
    mx_shape = jax.ShapeDtypeStruct((BATCH, DIL_MAX, SUB_MAX, D_ATTN), BF16)
    nat_shape = jax.ShapeDtypeStruct((N_TOK, D_REC), F32)
    return pl.pallas_call(
        _inproj_kernel,
        grid=(N_TOK // TOK_TILE,),
        in_specs=[pl.BlockSpec((TOK_TILE, D_MODEL), lambda i: (i, 0)),
                  pl.BlockSpec((1, 6, D_MODEL), lambda i: (i // tiles_per_seq, 0, 0)),
                  pl.BlockSpec((1, D_MODEL), lambda i: (0, 0)),
                  _resident(w_in.shape)],
        out_specs=[mid, mid, mid, mx, mx, mx, nat, nat, nat, nat],
        out_shape=[mid_shape] * 3 + [mx_shape] * 3 + [nat_shape] * 4,
        scratch_shapes=[pltpu.VMEM(w_in.shape, BF16),
                        pltpu.VMEM((D_ATTN // LANES, TOK_TILE, LANES), F32),
                        pltpu.VMEM((D_ATTN // LANES, DIL_MID, rows_mid, LANES), F32)],
        compiler_params=pltpu.CompilerParams(dimension_semantics=("arbitrary",),
                                             vmem_limit_bytes=VMEM_LIMIT),
        name="inproj",
    )(x2, mod3, g_pre, w_in)


def _attn_kernel(q4_ref, k4_ref, v4_ref, q16_ref, k16_ref, v16_ref, nw_ref, o_ref,
                 sel_scr, keep_scr, o_scr, m_scr, l_scr):
    bw = ATT_BLOCK
    lane = lax.broadcasted_iota(I32, (bw, LANES), 1)
    head0 = lane < HEAD_DIM_A
    head_masks = (jnp.where(head0, 1.0, 0.0).astype(BF16), jnp.where(head0, 0.0, 1.0).astype(BF16))

    rr = lax.broadcasted_iota(I32, (bw, bw), 0)
    cc = lax.broadcasted_iota(I32, (bw, bw), 1)
    piece = bw // DIL_MID
    rp = DIL_MID * (rr % piece) + rr // piece
    cp = DIL_MID * (cc % piece) + cc // piece
    for var, (r, c) in enumerate(((rr, cc), (rr, cc), (rp, cp), (rp, cp))):
        first = var % 2 == 1
        left = (c < r) if first else (c >= r)
        right = jnp.zeros_like(left) if first else jnp.logical_not(left)
        sel_scr[var] = jnp.where(left, 1.0, 0.0)
        keep_scr[var, 0] = jnp.where(left, 1.0, 0.0).astype(BF16)
        keep_scr[var, 1] = jnp.where(right, 1.0, 0.0).astype(BF16)

    def score_matmuls(loaded):
        return [_dot_nt(q * hm, k) for (q, k, _, _, _) in loaded for hm in head_masks]

    def finish_group(scores, loaded):
        probs = []
        for n, s in enumerate(scores):
            _, _, _, var, right_bias = loaded[n // 2]
            folded = jnp.where(sel_scr[var] > 0.5, s[:, 0:bw], s[:, bw:] + right_bias)
            m = jnp.max(folded, axis=-1, keepdims=True)
            p = jnp.exp2(folded - m)
            pb = p.astype(BF16)
            spread = jnp.concatenate([pb * keep_scr[var, 0], pb * keep_scr[var, 1]], axis=1)
            probs.append((spread, jnp.sum(p, axis=-1, keepdims=True), m))
        pvs = [_dot(p, loaded[n // 2][2]) for n, (p, _, _) in enumerate(probs)]
        results = []
        for u in range(len(loaded)):
            (_, l0, m0), (_, l1, m1) = probs[2 * u], probs[2 * u + 1]
            results.append((jnp.where(head0, pvs[2 * u], pvs[2 * u + 1]),
                            jnp.where(head0, jnp.broadcast_to(m0, (bw, LANES)),
                                      jnp.broadcast_to(m1, (bw, LANES))),
                            jnp.where(head0, jnp.broadcast_to(l0, (bw, LANES)),
                                      jnp.broadcast_to(l1, (bw, LANES)))))
        return results

    group = 8

    def first_block_bias(blk):
        return jnp.where(blk == 0, NEG_BIG, 0.0)

    def load_d1(g):
        loaded, starts = [], []
        for i in range(group):
            blk = g * group + i
            qs = pl.multiple_of(blk * piece, piece)
            ks = pl.multiple_of(jnp.maximum(blk - 1, 0) * piece, piece)
            q = jnp.concatenate([q4_ref[r, pl.ds(qs, piece), :] for r in range(DIL_MID)], axis=0)
            k = jnp.concatenate([k4_ref[r, pl.ds(ks + half * piece, piece), :]
                                 for half in range(2) for r in range(DIL_MID)], axis=0)
            v = jnp.concatenate([v4_ref[r, pl.ds(ks + half * piece, piece), :]
                                 for half in range(2) for r in range(DIL_MID)], axis=0)
            loaded.append((q, k, v, jnp.where(blk == 0, 3, 2), first_block_bias(blk)))
            starts.append(qs)

        def store(results):
            for qs, parts in zip(starts, results):
                for r in range(DIL_MID):
                    for scr, val in zip((o_scr, m_scr, l_scr), parts):
                        scr[0, r, pl.ds(qs, piece), :] = val[r * piece:(r + 1) * piece]

        return loaded, store

    def load_d4(g):
        loaded, dsts = [], []
        for i in range(group):
            r, blk = i % DIL_MID, g * (group // DIL_MID) + i // DIL_MID
            qs = pl.multiple_of(blk * bw, bw)
            ks = pl.multiple_of(jnp.maximum(blk - 1, 0) * bw, bw)
            loaded.append((q4_ref[r, pl.ds(qs, bw), :], k4_ref[r, pl.ds(ks, 2 * bw), :],
                           v4_ref[r, pl.ds(ks, 2 * bw), :], jnp.where(blk == 0, 1, 0),
                           first_block_bias(blk)))
            dsts.append((r, qs))

        def store(results):
            for (r, qs), parts in zip(dsts, results):
                for scr, val in zip((o_scr, m_scr, l_scr), parts):
                    scr[1, r, pl.ds(qs, bw), :] = val

        return loaded, store

    def load_d16(g):
        loaded, dsts = [], []
        for i in range(group):
            r, blk = g * (group // 2) + i // 2, i % 2
            loaded.append((q16_ref[r, blk * bw:(blk + 1) * bw, :], k16_ref[r], v16_ref[r],
                           1 - blk, NEG_BIG if blk == 0 else 0.0))
            dsts.append((r % DIL_MID, pl.ds(blk * (bw * DIL_MID) + r // DIL_MID, bw, stride=DIL_MID)))

        def store(results):
            for (sub, dst), parts in zip(dsts, results):
                for scr, val in zip((o_scr, m_scr, l_scr), parts):
                    scr[2, sub, dst, :] = val

        return loaded, store

    def run_group(load):
        def step(g, carry):
            loaded, store = load(g)
            store(finish_group(score_matmuls(loaded), loaded))
            return carry
        return step

    n_groups = SEQ // bw // group
    for load in (load_d1, load_d4, load_d16):
        lax.fori_loop(0, n_groups, run_group(load), 0)

    rows = 256
    hi = lax.broadcasted_iota(I32, (LANES, LANES), 0) // HEAD_DIM_A
    hj = lax.broadcasted_iota(I32, (LANES, LANES), 1) // HEAD_DIM_A
    head_sum = jnp.where(hi == hj, 1.0, 0.0).astype(BF16)
    nw = nw_ref[...]

    def head_sums(x):
        x_hi, x_lo = _split_bf16(x)
        return _dot(x_hi, head_sum) + _dot(x_lo, head_sum)

    def merge(t, carry):
        r = t // (SUB_MID // rows)
        start = pl.multiple_of((t % (SUB_MID // rows)) * rows, rows)
        sl = pl.ds(start, rows)
        own_score = head_sums(q4_ref[r, sl, :].astype(F32) * k4_ref[r, sl, :].astype(F32))
        ms = [m_scr[n, r, sl, :] for n in range(3)]
        mx = jnp.maximum(jnp.maximum(jnp.maximum(ms[0], ms[1]), ms[2]), own_score)
        ws = [jnp.exp2(m - mx) for m in ms]
        w_own = float(len(ms)) * jnp.exp2(own_score - mx)
        num = (ws[0] * o_scr[0, r, sl, :] + ws[1] * o_scr[1, r, sl, :] + ws[2] * o_scr[2, r, sl, :]
               + w_own * v4_ref[r, sl, :].astype(F32))
        den = ws[0] * l_scr[0, r, sl, :] + ws[1] * l_scr[1, r, sl, :] + ws[2] * l_scr[2, r, sl, :] + w_own
        o = num / den
        mean_sq = head_sums(o * o) * (1.0 / HEAD_DIM_A)
        o_ref[pl.ds(start * DIL_MID + r, rows, stride=DIL_MID), :] = o * lax.rsqrt(mean_sq + EPS) * nw
        return carry

    lax.fori_loop(0, SEQ // rows, merge, 0, unroll=8)


def _attention(q4, k4, v4, q16, k16, v16, attn_norm_w):
    hp = D_ATTN // LANES
    mid = pl.BlockSpec((None, DIL_MID, SUB_MID, LANES), lambda b, h: (b, 0, 0, h))
    mx = pl.BlockSpec((None, DIL_MAX, SUB_MAX, LANES), lambda b, h: (b, 0, 0, h))
    scr = pltpu.VMEM((3, DIL_MID, SUB_MID, LANES), F32)
    return pl.pallas_call(
        _attn_kernel,
        grid=(BATCH, hp),
        in_specs=[mid, mid, mid, mx, mx, mx, pl.BlockSpec((1, LANES), lambda b, h: (0, h))],
        out_specs=pl.BlockSpec((None, SEQ, LANES), lambda b, h: (b, 0, h)),
        out_shape=jax.ShapeDtypeStruct((BATCH, SEQ, D_ATTN), F32),
        scratch_shapes=[pltpu.VMEM((4, ATT_BLOCK, ATT_BLOCK), F32),
                        pltpu.VMEM((4, 2, ATT_BLOCK, ATT_BLOCK), BF16), scr, scr, scr],
        compiler_params=pltpu.CompilerParams(vmem_limit_bytes=VMEM_LIMIT),
        name="dilated_attn",
    )(q4, k4, v4, q16, k16, v16, attn_norm_w)


def _hgrn_kernel(qr_ref, fr_ref, ir_ref, gr_ref, lb_ref, nw_ref, o_ref,
                 qe_scr, oi_scr, delta_scr, dec_scr, st_scr):
    sup, c = HGRN_SUPER, HGRN_CHUNK
    nch = sup // c
    n_sup = SEQ // sup
    lbp = lb_ref[...]
    lmx = jnp.max(lbp, axis=0, keepdims=True)
    ex = jnp.exp(lbp - lmx)
    lb = ex[0:1] / (ex[0:1] + ex[1:2])
    nw = nw_ref[...]

    ri = lax.broadcasted_iota(I32, (sup, sup), 0)
    ci = lax.broadcasted_iota(I32, (sup, sup), 1)
    same_chunk = (ri // c) == (ci // c)
    causal = same_chunk & (ci <= ri)
    cum_op = jnp.where(causal, 1.0, 0.0).astype(BF16)

    def chunk_rows(kd, ch):
        parts = []
        if ch > 0:
            parts.append(jnp.zeros((ch * c, LANES), BF16))
        parts.append(kd[ch * c:(ch + 1) * c])
        if ch < nch - 1:
            parts.append(jnp.zeros(((nch - 1 - ch) * c, LANES), BF16))
        return jnp.concatenate(parts, axis=0)

    group = 4

    def independent(g, carry):
        ts = [g * group + i for i in range(group)]
        sls = [pl.ds(pl.multiple_of(t * sup, sup), sup) for t in ts]
        pre = []
        for sl in sls:
            f = lb + (1.0 - lb) * _sigmoid(fr_ref[sl, :])
            logf_hi, logf_lo = _split_bf16(jnp.log(f))
            pre.append((1.0 - f, _dot(cum_op, logf_hi) + _dot(cum_op, logf_lo)))
        mid = []
        for t, sl, (kk, b) in zip(ts, sls, pre):
            b_last = jnp.concatenate(
                [jnp.broadcast_to(b[(ch + 1) * c - 1:(ch + 1) * c], (c, LANES)) for ch in range(nch)],
                axis=0)
            q = qr_ref[sl, :]
            qeb = (q * _sigmoid(q) * jnp.exp(b)).astype(BF16)
            ke = (kk * jnp.exp(-b)).astype(BF16)
            kd = (kk * jnp.exp(b_last - b)).astype(BF16)
            qe_scr[sl, :] = qeb
            dec_rows = jnp.concatenate([b_last[ch * c:ch * c + 1] for ch in range(nch)], axis=0)
            dec_scr[pl.ds(pl.multiple_of(t * nch, nch), nch), :] = jnp.exp(dec_rows)
            v = ir_ref[sl, :]
            vt = v.T.astype(BF16)
            scores = _dot_nt(qeb, ke)
            for pair in range(nch // 2):
                rhs = jnp.concatenate([chunk_rows(kd, 2 * pair), chunk_rows(kd, 2 * pair + 1)], axis=1)
                d2 = _dot(vt, rhs)
                delta_scr[t * nch + 2 * pair] = d2[:, 0:LANES]
                delta_scr[t * nch + 2 * pair + 1] = d2[:, LANES:]
            mid.append((scores, v.astype(BF16)))
        for sl, (scores, vb) in zip(sls, mid):
            a = jnp.where(causal, scores, 0.0)
            oi_scr[sl, :] = _dot(a.astype(BF16), vb)
        return carry

    lax.fori_loop(0, n_sup // group, independent, 0)

    def recur(ch, st):
        st_scr[ch] = st.astype(BF16)
        return st * dec_scr[pl.ds(ch, 1), :] + delta_scr[ch]

    lax.fori_loop(0, SEQ // c, recur, jnp.zeros((LANES, LANES), F32), unroll=8)

    def finish(t, carry):
        sl = pl.ds(pl.multiple_of(t * sup, sup), sup)
        parts = [_dot_nt(qe_scr[pl.ds(pl.multiple_of(t * sup + ch * c, c), c), :], st_scr[t * nch + ch])
                 for ch in range(nch)]
        o = oi_scr[sl, :] + jnp.concatenate(parts, axis=0)
        g = gr_ref[sl, :]
        o_ref[sl, :] = (_rms(o) * nw * (g * _sigmoid(g))).astype(o_ref.dtype)
        return carry

    lax.fori_loop(0, n_sup, finish, 0, unroll=8)


def _hgrn(qr, fr, ir, gr, hgrn_lb, hgrn_norm_w):
    nh = D_REC // LANES
    n_chunks = SEQ // HGRN_CHUNK
    blk = pl.BlockSpec((None, SEQ, LANES), lambda b, h: (b, 0, h))
    return pl.pallas_call(
        _hgrn_kernel,
        grid=(BATCH, nh),
        in_specs=[blk, blk, blk, blk,
                  pl.BlockSpec((2, LANES), lambda b, h: (0, h)),
                  pl.BlockSpec((1, LANES), lambda b, h: (0, h))],
        out_specs=blk,
        out_shape=jax.ShapeDtypeStruct((BATCH, SEQ, D_REC), BF16),
        scratch_shapes=[pltpu.VMEM((SEQ, LANES), BF16),
                        pltpu.VMEM((SEQ, LANES), F32),
                        pltpu.VMEM((n_chunks, LANES, LANES), F32),
                        pltpu.VMEM((n_chunks, LANES), F32),
                        pltpu.VMEM((n_chunks, LANES, LANES), BF16)],
        compiler_params=pltpu.CompilerParams(vmem_limit_bytes=VMEM_LIMIT),
        name="hgrn2",
    )(qr, fr, ir, gr, hgrn_lb, hgrn_norm_w)


def _tile_batch(i):
    return i // (SEQ // TOK_TILE)


def _store_row_tiles(ref, val):
    rows, half = val.shape[0], D_MODEL // 2
    lo = lax.bitcast_convert_type(val[:, :half], U32)
    hi = lax.bitcast_convert_type(val[:, half:], U32)
    words = lax.shift_right_logical(lo, jnp.uint32(16)) | (hi & jnp.uint32(0xFFFF0000))
    for j in range(ROW_TILES):
        ref[pl.ds(j, rows, stride=ROW_TILES), :] = words[:, j * LANES:(j + 1) * LANES]


def _load_row_tiles(ref, rows):
    words = jnp.concatenate(
        [ref[pl.ds(j, rows, stride=ROW_TILES), :] for j in range(ROW_TILES)], axis=1)
    lo = lax.bitcast_convert_type(lax.shift_left(words, jnp.uint32(16)), F32)
    hi = lax.bitcast_convert_type(words & jnp.uint32(0xFFFF0000), F32)
    return jnp.concatenate([lo, hi], axis=1)


def _mid_kernel(ya_ref, yr_ref, x_ref, mod_ref, gpost_ref, gpre_ref, wo_ref, wr_ref, br_ref,
                x1_ref, h2_ref, idx_ref, gate_ref, rank_ref, cnt_ref, carry_ref, wo16):
    i = pl.program_id(0)
    _cast_weights_once(wo_ref, wo16)

    @pl.when(i == 0)
    def _():
        carry_ref[...] = jnp.zeros_like(carry_ref)

    mod = mod_ref[0]
    gate_m, shift_f, scale_f = mod[2:3], mod[3:4], mod[4:5]
    y = _dot(ya_ref[...].astype(BF16), wo16[0:D_ATTN, :]) + _dot(yr_ref[...], wo16[D_ATTN:, :])
    x1 = x_ref[...] + _rms(y) * (gate_m * gpost_ref[...])
    x1_ref[...] = x1
    h2 = _rms(x1) * (gpre_ref[...] * (1.0 + scale_f)) + shift_f
    h2_hi = h2.astype(BF16)
    h2_rounded = h2_hi.astype(F32)
    _store_row_tiles(h2_ref, h2_rounded)

    tm = h2.shape[0]
    h2_lo = (h2 - h2_rounded).astype(BF16)
    wr_hi, wr_lo = _split_bf16(wr_ref[...])
    parts = jnp.concatenate([_dot(h2_hi, jnp.concatenate([wr_hi, wr_lo], axis=1)),
                             _dot(h2_lo, wr_hi),
                             jnp.broadcast_to(br_ref[...], (tm, N_EXPERTS))], axis=1).T
    work = sum(parts[g * N_EXPERTS:(g + 1) * N_EXPERTS] for g in range(LANES // N_EXPERTS))
    eidx = lax.broadcasted_iota(I32, (N_EXPERTS, tm), 0).astype(F32)
    vals, idxs = [], []
    onehot = jnp.zeros((N_EXPERTS, tm), F32)
    for _ in range(TOP_K):
        m = jnp.max(work, axis=0, keepdims=True)
        sel = jnp.min(jnp.where(work == m, eidx, float(N_EXPERTS)), axis=0, keepdims=True)
        hit = eidx == sel
        work = jnp.where(hit, -jnp.inf, work)
        onehot = jnp.where(hit, 1.0, onehot)
        vals.append(m)
        idxs.append(sel)
    ex = [jnp.exp(vv - vals[0]) for vv in vals]
    inv_den = 1.0 / (ex[0] + ex[1] + ex[2] + ex[3])

    ri = lax.broadcasted_iota(I32, (tm, tm), 0)
    ci = lax.broadcasted_iota(I32, (tm, tm), 1)
    strict_upper = jnp.where(ri < ci, 1.0, 0.0).astype(BF16)
    before = _dot(onehot.astype(BF16), strict_upper) + carry_ref[...]
    ranks = [jnp.sum(jnp.where(eidx == idxs[kk], before, 0.0), axis=0, keepdims=True)
             for kk in range(TOP_K)]
    idx_ref[...] = jnp.concatenate(idxs, axis=0).astype(I32)
    rank_ref[...] = jnp.concatenate(ranks, axis=0).astype(I32)
    gates_t = jnp.concatenate([e * inv_den for e in ex]
                              + [jnp.zeros((LANES - TOP_K, tm), F32)], axis=0)
    gate_ref[...] = gates_t.T[:, 0:TOP_K]
    total = carry_ref[...] + jnp.sum(onehot, axis=1, keepdims=True)
    carry_ref[...] = total
    cnt_ref[...] = total.astype(I32)


def _mid(ya, yr, x2, mod3, g_post, g_pre, w_out, w_router, b_router):
    tok = lambda w: pl.BlockSpec((TOK_TILE, w), lambda i: (i, 0))
    const = lambda s: pl.BlockSpec(s, lambda i: (0,) * len(s))
    lanes_tok = pl.BlockSpec((TOP_K, TOK_TILE), lambda i: (0, i))
    return pl.pallas_call(
        _mid_kernel,
        grid=(N_TOK // TOK_TILE,),
        in_specs=[tok(D_ATTN), tok(D_REC), tok(D_MODEL),
                  pl.BlockSpec((1, 6, D_MODEL), lambda i: (_tile_batch(i), 0, 0)),
                  const((1, D_MODEL)), const((1, D_MODEL)),
                  _resident((D_MODEL, D_MODEL)), const((D_MODEL, N_EXPERTS)), const((1, N_EXPERTS))],
        out_specs=[tok(D_MODEL),
                   pl.BlockSpec((TOK_TILE * ROW_TILES, LANES), lambda i: (i, 0)),
                   lanes_tok, tok(TOP_K), lanes_tok, const((N_EXPERTS, 1))],
        out_shape=[jax.ShapeDtypeStruct((N_TOK, D_MODEL), F32),
                   jax.ShapeDtypeStruct((N_TOK * ROW_TILES, LANES), U32),
                   jax.ShapeDtypeStruct((TOP_K, N_TOK), I32),
                   jax.ShapeDtypeStruct((N_TOK, TOP_K), F32),
                   jax.ShapeDtypeStruct((TOP_K, N_TOK), I32),
                   jax.ShapeDtypeStruct((N_EXPERTS, 1), I32)],
        scratch_shapes=[pltpu.VMEM((N_EXPERTS, 1), F32), pltpu.VMEM((D_MODEL, D_MODEL), BF16)],
        compiler_params=pltpu.CompilerParams(dimension_semantics=("arbitrary",),
                                             vmem_limit_bytes=VMEM_LIMIT),
        name="outproj_router",
    )(ya, yr, x2, mod3, g_post, g_pre, w_out, w_router, b_router)


def _sc_mesh():
    return plsc.VectorSubcoreMesh(core_axis_name="c", subcore_axis_name="s")


def _sc_worker_count():
    info = plsc.get_sparse_core_info()
    return info.num_cores, info.num_cores * info.num_subcores


def _sc_dispatch(h_rows, dest_win):
    n_cores, n_workers = _sc_worker_count()
    n_win = N_TOK // SC_WINDOW
    per_worker = n_win // n_workers

    @functools.partial(
        pl.kernel, mesh=_sc_mesh(),
        out_type=jax.ShapeDtypeStruct((N_SLOTS, ROW_TILES, LANES), U32),
        scratch_types=[pltpu.VMEM((TOP_K, SC_WINDOW), I32),
                       pltpu.VMEM((SC_WINDOW, ROW_TILES, LANES), U32),
                       pltpu.SemaphoreType.DMA],
        name="sc_dispatch")
    def run(h_hbm, dest_hbm, xs_hbm, idx_v, rows_v, sem):
        wid = lax.axis_index("s") * n_cores + lax.axis_index("c")

        @pl.loop(0, per_worker)
        def _(j):
            win = wid * per_worker + j
            pltpu.sync_copy(dest_hbm.at[win], idx_v)
            pltpu.sync_copy(h_hbm.at[pl.ds(win * SC_WINDOW, SC_WINDOW)], rows_v)
            copies = [pltpu.async_copy(rows_v, xs_hbm.at[idx_v.at[kk]], sem)
                      for kk in range(TOP_K)]
            for cp in copies:
                cp.wait()

    return run(h_rows, dest_win)


def _sc_collect(y_rows, dest_win):
    n_cores, n_workers = _sc_worker_count()
    n_win = dest_win.shape[0]
    per_worker = n_win // n_workers

    @functools.partial(
        pl.kernel, mesh=_sc_mesh(),
        out_type=jax.ShapeDtypeStruct((TOP_K, n_win * SC_WINDOW, ROW_TILES, LANES), U32),
        scratch_types=[pltpu.VMEM((TOP_K, SC_WINDOW), I32),
                       pltpu.VMEM((SC_WINDOW, ROW_TILES, LANES), U32),
                       pltpu.SemaphoreType.DMA],
        name="sc_collect")
    def run(y_hbm, dest_hbm, yg_hbm, idx_v, rows_v, sem):
        wid = lax.axis_index("s") * n_cores + lax.axis_index("c")

        @pl.loop(0, per_worker)
        def _(j):
            win = wid * per_worker + j
            pltpu.sync_copy(dest_hbm.at[win], idx_v)
            for kk in range(TOP_K):
                pltpu.async_copy(y_hbm.at[idx_v.at[kk]], rows_v, sem).wait()
                pltpu.sync_copy(rows_v, yg_hbm.at[kk, pl.ds(win * SC_WINDOW, SC_WINDOW)])

    return run(y_rows, dest_win)


def _expert_kernel(be_ref, nu_ref, nx_ref, nv_ref, x_ref, wgu_hbm, bgu_ref, wd_hbm, bd_ref, y_ref,
                   wgu32, wd32, wgu16, wd16, sems):
    i = pl.program_id(0)
    e = be_ref[i]
    prev = be_ref[jnp.maximum(i - 1, 0)]

    def weight_copies(ex):
        return (pltpu.make_async_copy(wgu_hbm.at[ex], wgu32, sems.at[0]),
                pltpu.make_async_copy(wd_hbm.at[ex], wd32, sems.at[1]))

    @pl.when(i == 0)
    def _():
        for cp in weight_copies(e):
            cp.start()

    @pl.when((i == 0) | (e != prev))
    def _():
        for cp in weight_copies(e):
            cp.wait()
        rows = 128

        def cast(r, carry):
            sl = pl.ds(pl.multiple_of(r * rows, rows), rows)
            wgu16[sl, :] = wgu32[sl, :].astype(BF16)
            wd16[sl, :] = wd32[sl, :].astype(BF16)
            return carry

        lax.fori_loop(0, D_MODEL // rows, cast, 0)
        nxt = nx_ref[i]

        @pl.when(nxt >= 0)
        def _():
            for cp in weight_copies(nxt):
                cp.start()

    def run_rows(rows):
        x = _load_row_tiles(x_ref, rows).astype(BF16)
        bgu = bgu_ref[0]
        glu = _dot(x, wgu16[:, 0:D_FF]) + bgu[:, 0:D_FF]
        lin = _dot(x, wgu16[:, D_FF:]) + bgu[:, D_FF:]
        glu = jnp.minimum(glu, SWIGLU_LIMIT)
        lin = jnp.clip(lin, -SWIGLU_LIMIT, SWIGLU_LIMIT)
        act = glu * _sigmoid(SWIGLU_ALPHA * glu) * (lin + 1.0)
        y = _dot(act.astype(BF16), wd16[...]) + bd_ref[0]
        _store_row_tiles(y_ref, y.astype(BF16).astype(F32))

    valid = nv_ref[i]
    for rows in range(MOE_ROWS_STEP, MOE_BLOCK + 1, MOE_ROWS_STEP):
        pl.when((valid > rows - MOE_ROWS_STEP) & (valid <= rows))(functools.partial(run_rows, rows))


def _experts(blk_e, n_used, next_e, blk_valid, xs2, w_gu, b_gu3, w_down, b_down3):
    row_blk = pl.BlockSpec((MOE_BLOCK * ROW_TILES, LANES),
                           lambda i, be, nu, nx, nv: (jnp.minimum(i, nu[0] - 1), 0))
    grid_spec = pltpu.PrefetchScalarGridSpec(
        num_scalar_prefetch=4,
        grid=(N_BLOCKS,),
        in_specs=[row_blk,
                  pl.BlockSpec(memory_space=pl.ANY),
                  pl.BlockSpec((1, 1, 2 * D_FF), lambda i, be, nu, nx, nv: (be[i], 0, 0)),
                  pl.BlockSpec(memory_space=pl.ANY),
                  pl.BlockSpec((1, 1, D_MODEL), lambda i, be, nu, nx, nv: (be[i], 0, 0))],
        out_specs=row_blk,
        scratch_shapes=[pltpu.VMEM((D_MODEL, 2 * D_FF), F32),
                        pltpu.VMEM((D_FF, D_MODEL), F32),
                        pltpu.VMEM((D_MODEL, 2 * D_FF), BF16),
                        pltpu.VMEM((D_FF, D_MODEL), BF16),
                        pltpu.SemaphoreType.DMA((2,))],
    )
    return pl.pallas_call(
        _expert_kernel,
        grid_spec=grid_spec,
        out_shape=jax.ShapeDtypeStruct((N_SLOTS * ROW_TILES, LANES), U32),
        compiler_params=pltpu.CompilerParams(dimension_semantics=("arbitrary",),
                                             vmem_limit_bytes=VMEM_LIMIT),
        name="experts",
    )(blk_e, n_used, next_e, blk_valid, xs2, w_gu, b_gu3, w_down, b_down3)


def _final_kernel(yg_ref, gate_ref, x1_ref, mod_ref, gpost_ref, *maybe_alias_and_out):
    o_ref = maybe_alias_and_out[-1]
    gates = gate_ref[...]
    y = None
    for kk in range(TOP_K):
        part = _load_row_tiles(yg_ref.at[kk], TOK_TILE) * gates[:, kk:kk + 1]
        y = part if y is None else y + part
    gate_f = mod_ref[0][5:6]
    o_ref[...] = x1_ref[...] + _rms(y) * (gate_f * gpost_ref[...])


def _final(part, yg3, gates, x1, mod3, g_post, out_so_far):
    tiles = SEQ // TOK_TILE
    tok = lambda w: pl.BlockSpec((TOK_TILE, w), lambda i: (part * tiles + i, 0))
    in_specs = [pl.BlockSpec((TOP_K, TOK_TILE * ROW_TILES, LANES), lambda i: (0, i, 0)),
                tok(TOP_K), tok(D_MODEL),
                pl.BlockSpec((1, 6, D_MODEL), lambda i: (part, 0, 0)),
                pl.BlockSpec((1, D_MODEL), lambda i: (0, 0))]
    args = [yg3, gates, x1, mod3, g_post]
    aliases = {}
    if out_so_far is not None:
        in_specs.append(pl.BlockSpec(memory_space=pl.ANY))
        args.append(out_so_far)
        aliases = {len(args) - 1: 0}
    return pl.pallas_call(
        _final_kernel,
        grid=(tiles,),
        in_specs=in_specs,
        out_specs=tok(D_MODEL),
        out_shape=jax.ShapeDtypeStruct((N_TOK, D_MODEL), F32),
        input_output_aliases=aliases,
        compiler_params=pltpu.CompilerParams(vmem_limit_bytes=VMEM_LIMIT),
        name="combine_final",
    )(*args)


def _slot_kernel(idx_ref, rank_ref, start_ref, dest_ref):
    tn = idx_ref.shape[1]
    experts = lax.broadcasted_iota(I32, (N_EXPERTS, tn), 0)
    starts = start_ref[...].astype(F32)
    rows = [jnp.sum(jnp.where(experts == idx_ref[kk:kk + 1, :], starts, 0.0), axis=0, keepdims=True)
            for kk in range(TOP_K)]
    dest_ref[...] = jnp.concatenate(rows, axis=0).astype(I32) + rank_ref[...]


def _slots(idx, rank, pstart):
    tn = N_TOK // 2
    lanes_tok = pl.BlockSpec((TOP_K, tn), lambda i: (0, i))
    return pl.pallas_call(
        _slot_kernel,
        grid=(N_TOK // tn,),
        in_specs=[lanes_tok, lanes_tok, pl.BlockSpec((N_EXPERTS, 1), lambda i: (0, 0))],
        out_specs=lanes_tok,
        out_shape=jax.ShapeDtypeStruct((TOP_K, N_TOK), I32),
        name="moe_slots",
    )(idx, rank, pstart.reshape(N_EXPERTS, 1))


def _routing_tables(idx, rank, counts):
    counts = counts.reshape(N_EXPERTS)
    experts = jnp.arange(N_EXPERTS, dtype=I32)
    padded = ((counts + MOE_BLOCK - 1) // MOE_BLOCK) * MOE_BLOCK
    pend = jnp.cumsum(padded)
    pstart = pend - padded
    dest = _slots(idx, rank, pstart)
    dest_win = dest.reshape(TOP_K, N_TOK // SC_WINDOW, SC_WINDOW).transpose(1, 0, 2)
    n_used = (pend[-1] // MOE_BLOCK).astype(I32).reshape(1)
    blk_start = jnp.arange(N_BLOCKS, dtype=I32) * MOE_BLOCK
    blk_e = jnp.sum(blk_start[:, None] >= pend[None, :], axis=1).astype(I32)
    last_e = jnp.max(jnp.where(counts > 0, experts, 0))
    blk_e = jnp.minimum(blk_e, last_e)
    later = (experts[None, :] > experts[:, None]) & (counts[None, :] > 0)
    next_nonempty = jnp.min(jnp.where(later, experts[None, :], N_EXPERTS), axis=1)
    next_nonempty = jnp.where(next_nonempty == N_EXPERTS, -1, next_nonempty).astype(I32)
    of_block = lambda table: jnp.sum(jnp.where(blk_e[:, None] == experts, table, 0), axis=1).astype(I32)
    next_e = of_block(next_nonempty)
    blk_valid = jnp.clip(of_block(counts) - (blk_start - of_block(pstart)), 0, MOE_BLOCK)
    blk_valid = jnp.where(blk_start < pend[-1], blk_valid, 0).astype(I32)
    return dest_win, blk_e, n_used, next_e, blk_valid


def kernel(x, c, w_ada, b_ada, g_pre_mix, g_post_mix, w_in, attn_norm_w, hgrn_lb, hgrn_norm_w,
           w_out, g_pre_ffn, g_post_ffn, w_router, b_router, w_gu, b_gu, w_down, b_down):
    c_pad = jnp.pad(c, ((0, SUBLANES - BATCH), (0, 0)))
    mod = _ada_mod(c_pad, w_ada[0], b_ada)
    mod3 = mod[:BATCH].reshape(BATCH, 6, D_MODEL)

    x2 = x.reshape(N_TOK, D_MODEL)
    q4, k4, v4, q16, k16, v16, qr, fr, ir, gr = _inproj(x2, mod3, g_pre_mix, w_in[0])
    nat = lambda t: t.reshape(BATCH, SEQ, D_REC)
    ya = _attention(q4, k4, v4, q16, k16, v16, attn_norm_w)
    yr = _hgrn(nat(qr), nat(fr), nat(ir), nat(gr), hgrn_lb, hgrn_norm_w)

    x1, h2, idx, gates, rank, counts = _mid(
        ya.reshape(N_TOK, D_ATTN), yr.reshape(N_TOK, D_REC), x2, mod3, g_post_mix, g_pre_ffn,
        w_out[0], w_router[0], b_router)

    dest_win, blk_e, n_used, next_e, blk_valid = _routing_tables(idx, rank, counts)
    xs = _sc_dispatch(h2.reshape(N_TOK, ROW_TILES, LANES), dest_win)
    ys = _experts(blk_e, n_used, next_e, blk_valid, xs.reshape(N_SLOTS * ROW_TILES, LANES),
                  w_gu[0], b_gu[0].reshape(N_EXPERTS, 1, 2 * D_FF),
                  w_down[0], b_down[0].reshape(N_EXPERTS, 1, D_MODEL))
    ys3 = ys.reshape(N_SLOTS, ROW_TILES, LANES)
    win_per_seq = SEQ // SC_WINDOW
    out = None
    for b in range(BATCH):
        yg = _sc_collect(ys3, dest_win[b * win_per_seq:(b + 1) * win_per_seq])
        out = _final(b, yg.reshape(TOP_K, SEQ * ROW_TILES, LANES), gates, x1, mod3, g_post_ffn, out)
    return out.reshape(BATCH, SEQ, D_MODEL)
```

```python
import functools
import math

import jax
import jax.numpy as jnp
from jax import lax
from jax.experimental import pallas as pl
from jax.experimental.pallas import tpu as pltpu
from jax.experimental.pallas import tpu_sc as plsc

F32 = jnp.float32
BF16 = jnp.bfloat16
I32 = jnp.int32
U32 = jnp.uint32

D_MODEL = 1024
BATCH = 4
SEQ = 4096
N_TOK = BATCH * SEQ
D_ATTN = 512
HEAD_DIM_A = 64
ATT_BLOCK = 128
DIL_MID = 4
DIL_MAX = 16
SUB_MID = SEQ // DIL_MID
SUB_MAX = SEQ // DIL_MAX
D_REC = 512
HGRN_CHUNK = 32
HGRN_SUPER = 256
N_EXPERTS = 32
TOP_K = 4
D_FF = 1024
SWIGLU_LIMIT = 7.0
SWIGLU_ALPHA = 1.702
EPS = 1e-6
NEG_BIG = -1e30
Q_SCALE = HEAD_DIM_A ** -0.5 * math.log2(math.e)

LANES = 128
SUBLANES = 8
ROW_TILES = D_MODEL // 2 // LANES

TOK_TILE = 512
MOE_BLOCK = 1024
MOE_ROWS_STEP = 128
N_SLOTS = N_TOK * TOP_K + N_EXPERTS * MOE_BLOCK
N_BLOCKS = N_SLOTS // MOE_BLOCK
SC_WINDOW = 128
SC_SUM_WINDOW = 32
SC_LANES = 16
VMEM_LIMIT = 56 * 1024 * 1024


def _sigmoid(x):
    return 1.0 / (1.0 + jnp.exp(-x))


def _dot(a, b):
    return jnp.dot(a, b, preferred_element_type=F32)


def _dot_nt(a, b):
    return lax.dot_general(a, b, (((1,), (1,)), ((), ())), preferred_element_type=F32)


def _split_bf16(x):
    hi = x.astype(BF16)
    return hi, (x - hi.astype(F32)).astype(BF16)


def _rms(x):
    return x * lax.rsqrt(jnp.mean(x * x, axis=-1, keepdims=True) + EPS)


def _ada_kernel(c_ref, w_ref, b_ref, o_ref):
    c = c_ref[...]
    cond = c * _sigmoid(c)
    c_hi, c_lo = _split_bf16(cond)
    w_hi, w_lo = _split_bf16(w_ref[...])
    o_ref[...] = _dot(c_hi, w_hi) + _dot(c_lo, w_hi) + _dot(c_hi, w_lo) + b_ref[...]


def _ada_mod(c_pad, w_ada, b_ada):
    n = w_ada.shape[1]
    tn = 1536
    return pl.pallas_call(
        _ada_kernel,
        grid=(n // tn,),
        in_specs=[pl.BlockSpec((SUBLANES, D_MODEL), lambda j: (0, 0)),
                  pl.BlockSpec((D_MODEL, tn), lambda j: (0, j)),
                  pl.BlockSpec((1, tn), lambda j: (0, j))],
        out_specs=pl.BlockSpec((SUBLANES, tn), lambda j: (0, j)),
        out_shape=jax.ShapeDtypeStruct((SUBLANES, n), F32),
        compiler_params=pltpu.CompilerParams(vmem_limit_bytes=VMEM_LIMIT),
        name="ada_mod",
    )(c_pad, w_ada, b_ada)


def _cast_weights_once(w_ref, w16):
    rows = 128

    @pl.when(pl.program_id(0) == 0)
    def _():
        def cast(r, carry):
            sl = pl.ds(pl.multiple_of(r * rows, rows), rows)
            w16[sl, :] = w_ref[sl, :].astype(BF16)
            return carry

        lax.fori_loop(0, w_ref.shape[0] // rows, cast, 0)


def _resident(shape):
    return pl.BlockSpec(shape, lambda i: (0,) * len(shape), pipeline_mode=pl.Buffered(1))


def _inproj_kernel(x_ref, mod_ref, g_ref, w_ref, q4, k4, v4, q16, k16, v16, qr, fr, ir, gr,
                   w16, stage_nat, stage_mid):
    _cast_weights_once(w_ref, w16)
    mod = mod_ref[0]
    shift, scale = mod[0:1], mod[1:2]
    h = _rms(x_ref[...]) * (g_ref[...] * (1.0 + scale)) + shift
    hb = h.astype(BF16)
    slabs = D_ATTN // LANES
    rows_mid = TOK_TILE // DIL_MID
    rows_max = TOK_TILE // DIL_MAX

    def proj(j):
        return _dot(hb, w16[:, j * D_ATTN:(j + 1) * D_ATTN])

    for j, (o_mid, o_max) in enumerate(((q4, q16), (k4, k16), (v4, v16))):
        r = proj(j)
        if j == 0:
            r = r * Q_SCALE
        for cs in range(slabs):
            stage_nat[cs] = r[:, cs * LANES:(cs + 1) * LANES]
        for cs in range(slabs):
            lanes = slice(cs * LANES, (cs + 1) * LANES)
            for sub in range(DIL_MID):
                piece = stage_nat[cs, pl.ds(sub, rows_mid, stride=DIL_MID), :]
                o_mid[sub, :, lanes] = piece.astype(BF16)
                stage_mid[cs, sub] = piece
            for sub in range(DIL_MAX):
                piece = stage_mid[cs, sub % DIL_MID, pl.ds(sub // DIL_MID, rows_max, stride=DIL_MID), :]
                o_max[sub, :, lanes] = piece.astype(BF16)
    for j, o_ref in enumerate((qr, fr, ir, gr)):
        o_ref[...] = proj(3 + j)


def _inproj(x2, mod3, g_pre, w_in):
    tiles_per_seq = SEQ // TOK_TILE
    rows_mid = TOK_TILE // DIL_MID
    rows_max = TOK_TILE // DIL_MAX
    mid = pl.BlockSpec((None, DIL_MID, rows_mid, D_ATTN),
                       lambda i: (i // tiles_per_seq, 0, i % tiles_per_seq, 0))
    mx = pl.BlockSpec((None, DIL_MAX, rows_max, D_ATTN),
                      lambda i: (i // tiles_per_seq, 0, i % tiles_per_seq, 0))
    nat = pl.BlockSpec((TOK_TILE, D_REC), lambda i: (i, 0))
    mid_shape = jax.ShapeDtypeStruct((BATCH, DIL_MID, SUB_MID, D_ATTN), BF16)
    mx_shape = jax.ShapeDtypeStruct((BATCH, DIL_MAX, SUB_MAX, D_ATTN), BF16)
    nat_shape = jax.ShapeDtypeStruct((N_TOK, D_REC), F32)
    return pl.pallas_call(
        _inproj_kernel,
        grid=(N_TOK // TOK_TILE,),
        in_specs=[pl.BlockSpec((TOK_TILE, D_MODEL), lambda i: (i, 0)),
                  pl.BlockSpec((1, 6, D_MODEL), lambda i: (i // tiles_per_seq, 0, 0)),
                  pl.BlockSpec((1, D_MODEL), lambda i: (0, 0)),
                  _resident(w_in.shape)],
        out_specs=[mid, mid, mid, mx, mx, mx, nat, nat, nat, nat],
        out_shape=[mid_shape] * 3 + [mx_shape] * 3 + [nat_shape] * 4,
        scratch_shapes=[pltpu.VMEM(w_in.shape, BF16),
                        pltpu.VMEM((D_ATTN // LANES, TOK_TILE, LANES), F32),
                        pltpu.VMEM((D_ATTN // LANES, DIL_MID, rows_mid, LANES), F32)],
        compiler_params=pltpu.CompilerParams(dimension_semantics=("arbitrary",),
                                             vmem_limit_bytes=VMEM_LIMIT),
        name="inproj",
    )(x2, mod3, g_pre, w_in)


def _attn_kernel(q4_ref, k4_ref, v4_ref, q16_ref, k16_ref, v16_ref, nw_ref, o_ref,
                 sel_scr, keep_scr, o_scr, m_scr, l_scr):
    bw = ATT_BLOCK
    lane = lax.broadcasted_iota(I32, (bw, LANES), 1)
    head0 = lane < HEAD_DIM_A
    head_masks = (jnp.where(head0, 1.0, 0.0).astype(BF16), jnp.where(head0, 0.0, 1.0).astype(BF16))

    rr = lax.broadcasted_iota(I32, (bw, bw), 0)
    cc = lax.broadcasted_iota(I32, (bw, bw), 1)
    piece = bw // DIL_MID
    rp = DIL_MID * (rr % piece) + rr // piece
    cp = DIL_MID * (cc % piece) + cc // piece
    for var, (r, c) in enumerate(((rr, cc), (rr, cc), (rp, cp), (rp, cp))):
        first = var % 2 == 1
        left = (c < r) if first else (c >= r)
        right = jnp.zeros_like(left) if first else jnp.logical_not(left)
        sel_scr[var] = jnp.where(left, 1.0, 0.0)
        keep_scr[var, 0] = jnp.where(left, 1.0, 0.0).astype(BF16)
        keep_scr[var, 1] = jnp.where(right, 1.0, 0.0).astype(BF16)

    def score_matmuls(loaded):
        return [_dot_nt(q * hm, k) for (q, k, _, _, _) in loaded for hm in head_masks]

    def finish_group(scores, loaded):
        probs = []
        for n, s in enumerate(scores):
            _, _, _, var, right_bias = loaded[n // 2]
            folded = jnp.where(sel_scr[var] > 0.5, s[:, 0:bw], s[:, bw:] + right_bias)
            m = jnp.max(folded, axis=-1, keepdims=True)
            p = jnp.exp2(folded - m)
            pb = p.astype(BF16)
            spread = jnp.concatenate([pb * keep_scr[var, 0], pb * keep_scr[var, 1]], axis=1)
            probs.append((spread, jnp.sum(p, axis=-1, keepdims=True), m))
        pvs = [_dot(p, loaded[n // 2][2]) for n, (p, _, _) in enumerate(probs)]
        results = []
        for u in range(len(loaded)):
            (_, l0, m0), (_, l1, m1) = probs[2 * u], probs[2 * u + 1]
            results.append((jnp.where(head0, pvs[2 * u], pvs[2 * u + 1]),
                            jnp.where(head0, jnp.broadcast_to(m0, (bw, LANES)),
                                      jnp.broadcast_to(m1, (bw, LANES))),
                            jnp.where(head0, jnp.broadcast_to(l0, (bw, LANES)),
                                      jnp.broadcast_to(l1, (bw, LANES)))))
        return results

    group = 8

    def first_block_bias(blk):
        return jnp.where(blk == 0, NEG_BIG, 0.0)

    def load_d1(g):
        loaded, starts = [], []
        for i in range(group):
            blk = g * group + i
            qs = pl.multiple_of(blk * piece, piece)
            ks = pl.multiple_of(jnp.maximum(blk - 1, 0) * piece, piece)
            q = jnp.concatenate([q4_ref[r, pl.ds(qs, piece), :] for r in range(DIL_MID)], axis=0)
            k = jnp.concatenate([k4_ref[r, pl.ds(ks + half * piece, piece), :]
                                 for half in range(2) for r in range(DIL_MID)], axis=0)
            v = jnp.concatenate([v4_ref[r, pl.ds(ks + half * piece, piece), :]
                                 for half in range(2) for r in range(DIL_MID)], axis=0)
            loaded.append((q, k, v, jnp.where(blk == 0, 3, 2), first_block_bias(blk)))
            starts.append(qs)

        def store(results):
            for qs, parts in zip(starts, results):
                for r in range(DIL_MID):
                    for scr, val in zip((o_scr, m_scr, l_scr), parts):
                        scr[0, r, pl.ds(qs, piece), :] = val[r * piece:(r + 1) * piece]

        return loaded, store

    def load_d4(g):
        loaded, dsts = [], []
        for i in range(group):
            r, blk = i % DIL_MID, g * (group // DIL_MID) + i // DIL_MID
            qs = pl.multiple_of(blk * bw, bw)
            ks = pl.multiple_of(jnp.maximum(blk - 1, 0) * bw, bw)
            loaded.append((q4_ref[r, pl.ds(qs, bw), :], k4_ref[r, pl.ds(ks, 2 * bw), :],
                           v4_ref[r, pl.ds(ks, 2 * bw), :], jnp.where(blk == 0, 1, 0),
                           first_block_bias(blk)))
            dsts.append((r, qs))

        def store(results):
            for (r, qs), parts in zip(dsts, results):
                for scr, val in zip((o_scr, m_scr, l_scr), parts):
                    scr[1, r, pl.ds(qs, bw), :] = val

        return loaded, store

    def load_d16(g):
        loaded, dsts = [], []
        for i in range(group):
            r, blk = g * (group // 2) + i // 2, i % 2
            loaded.append((q16_ref[r, blk * bw:(blk + 1) * bw, :], k16_ref[r], v16_ref[r],
                           1 - blk, NEG_BIG if blk == 0 else 0.0))
            dsts.append((r % DIL_MID, pl.ds(blk * (bw * DIL_MID) + r // DIL_MID, bw, stride=DIL_MID)))

        def store(results):
            for (sub, dst), parts in zip(dsts, results):
                for scr, val in zip((o_scr, m_scr, l_scr), parts):
                    scr[2, sub, dst, :] = val

        return loaded, store

    def run_group(load):
        def step(g, carry):
            loaded, store = load(g)
            store(finish_group(score_matmuls(loaded), loaded))
            return carry
        return step

    n_groups = SEQ // bw // group
    for load in (load_d1, load_d4, load_d16):
        lax.fori_loop(0, n_groups, run_group(load), 0)

    rows = 256
    hi = lax.broadcasted_iota(I32, (LANES, LANES), 0) // HEAD_DIM_A
    hj = lax.broadcasted_iota(I32, (LANES, LANES), 1) // HEAD_DIM_A
    head_sum = jnp.where(hi == hj, 1.0, 0.0).astype(BF16)
    nw = nw_ref[...]

    def head_sums(x):
        x_hi, x_lo = _split_bf16(x)
        return _dot(x_hi, head_sum) + _dot(x_lo, head_sum)

    def merge(t, carry):
        r = t // (SUB_MID // rows)
        start = pl.multiple_of((t % (SUB_MID // rows)) * rows, rows)
        sl = pl.ds(start, rows)
        own_score = head_sums(q4_ref[r, sl, :].astype(F32) * k4_ref[r, sl, :].astype(F32))
        ms = [m_scr[n, r, sl, :] for n in range(3)]
        mx = jnp.maximum(jnp.maximum(jnp.maximum(ms[0], ms[1]), ms[2]), own_score)
        ws = [jnp.exp2(m - mx) for m in ms]
        w_own = float(len(ms)) * jnp.exp2(own_score - mx)
        num = (ws[0] * o_scr[0, r, sl, :] + ws[1] * o_scr[1, r, sl, :] + ws[2] * o_scr[2, r, sl, :]
               + w_own * v4_ref[r, sl, :].astype(F32))
        den = ws[0] * l_scr[0, r, sl, :] + ws[1] * l_scr[1, r, sl, :] + ws[2] * l_scr[2, r, sl, :] + w_own
        o = num / den
        mean_sq = head_sums(o * o) * (1.0 / HEAD_DIM_A)
        o_ref[pl.ds(start * DIL_MID + r, rows, stride=DIL_MID), :] = o * lax.rsqrt(mean_sq + EPS) * nw
        return carry

    lax.fori_loop(0, SEQ // rows, merge, 0, unroll=8)


def _attention(q4, k4, v4, q16, k16, v16, attn_norm_w):
    hp = D_ATTN // LANES
    mid = pl.BlockSpec((None, DIL_MID, SUB_MID, LANES), lambda b, h: (b, 0, 0, h))
    mx = pl.BlockSpec((None, DIL_MAX, SUB_MAX, LANES), lambda b, h: (b, 0, 0, h))
    scr = pltpu.VMEM((3, DIL_MID, SUB_MID, LANES), F32)
    return pl.pallas_call(
        _attn_kernel,
        grid=(BATCH, hp),
        in_specs=[mid, mid, mid, mx, mx, mx, pl.BlockSpec((1, LANES), lambda b, h: (0, h))],
        out_specs=pl.BlockSpec((None, SEQ, LANES), lambda b, h: (b, 0, h)),
        out_shape=jax.ShapeDtypeStruct((BATCH, SEQ, D_ATTN), F32),
        scratch_shapes=[pltpu.VMEM((4, ATT_BLOCK, ATT_BLOCK), F32),
                        pltpu.VMEM((4, 2, ATT_BLOCK, ATT_BLOCK), BF16), scr, scr, scr],
        compiler_params=pltpu.CompilerParams(vmem_limit_bytes=VMEM_LIMIT),
        name="dilated_attn",
    )(q4, k4, v4, q16, k16, v16, attn_norm_w)


def _hgrn_kernel(qr_ref, fr_ref, ir_ref, gr_ref, lb_ref, nw_ref, o_ref,
                 qe_scr, oi_scr, delta_scr, dec_scr, st_scr):
    sup, c = HGRN_SUPER, HGRN_CHUNK
    nch = sup // c
    n_sup = SEQ // sup
    lbp = lb_ref[...]
    lmx = jnp.max(lbp, axis=0, keepdims=True)
    ex = jnp.exp(lbp - lmx)
    lb = ex[0:1] / (ex[0:1] + ex[1:2])
    nw = nw_ref[...]

    ri = lax.broadcasted_iota(I32, (sup, sup), 0)
    ci = lax.broadcasted_iota(I32, (sup, sup), 1)
    same_chunk = (ri // c) == (ci // c)
    causal = same_chunk & (ci <= ri)
    cum_op = jnp.where(causal, 1.0, 0.0).astype(BF16)

    def chunk_rows(kd, ch):
        parts = []
        if ch > 0:
            parts.append(jnp.zeros((ch * c, LANES), BF16))
        parts.append(kd[ch * c:(ch + 1) * c])
        if ch < nch - 1:
            parts.append(jnp.zeros(((nch - 1 - ch) * c, LANES), BF16))
        return jnp.concatenate(parts, axis=0)

    group = 4

    def independent(g, carry):
        ts = [g * group + i for i in range(group)]
        sls = [pl.ds(pl.multiple_of(t * sup, sup), sup) for t in ts]
        pre = []
        for sl in sls:
            f = lb + (1.0 - lb) * _sigmoid(fr_ref[sl, :])
            logf_hi, logf_lo = _split_bf16(jnp.log(f))
            pre.append((1.0 - f, _dot(cum_op, logf_hi) + _dot(cum_op, logf_lo)))
        mid = []
        for t, sl, (kk, b) in zip(ts, sls, pre):
            b_last = jnp.concatenate(
                [jnp.broadcast_to(b[(ch + 1) * c - 1:(ch + 1) * c], (c, LANES)) for ch in range(nch)],
                axis=0)
            q = qr_ref[sl, :]
            qeb = (q * _sigmoid(q) * jnp.exp(b)).astype(BF16)
            ke = (kk * jnp.exp(-b)).astype(BF16)
            kd = (kk * jnp.exp(b_last - b)).astype(BF16)
            qe_scr[sl, :] = qeb
            dec_rows = jnp.concatenate([b_last[ch * c:ch * c + 1] for ch in range(nch)], axis=0)
            dec_scr[pl.ds(pl.multiple_of(t * nch, nch), nch), :] = jnp.exp(dec_rows)
            v = ir_ref[sl, :]
            vt = v.T.astype(BF16)
            scores = _dot_nt(qeb, ke)
            for pair in range(nch // 2):
                rhs = jnp.concatenate([chunk_rows(kd, 2 * pair), chunk_rows(kd, 2 * pair + 1)], axis=1)
                d2 = _dot(vt, rhs)
                delta_scr[t * nch + 2 * pair] = d2[:, 0:LANES]
                delta_scr[t * nch + 2 * pair + 1] = d2[:, LANES:]
            mid.append((scores, v.astype(BF16)))
        for sl, (scores, vb) in zip(sls, mid):
            a = jnp.where(causal, scores, 0.0)
            oi_scr[sl, :] = _dot(a.astype(BF16), vb)
        return carry

    lax.fori_loop(0, n_sup // group, independent, 0)

    def recur(ch, st):
        st_scr[ch] = st.astype(BF16)
        return st * dec_scr[pl.ds(ch, 1), :] + delta_scr[ch]

    lax.fori_loop(0, SEQ // c, recur, jnp.zeros((LANES, LANES), F32), unroll=8)

    def finish(t, carry):
        sl = pl.ds(pl.multiple_of(t * sup, sup), sup)
        parts = [_dot_nt(qe_scr[pl.ds(pl.multiple_of(t * sup + ch * c, c), c), :], st_scr[t * nch + ch])
                 for ch in range(nch)]
        o = oi_scr[sl, :] + jnp.concatenate(parts, axis=0)
        g = gr_ref[sl, :]
        o_ref[sl, :] = (_rms(o) * nw * (g * _sigmoid(g))).astype(o_ref.dtype)
        return carry

    lax.fori_loop(0, n_sup, finish, 0, unroll=8)


def _hgrn(qr, fr, ir, gr, hgrn_lb, hgrn_norm_w):
    nh = D_REC // LANES
    n_chunks = SEQ // HGRN_CHUNK
    blk = pl.BlockSpec((None, SEQ, LANES), lambda b, h: (b, 0, h))
    return pl.pallas_call(
        _hgrn_kernel,
        grid=(BATCH, nh),
        in_specs=[blk, blk, blk, blk,
                  pl.BlockSpec((2, LANES), lambda b, h: (0, h)),
                  pl.BlockSpec((1, LANES), lambda b, h: (0, h))],
        out_specs=blk,
        out_shape=jax.ShapeDtypeStruct((BATCH, SEQ, D_REC), BF16),
        scratch_shapes=[pltpu.VMEM((SEQ, LANES), BF16),
                        pltpu.VMEM((SEQ, LANES), F32),
                        pltpu.VMEM((n_chunks, LANES, LANES), F32),
                        pltpu.VMEM((n_chunks, LANES), F32),
                        pltpu.VMEM((n_chunks, LANES, LANES), BF16)],
        compiler_params=pltpu.CompilerParams(vmem_limit_bytes=VMEM_LIMIT),
        name="hgrn2",
    )(qr, fr, ir, gr, hgrn_lb, hgrn_norm_w)


def _tile_batch(i):
    return i // (SEQ // TOK_TILE)


def _store_row_tiles(ref, val):
    rows, half = val.shape[0], D_MODEL // 2
    lo = lax.bitcast_convert_type(val[:, :half], U32)
    hi = lax.bitcast_convert_type(val[:, half:], U32)
    words = lax.shift_right_logical(lo, jnp.uint32(16)) | (hi & jnp.uint32(0xFFFF0000))
    for j in range(ROW_TILES):
        ref[pl.ds(j, rows, stride=ROW_TILES), :] = words[:, j * LANES:(j + 1) * LANES]


def _load_row_tiles(ref, rows):
    words = jnp.concatenate(
        [ref[pl.ds(j, rows, stride=ROW_TILES), :] for j in range(ROW_TILES)], axis=1)
    lo = lax.bitcast_convert_type(lax.shift_left(words, jnp.uint32(16)), F32)
    hi = lax.bitcast_convert_type(words & jnp.uint32(0xFFFF0000), F32)
    return jnp.concatenate([lo, hi], axis=1)


def _mid_kernel(ya_ref, yr_ref, x_ref, mod_ref, gpost_ref, gpre_ref, wo_ref, wr_ref, br_ref,
                x1_ref, h2_ref, idx_ref, gate_ref, rank_ref, cnt_ref, carry_ref, wo16):
    i = pl.program_id(0)
    _cast_weights_once(wo_ref, wo16)

    @pl.when(i == 0)
    def _():
        carry_ref[...] = jnp.zeros_like(carry_ref)

    mod = mod_ref[0]
    gate_m, shift_f, scale_f = mod[2:3], mod[3:4], mod[4:5]
    y = _dot(ya_ref[...].astype(BF16), wo16[0:D_ATTN, :]) + _dot(yr_ref[...], wo16[D_ATTN:, :])
    x1 = x_ref[...] + _rms(y) * (gate_m * gpost_ref[...])
    x1_ref[...] = x1
    h2 = _rms(x1) * (gpre_ref[...] * (1.0 + scale_f)) + shift_f
    h2_hi = h2.astype(BF16)
    h2_rounded = h2_hi.astype(F32)
    _store_row_tiles(h2_ref, h2_rounded)

    tm = h2.shape[0]
    h2_lo = (h2 - h2_rounded).astype(BF16)
    wr_hi, wr_lo = _split_bf16(wr_ref[...])
    parts = jnp.concatenate([_dot(h2_hi, jnp.concatenate([wr_hi, wr_lo], axis=1)),
                             _dot(h2_lo, wr_hi),
                             jnp.broadcast_to(br_ref[...], (tm, N_EXPERTS))], axis=1).T
    work = sum(parts[g * N_EXPERTS:(g + 1) * N_EXPERTS] for g in range(LANES // N_EXPERTS))
    eidx = lax.broadcasted_iota(I32, (N_EXPERTS, tm), 0).astype(F32)
    vals, idxs = [], []
    onehot = jnp.zeros((N_EXPERTS, tm), F32)
    for _ in range(TOP_K):
        m = jnp.max(work, axis=0, keepdims=True)
        sel = jnp.min(jnp.where(work == m, eidx, float(N_EXPERTS)), axis=0, keepdims=True)
        hit = eidx == sel
        work = jnp.where(hit, -jnp.inf, work)
        onehot = jnp.where(hit, 1.0, onehot)
        vals.append(m)
        idxs.append(sel)
    ex = [jnp.exp(vv - vals[0]) for vv in vals]
    inv_den = 1.0 / (ex[0] + ex[1] + ex[2] + ex[3])

    ri = lax.broadcasted_iota(I32, (tm, tm), 0)
    ci = lax.broadcasted_iota(I32, (tm, tm), 1)
    strict_upper = jnp.where(ri < ci, 1.0, 0.0).astype(BF16)
    before = _dot(onehot.astype(BF16), strict_upper) + carry_ref[...]
    ranks = [jnp.sum(jnp.where(eidx == idxs[kk], before, 0.0), axis=0, keepdims=True)
             for kk in range(TOP_K)]
    idx_ref[...] = jnp.concatenate(idxs, axis=0).astype(I32)
    rank_ref[...] = jnp.concatenate(ranks, axis=0).astype(I32)
    gate_ref[...] = jnp.concatenate([e * inv_den for e in ex], axis=0)
    total = carry_ref[...] + jnp.sum(onehot, axis=1, keepdims=True)
    carry_ref[...] = total
    cnt_ref[...] = total.astype(I32)


def _mid(ya, yr, x2, mod3, g_post, g_pre, w_out, w_router, b_router):
    tok = lambda w: pl.BlockSpec((TOK_TILE, w), lambda i: (i, 0))
    const = lambda s: pl.BlockSpec(s, lambda i: (0,) * len(s))
    lanes_tok = pl.BlockSpec((TOP_K, TOK_TILE), lambda i: (0, i))
    return pl.pallas_call(
        _mid_kernel,
        grid=(N_TOK // TOK_TILE,),
        in_specs=[tok(D_ATTN), tok(D_REC), tok(D_MODEL),
                  pl.BlockSpec((1, 6, D_MODEL), lambda i: (_tile_batch(i), 0, 0)),
                  const((1, D_MODEL)), const((1, D_MODEL)),
                  _resident((D_MODEL, D_MODEL)), const((D_MODEL, N_EXPERTS)), const((1, N_EXPERTS))],
        out_specs=[tok(D_MODEL),
                   pl.BlockSpec((TOK_TILE * ROW_TILES, LANES), lambda i: (i, 0)),
                   lanes_tok, lanes_tok, lanes_tok, const((N_EXPERTS, 1))],
        out_shape=[jax.ShapeDtypeStruct((N_TOK, D_MODEL), F32),
                   jax.ShapeDtypeStruct((N_TOK * ROW_TILES, LANES), U32),
                   jax.ShapeDtypeStruct((TOP_K, N_TOK), I32),
                   jax.ShapeDtypeStruct((TOP_K, N_TOK), F32),
                   jax.ShapeDtypeStruct((TOP_K, N_TOK), I32),
                   jax.ShapeDtypeStruct((N_EXPERTS, 1), I32)],
        scratch_shapes=[pltpu.VMEM((N_EXPERTS, 1), F32), pltpu.VMEM((D_MODEL, D_MODEL), BF16)],
        compiler_params=pltpu.CompilerParams(dimension_semantics=("arbitrary",),
                                             vmem_limit_bytes=VMEM_LIMIT),
        name="outproj_router",
    )(ya, yr, x2, mod3, g_post, g_pre, w_out, w_router, b_router)


def _sc_mesh():
    return plsc.VectorSubcoreMesh(core_axis_name="c", subcore_axis_name="s")


def _sc_worker_count():
    info = plsc.get_sparse_core_info()
    return info.num_cores, info.num_cores * info.num_subcores


def _sc_dispatch(h_rows, dest_win):
    n_cores, n_workers = _sc_worker_count()
    n_win = N_TOK // SC_WINDOW
    per_worker = n_win // n_workers

    @functools.partial(
        pl.kernel, mesh=_sc_mesh(),
        out_type=jax.ShapeDtypeStruct((N_SLOTS, ROW_TILES, LANES), U32),
        scratch_types=[pltpu.VMEM((TOP_K, SC_WINDOW), I32),
                       pltpu.VMEM((SC_WINDOW, ROW_TILES, LANES), U32),
                       pltpu.SemaphoreType.DMA],
        name="sc_dispatch")
    def run(h_hbm, dest_hbm, xs_hbm, idx_v, rows_v, sem):
        wid = lax.axis_index("s") * n_cores + lax.axis_index("c")

        @pl.loop(0, per_worker)
        def _(j):
            win = wid * per_worker + j
            pltpu.sync_copy(dest_hbm.at[win], idx_v)
            pltpu.sync_copy(h_hbm.at[pl.ds(win * SC_WINDOW, SC_WINDOW)], rows_v)
            copies = [pltpu.async_copy(rows_v, xs_hbm.at[idx_v.at[kk]], sem)
                      for kk in range(TOP_K)]
            for cp in copies:
                cp.wait()

    return run(h_rows, dest_win)


def _sc_combine(y_rows, dest_win, gate_win):
    n_cores, n_workers = _sc_worker_count()
    n_win, _, w = dest_win.shape
    per_worker = n_win // n_workers
    high_half = jnp.uint32(0xFFFF0000)

    @functools.partial(
        pl.kernel, mesh=_sc_mesh(),
        out_type=jax.ShapeDtypeStruct((n_win * w, 2 * ROW_TILES, LANES), F32),
        scratch_types=[pltpu.VMEM((TOP_K, w), I32),
                       pltpu.VMEM((TOP_K, w, SC_LANES), F32),
                       pltpu.VMEM((TOP_K, w, ROW_TILES, LANES), U32),
                       pltpu.VMEM((w, 2 * ROW_TILES, LANES), F32),
                       pltpu.SemaphoreType.DMA],
        compiler_params=pltpu.CompilerParams(needs_layout_passes=False),
        name="sc_combine")
    def run(y_hbm, dest_hbm, gate_hbm, out_hbm, idx_v, gate_v, rows_v, out_v, sem):
        wid = lax.axis_index("s") * n_cores + lax.axis_index("c")

        @pl.loop(0, per_worker)
        def _(it):
            win = wid * per_worker + it
            pltpu.sync_copy(dest_hbm.at[win], idx_v)
            pltpu.sync_copy(gate_hbm.at[win], gate_v)
            copies = [pltpu.async_copy(y_hbm.at[idx_v.at[kk]], rows_v.at[kk], sem)
                      for kk in range(TOP_K)]
            for cp in copies:
                cp.wait()

            @plsc.parallel_loop(0, w)
            def _(j):
                gates = [gate_v[kk, j, :] for kk in range(TOP_K)]
                for t in range(ROW_TILES):
                    for c in range(LANES // SC_LANES):
                        lanes = pl.ds(c * SC_LANES, SC_LANES)
                        words = [rows_v[kk, j, t, lanes] for kk in range(TOP_K)]
                        lo = [plsc.bitcast(wk << jnp.uint32(16), F32) * g for wk, g in zip(words, gates)]
                        hi = [plsc.bitcast(wk & high_half, F32) * g for wk, g in zip(words, gates)]
                        out_v[j, t, lanes] = (lo[0] + lo[1]) + (lo[2] + lo[3])
                        out_v[j, ROW_TILES + t, lanes] = (hi[0] + hi[1]) + (hi[2] + hi[3])

            pltpu.sync_copy(out_v, out_hbm.at[pl.ds(win * w, w)])

    return run(y_rows, dest_win, gate_win)


def _expert_kernel(be_ref, nu_ref, nx_ref, nv_ref, x_ref, wgu_hbm, bgu_ref, wd_hbm, bd_ref, y_ref,
                   wgu32, wd32, wgu16, wd16, sems):
    i = pl.program_id(0)
    e = be_ref[i]
    prev = be_ref[jnp.maximum(i - 1, 0)]

    def weight_copies(ex):
        return (pltpu.make_async_copy(wgu_hbm.at[ex], wgu32, sems.at[0]),
                pltpu.make_async_copy(wd_hbm.at[ex], wd32, sems.at[1]))

    @pl.when(i == 0)
    def _():
        for cp in weight_copies(e):
            cp.start()

    @pl.when((i == 0) | (e != prev))
    def _():
        for cp in weight_copies(e):
            cp.wait()
        rows = 128

        def cast(r, carry):
            sl = pl.ds(pl.multiple_of(r * rows, rows), rows)
            wgu16[sl, :] = wgu32[sl, :].astype(BF16)
            wd16[sl, :] = wd32[sl, :].astype(BF16)
            return carry

        lax.fori_loop(0, D_MODEL // rows, cast, 0)
        nxt = nx_ref[i]

        @pl.when(nxt >= 0)
        def _():
            for cp in weight_copies(nxt):
                cp.start()

    def run_rows(rows):
        x = _load_row_tiles(x_ref, rows).astype(BF16)
        bgu = bgu_ref[0]
        glu = _dot(x, wgu16[:, 0:D_FF]) + bgu[:, 0:D_FF]
        lin = _dot(x, wgu16[:, D_FF:]) + bgu[:, D_FF:]
        glu = jnp.minimum(glu, SWIGLU_LIMIT)
        lin = jnp.clip(lin, -SWIGLU_LIMIT, SWIGLU_LIMIT)
        act = glu * _sigmoid(SWIGLU_ALPHA * glu) * (lin + 1.0)
        y = _dot(act.astype(BF16), wd16[...]) + bd_ref[0]
        _store_row_tiles(y_ref, y.astype(BF16).astype(F32))

    valid = nv_ref[i]
    for rows in range(MOE_ROWS_STEP, MOE_BLOCK + 1, MOE_ROWS_STEP):
        pl.when((valid > rows - MOE_ROWS_STEP) & (valid <= rows))(functools.partial(run_rows, rows))


def _experts(blk_e, n_used, next_e, blk_valid, xs2, w_gu, b_gu3, w_down, b_down3):
    row_blk = pl.BlockSpec((MOE_BLOCK * ROW_TILES, LANES),
                           lambda i, be, nu, nx, nv: (jnp.minimum(i, nu[0] - 1), 0))
    grid_spec = pltpu.PrefetchScalarGridSpec(
        num_scalar_prefetch=4,
        grid=(N_BLOCKS,),
        in_specs=[row_blk,
                  pl.BlockSpec(memory_space=pl.ANY),
                  pl.BlockSpec((1, 1, 2 * D_FF), lambda i, be, nu, nx, nv: (be[i], 0, 0)),
                  pl.BlockSpec(memory_space=pl.ANY),
                  pl.BlockSpec((1, 1, D_MODEL), lambda i, be, nu, nx, nv: (be[i], 0, 0))],
        out_specs=row_blk,
        scratch_shapes=[pltpu.VMEM((D_MODEL, 2 * D_FF), F32),
                        pltpu.VMEM((D_FF, D_MODEL), F32),
                        pltpu.VMEM((D_MODEL, 2 * D_FF), BF16),
                        pltpu.VMEM((D_FF, D_MODEL), BF16),
                        pltpu.SemaphoreType.DMA((2,))],
    )
    return pl.pallas_call(
        _expert_kernel,
        grid_spec=grid_spec,
        out_shape=jax.ShapeDtypeStruct((N_SLOTS * ROW_TILES, LANES), U32),
        compiler_params=pltpu.CompilerParams(dimension_semantics=("arbitrary",),
                                             vmem_limit_bytes=VMEM_LIMIT),
        name="experts",
    )(blk_e, n_used, next_e, blk_valid, xs2, w_gu, b_gu3, w_down, b_down3)


def _final_kernel(ysum_ref, x1_ref, mod_ref, gpost_ref, *maybe_alias_and_out):
    o_ref = maybe_alias_and_out[-1]
    sum_tiles = 2 * ROW_TILES
    y = jnp.concatenate([ysum_ref[pl.ds(j, TOK_TILE, stride=sum_tiles), :] for j in range(sum_tiles)],
                        axis=1)
    gate_f = mod_ref[0][5:6]
    o_ref[...] = x1_ref[...] + _rms(y) * (gate_f * gpost_ref[...])


def _final(part, ysum, x1, mod3, g_post, out_so_far):
    tiles = SEQ // TOK_TILE
    tok = lambda w: pl.BlockSpec((TOK_TILE, w), lambda i: (part * tiles + i, 0))
    in_specs = [pl.BlockSpec((TOK_TILE * 2 * ROW_TILES, LANES), lambda i: (i, 0)),
                tok(D_MODEL),
                pl.BlockSpec((1, 6, D_MODEL), lambda i: (part, 0, 0)),
                pl.BlockSpec((1, D_MODEL), lambda i: (0, 0))]
    args = [ysum, x1, mod3, g_post]
    aliases = {}
    if out_so_far is not None:
        in_specs.append(pl.BlockSpec(memory_space=pl.ANY))
        args.append(out_so_far)
        aliases = {len(args) - 1: 0}
    return pl.pallas_call(
        _final_kernel,
        grid=(tiles,),
        in_specs=in_specs,
        out_specs=tok(D_MODEL),
        out_shape=jax.ShapeDtypeStruct((N_TOK, D_MODEL), F32),
        input_output_aliases=aliases,
        compiler_params=pltpu.CompilerParams(vmem_limit_bytes=VMEM_LIMIT),
        name="combine_final",
    )(*args)


def _slot_kernel(idx_ref, rank_ref, start_ref, dest_ref):
    tn = idx_ref.shape[1]
    experts = lax.broadcasted_iota(I32, (N_EXPERTS, tn), 0)
    starts = start_ref[...].astype(F32)
    rows = [jnp.sum(jnp.where(experts == idx_ref[kk:kk + 1, :], starts, 0.0), axis=0, keepdims=True)
            for kk in range(TOP_K)]
    dest_ref[...] = jnp.concatenate(rows, axis=0).astype(I32) + rank_ref[...]


def _slots(idx, rank, pstart):
    tn = N_TOK // 2
    lanes_tok = pl.BlockSpec((TOP_K, tn), lambda i: (0, i))
    return pl.pallas_call(
        _slot_kernel,
        grid=(N_TOK // tn,),
        in_specs=[lanes_tok, lanes_tok, pl.BlockSpec((N_EXPERTS, 1), lambda i: (0, 0))],
        out_specs=lanes_tok,
        out_shape=jax.ShapeDtypeStruct((TOP_K, N_TOK), I32),
        name="moe_slots",
    )(idx, rank, pstart.reshape(N_EXPERTS, 1))


def _routing_tables(idx, rank, counts):
    counts = counts.reshape(N_EXPERTS)
    experts = jnp.arange(N_EXPERTS, dtype=I32)
    padded = ((counts + MOE_BLOCK - 1) // MOE_BLOCK) * MOE_BLOCK
    pend = jnp.cumsum(padded)
    pstart = pend - padded
    dest = _slots(idx, rank, pstart)
    dest_win = dest.reshape(TOP_K, N_TOK // SC_WINDOW, SC_WINDOW).transpose(1, 0, 2)
    n_used = (pend[-1] // MOE_BLOCK).astype(I32).reshape(1)
    blk_start = jnp.arange(N_BLOCKS, dtype=I32) * MOE_BLOCK
    blk_e = jnp.sum(blk_start[:, None] >= pend[None, :], axis=1).astype(I32)
    last_e = jnp.max(jnp.where(counts > 0, experts, 0))
    blk_e = jnp.minimum(blk_e, last_e)
    later = (experts[None, :] > experts[:, None]) & (counts[None, :] > 0)
    next_nonempty = jnp.min(jnp.where(later, experts[None, :], N_EXPERTS), axis=1)
    next_nonempty = jnp.where(next_nonempty == N_EXPERTS, -1, next_nonempty).astype(I32)
    of_block = lambda table: jnp.sum(jnp.where(blk_e[:, None] == experts, table, 0), axis=1).astype(I32)
    next_e = of_block(next_nonempty)
    blk_valid = jnp.clip(of_block(counts) - (blk_start - of_block(pstart)), 0, MOE_BLOCK)
    blk_valid = jnp.where(blk_start < pend[-1], blk_valid, 0).astype(I32)
    return dest, dest_win, blk_e, n_used, next_e, blk_valid


def kernel(x, c, w_ada, b_ada, g_pre_mix, g_post_mix, w_in, attn_norm_w, hgrn_lb, hgrn_norm_w,
           w_out, g_pre_ffn, g_post_ffn, w_router, b_router, w_gu, b_gu, w_down, b_down):
    c_pad = jnp.pad(c, ((0, SUBLANES - BATCH), (0, 0)))
    mod = _ada_mod(c_pad, w_ada[0], b_ada)
    mod3 = mod[:BATCH].reshape(BATCH, 6, D_MODEL)

    x2 = x.reshape(N_TOK, D_MODEL)
    q4, k4, v4, q16, k16, v16, qr, fr, ir, gr = _inproj(x2, mod3, g_pre_mix, w_in[0])
    nat = lambda t: t.reshape(BATCH, SEQ, D_REC)
    ya = _attention(q4, k4, v4, q16, k16, v16, attn_norm_w)
    yr = _hgrn(nat(qr), nat(fr), nat(ir), nat(gr), hgrn_lb, hgrn_norm_w)

    x1, h2, idx, gates, rank, counts = _mid(
        ya.reshape(N_TOK, D_ATTN), yr.reshape(N_TOK, D_REC), x2, mod3, g_post_mix, g_pre_ffn,
        w_out[0], w_router[0], b_router)

    dest, dest_win, blk_e, n_used, next_e, blk_valid = _routing_tables(idx, rank, counts)
    xs = _sc_dispatch(h2.reshape(N_TOK, ROW_TILES, LANES), dest_win)
    ys = _experts(blk_e, n_used, next_e, blk_valid, xs.reshape(N_SLOTS * ROW_TILES, LANES),
                  w_gu[0], b_gu[0].reshape(N_EXPERTS, 1, 2 * D_FF),
                  w_down[0], b_down[0].reshape(N_EXPERTS, 1, D_MODEL))
    ys3 = ys.reshape(N_SLOTS, ROW_TILES, LANES)
    by_window = lambda t: t.reshape(TOP_K, N_TOK // SC_SUM_WINDOW, SC_SUM_WINDOW).transpose(1, 0, 2)
    sum_win = by_window(dest)
    gate_win = jnp.broadcast_to(by_window(gates)[..., None], sum_win.shape + (SC_LANES,))
    win_per_seq = SEQ // SC_SUM_WINDOW
    out = None
    for b in range(BATCH):
        wins = slice(b * win_per_seq, (b + 1) * win_per_seq)
        ysum = _sc_combine(ys3, sum_win[wins], gate_win[wins])
        out = _final(b, ysum.reshape(SEQ * 2 * ROW_TILES, LANES), x1, mod3, g_post_ffn, out)
    return out.reshape(BATCH, SEQ, D_MODEL)
```

```python
import functools
import math

import jax
import jax.numpy as jnp
from jax import lax
from jax.experimental import pallas as pl
from jax.experimental.pallas import tpu as pltpu
from jax.experimental.pallas import tpu_sc as plsc

F32 = jnp.float32
BF16 = jnp.bfloat16
I32 = jnp.int32
U32 = jnp.uint32

D_MODEL = 1024
BATCH = 4
SEQ = 4096
N_TOK = BATCH * SEQ
D_ATTN = 512
HEAD_DIM_A = 64
ATT_BLOCK = 128
DIL_MID = 4
DIL_MAX = 16
SUB_MID = SEQ // DIL_MID
SUB_MAX = SEQ // DIL_MAX
D_REC = 512
HGRN_CHUNK = 32
HGRN_SUPER = 256
N_EXPERTS = 32
TOP_K = 4
D_FF = 1024
SWIGLU_LIMIT = 7.0
SWIGLU_ALPHA = 1.702
EPS = 1e-6
NEG_BIG = -1e30
Q_SCALE = HEAD_DIM_A ** -0.5 * math.log2(math.e)

LANES = 128
SUBLANES = 8
ROW_TILES = D_MODEL // 2 // LANES

TOK_TILE = 512
MOE_BLOCK = 1024
MOE_ROWS_STEP = 128
N_SLOTS = N_TOK * TOP_K + N_EXPERTS * MOE_BLOCK
N_BLOCKS = N_SLOTS // MOE_BLOCK
SC_WINDOW = 128
SC_SUM_WINDOW = 16
SC_LANES = 16
VMEM_LIMIT = 56 * 1024 * 1024


def _sigmoid(x):
    return 1.0 / (1.0 + jnp.exp(-x))


def _dot(a, b):
    return jnp.dot(a, b, preferred_element_type=F32)


def _dot_nt(a, b):
    return lax.dot_general(a, b, (((1,), (1,)), ((), ())), preferred_element_type=F32)


def _split_bf16(x):
    hi = x.astype(BF16)
    return hi, (x - hi.astype(F32)).astype(BF16)


def _rms(x):
    return x * lax.rsqrt(jnp.mean(x * x, axis=-1, keepdims=True) + EPS)


def _ada_kernel(c_ref, w_ref, b_ref, o_ref):
    c = c_ref[...]
    cond = c * _sigmoid(c)
    c_hi, c_lo = _split_bf16(cond)
    w_hi, w_lo = _split_bf16(w_ref[...])
    o_ref[...] = _dot(c_hi, w_hi) + _dot(c_lo, w_hi) + _dot(c_hi, w_lo) + b_ref[...]


def _ada_mod(c_pad, w_ada, b_ada):
    n = w_ada.shape[1]
    tn = 1536
    return pl.pallas_call(
        _ada_kernel,
        grid=(n // tn,),
        in_specs=[pl.BlockSpec((SUBLANES, D_MODEL), lambda j: (0, 0)),
                  pl.BlockSpec((D_MODEL, tn), lambda j: (0, j)),
                  pl.BlockSpec((1, tn), lambda j: (0, j))],
        out_specs=pl.BlockSpec((SUBLANES, tn), lambda j: (0, j)),
        out_shape=jax.ShapeDtypeStruct((SUBLANES, n), F32),
        compiler_params=pltpu.CompilerParams(vmem_limit_bytes=VMEM_LIMIT),
        name="ada_mod",
    )(c_pad, w_ada, b_ada)


def _cast_weights_once(w_ref, w16):
    rows = 128

    @pl.when(pl.program_id(0) == 0)
    def _():
        def cast(r, carry):
            sl = pl.ds(pl.multiple_of(r * rows, rows), rows)
            w16[sl, :] = w_ref[sl, :].astype(BF16)
            return carry

        lax.fori_loop(0, w_ref.shape[0] // rows, cast, 0)


def _resident(shape):
    return pl.BlockSpec(shape, lambda i: (0,) * len(shape), pipeline_mode=pl.Buffered(1))


def _inproj_kernel(x_ref, mod_ref, g_ref, w_ref, q4, k4, v4, q16, k16, v16, qr, fr, ir, gr,
                   w16, stage_nat, stage_mid):
    _cast_weights_once(w_ref, w16)
    mod = mod_ref[0]
    shift, scale = mod[0:1], mod[1:2]
    h = _rms(x_ref[...]) * (g_ref[...] * (1.0 + scale)) + shift
    hb = h.astype(BF16)
    slabs = D_ATTN // LANES
    rows_mid = TOK_TILE // DIL_MID
    rows_max = TOK_TILE // DIL_MAX

    def proj(j):
        return _dot(hb, w16[:, j * D_ATTN:(j + 1) * D_ATTN])

    for j, (o_mid, o_max) in enumerate(((q4, q16), (k4, k16), (v4, v16))):
        r = proj(j)
        if j == 0:
            r = r * Q_SCALE
        for cs in range(slabs):
            stage_nat[cs] = r[:, cs * LANES:(cs + 1) * LANES]
        for cs in range(slabs):
            lanes = slice(cs * LANES, (cs + 1) * LANES)
            for sub in range(DIL_MID):
                piece = stage_nat[cs, pl.ds(sub, rows_mid, stride=DIL_MID), :]
                o_mid[sub, :, lanes] = piece.astype(BF16)
                stage_mid[cs, sub] = piece
            for sub in range(DIL_MAX):
                piece = stage_mid[cs, sub % DIL_MID, pl.ds(sub // DIL_MID, rows_max, stride=DIL_MID), :]
                o_max[sub, :, lanes] = piece.astype(BF16)
    for j, o_ref in enumerate((qr, fr, ir, gr)):
        o_ref[...] = proj(3 + j)


def _inproj(x2, mod3, g_pre, w_in):
    tiles_per_seq = SEQ // TOK_TILE
    rows_mid = TOK_TILE // DIL_MID
    rows_max = TOK_TILE // DIL_MAX
    mid = pl.BlockSpec((None, DIL_MID, rows_mid, D_ATTN),
                       lambda i: (i // tiles_per_seq, 0, i % tiles_per_seq, 0))
    mx = pl.BlockSpec((None, DIL_MAX, rows_max, D_ATTN),
                      lambda i: (i // tiles_per_seq, 0, i % tiles_per_seq, 0))
    nat = pl.BlockSpec((TOK_TILE, D_REC), lambda i: (i, 0))
    mid_shape = jax.ShapeDtypeStruct((BATCH, DIL_MID, SUB_MID, D_ATTN), BF16)
    mx_shape = jax.ShapeDtypeStruct((BATCH, DIL_MAX, SUB_MAX, D_ATTN), BF16)
    nat_shape = jax.ShapeDtypeStruct((N_TOK, D_REC), F32)
    return pl.pallas_call(
        _inproj_kernel,
        grid=(N_TOK // TOK_TILE,),
        in_specs=[pl.BlockSpec((TOK_TILE, D_MODEL), lambda i: (i, 0)),
                  pl.BlockSpec((1, 6, D_MODEL), lambda i: (i // tiles_per_seq, 0, 0)),
                  pl.BlockSpec((1, D_MODEL), lambda i: (0, 0)),
                  _resident(w_in.shape)],
        out_specs=[mid, mid, mid, mx, mx, mx, nat, nat, nat, nat],
        out_shape=[mid_shape] * 3 + [mx_shape] * 3 + [nat_shape] * 4,
        scratch_shapes=[pltpu.VMEM(w_in.shape, BF16),
                        pltpu.VMEM((D_ATTN // LANES, TOK_TILE, LANES), F32),
                        pltpu.VMEM((D_ATTN // LANES, DIL_MID, rows_mid, LANES), F32)],
        compiler_params=pltpu.CompilerParams(dimension_semantics=("arbitrary",),
                                             vmem_limit_bytes=VMEM_LIMIT),
        name="inproj",
    )(x2, mod3, g_pre, w_in)


def _attn_kernel(q4_ref, k4_ref, v4_ref, q16_ref, k16_ref, v16_ref, nw_ref, o_ref,
                 sel_scr, keep_scr, o_scr, m_scr, l_scr):
    bw = ATT_BLOCK
    lane = lax.broadcasted_iota(I32, (bw, LANES), 1)
    head0 = lane < HEAD_DIM_A
    head_masks = (jnp.where(head0, 1.0, 0.0).astype(BF16), jnp.where(head0, 0.0, 1.0).astype(BF16))

    rr = lax.broadcasted_iota(I32, (bw, bw), 0)
    cc = lax.broadcasted_iota(I32, (bw, bw), 1)
    piece = bw // DIL_MID
    rp = DIL_MID * (rr % piece) + rr // piece
    cp = DIL_MID * (cc % piece) + cc // piece
    for var, (r, c) in enumerate(((rr, cc), (rr, cc), (rp, cp), (rp, cp))):
        first = var % 2 == 1
        left = (c < r) if first else (c >= r)
        right = jnp.zeros_like(left) if first else jnp.logical_not(left)
        sel_scr[var] = jnp.where(left, 1.0, 0.0)
        keep_scr[var, 0] = jnp.where(left, 1.0, 0.0).astype(BF16)
        keep_scr[var, 1] = jnp.where(right, 1.0, 0.0).astype(BF16)

    def score_matmuls(loaded):
        return [_dot_nt(q * hm, k) for (q, k, _, _, _) in loaded for hm in head_masks]

    def finish_group(scores, loaded):
        probs = []
        for n, s in enumerate(scores):
            _, _, _, var, right_bias = loaded[n // 2]
            folded = jnp.where(sel_scr[var] > 0.5, s[:, 0:bw], s[:, bw:] + right_bias)
            m = jnp.max(folded, axis=-1, keepdims=True)
            p = jnp.exp2(folded - m)
            pb = p.astype(BF16)
            spread = jnp.concatenate([pb * keep_scr[var, 0], pb * keep_scr[var, 1]], axis=1)
            probs.append((spread, jnp.sum(p, axis=-1, keepdims=True), m))
        pvs = [_dot(p, loaded[n // 2][2]) for n, (p, _, _) in enumerate(probs)]
        results = []
        for u in range(len(loaded)):
            (_, l0, m0), (_, l1, m1) = probs[2 * u], probs[2 * u + 1]
            results.append((jnp.where(head0, pvs[2 * u], pvs[2 * u + 1]),
                            jnp.where(head0, jnp.broadcast_to(m0, (bw, LANES)),
                                      jnp.broadcast_to(m1, (bw, LANES))),
                            jnp.where(head0, jnp.broadcast_to(l0, (bw, LANES)),
                                      jnp.broadcast_to(l1, (bw, LANES)))))
        return results

    group = 8

    def first_block_bias(blk):
        return jnp.where(blk == 0, NEG_BIG, 0.0)

    def load_d1(g):
        loaded, starts = [], []
        for i in range(group):
            blk = g * group + i
            qs = pl.multiple_of(blk * piece, piece)
            ks = pl.multiple_of(jnp.maximum(blk - 1, 0) * piece, piece)
            q = jnp.concatenate([q4_ref[r, pl.ds(qs, piece), :] for r in range(DIL_MID)], axis=0)
            k = jnp.concatenate([k4_ref[r, pl.ds(ks + half * piece, piece), :]
                                 for half in range(2) for r in range(DIL_MID)], axis=0)
            v = jnp.concatenate([v4_ref[r, pl.ds(ks + half * piece, piece), :]
                                 for half in range(2) for r in range(DIL_MID)], axis=0)
            loaded.append((q, k, v, jnp.where(blk == 0, 3, 2), first_block_bias(blk)))
            starts.append(qs)

        def store(results):
            for qs, parts in zip(starts, results):
                for r in range(DIL_MID):
                    for scr, val in zip((o_scr, m_scr, l_scr), parts):
                        scr[0, r, pl.ds(qs, piece), :] = val[r * piece:(r + 1) * piece]

        return loaded, store

    def load_d4(g):
        loaded, dsts = [], []
        for i in range(group):
            r, blk = i % DIL_MID, g * (group // DIL_MID) + i // DIL_MID
            qs = pl.multiple_of(blk * bw, bw)
            ks = pl.multiple_of(jnp.maximum(blk - 1, 0) * bw, bw)
            loaded.append((q4_ref[r, pl.ds(qs, bw), :], k4_ref[r, pl.ds(ks, 2 * bw), :],
                           v4_ref[r, pl.ds(ks, 2 * bw), :], jnp.where(blk == 0, 1, 0),
                           first_block_bias(blk)))
            dsts.append((r, qs))

        def store(results):
            for (r, qs), parts in zip(dsts, results):
                for scr, val in zip((o_scr, m_scr, l_scr), parts):
                    scr[1, r, pl.ds(qs, bw), :] = val

        return loaded, store

    def load_d16(g):
        loaded, dsts = [], []
        for i in range(group):
            r, blk = g * (group // 2) + i // 2, i % 2
            loaded.append((q16_ref[r, blk * bw:(blk + 1) * bw, :], k16_ref[r], v16_ref[r],
                           1 - blk, NEG_BIG if blk == 0 else 0.0))
            dsts.append((r % DIL_MID, pl.ds(blk * (bw * DIL_MID) + r // DIL_MID, bw, stride=DIL_MID)))

        def store(results):
            for (sub, dst), parts in zip(dsts, results):
                for scr, val in zip((o_scr, m_scr, l_scr), parts):
                    scr[2, sub, dst, :] = val

        return loaded, store

    def run_group(load):
        def step(g, carry):
            loaded, store = load(g)
            store(finish_group(score_matmuls(loaded), loaded))
            return carry
        return step

    n_groups = SEQ // bw // group
    for load in (load_d1, load_d4, load_d16):
        lax.fori_loop(0, n_groups, run_group(load), 0)

    rows = 256
    hi = lax.broadcasted_iota(I32, (LANES, LANES), 0) // HEAD_DIM_A
    hj = lax.broadcasted_iota(I32, (LANES, LANES), 1) // HEAD_DIM_A
    head_sum = jnp.where(hi == hj, 1.0, 0.0).astype(BF16)
    nw = nw_ref[...]

    def head_sums(x):
        x_hi, x_lo = _split_bf16(x)
        return _dot(x_hi, head_sum) + _dot(x_lo, head_sum)

    def merge(t, carry):
        r = t // (SUB_MID // rows)
        start = pl.multiple_of((t % (SUB_MID // rows)) * rows, rows)
        sl = pl.ds(start, rows)
        own_score = head_sums(q4_ref[r, sl, :].astype(F32) * k4_ref[r, sl, :].astype(F32))
        ms = [m_scr[n, r, sl, :] for n in range(3)]
        mx = jnp.maximum(jnp.maximum(jnp.maximum(ms[0], ms[1]), ms[2]), own_score)
        ws = [jnp.exp2(m - mx) for m in ms]
        w_own = float(len(ms)) * jnp.exp2(own_score - mx)
        num = (ws[0] * o_scr[0, r, sl, :] + ws[1] * o_scr[1, r, sl, :] + ws[2] * o_scr[2, r, sl, :]
               + w_own * v4_ref[r, sl, :].astype(F32))
        den = ws[0] * l_scr[0, r, sl, :] + ws[1] * l_scr[1, r, sl, :] + ws[2] * l_scr[2, r, sl, :] + w_own
        o = num / den
        mean_sq = head_sums(o * o) * (1.0 / HEAD_DIM_A)
        o_ref[pl.ds(start * DIL_MID + r, rows, stride=DIL_MID), :] = o * lax.rsqrt(mean_sq + EPS) * nw
        return carry

    lax.fori_loop(0, SEQ // rows, merge, 0, unroll=8)


def _attention(q4, k4, v4, q16, k16, v16, attn_norm_w):
    hp = D_ATTN // LANES
    mid = pl.BlockSpec((None, DIL_MID, SUB_MID, LANES), lambda b, h: (b, 0, 0, h))
    mx = pl.BlockSpec((None, DIL_MAX, SUB_MAX, LANES), lambda b, h: (b, 0, 0, h))
    scr = pltpu.VMEM((3, DIL_MID, SUB_MID, LANES), F32)
    return pl.pallas_call(
        _attn_kernel,
        grid=(BATCH, hp),
        in_specs=[mid, mid, mid, mx, mx, mx, pl.BlockSpec((1, LANES), lambda b, h: (0, h))],
        out_specs=pl.BlockSpec((None, SEQ, LANES), lambda b, h: (b, 0, h)),
        out_shape=jax.ShapeDtypeStruct((BATCH, SEQ, D_ATTN), F32),
        scratch_shapes=[pltpu.VMEM((4, ATT_BLOCK, ATT_BLOCK), F32),
                        pltpu.VMEM((4, 2, ATT_BLOCK, ATT_BLOCK), BF16), scr, scr, scr],
        compiler_params=pltpu.CompilerParams(vmem_limit_bytes=VMEM_LIMIT),
        name="dilated_attn",
    )(q4, k4, v4, q16, k16, v16, attn_norm_w)


def _hgrn_kernel(qr_ref, fr_ref, ir_ref, gr_ref, lb_ref, nw_ref, o_ref,
                 qe_scr, oi_scr, delta_scr, dec_scr, st_scr):
    sup, c = HGRN_SUPER, HGRN_CHUNK
    nch = sup // c
    n_sup = SEQ // sup
    lbp = lb_ref[...]
    lmx = jnp.max(lbp, axis=0, keepdims=True)
    ex = jnp.exp(lbp - lmx)
    lb = ex[0:1] / (ex[0:1] + ex[1:2])
    nw = nw_ref[...]

    ri = lax.broadcasted_iota(I32, (sup, sup), 0)
    ci = lax.broadcasted_iota(I32, (sup, sup), 1)
    same_chunk = (ri // c) == (ci // c)
    causal = same_chunk & (ci <= ri)
    cum_op = jnp.where(causal, 1.0, 0.0).astype(BF16)

    def chunk_rows(kd, ch):
        parts = []
        if ch > 0:
            parts.append(jnp.zeros((ch * c, LANES), BF16))
        parts.append(kd[ch * c:(ch + 1) * c])
        if ch < nch - 1:
            parts.append(jnp.zeros(((nch - 1 - ch) * c, LANES), BF16))
        return jnp.concatenate(parts, axis=0)

    group = 4

    def independent(g, carry):
        ts = [g * group + i for i in range(group)]
        sls = [pl.ds(pl.multiple_of(t * sup, sup), sup) for t in ts]
        pre = []
        for sl in sls:
            f = lb + (1.0 - lb) * _sigmoid(fr_ref[sl, :])
            logf_hi, logf_lo = _split_bf16(jnp.log(f))
            pre.append((1.0 - f, _dot(cum_op, logf_hi) + _dot(cum_op, logf_lo)))
        mid = []
        for t, sl, (kk, b) in zip(ts, sls, pre):
            b_last = jnp.concatenate(
                [jnp.broadcast_to(b[(ch + 1) * c - 1:(ch + 1) * c], (c, LANES)) for ch in range(nch)],
                axis=0)
            q = qr_ref[sl, :]
            qeb = (q * _sigmoid(q) * jnp.exp(b)).astype(BF16)
            ke = (kk * jnp.exp(-b)).astype(BF16)
            kd = (kk * jnp.exp(b_last - b)).astype(BF16)
            qe_scr[sl, :] = qeb
            dec_rows = jnp.concatenate([b_last[ch * c:ch * c + 1] for ch in range(nch)], axis=0)
            dec_scr[pl.ds(pl.multiple_of(t * nch, nch), nch), :] = jnp.exp(dec_rows)
            v = ir_ref[sl, :]
            vt = v.T.astype(BF16)
            scores = _dot_nt(qeb, ke)
            for pair in range(nch // 2):
                rhs = jnp.concatenate([chunk_rows(kd, 2 * pair), chunk_rows(kd, 2 * pair + 1)], axis=1)
                d2 = _dot(vt, rhs)
                delta_scr[t * nch + 2 * pair] = d2[:, 0:LANES]
                delta_scr[t * nch + 2 * pair + 1] = d2[:, LANES:]
            mid.append((scores, v.astype(BF16)))
        for sl, (scores, vb) in zip(sls, mid):
            a = jnp.where(causal, scores, 0.0)
            oi_scr[sl, :] = _dot(a.astype(BF16), vb)
        return carry

    lax.fori_loop(0, n_sup // group, independent, 0)

    def recur(ch, st):
        st_scr[ch] = st.astype(BF16)
        return st * dec_scr[pl.ds(ch, 1), :] + delta_scr[ch]

    lax.fori_loop(0, SEQ // c, recur, jnp.zeros((LANES, LANES), F32), unroll=8)

    def finish(t, carry):
        sl = pl.ds(pl.multiple_of(t * sup, sup), sup)
        parts = [_dot_nt(qe_scr[pl.ds(pl.multiple_of(t * sup + ch * c, c), c), :], st_scr[t * nch + ch])
                 for ch in range(nch)]
        o = oi_scr[sl, :] + jnp.concatenate(parts, axis=0)
        g = gr_ref[sl, :]
        o_ref[sl, :] = (_rms(o) * nw * (g * _sigmoid(g))).astype(o_ref.dtype)
        return carry

    lax.fori_loop(0, n_sup, finish, 0, unroll=8)


def _hgrn(qr, fr, ir, gr, hgrn_lb, hgrn_norm_w):
    nh = D_REC // LANES
    n_chunks = SEQ // HGRN_CHUNK
    blk = pl.BlockSpec((None, SEQ, LANES), lambda b, h: (b, 0, h))
    return pl.pallas_call(
        _hgrn_kernel,
        grid=(BATCH, nh),
        in_specs=[blk, blk, blk, blk,
                  pl.BlockSpec((2, LANES), lambda b, h: (0, h)),
                  pl.BlockSpec((1, LANES), lambda b, h: (0, h))],
        out_specs=blk,
        out_shape=jax.ShapeDtypeStruct((BATCH, SEQ, D_REC), BF16),
        scratch_shapes=[pltpu.VMEM((SEQ, LANES), BF16),
                        pltpu.VMEM((SEQ, LANES), F32),
                        pltpu.VMEM((n_chunks, LANES, LANES), F32),
                        pltpu.VMEM((n_chunks, LANES), F32),
                        pltpu.VMEM((n_chunks, LANES, LANES), BF16)],
        compiler_params=pltpu.CompilerParams(vmem_limit_bytes=VMEM_LIMIT),
        name="hgrn2",
    )(qr, fr, ir, gr, hgrn_lb, hgrn_norm_w)


def _tile_batch(i):
    return i // (SEQ // TOK_TILE)


def _store_row_tiles(ref, val):
    rows, half = val.shape[0], D_MODEL // 2
    lo = lax.bitcast_convert_type(val[:, :half], U32)
    hi = lax.bitcast_convert_type(val[:, half:], U32)
    words = lax.shift_right_logical(lo, jnp.uint32(16)) | (hi & jnp.uint32(0xFFFF0000))
    for j in range(ROW_TILES):
        ref[pl.ds(j, rows, stride=ROW_TILES), :] = words[:, j * LANES:(j + 1) * LANES]


def _load_row_tiles(ref, rows):
    words = jnp.concatenate(
        [ref[pl.ds(j, rows, stride=ROW_TILES), :] for j in range(ROW_TILES)], axis=1)
    lo = lax.bitcast_convert_type(lax.shift_left(words, jnp.uint32(16)), F32)
    hi = lax.bitcast_convert_type(words & jnp.uint32(0xFFFF0000), F32)
    return jnp.concatenate([lo, hi], axis=1)


def _mid_kernel(ya_ref, yr_ref, x_ref, mod_ref, gpost_ref, gpre_ref, wo_ref, wr_ref, br_ref,
                x1_ref, h2_ref, idx_ref, gate_ref, rank_ref, cnt_ref, carry_ref, wo16):
    i = pl.program_id(0)
    _cast_weights_once(wo_ref, wo16)

    @pl.when(i == 0)
    def _():
        carry_ref[...] = jnp.zeros_like(carry_ref)

    mod = mod_ref[0]
    gate_m, shift_f, scale_f = mod[2:3], mod[3:4], mod[4:5]
    y = _dot(ya_ref[...].astype(BF16), wo16[0:D_ATTN, :]) + _dot(yr_ref[...], wo16[D_ATTN:, :])
    x1 = x_ref[...] + _rms(y) * (gate_m * gpost_ref[...])
    x1_ref[...] = x1
    h2 = _rms(x1) * (gpre_ref[...] * (1.0 + scale_f)) + shift_f
    h2_hi = h2.astype(BF16)
    h2_rounded = h2_hi.astype(F32)
    _store_row_tiles(h2_ref, h2_rounded)

    tm = h2.shape[0]
    h2_lo = (h2 - h2_rounded).astype(BF16)
    wr_hi, wr_lo = _split_bf16(wr_ref[...])
    parts = jnp.concatenate([_dot(h2_hi, jnp.concatenate([wr_hi, wr_lo], axis=1)),
                             _dot(h2_lo, wr_hi),
                             jnp.broadcast_to(br_ref[...], (tm, N_EXPERTS))], axis=1).T
    work = sum(parts[g * N_EXPERTS:(g + 1) * N_EXPERTS] for g in range(LANES // N_EXPERTS))
    eidx = lax.broadcasted_iota(I32, (N_EXPERTS, tm), 0).astype(F32)
    vals, idxs = [], []
    onehot = jnp.zeros((N_EXPERTS, tm), F32)
    for _ in range(TOP_K):
        m = jnp.max(work, axis=0, keepdims=True)
        sel = jnp.min(jnp.where(work == m, eidx, float(N_EXPERTS)), axis=0, keepdims=True)
        hit = eidx == sel
        work = jnp.where(hit, -jnp.inf, work)
        onehot = jnp.where(hit, 1.0, onehot)
        vals.append(m)
        idxs.append(sel)
    ex = [jnp.exp(vv - vals[0]) for vv in vals]
    inv_den = 1.0 / (ex[0] + ex[1] + ex[2] + ex[3])

    ri = lax.broadcasted_iota(I32, (tm, tm), 0)
    ci = lax.broadcasted_iota(I32, (tm, tm), 1)
    strict_upper = jnp.where(ri < ci, 1.0, 0.0).astype(BF16)
    before = _dot(onehot.astype(BF16), strict_upper) + carry_ref[...]
    ranks = [jnp.sum(jnp.where(eidx == idxs[kk], before, 0.0), axis=0, keepdims=True)
             for kk in range(TOP_K)]
    idx_ref[...] = jnp.concatenate(idxs, axis=0).astype(I32)
    rank_ref[...] = jnp.concatenate(ranks, axis=0).astype(I32)
    gate_ref[...] = jnp.concatenate([e * inv_den for e in ex], axis=0)
    total = carry_ref[...] + jnp.sum(onehot, axis=1, keepdims=True)
    carry_ref[...] = total
    cnt_ref[...] = total.astype(I32)


def _mid(ya, yr, x2, mod3, g_post, g_pre, w_out, w_router, b_router):
    tok = lambda w: pl.BlockSpec((TOK_TILE, w), lambda i: (i, 0))
    const = lambda s: pl.BlockSpec(s, lambda i: (0,) * len(s))
    lanes_tok = pl.BlockSpec((TOP_K, TOK_TILE), lambda i: (0, i))
    return pl.pallas_call(
        _mid_kernel,
        grid=(N_TOK // TOK_TILE,),
        in_specs=[tok(D_ATTN), tok(D_REC), tok(D_MODEL),
                  pl.BlockSpec((1, 6, D_MODEL), lambda i: (_tile_batch(i), 0, 0)),
                  const((1, D_MODEL)), const((1, D_MODEL)),
                  _resident((D_MODEL, D_MODEL)), const((D_MODEL, N_EXPERTS)), const((1, N_EXPERTS))],
        out_specs=[tok(D_MODEL),
                   pl.BlockSpec((TOK_TILE * ROW_TILES, LANES), lambda i: (i, 0)),
                   lanes_tok, lanes_tok, lanes_tok, const((N_EXPERTS, 1))],
        out_shape=[jax.ShapeDtypeStruct((N_TOK, D_MODEL), F32),
                   jax.ShapeDtypeStruct((N_TOK * ROW_TILES, LANES), U32),
                   jax.ShapeDtypeStruct((TOP_K, N_TOK), I32),
                   jax.ShapeDtypeStruct((TOP_K, N_TOK), F32),
                   jax.ShapeDtypeStruct((TOP_K, N_TOK), I32),
                   jax.ShapeDtypeStruct((N_EXPERTS, 1), I32)],
        scratch_shapes=[pltpu.VMEM((N_EXPERTS, 1), F32), pltpu.VMEM((D_MODEL, D_MODEL), BF16)],
        compiler_params=pltpu.CompilerParams(dimension_semantics=("arbitrary",),
                                             vmem_limit_bytes=VMEM_LIMIT),
        name="outproj_router",
    )(ya, yr, x2, mod3, g_post, g_pre, w_out, w_router, b_router)


def _sc_mesh():
    return plsc.VectorSubcoreMesh(core_axis_name="c", subcore_axis_name="s")


def _sc_worker_count():
    info = plsc.get_sparse_core_info()
    return info.num_cores, info.num_cores * info.num_subcores


def _sc_dispatch(h_rows, dest_win):
    n_cores, n_workers = _sc_worker_count()
    n_win = N_TOK // SC_WINDOW
    per_worker = n_win // n_workers

    @functools.partial(
        pl.kernel, mesh=_sc_mesh(),
        out_type=jax.ShapeDtypeStruct((N_SLOTS, ROW_TILES, LANES), U32),
        scratch_types=[pltpu.VMEM((TOP_K, SC_WINDOW), I32),
                       pltpu.VMEM((SC_WINDOW, ROW_TILES, LANES), U32),
                       pltpu.SemaphoreType.DMA],
        name="sc_dispatch")
    def run(h_hbm, dest_hbm, xs_hbm, idx_v, rows_v, sem):
        wid = lax.axis_index("s") * n_cores + lax.axis_index("c")

        @pl.loop(0, per_worker)
        def _(j):
            win = wid * per_worker + j
            pltpu.sync_copy(dest_hbm.at[win], idx_v)
            pltpu.sync_copy(h_hbm.at[pl.ds(win * SC_WINDOW, SC_WINDOW)], rows_v)
            copies = [pltpu.async_copy(rows_v, xs_hbm.at[idx_v.at[kk]], sem)
                      for kk in range(TOP_K)]
            for cp in copies:
                cp.wait()

    return run(h_rows, dest_win)


def _sc_combine(y_rows, dest_win, gate_words, first_win, n_win):
    n_cores, n_workers = _sc_worker_count()
    w = dest_win.shape[2]
    per_worker = n_win // n_workers

    @functools.partial(
        pl.kernel, mesh=_sc_mesh(),
        out_type=jax.ShapeDtypeStruct((n_win * w, ROW_TILES, LANES), U32),
        scratch_types=[pltpu.VMEM((per_worker, TOP_K, w), I32),
                       pltpu.VMEM((per_worker, TOP_K, w * SC_LANES), U32),
                       pltpu.VMEM((2, TOP_K, w, ROW_TILES, LANES), U32),
                       pltpu.VMEM((2, w, ROW_TILES, LANES), U32),
                       pltpu.SemaphoreType.DMA((2,)),
                       pltpu.SemaphoreType.DMA((2,))],
        compiler_params=pltpu.CompilerParams(needs_layout_passes=False),
        name="sc_combine")
    def run(y_hbm, dest_hbm, gate_hbm, out_hbm, idx_v, gate_v, rows_v, out_v, in_sems, out_sems):
        wid = lax.axis_index("s") * n_cores + lax.axis_index("c")
        first = wid * per_worker
        pltpu.sync_copy(dest_hbm.at[pl.ds(first_win + first, per_worker)], idx_v)
        pltpu.sync_copy(gate_hbm.at[pl.ds(first_win + first, per_worker)], gate_v)

        def gather(it, slot):
            return [pltpu.make_async_copy(y_hbm.at[idx_v.at[it, kk]], rows_v.at[slot, kk], in_sems.at[slot])
                    for kk in range(TOP_K)]

        def put(it, slot):
            return pltpu.make_async_copy(out_v.at[slot], out_hbm.at[pl.ds((first + it) * w, w)],
                                         out_sems.at[slot])

        def step(it, slot):
            @pl.when(it + 1 < per_worker)
            def _():
                for cp in gather(it + 1, 1 - slot):
                    cp.start()

            for cp in gather(it, slot):
                cp.wait()

            @pl.when(it >= 2)
            def _():
                put(it - 2, slot).wait()

            @plsc.parallel_loop(0, w)
            def _(j):
                own = pl.ds(pl.multiple_of(j * SC_LANES, SC_LANES), SC_LANES)
                gates = [plsc.bitcast(gate_v[it, kk, own], BF16) for kk in range(TOP_K)]
                for t in range(ROW_TILES):
                    for c in range(LANES // SC_LANES):
                        lanes = pl.ds(c * SC_LANES, SC_LANES)
                        v = [plsc.bitcast(rows_v[slot, kk, j, t, lanes], BF16) * gates[kk]
                             for kk in range(TOP_K)]
                        out_v[slot, j, t, lanes] = plsc.bitcast((v[0] + v[1]) + (v[2] + v[3]), U32)

            put(it, slot).start()

        for cp in gather(0, 0):
            cp.start()

        @pl.loop(0, per_worker // 2)
        def _(p):
            step(2 * p, 0)
            step(2 * p + 1, 1)

        put(per_worker - 2, 0).wait()
        put(per_worker - 1, 1).wait()

    return run(y_rows, dest_win, gate_words)


def _expert_kernel(be_ref, nu_ref, nx_ref, nv_ref, x_ref, wgu_hbm, bgu_ref, wd_hbm, bd_ref, y_ref,
                   wgu32, wd32, wgu16, wd16, sems):
    i = pl.program_id(0)
    e = be_ref[i]
    prev = be_ref[jnp.maximum(i - 1, 0)]

    def weight_copies(ex):
        return (pltpu.make_async_copy(wgu_hbm.at[ex], wgu32, sems.at[0]),
                pltpu.make_async_copy(wd_hbm.at[ex], wd32, sems.at[1]))

    @pl.when(i == 0)
    def _():
        for cp in weight_copies(e):
            cp.start()

    @pl.when((i == 0) | (e != prev))
    def _():
        for cp in weight_copies(e):
            cp.wait()
        rows = 128

        def cast(r, carry):
            sl = pl.ds(pl.multiple_of(r * rows, rows), rows)
            wgu16[sl, :] = wgu32[sl, :].astype(BF16)
            wd16[sl, :] = wd32[sl, :].astype(BF16)
            return carry

        lax.fori_loop(0, D_MODEL // rows, cast, 0)
        nxt = nx_ref[i]

        @pl.when(nxt >= 0)
        def _():
            for cp in weight_copies(nxt):
                cp.start()

    def run_rows(rows):
        x = _load_row_tiles(x_ref, rows).astype(BF16)
        bgu = bgu_ref[0]
        glu = _dot(x, wgu16[:, 0:D_FF]) + bgu[:, 0:D_FF]
        lin = _dot(x, wgu16[:, D_FF:]) + bgu[:, D_FF:]
        glu = jnp.minimum(glu, SWIGLU_LIMIT)
        lin = jnp.clip(lin, -SWIGLU_LIMIT, SWIGLU_LIMIT)
        act = glu * _sigmoid(SWIGLU_ALPHA * glu) * (lin + 1.0)
        y = _dot(act.astype(BF16), wd16[...]) + bd_ref[0]
        _store_row_tiles(y_ref, y.astype(BF16).astype(F32))

    valid = nv_ref[i]
    for rows in range(MOE_ROWS_STEP, MOE_BLOCK + 1, MOE_ROWS_STEP):
        pl.when((valid > rows - MOE_ROWS_STEP) & (valid <= rows))(functools.partial(run_rows, rows))


def _experts(blk_e, n_used, next_e, blk_valid, xs2, w_gu, b_gu3, w_down, b_down3):
    row_blk = pl.BlockSpec((MOE_BLOCK * ROW_TILES, LANES),
                           lambda i, be, nu, nx, nv: (jnp.minimum(i, nu[0] - 1), 0))
    grid_spec = pltpu.PrefetchScalarGridSpec(
        num_scalar_prefetch=4,
        grid=(N_BLOCKS,),
        in_specs=[row_blk,
                  pl.BlockSpec(memory_space=pl.ANY),
                  pl.BlockSpec((1, 1, 2 * D_FF), lambda i, be, nu, nx, nv: (be[i], 0, 0)),
                  pl.BlockSpec(memory_space=pl.ANY),
                  pl.BlockSpec((1, 1, D_MODEL), lambda i, be, nu, nx, nv: (be[i], 0, 0))],
        out_specs=row_blk,
        scratch_shapes=[pltpu.VMEM((D_MODEL, 2 * D_FF), F32),
                        pltpu.VMEM((D_FF, D_MODEL), F32),
                        pltpu.VMEM((D_MODEL, 2 * D_FF), BF16),
                        pltpu.VMEM((D_FF, D_MODEL), BF16),
                        pltpu.SemaphoreType.DMA((2,))],
    )
    return pl.pallas_call(
        _expert_kernel,
        grid_spec=grid_spec,
        out_shape=jax.ShapeDtypeStruct((N_SLOTS * ROW_TILES, LANES), U32),
        compiler_params=pltpu.CompilerParams(dimension_semantics=("arbitrary",),
                                             vmem_limit_bytes=VMEM_LIMIT),
        name="experts",
    )(blk_e, n_used, next_e, blk_valid, xs2, w_gu, b_gu3, w_down, b_down3)


def _final_kernel(ysum_ref, x1_ref, mod_ref, gpost_ref, *maybe_alias_and_out):
    o_ref = maybe_alias_and_out[-1]
    y = _load_row_tiles(ysum_ref, TOK_TILE)
    gate_f = mod_ref[0][5:6]
    o_ref[...] = x1_ref[...] + _rms(y) * (gate_f * gpost_ref[...])


def _final(part, ysum, x1, mod3, g_post, out_so_far):
    tiles = SEQ // TOK_TILE
    tok = lambda w: pl.BlockSpec((TOK_TILE, w), lambda i: (part * tiles + i, 0))
    in_specs = [pl.BlockSpec((TOK_TILE * ROW_TILES, LANES), lambda i: (i, 0)),
                tok(D_MODEL),
                pl.BlockSpec((1, 6, D_MODEL), lambda i: (part, 0, 0)),
                pl.BlockSpec((1, D_MODEL), lambda i: (0, 0))]
    args = [ysum, x1, mod3, g_post]
    aliases = {}
    if out_so_far is not None:
        in_specs.append(pl.BlockSpec(memory_space=pl.ANY))
        args.append(out_so_far)
        aliases = {len(args) - 1: 0}
    return pl.pallas_call(
        _final_kernel,
        grid=(tiles,),
        in_specs=in_specs,
        out_specs=tok(D_MODEL),
        out_shape=jax.ShapeDtypeStruct((N_TOK, D_MODEL), F32),
        input_output_aliases=aliases,
        compiler_params=pltpu.CompilerParams(vmem_limit_bytes=VMEM_LIMIT),
        name="combine_final",
    )(*args)


def _slot_kernel(idx_ref, rank_ref, start_ref, dest_ref):
    tn = idx_ref.shape[1]
    experts = lax.broadcasted_iota(I32, (N_EXPERTS, tn), 0)
    starts = start_ref[...].astype(F32)
    rows = [jnp.sum(jnp.where(experts == idx_ref[kk:kk + 1, :], starts, 0.0), axis=0, keepdims=True)
            for kk in range(TOP_K)]
    dest_ref[...] = jnp.concatenate(rows, axis=0).astype(I32) + rank_ref[...]


def _slots(idx, rank, pstart):
    tn = N_TOK // 2
    lanes_tok = pl.BlockSpec((TOP_K, tn), lambda i: (0, i))
    return pl.pallas_call(
        _slot_kernel,
        grid=(N_TOK // tn,),
        in_specs=[lanes_tok, lanes_tok, pl.BlockSpec((N_EXPERTS, 1), lambda i: (0, 0))],
        out_specs=lanes_tok,
        out_shape=jax.ShapeDtypeStruct((TOP_K, N_TOK), I32),
        name="moe_slots",
    )(idx, rank, pstart.reshape(N_EXPERTS, 1))


def _routing_tables(idx, rank, counts):
    counts = counts.reshape(N_EXPERTS)
    experts = jnp.arange(N_EXPERTS, dtype=I32)
    padded = ((counts + MOE_BLOCK - 1) // MOE_BLOCK) * MOE_BLOCK
    pend = jnp.cumsum(padded)
    pstart = pend - padded
    dest = _slots(idx, rank, pstart)
    dest_win = dest.reshape(TOP_K, N_TOK // SC_WINDOW, SC_WINDOW).transpose(1, 0, 2)
    n_used = (pend[-1] // MOE_BLOCK).astype(I32).reshape(1)
    blk_start = jnp.arange(N_BLOCKS, dtype=I32) * MOE_BLOCK
    blk_e = jnp.sum(blk_start[:, None] >= pend[None, :], axis=1).astype(I32)
    last_e = jnp.max(jnp.where(counts > 0, experts, 0))
    blk_e = jnp.minimum(blk_e, last_e)
    later = (experts[None, :] > experts[:, None]) & (counts[None, :] > 0)
    next_nonempty = jnp.min(jnp.where(later, experts[None, :], N_EXPERTS), axis=1)
    next_nonempty = jnp.where(next_nonempty == N_EXPERTS, -1, next_nonempty).astype(I32)
    of_block = lambda table: jnp.sum(jnp.where(blk_e[:, None] == experts, table, 0), axis=1).astype(I32)
    next_e = of_block(next_nonempty)
    blk_valid = jnp.clip(of_block(counts) - (blk_start - of_block(pstart)), 0, MOE_BLOCK)
    blk_valid = jnp.where(blk_start < pend[-1], blk_valid, 0).astype(I32)
    return dest, dest_win, blk_e, n_used, next_e, blk_valid


def kernel(x, c, w_ada, b_ada, g_pre_mix, g_post_mix, w_in, attn_norm_w, hgrn_lb, hgrn_norm_w,
           w_out, g_pre_ffn, g_post_ffn, w_router, b_router, w_gu, b_gu, w_down, b_down):
    c_pad = jnp.pad(c, ((0, SUBLANES - BATCH), (0, 0)))
    mod = _ada_mod(c_pad, w_ada[0], b_ada)
    mod3 = mod[:BATCH].reshape(BATCH, 6, D_MODEL)

    x2 = x.reshape(N_TOK, D_MODEL)
    q4, k4, v4, q16, k16, v16, qr, fr, ir, gr = _inproj(x2, mod3, g_pre_mix, w_in[0])
    nat = lambda t: t.reshape(BATCH, SEQ, D_REC)
    ya = _attention(q4, k4, v4, q16, k16, v16, attn_norm_w)
    yr = _hgrn(nat(qr), nat(fr), nat(ir), nat(gr), hgrn_lb, hgrn_norm_w)

    x1, h2, idx, gates, rank, counts = _mid(
        ya.reshape(N_TOK, D_ATTN), yr.reshape(N_TOK, D_REC), x2, mod3, g_post_mix, g_pre_ffn,
        w_out[0], w_router[0], b_router)

    dest, dest_win, blk_e, n_used, next_e, blk_valid = _routing_tables(idx, rank, counts)
    xs = _sc_dispatch(h2.reshape(N_TOK, ROW_TILES, LANES), dest_win)
    ys = _experts(blk_e, n_used, next_e, blk_valid, xs.reshape(N_SLOTS * ROW_TILES, LANES),
                  w_gu[0], b_gu[0].reshape(N_EXPERTS, 1, 2 * D_FF),
                  w_down[0], b_down[0].reshape(N_EXPERTS, 1, D_MODEL))
    ys3 = ys.reshape(N_SLOTS, ROW_TILES, LANES)
    by_window = lambda t: t.reshape(TOP_K, N_TOK // SC_SUM_WINDOW, SC_SUM_WINDOW).transpose(1, 0, 2)
    sum_win = by_window(dest)
    gate_bits = lax.bitcast_convert_type(gates.astype(BF16), jnp.uint16).astype(U32)
    gate_words = jnp.broadcast_to(by_window(gate_bits | (gate_bits << 16))[..., None],
                                  sum_win.shape + (SC_LANES,)).reshape(sum_win.shape[:2] + (-1,))
    win_per_seq = SEQ // SC_SUM_WINDOW
    out = None
    for b in range(BATCH):
        ysum = _sc_combine(ys3, sum_win, gate_words, b * win_per_seq, win_per_seq)
        out = _final(b, ysum.reshape(SEQ * ROW_TILES, LANES), x1, mod3, g_post_ffn, out)
    return out.reshape(BATCH, SEQ, D_MODEL)
```

```python
import functools
import math

import jax
import jax.numpy as jnp
from jax import lax
from jax.experimental import pallas as pl
from jax.experimental.pallas import tpu as pltpu
from jax.experimental.pallas import tpu_sc as plsc

F32 = jnp.float32
BF16 = jnp.bfloat16
I32 = jnp.int32
U32 = jnp.uint32

D_MODEL = 1024
BATCH = 4
SEQ = 4096
N_TOK = BATCH * SEQ
D_ATTN = 512
HEAD_DIM_A = 64
ATT_BLOCK = 128
DIL_MID = 4
DIL_MAX = 16
SUB_MID = SEQ // DIL_MID
SUB_MAX = SEQ // DIL_MAX
D_REC = 512
HGRN_CHUNK = 32
HGRN_SUPER = 256
N_EXPERTS = 32
TOP_K = 4
D_FF = 1024
SWIGLU_LIMIT = 7.0
SWIGLU_ALPHA = 1.702
EPS = 1e-6
NEG_BIG = -1e30
Q_SCALE = HEAD_DIM_A ** -0.5 * math.log2(math.e)

LANES = 128
SUBLANES = 8
ROW_TILES = D_MODEL // 2 // LANES

TOK_TILE = 512
MOE_BLOCK = 1024
MOE_ROWS_STEP = 128
N_SLOTS = N_TOK * TOP_K + N_EXPERTS * MOE_BLOCK
N_BLOCKS = N_SLOTS // MOE_BLOCK
SC_WINDOW = 128
SC_SUM_WINDOW = 8
SC_SUM_DEPTH = 4
FINISH_UNIT = 1024
FINISH_PIECES = (4, 4, 4, 4)
SC_LANES = 16
VMEM_LIMIT = 56 * 1024 * 1024


def _sigmoid(x):
    return 1.0 / (1.0 + jnp.exp(-x))


def _dot(a, b):
    return jnp.dot(a, b, preferred_element_type=F32)


def _dot_nt(a, b):
    return lax.dot_general(a, b, (((1,), (1,)), ((), ())), preferred_element_type=F32)


def _split_bf16(x):
    hi = x.astype(BF16)
    return hi, (x - hi.astype(F32)).astype(BF16)


def _rms(x):
    return x * lax.rsqrt(jnp.mean(x * x, axis=-1, keepdims=True) + EPS)


def _ada_kernel(c_ref, w_ref, b_ref, o_ref):
    c = c_ref[...]
    cond = c * _sigmoid(c)
    c_hi, c_lo = _split_bf16(cond)
    w_hi, w_lo = _split_bf16(w_ref[...])
    o_ref[...] = _dot(c_hi, w_hi) + _dot(c_lo, w_hi) + _dot(c_hi, w_lo) + b_ref[...]


def _ada_mod(c_pad, w_ada, b_ada):
    n = w_ada.shape[1]
    tn = 1536
    return pl.pallas_call(
        _ada_kernel,
        grid=(n // tn,),
        in_specs=[pl.BlockSpec((SUBLANES, D_MODEL), lambda j: (0, 0)),
                  pl.BlockSpec((D_MODEL, tn), lambda j: (0, j)),
                  pl.BlockSpec((1, tn), lambda j: (0, j))],
        out_specs=pl.BlockSpec((SUBLANES, tn), lambda j: (0, j)),
        out_shape=jax.ShapeDtypeStruct((SUBLANES, n), F32),
        compiler_params=pltpu.CompilerParams(vmem_limit_bytes=VMEM_LIMIT),
        name="ada_mod",
    )(c_pad, w_ada, b_ada)


def _cast_weights_once(w_ref, w16):
    rows = 128

    @pl.when(pl.program_id(0) == 0)
    def _():
        def cast(r, carry):
            sl = pl.ds(pl.multiple_of(r * rows, rows), rows)
            w16[sl, :] = w_ref[sl, :].astype(BF16)
            return carry

        lax.fori_loop(0, w_ref.shape[0] // rows, cast, 0)


def _resident(shape):
    return pl.BlockSpec(shape, lambda i: (0,) * len(shape), pipeline_mode=pl.Buffered(1))


def _inproj_kernel(x_ref, mod_ref, g_ref, w_ref, q4, k4, v4, q16, k16, v16, qr, fr, ir, gr,
                   w16, stage_nat, stage_mid):
    _cast_weights_once(w_ref, w16)
    mod = mod_ref[0]
    shift, scale = mod[0:1], mod[1:2]
    h = _rms(x_ref[...]) * (g_ref[...] * (1.0 + scale)) + shift
    hb = h.astype(BF16)
    slabs = D_ATTN // LANES
    rows_mid = TOK_TILE // DIL_MID
    rows_max = TOK_TILE // DIL_MAX

    def proj(j):
        return _dot(hb, w16[:, j * D_ATTN:(j + 1) * D_ATTN])

    for j, (o_mid, o_max) in enumerate(((q4, q16), (k4, k16), (v4, v16))):
        r = proj(j)
        if j == 0:
            r = r * Q_SCALE
        for cs in range(slabs):
            stage_nat[cs] = r[:, cs * LANES:(cs + 1) * LANES]
        for cs in range(slabs):
            lanes = slice(cs * LANES, (cs + 1) * LANES)
            for sub in range(DIL_MID):
                piece = stage_nat[cs, pl.ds(sub, rows_mid, stride=DIL_MID), :]
                o_mid[sub, :, lanes] = piece.astype(BF16)
                stage_mid[cs, sub] = piece
            for sub in range(DIL_MAX):
                piece = stage_mid[cs, sub % DIL_MID, pl.ds(sub // DIL_MID, rows_max, stride=DIL_MID), :]
                o_max[sub, :, lanes] = piece.astype(BF16)
    for j, o_ref in enumerate((qr, fr, ir, gr)):
        o_ref[...] = proj(3 + j)


def _inproj(x2, mod3, g_pre, w_in):
    tiles_per_seq = SEQ // TOK_TILE
    rows_mid = TOK_TILE // DIL_MID
    rows_max = TOK_TILE // DIL_MAX
    mid = pl.BlockSpec((None, DIL_MID, rows_mid, D_ATTN),
                       lambda i: (i // tiles_per_seq, 0, i % tiles_per_seq, 0))
    mx = pl.BlockSpec((None, DIL_MAX, rows_max, D_ATTN),
                      lambda i: (i // tiles_per_seq, 0, i % tiles_per_seq, 0))
    nat = pl.BlockSpec((TOK_TILE, D_REC), lambda i: (i, 0))
    mid_shape = jax.ShapeDtypeStruct((BATCH, DIL_MID, SUB_MID, D_ATTN), BF16)
    mx_shape = jax.ShapeDtypeStruct((BATCH, DIL_MAX, SUB_MAX, D_ATTN), BF16)
    nat_shape = jax.ShapeDtypeStruct((N_TOK, D_REC), F32)
    return pl.pallas_call(
        _inproj_kernel,
        grid=(N_TOK // TOK_TILE,),
        in_specs=[pl.BlockSpec((TOK_TILE, D_MODEL), lambda i: (i, 0)),
                  pl.BlockSpec((1, 6, D_MODEL), lambda i: (i // tiles_per_seq, 0, 0)),
                  pl.BlockSpec((1, D_MODEL), lambda i: (0, 0)),
                  _resident(w_in.shape)],
        out_specs=[mid, mid, mid, mx, mx, mx, nat, nat, nat, nat],
        out_shape=[mid_shape] * 3 + [mx_shape] * 3 + [nat_shape] * 4,
        scratch_shapes=[pltpu.VMEM(w_in.shape, BF16),
                        pltpu.VMEM((D_ATTN // LANES, TOK_TILE, LANES), F32),
                        pltpu.VMEM((D_ATTN // LANES, DIL_MID, rows_mid, LANES), F32)],
        compiler_params=pltpu.CompilerParams(dimension_semantics=("arbitrary",),
                                             vmem_limit_bytes=VMEM_LIMIT),
        name="inproj",
    )(x2, mod3, g_pre, w_in)


def _attn_kernel(q4_ref, k4_ref, v4_ref, q16_ref, k16_ref, v16_ref, nw_ref, o_ref,
                 sel_scr, keep_scr, o_scr, m_scr, l_scr):
    bw = ATT_BLOCK
    lane = lax.broadcasted_iota(I32, (bw, LANES), 1)
    head0 = lane < HEAD_DIM_A
    head_masks = (jnp.where(head0, 1.0, 0.0).astype(BF16), jnp.where(head0, 0.0, 1.0).astype(BF16))

    rr = lax.broadcasted_iota(I32, (bw, bw), 0)
    cc = lax.broadcasted_iota(I32, (bw, bw), 1)
    piece = bw // DIL_MID
    rp = DIL_MID * (rr % piece) + rr // piece
    cp = DIL_MID * (cc % piece) + cc // piece
    for var, (r, c) in enumerate(((rr, cc), (rr, cc), (rp, cp), (rp, cp))):
        first = var % 2 == 1
        left = (c < r) if first else (c >= r)
        right = jnp.zeros_like(left) if first else jnp.logical_not(left)
        sel_scr[var] = jnp.where(left, 1.0, 0.0)
        keep_scr[var, 0] = jnp.where(left, 1.0, 0.0).astype(BF16)
        keep_scr[var, 1] = jnp.where(right, 1.0, 0.0).astype(BF16)

    def score_matmuls(loaded):
        return [_dot_nt(q * hm, k) for (q, k, _, _, _) in loaded for hm in head_masks]

    def finish_group(scores, loaded):
        probs = []
        for n, s in enumerate(scores):
            _, _, _, var, right_bias = loaded[n // 2]
            folded = jnp.where(sel_scr[var] > 0.5, s[:, 0:bw], s[:, bw:] + right_bias)
            m = jnp.max(folded, axis=-1, keepdims=True)
            p = jnp.exp2(folded - m)
            pb = p.astype(BF16)
            spread = jnp.concatenate([pb * keep_scr[var, 0], pb * keep_scr[var, 1]], axis=1)
            probs.append((spread, jnp.sum(p, axis=-1, keepdims=True), m))
        pvs = [_dot(p, loaded[n // 2][2]) for n, (p, _, _) in enumerate(probs)]
        results = []
        for u in range(len(loaded)):
            (_, l0, m0), (_, l1, m1) = probs[2 * u], probs[2 * u + 1]
            results.append((jnp.where(head0, pvs[2 * u], pvs[2 * u + 1]),
                            jnp.where(head0, jnp.broadcast_to(m0, (bw, LANES)),
                                      jnp.broadcast_to(m1, (bw, LANES))),
                            jnp.where(head0, jnp.broadcast_to(l0, (bw, LANES)),
                                      jnp.broadcast_to(l1, (bw, LANES)))))
        return results

    group = 8

    def first_block_bias(blk):
        return jnp.where(blk == 0, NEG_BIG, 0.0)

    def load_d1(g):
        loaded, starts = [], []
        for i in range(group):
            blk = g * group + i
            qs = pl.multiple_of(blk * piece, piece)
            ks = pl.multiple_of(jnp.maximum(blk - 1, 0) * piece, piece)
            q = jnp.concatenate([q4_ref[r, pl.ds(qs, piece), :] for r in range(DIL_MID)], axis=0)
            k = jnp.concatenate([k4_ref[r, pl.ds(ks + half * piece, piece), :]
                                 for half in range(2) for r in range(DIL_MID)], axis=0)
            v = jnp.concatenate([v4_ref[r, pl.ds(ks + half * piece, piece), :]
                                 for half in range(2) for r in range(DIL_MID)], axis=0)
            loaded.append((q, k, v, jnp.where(blk == 0, 3, 2), first_block_bias(blk)))
            starts.append(qs)

        def store(results):
            for qs, parts in zip(starts, results):
                for r in range(DIL_MID):
                    for scr, val in zip((o_scr, m_scr, l_scr), parts):
                        scr[0, r, pl.ds(qs, piece), :] = val[r * piece:(r + 1) * piece]

        return loaded, store

    def load_d4(g):
        loaded, dsts = [], []
        for i in range(group):
            r, blk = i % DIL_MID, g * (group // DIL_MID) + i // DIL_MID
            qs = pl.multiple_of(blk * bw, bw)
            ks = pl.multiple_of(jnp.maximum(blk - 1, 0) * bw, bw)
            loaded.append((q4_ref[r, pl.ds(qs, bw), :], k4_ref[r, pl.ds(ks, 2 * bw), :],
                           v4_ref[r, pl.ds(ks, 2 * bw), :], jnp.where(blk == 0, 1, 0),
                           first_block_bias(blk)))
            dsts.append((r, qs))

        def store(results):
            for (r, qs), parts in zip(dsts, results):
                for scr, val in zip((o_scr, m_scr, l_scr), parts):
                    scr[1, r, pl.ds(qs, bw), :] = val

        return loaded, store

    def load_d16(g):
        loaded, dsts = [], []
        for i in range(group):
            r, blk = g * (group // 2) + i // 2, i % 2
            loaded.append((q16_ref[r, blk * bw:(blk + 1) * bw, :], k16_ref[r], v16_ref[r],
                           1 - blk, NEG_BIG if blk == 0 else 0.0))
            dsts.append((r % DIL_MID, pl.ds(blk * (bw * DIL_MID) + r // DIL_MID, bw, stride=DIL_MID)))

        def store(results):
            for (sub, dst), parts in zip(dsts, results):
                for scr, val in zip((o_scr, m_scr, l_scr), parts):
                    scr[2, sub, dst, :] = val

        return loaded, store

    def run_group(load):
        def step(g, carry):
            loaded, store = load(g)
            store(finish_group(score_matmuls(loaded), loaded))
            return carry
        return step

    n_groups = SEQ // bw // group
    for load in (load_d1, load_d4, load_d16):
        lax.fori_loop(0, n_groups, run_group(load), 0)

    rows = 256
    hi = lax.broadcasted_iota(I32, (LANES, LANES), 0) // HEAD_DIM_A
    hj = lax.broadcasted_iota(I32, (LANES, LANES), 1) // HEAD_DIM_A
    head_sum = jnp.where(hi == hj, 1.0, 0.0).astype(BF16)
    nw = nw_ref[...]

    def head_sums(x):
        x_hi, x_lo = _split_bf16(x)
        return _dot(x_hi, head_sum) + _dot(x_lo, head_sum)

    def merge(t, carry):
        r = t // (SUB_MID // rows)
        start = pl.multiple_of((t % (SUB_MID // rows)) * rows, rows)
        sl = pl.ds(start, rows)
        own_score = head_sums(q4_ref[r, sl, :].astype(F32) * k4_ref[r, sl, :].astype(F32))
        ms = [m_scr[n, r, sl, :] for n in range(3)]
        mx = jnp.maximum(jnp.maximum(jnp.maximum(ms[0], ms[1]), ms[2]), own_score)
        ws = [jnp.exp2(m - mx) for m in ms]
        w_own = float(len(ms)) * jnp.exp2(own_score - mx)
        num = (ws[0] * o_scr[0, r, sl, :] + ws[1] * o_scr[1, r, sl, :] + ws[2] * o_scr[2, r, sl, :]
               + w_own * v4_ref[r, sl, :].astype(F32))
        den = ws[0] * l_scr[0, r, sl, :] + ws[1] * l_scr[1, r, sl, :] + ws[2] * l_scr[2, r, sl, :] + w_own
        o = num / den
        mean_sq = head_sums(o * o) * (1.0 / HEAD_DIM_A)
        o_ref[pl.ds(start * DIL_MID + r, rows, stride=DIL_MID), :] = o * lax.rsqrt(mean_sq + EPS) * nw
        return carry

    lax.fori_loop(0, SEQ // rows, merge, 0, unroll=8)


def _attention(q4, k4, v4, q16, k16, v16, attn_norm_w):
    hp = D_ATTN // LANES
    mid = pl.BlockSpec((None, DIL_MID, SUB_MID, LANES), lambda b, h: (b, 0, 0, h))
    mx = pl.BlockSpec((None, DIL_MAX, SUB_MAX, LANES), lambda b, h: (b, 0, 0, h))
    scr = pltpu.VMEM((3, DIL_MID, SUB_MID, LANES), F32)
    return pl.pallas_call(
        _attn_kernel,
        grid=(BATCH, hp),
        in_specs=[mid, mid, mid, mx, mx, mx, pl.BlockSpec((1, LANES), lambda b, h: (0, h))],
        out_specs=pl.BlockSpec((None, SEQ, LANES), lambda b, h: (b, 0, h)),
        out_shape=jax.ShapeDtypeStruct((BATCH, SEQ, D_ATTN), F32),
        scratch_shapes=[pltpu.VMEM((4, ATT_BLOCK, ATT_BLOCK), F32),
                        pltpu.VMEM((4, 2, ATT_BLOCK, ATT_BLOCK), BF16), scr, scr, scr],
        compiler_params=pltpu.CompilerParams(vmem_limit_bytes=VMEM_LIMIT),
        name="dilated_attn",
    )(q4, k4, v4, q16, k16, v16, attn_norm_w)


def _hgrn_kernel(qr_ref, fr_ref, ir_ref, gr_ref, lb_ref, nw_ref, o_ref,
                 qe_scr, oi_scr, delta_scr, dec_scr, st_scr):
    sup, c = HGRN_SUPER, HGRN_CHUNK
    nch = sup // c
    n_sup = SEQ // sup
    lbp = lb_ref[...]
    lmx = jnp.max(lbp, axis=0, keepdims=True)
    ex = jnp.exp(lbp - lmx)
    lb = ex[0:1] / (ex[0:1] + ex[1:2])
    nw = nw_ref[...]

    ri = lax.broadcasted_iota(I32, (sup, sup), 0)
    ci = lax.broadcasted_iota(I32, (sup, sup), 1)
    same_chunk = (ri // c) == (ci // c)
    causal = same_chunk & (ci <= ri)
    cum_op = jnp.where(causal, 1.0, 0.0).astype(BF16)

    def chunk_rows(kd, ch):
        parts = []
        if ch > 0:
            parts.append(jnp.zeros((ch * c, LANES), BF16))
        parts.append(kd[ch * c:(ch + 1) * c])
        if ch < nch - 1:
            parts.append(jnp.zeros(((nch - 1 - ch) * c, LANES), BF16))
        return jnp.concatenate(parts, axis=0)

    group = 4

    def independent(g, carry):
        ts = [g * group + i for i in range(group)]
        sls = [pl.ds(pl.multiple_of(t * sup, sup), sup) for t in ts]
        pre = []
        for sl in sls:
            f = lb + (1.0 - lb) * _sigmoid(fr_ref[sl, :])
            logf_hi, logf_lo = _split_bf16(jnp.log(f))
            pre.append((1.0 - f, _dot(cum_op, logf_hi) + _dot(cum_op, logf_lo)))
        mid = []
        for t, sl, (kk, b) in zip(ts, sls, pre):
            b_last = jnp.concatenate(
                [jnp.broadcast_to(b[(ch + 1) * c - 1:(ch + 1) * c], (c, LANES)) for ch in range(nch)],
                axis=0)
            q = qr_ref[sl, :]
            qeb = (q * _sigmoid(q) * jnp.exp(b)).astype(BF16)
            ke = (kk * jnp.exp(-b)).astype(BF16)
            kd = (kk * jnp.exp(b_last - b)).astype(BF16)
            qe_scr[sl, :] = qeb
            dec_rows = jnp.concatenate([b_last[ch * c:ch * c + 1] for ch in range(nch)], axis=0)
            dec_scr[pl.ds(pl.multiple_of(t * nch, nch), nch), :] = jnp.exp(dec_rows)
            v = ir_ref[sl, :]
            vt = v.T.astype(BF16)
            scores = _dot_nt(qeb, ke)
            for pair in range(nch // 2):
                rhs = jnp.concatenate([chunk_rows(kd, 2 * pair), chunk_rows(kd, 2 * pair + 1)], axis=1)
                d2 = _dot(vt, rhs)
                delta_scr[t * nch + 2 * pair] = d2[:, 0:LANES]
                delta_scr[t * nch + 2 * pair + 1] = d2[:, LANES:]
            mid.append((scores, v.astype(BF16)))
        for sl, (scores, vb) in zip(sls, mid):
            a = jnp.where(causal, scores, 0.0)
            oi_scr[sl, :] = _dot(a.astype(BF16), vb)
        return carry

    lax.fori_loop(0, n_sup // group, independent, 0)

    def recur(ch, st):
        st_scr[ch] = st.astype(BF16)
        return st * dec_scr[pl.ds(ch, 1), :] + delta_scr[ch]

    lax.fori_loop(0, SEQ // c, recur, jnp.zeros((LANES, LANES), F32), unroll=8)

    def finish(t, carry):
        sl = pl.ds(pl.multiple_of(t * sup, sup), sup)
        parts = [_dot_nt(qe_scr[pl.ds(pl.multiple_of(t * sup + ch * c, c), c), :], st_scr[t * nch + ch])
                 for ch in range(nch)]
        o = oi_scr[sl, :] + jnp.concatenate(parts, axis=0)
        g = gr_ref[sl, :]
        o_ref[sl, :] = (_rms(o) * nw * (g * _sigmoid(g))).astype(o_ref.dtype)
        return carry

    lax.fori_loop(0, n_sup, finish, 0, unroll=8)


def _hgrn(qr, fr, ir, gr, hgrn_lb, hgrn_norm_w):
    nh = D_REC // LANES
    n_chunks = SEQ // HGRN_CHUNK
    blk = pl.BlockSpec((None, SEQ, LANES), lambda b, h: (b, 0, h))
    return pl.pallas_call(
        _hgrn_kernel,
        grid=(BATCH, nh),
        in_specs=[blk, blk, blk, blk,
                  pl.BlockSpec((2, LANES), lambda b, h: (0, h)),
                  pl.BlockSpec((1, LANES), lambda b, h: (0, h))],
        out_specs=blk,
        out_shape=jax.ShapeDtypeStruct((BATCH, SEQ, D_REC), BF16),
        scratch_shapes=[pltpu.VMEM((SEQ, LANES), BF16),
                        pltpu.VMEM((SEQ, LANES), F32),
                        pltpu.VMEM((n_chunks, LANES, LANES), F32),
                        pltpu.VMEM((n_chunks, LANES), F32),
                        pltpu.VMEM((n_chunks, LANES, LANES), BF16)],
        compiler_params=pltpu.CompilerParams(vmem_limit_bytes=VMEM_LIMIT),
        name="hgrn2",
    )(qr, fr, ir, gr, hgrn_lb, hgrn_norm_w)


def _tile_batch(i):
    return i // (SEQ // TOK_TILE)


def _store_row_tiles(ref, val):
    rows, half = val.shape[0], D_MODEL // 2
    lo = lax.bitcast_convert_type(val[:, :half], U32)
    hi = lax.bitcast_convert_type(val[:, half:], U32)
    words = lax.shift_right_logical(lo, jnp.uint32(16)) | (hi & jnp.uint32(0xFFFF0000))
    for j in range(ROW_TILES):
        ref[pl.ds(j, rows, stride=ROW_TILES), :] = words[:, j * LANES:(j + 1) * LANES]


def _load_row_tiles(ref, rows):
    words = jnp.concatenate(
        [ref[pl.ds(j, rows, stride=ROW_TILES), :] for j in range(ROW_TILES)], axis=1)
    lo = lax.bitcast_convert_type(lax.shift_left(words, jnp.uint32(16)), F32)
    hi = lax.bitcast_convert_type(words & jnp.uint32(0xFFFF0000), F32)
    return jnp.concatenate([lo, hi], axis=1)


def _mid_kernel(ya_ref, yr_ref, x_ref, mod_ref, gpost_ref, gpre_ref, wo_ref, wr_ref, br_ref,
                x1_ref, h2_ref, idx_ref, gate_ref, rank_ref, cnt_ref, carry_ref, wo16):
    i = pl.program_id(0)
    _cast_weights_once(wo_ref, wo16)

    @pl.when(i == 0)
    def _():
        carry_ref[...] = jnp.zeros_like(carry_ref)

    mod = mod_ref[0]
    gate_m, shift_f, scale_f = mod[2:3], mod[3:4], mod[4:5]
    y = _dot(ya_ref[...].astype(BF16), wo16[0:D_ATTN, :]) + _dot(yr_ref[...], wo16[D_ATTN:, :])
    x1 = x_ref[...] + _rms(y) * (gate_m * gpost_ref[...])
    x1_ref[...] = x1
    h2 = _rms(x1) * (gpre_ref[...] * (1.0 + scale_f)) + shift_f
    h2_hi = h2.astype(BF16)
    h2_rounded = h2_hi.astype(F32)
    _store_row_tiles(h2_ref, h2_rounded)

    tm = h2.shape[0]
    h2_lo = (h2 - h2_rounded).astype(BF16)
    wr_hi, wr_lo = _split_bf16(wr_ref[...])
    parts = jnp.concatenate([_dot(h2_hi, jnp.concatenate([wr_hi, wr_lo], axis=1)),
                             _dot(h2_lo, wr_hi),
                             jnp.broadcast_to(br_ref[...], (tm, N_EXPERTS))], axis=1).T
    work = sum(parts[g * N_EXPERTS:(g + 1) * N_EXPERTS] for g in range(LANES // N_EXPERTS))
    eidx = lax.broadcasted_iota(I32, (N_EXPERTS, tm), 0).astype(F32)
    vals, idxs = [], []
    onehot = jnp.zeros((N_EXPERTS, tm), F32)
    for _ in range(TOP_K):
        m = jnp.max(work, axis=0, keepdims=True)
        sel = jnp.min(jnp.where(work == m, eidx, float(N_EXPERTS)), axis=0, keepdims=True)
        hit = eidx == sel
        work = jnp.where(hit, -jnp.inf, work)
        onehot = jnp.where(hit, 1.0, onehot)
        vals.append(m)
        idxs.append(sel)
    ex = [jnp.exp(vv - vals[0]) for vv in vals]
    inv_den = 1.0 / (ex[0] + ex[1] + ex[2] + ex[3])

    ri = lax.broadcasted_iota(I32, (tm, tm), 0)
    ci = lax.broadcasted_iota(I32, (tm, tm), 1)
    strict_upper = jnp.where(ri < ci, 1.0, 0.0).astype(BF16)
    before = _dot(onehot.astype(BF16), strict_upper) + carry_ref[...]
    ranks = [jnp.sum(jnp.where(eidx == idxs[kk], before, 0.0), axis=0, keepdims=True)
             for kk in range(TOP_K)]
    idx_ref[...] = jnp.concatenate(idxs, axis=0).astype(I32)
    rank_ref[...] = jnp.concatenate(ranks, axis=0).astype(I32)
    gate_ref[...] = jnp.concatenate([e * inv_den for e in ex], axis=0)
    total = carry_ref[...] + jnp.sum(onehot, axis=1, keepdims=True)
    carry_ref[...] = total
    cnt_ref[...] = total.astype(I32)


def _mid(ya, yr, x2, mod3, g_post, g_pre, w_out, w_router, b_router):
    tok = lambda w: pl.BlockSpec((TOK_TILE, w), lambda i: (i, 0))
    const = lambda s: pl.BlockSpec(s, lambda i: (0,) * len(s))
    lanes_tok = pl.BlockSpec((TOP_K, TOK_TILE), lambda i: (0, i))
    return pl.pallas_call(
        _mid_kernel,
        grid=(N_TOK // TOK_TILE,),
        in_specs=[tok(D_ATTN), tok(D_REC), tok(D_MODEL),
                  pl.BlockSpec((1, 6, D_MODEL), lambda i: (_tile_batch(i), 0, 0)),
                  const((1, D_MODEL)), const((1, D_MODEL)),
                  _resident((D_MODEL, D_MODEL)), const((D_MODEL, N_EXPERTS)), const((1, N_EXPERTS))],
        out_specs=[tok(D_MODEL),
                   pl.BlockSpec((TOK_TILE * ROW_TILES, LANES), lambda i: (i, 0)),
                   lanes_tok, lanes_tok, lanes_tok, const((N_EXPERTS, 1))],
        out_shape=[jax.ShapeDtypeStruct((N_TOK, D_MODEL), F32),
                   jax.ShapeDtypeStruct((N_TOK * ROW_TILES, LANES), U32),
                   jax.ShapeDtypeStruct((TOP_K, N_TOK), I32),
                   jax.ShapeDtypeStruct((TOP_K, N_TOK), F32),
                   jax.ShapeDtypeStruct((TOP_K, N_TOK), I32),
                   jax.ShapeDtypeStruct((N_EXPERTS, 1), I32)],
        scratch_shapes=[pltpu.VMEM((N_EXPERTS, 1), F32), pltpu.VMEM((D_MODEL, D_MODEL), BF16)],
        compiler_params=pltpu.CompilerParams(dimension_semantics=("arbitrary",),
                                             vmem_limit_bytes=VMEM_LIMIT),
        name="outproj_router",
    )(ya, yr, x2, mod3, g_post, g_pre, w_out, w_router, b_router)


def _sc_mesh():
    return plsc.VectorSubcoreMesh(core_axis_name="c", subcore_axis_name="s")


def _sc_worker_count():
    info = plsc.get_sparse_core_info()
    return info.num_cores, info.num_cores * info.num_subcores


def _sc_dispatch(h_rows, dest_win):
    n_cores, n_workers = _sc_worker_count()
    n_win = N_TOK // SC_WINDOW
    per_worker = n_win // n_workers

    @functools.partial(
        pl.kernel, mesh=_sc_mesh(),
        out_type=jax.ShapeDtypeStruct((N_SLOTS, ROW_TILES, LANES), U32),
        scratch_types=[pltpu.VMEM((TOP_K, SC_WINDOW), I32),
                       pltpu.VMEM((SC_WINDOW, ROW_TILES, LANES), U32),
                       pltpu.SemaphoreType.DMA],
        name="sc_dispatch")
    def run(h_hbm, dest_hbm, xs_hbm, idx_v, rows_v, sem):
        wid = lax.axis_index("s") * n_cores + lax.axis_index("c")

        @pl.loop(0, per_worker)
        def _(j):
            win = wid * per_worker + j
            pltpu.sync_copy(dest_hbm.at[win], idx_v)
            pltpu.sync_copy(h_hbm.at[pl.ds(win * SC_WINDOW, SC_WINDOW)], rows_v)
            copies = [pltpu.async_copy(rows_v, xs_hbm.at[idx_v.at[kk]], sem)
                      for kk in range(TOP_K)]
            for cp in copies:
                cp.wait()

    return run(h_rows, dest_win)


def _sc_combine(y_rows, dest_win, gate_words, first_win, n_win):
    n_cores, n_workers = _sc_worker_count()
    w = dest_win.shape[2]
    per_worker = n_win // n_workers
    depth = SC_SUM_DEPTH
    assert per_worker * n_workers == n_win and per_worker % depth == 0 and depth % 2 == 0, (n_win, n_workers)

    @functools.partial(
        pl.kernel, mesh=_sc_mesh(),
        out_type=jax.ShapeDtypeStruct((n_win * w, ROW_TILES, LANES), U32),
        scratch_types=[pltpu.VMEM((per_worker, TOP_K, w), I32),
                       pltpu.VMEM((per_worker, TOP_K, w * SC_LANES), U32),
                       pltpu.VMEM((depth, TOP_K, w, ROW_TILES, LANES), U32),
                       pltpu.VMEM((2, w, ROW_TILES, LANES), U32),
                       pltpu.SemaphoreType.DMA((depth,)),
                       pltpu.SemaphoreType.DMA((2,))],
        compiler_params=pltpu.CompilerParams(needs_layout_passes=False),
        name="sc_combine")
    def run(y_hbm, dest_hbm, gate_hbm, out_hbm, idx_v, gate_v, rows_v, out_v, in_sems, out_sems):
        wid = lax.axis_index("s") * n_cores + lax.axis_index("c")
        first = wid * per_worker
        pltpu.sync_copy(dest_hbm.at[pl.ds(first_win + first, per_worker)], idx_v)
        pltpu.sync_copy(gate_hbm.at[pl.ds(first_win + first, per_worker)], gate_v)

        def gather(it, slot):
            return [pltpu.make_async_copy(y_hbm.at[idx_v.at[it, kk]], rows_v.at[slot, kk], in_sems.at[slot])
                    for kk in range(TOP_K)]

        def put(it, slot):
            return pltpu.make_async_copy(out_v.at[slot], out_hbm.at[pl.ds((first + it) * w, w)],
                                         out_sems.at[slot])

        def step(it, slot):
            @pl.when(it + depth - 1 < per_worker)
            def _():
                for cp in gather(it + depth - 1, (slot + depth - 1) % depth):
                    cp.start()

            for cp in gather(it, slot):
                cp.wait()

            slot2 = slot % 2

            @pl.when(it >= 2)
            def _():
                put(it - 2, slot2).wait()

            @plsc.parallel_loop(0, w)
            def _(j):
                own = pl.ds(pl.multiple_of(j * SC_LANES, SC_LANES), SC_LANES)
                gates = [plsc.bitcast(gate_v[it, kk, own], BF16) for kk in range(TOP_K)]
                for t in range(ROW_TILES):
                    for c in range(LANES // SC_LANES):
                        lanes = pl.ds(c * SC_LANES, SC_LANES)
                        v = [plsc.bitcast(rows_v[slot, kk, j, t, lanes], BF16) * gates[kk]
                             for kk in range(TOP_K)]
                        out_v[slot2, j, t, lanes] = plsc.bitcast((v[0] + v[1]) + (v[2] + v[3]), U32)

            put(it, slot2).start()

        for ahead in range(depth - 1):
            for cp in gather(ahead, ahead):
                cp.start()

        @pl.loop(0, per_worker // depth)
        def _(p):
            for slot in range(depth):
                step(depth * p + slot, slot)

        put(per_worker - 2, 0).wait()
        put(per_worker - 1, 1).wait()

    return run(y_rows, dest_win, gate_words)


def _expert_kernel(be_ref, nu_ref, nx_ref, nv_ref, x_ref, wgu_hbm, bgu_ref, wd_hbm, bd_ref, y_ref,
                   wgu32, wd32, wgu16, wd16, sems):
    i = pl.program_id(0)
    e = be_ref[i]
    prev = be_ref[jnp.maximum(i - 1, 0)]

    def weight_copies(ex):
        return (pltpu.make_async_copy(wgu_hbm.at[ex], wgu32, sems.at[0]),
                pltpu.make_async_copy(wd_hbm.at[ex], wd32, sems.at[1]))

    @pl.when(i == 0)
    def _():
        for cp in weight_copies(e):
            cp.start()

    @pl.when((i == 0) | (e != prev))
    def _():
        for cp in weight_copies(e):
            cp.wait()
        rows = 128

        def cast(r, carry):
            sl = pl.ds(pl.multiple_of(r * rows, rows), rows)
            wgu16[sl, :] = wgu32[sl, :].astype(BF16)
            wd16[sl, :] = wd32[sl, :].astype(BF16)
            return carry

        lax.fori_loop(0, D_MODEL // rows, cast, 0)
        nxt = nx_ref[i]

        @pl.when(nxt >= 0)
        def _():
            for cp in weight_copies(nxt):
                cp.start()

    def run_rows(rows):
        x = _load_row_tiles(x_ref, rows).astype(BF16)
        bgu = bgu_ref[0]
        glu = _dot(x, wgu16[:, 0:D_FF]) + bgu[:, 0:D_FF]
        lin = _dot(x, wgu16[:, D_FF:]) + bgu[:, D_FF:]
        glu = jnp.minimum(glu, SWIGLU_LIMIT)
        lin = jnp.clip(lin, -SWIGLU_LIMIT, SWIGLU_LIMIT)
        act = glu * _sigmoid(SWIGLU_ALPHA * glu) * (lin + 1.0)
        y = _dot(act.astype(BF16), wd16[...]) + bd_ref[0]
        _store_row_tiles(y_ref, y.astype(BF16).astype(F32))

    valid = nv_ref[i]
    for rows in range(MOE_ROWS_STEP, MOE_BLOCK + 1, MOE_ROWS_STEP):
        pl.when((valid > rows - MOE_ROWS_STEP) & (valid <= rows))(functools.partial(run_rows, rows))


def _experts(blk_e, n_used, next_e, blk_valid, xs2, w_gu, b_gu3, w_down, b_down3):
    row_blk = pl.BlockSpec((MOE_BLOCK * ROW_TILES, LANES),
                           lambda i, be, nu, nx, nv: (jnp.minimum(i, nu[0] - 1), 0))
    grid_spec = pltpu.PrefetchScalarGridSpec(
        num_scalar_prefetch=4,
        grid=(N_BLOCKS,),
        in_specs=[row_blk,
                  pl.BlockSpec(memory_space=pl.ANY),
                  pl.BlockSpec((1, 1, 2 * D_FF), lambda i, be, nu, nx, nv: (be[i], 0, 0)),
                  pl.BlockSpec(memory_space=pl.ANY),
                  pl.BlockSpec((1, 1, D_MODEL), lambda i, be, nu, nx, nv: (be[i], 0, 0))],
        out_specs=row_blk,
        scratch_shapes=[pltpu.VMEM((D_MODEL, 2 * D_FF), F32),
                        pltpu.VMEM((D_FF, D_MODEL), F32),
                        pltpu.VMEM((D_MODEL, 2 * D_FF), BF16),
                        pltpu.VMEM((D_FF, D_MODEL), BF16),
                        pltpu.SemaphoreType.DMA((2,))],
    )
    return pl.pallas_call(
        _expert_kernel,
        grid_spec=grid_spec,
        out_shape=jax.ShapeDtypeStruct((N_SLOTS * ROW_TILES, LANES), U32),
        compiler_params=pltpu.CompilerParams(dimension_semantics=("arbitrary",),
                                             vmem_limit_bytes=VMEM_LIMIT),
        name="experts",
    )(blk_e, n_used, next_e, blk_valid, xs2, w_gu, b_gu3, w_down, b_down3)


def _final_kernel(ysum_ref, x1_ref, mod_ref, gpost_ref, *maybe_alias_and_out):
    o_ref = maybe_alias_and_out[-1]
    y = _load_row_tiles(ysum_ref, TOK_TILE)
    gate_f = mod_ref[0][5:6]
    o_ref[...] = x1_ref[...] + _rms(y) * (gate_f * gpost_ref[...])


def _final(first_tile, tiles, ysum, x1, mod3, g_post, out_so_far):
    tok = lambda w: pl.BlockSpec((TOK_TILE, w), lambda i: (first_tile + i, 0))
    in_specs = [pl.BlockSpec((TOK_TILE * ROW_TILES, LANES), lambda i: (i, 0)),
                tok(D_MODEL),
                pl.BlockSpec((1, 6, D_MODEL), lambda i: (_tile_batch(first_tile + i), 0, 0)),
                pl.BlockSpec((1, D_MODEL), lambda i: (0, 0))]
    args = [ysum, x1, mod3, g_post]
    aliases = {}
    if out_so_far is not None:
        in_specs.append(pl.BlockSpec(memory_space=pl.ANY))
        args.append(out_so_far)
        aliases = {len(args) - 1: 0}
    return pl.pallas_call(
        _final_kernel,
        grid=(tiles,),
        in_specs=in_specs,
        out_specs=tok(D_MODEL),
        out_shape=jax.ShapeDtypeStruct((N_TOK, D_MODEL), F32),
        input_output_aliases=aliases,
        compiler_params=pltpu.CompilerParams(vmem_limit_bytes=VMEM_LIMIT),
        name="combine_final",
    )(*args)


def _slot_kernel(idx_ref, rank_ref, start_ref, dest_ref):
    tn = idx_ref.shape[1]
    experts = lax.broadcasted_iota(I32, (N_EXPERTS, tn), 0)
    starts = start_ref[...].astype(F32)
    rows = [jnp.sum(jnp.where(experts == idx_ref[kk:kk + 1, :], starts, 0.0), axis=0, keepdims=True)
            for kk in range(TOP_K)]
    dest_ref[...] = jnp.concatenate(rows, axis=0).astype(I32) + rank_ref[...]


def _slots(idx, rank, pstart):
    tn = N_TOK // 2
    lanes_tok = pl.BlockSpec((TOP_K, tn), lambda i: (0, i))
    return pl.pallas_call(
        _slot_kernel,
        grid=(N_TOK // tn,),
        in_specs=[lanes_tok, lanes_tok, pl.BlockSpec((N_EXPERTS, 1), lambda i: (0, 0))],
        out_specs=lanes_tok,
        out_shape=jax.ShapeDtypeStruct((TOP_K, N_TOK), I32),
        name="moe_slots",
    )(idx, rank, pstart.reshape(N_EXPERTS, 1))


def _routing_tables(idx, rank, counts):
    counts = counts.reshape(N_EXPERTS)
    experts = jnp.arange(N_EXPERTS, dtype=I32)
    padded = ((counts + MOE_BLOCK - 1) // MOE_BLOCK) * MOE_BLOCK
    pend = jnp.cumsum(padded)
    pstart = pend - padded
    dest = _slots(idx, rank, pstart)
    dest_win = dest.reshape(TOP_K, N_TOK // SC_WINDOW, SC_WINDOW).transpose(1, 0, 2)
    n_used = (pend[-1] // MOE_BLOCK).astype(I32).reshape(1)
    blk_start = jnp.arange(N_BLOCKS, dtype=I32) * MOE_BLOCK
    blk_e = jnp.sum(blk_start[:, None] >= pend[None, :], axis=1).astype(I32)
    last_e = jnp.max(jnp.where(counts > 0, experts, 0))
    blk_e = jnp.minimum(blk_e, last_e)
    later = (experts[None, :] > experts[:, None]) & (counts[None, :] > 0)
    next_nonempty = jnp.min(jnp.where(later, experts[None, :], N_EXPERTS), axis=1)
    next_nonempty = jnp.where(next_nonempty == N_EXPERTS, -1, next_nonempty).astype(I32)
    of_block = lambda table: jnp.sum(jnp.where(blk_e[:, None] == experts, table, 0), axis=1).astype(I32)
    next_e = of_block(next_nonempty)
    blk_valid = jnp.clip(of_block(counts) - (blk_start - of_block(pstart)), 0, MOE_BLOCK)
    blk_valid = jnp.where(blk_start < pend[-1], blk_valid, 0).astype(I32)
    return dest, dest_win, blk_e, n_used, next_e, blk_valid


def kernel(x, c, w_ada, b_ada, g_pre_mix, g_post_mix, w_in, attn_norm_w, hgrn_lb, hgrn_norm_w,
           w_out, g_pre_ffn, g_post_ffn, w_router, b_router, w_gu, b_gu, w_down, b_down):
    c_pad = jnp.pad(c, ((0, SUBLANES - BATCH), (0, 0)))
    mod = _ada_mod(c_pad, w_ada[0], b_ada)
    mod3 = mod[:BATCH].reshape(BATCH, 6, D_MODEL)

    x2 = x.reshape(N_TOK, D_MODEL)
    q4, k4, v4, q16, k16, v16, qr, fr, ir, gr = _inproj(x2, mod3, g_pre_mix, w_in[0])
    nat = lambda t: t.reshape(BATCH, SEQ, D_REC)
    ya = _attention(q4, k4, v4, q16, k16, v16, attn_norm_w)
    yr = _hgrn(nat(qr), nat(fr), nat(ir), nat(gr), hgrn_lb, hgrn_norm_w)

    x1, h2, idx, gates, rank, counts = _mid(
        ya.reshape(N_TOK, D_ATTN), yr.reshape(N_TOK, D_REC), x2, mod3, g_post_mix, g_pre_ffn,
        w_out[0], w_router[0], b_router)

    dest, dest_win, blk_e, n_used, next_e, blk_valid = _routing_tables(idx, rank, counts)
    xs = _sc_dispatch(h2.reshape(N_TOK, ROW_TILES, LANES), dest_win)
    ys = _experts(blk_e, n_used, next_e, blk_valid, xs.reshape(N_SLOTS * ROW_TILES, LANES),
                  w_gu[0], b_gu[0].reshape(N_EXPERTS, 1, 2 * D_FF),
                  w_down[0], b_down[0].reshape(N_EXPERTS, 1, D_MODEL))
    ys3 = ys.reshape(N_SLOTS, ROW_TILES, LANES)
    by_window = lambda t: t.reshape(TOP_K, N_TOK // SC_SUM_WINDOW, SC_SUM_WINDOW).transpose(1, 0, 2)
    sum_win = by_window(dest)
    gate_bits = lax.bitcast_convert_type(gates.astype(BF16), jnp.uint16).astype(U32)
    gate_words = jnp.broadcast_to(by_window(gate_bits | (gate_bits << 16))[..., None],
                                  sum_win.shape + (SC_LANES,)).reshape(sum_win.shape[:2] + (-1,))
    out, done = None, 0
    for units in FINISH_PIECES:
        n_tok = units * FINISH_UNIT
        ysum = _sc_combine(ys3, sum_win, gate_words, done // SC_SUM_WINDOW, n_tok // SC_SUM_WINDOW)
        out = _final(done // TOK_TILE, n_tok // TOK_TILE, ysum.reshape(n_tok * ROW_TILES, LANES),
                     x1, mod3, g_post_ffn, out)
        done += n_tok
    return out.reshape(BATCH, SEQ, D_MODEL)
```

```python
import functools
import math

import jax
import jax.numpy as jnp
from jax import lax
from jax.experimental import pallas as pl
from jax.experimental.pallas import tpu as pltpu
from jax.experimental.pallas import tpu_sc as plsc

F32 = jnp.float32
BF16 = jnp.bfloat16
I32 = jnp.int32
U32 = jnp.uint32

D_MODEL = 1024
BATCH = 4
SEQ = 4096
N_TOK = BATCH * SEQ
D_ATTN = 512
HEAD_DIM_A = 64
ATT_BLOCK = 128
DIL_MID = 4
DIL_MAX = 16
SUB_MID = SEQ // DIL_MID
SUB_MAX = SEQ // DIL_MAX
D_REC = 512
HGRN_CHUNK = 32
HGRN_SUPER = 256
N_EXPERTS = 32
TOP_K = 4
D_FF = 1024
SWIGLU_LIMIT = 7.0
SWIGLU_ALPHA = 1.702
EPS = 1e-6
NEG_BIG = -1e30
Q_SCALE = HEAD_DIM_A ** -0.5 * math.log2(math.e)

LANES = 128
SUBLANES = 8
ROW_TILES = D_MODEL // 2 // LANES

TOK_TILE = 512
MOE_BLOCK = 1024
MOE_ROWS_STEP = 128
N_SLOTS = N_TOK * TOP_K + N_EXPERTS * MOE_BLOCK
N_BLOCKS = N_SLOTS // MOE_BLOCK
SC_WINDOW = 128
SC_SUM_WINDOW = 8
SC_SUM_DEPTH = 4
FINISH_UNIT = 1024
FINISH_PIECES = (4, 4, 4, 4)
SC_LANES = 16
VMEM_LIMIT = 56 * 1024 * 1024


def _sigmoid(x):
    return 1.0 / (1.0 + jnp.exp(-x))


def _dot(a, b):
    return jnp.dot(a, b, preferred_element_type=F32)


def _dot_nt(a, b):
    return lax.dot_general(a, b, (((1,), (1,)), ((), ())), preferred_element_type=F32)


def _split_bf16(x):
    hi = x.astype(BF16)
    return hi, (x - hi.astype(F32)).astype(BF16)


def _rms(x):
    return x * lax.rsqrt(jnp.mean(x * x, axis=-1, keepdims=True) + EPS)


def _ada_kernel(c_ref, w_ref, b_ref, o_ref):
    c = c_ref[...]
    cond = c * _sigmoid(c)
    c_hi, c_lo = _split_bf16(cond)
    w_hi, w_lo = _split_bf16(w_ref[...])
    o_ref[...] = _dot(c_hi, w_hi) + _dot(c_lo, w_hi) + _dot(c_hi, w_lo) + b_ref[...]


def _ada_mod(c_pad, w_ada, b_ada):
    n = w_ada.shape[1]
    tn = 1536
    return pl.pallas_call(
        _ada_kernel,
        grid=(n // tn,),
        in_specs=[pl.BlockSpec((SUBLANES, D_MODEL), lambda j: (0, 0)),
                  pl.BlockSpec((D_MODEL, tn), lambda j: (0, j)),
                  pl.BlockSpec((1, tn), lambda j: (0, j))],
        out_specs=pl.BlockSpec((SUBLANES, tn), lambda j: (0, j)),
        out_shape=jax.ShapeDtypeStruct((SUBLANES, n), F32),
        compiler_params=pltpu.CompilerParams(vmem_limit_bytes=VMEM_LIMIT),
        name="ada_mod",
    )(c_pad, w_ada, b_ada)


def _cast_weights_once(w_ref, w16):
    rows = 128

    @pl.when(pl.program_id(0) == 0)
    def _():
        def cast(r, carry):
            sl = pl.ds(pl.multiple_of(r * rows, rows), rows)
            w16[sl, :] = w_ref[sl, :].astype(BF16)
            return carry

        lax.fori_loop(0, w_ref.shape[0] // rows, cast, 0)


def _resident(shape):
    return pl.BlockSpec(shape, lambda i: (0,) * len(shape), pipeline_mode=pl.Buffered(1))


def _inproj_kernel(x_ref, mod_ref, g_ref, w_ref, q4, k4, v4, q16, k16, v16, qr, fr, ir, gr,
                   w16, stage_nat, stage_mid):
    _cast_weights_once(w_ref, w16)
    mod = mod_ref[0]
    shift, scale = mod[0:1], mod[1:2]
    h = _rms(x_ref[...]) * (g_ref[...] * (1.0 + scale)) + shift
    hb = h.astype(BF16)
    slabs = D_ATTN // LANES
    rows_mid = TOK_TILE // DIL_MID
    rows_max = TOK_TILE // DIL_MAX

    def proj(j):
        return _dot(hb, w16[:, j * D_ATTN:(j + 1) * D_ATTN])

    for j, (o_mid, o_max) in enumerate(((q4, q16), (k4, k16), (v4, v16))):
        r = proj(j)
        if j == 0:
            r = r * Q_SCALE
        for cs in range(slabs):
            stage_nat[cs] = r[:, cs * LANES:(cs + 1) * LANES]
        for cs in range(slabs):
            lanes = slice(cs * LANES, (cs + 1) * LANES)
            for sub in range(DIL_MID):
                piece = stage_nat[cs, pl.ds(sub, rows_mid, stride=DIL_MID), :]
                o_mid[sub, :, lanes] = piece.astype(BF16)
                stage_mid[cs, sub] = piece
            for sub in range(DIL_MAX):
                piece = stage_mid[cs, sub % DIL_MID, pl.ds(sub // DIL_MID, rows_max, stride=DIL_MID), :]
                o_max[sub, :, lanes] = piece.astype(BF16)
    for j, o_ref in enumerate((qr, fr, ir, gr)):
        o_ref[...] = proj(3 + j)


def _inproj(x2, mod3, g_pre, w_in):
    tiles_per_seq = SEQ // TOK_TILE
    rows_mid = TOK_TILE // DIL_MID
    rows_max = TOK_TILE // DIL_MAX
    mid = pl.BlockSpec((None, DIL_MID, rows_mid, D_ATTN),
                       lambda i: (i // tiles_per_seq, 0, i % tiles_per_seq, 0))
    mx = pl.BlockSpec((None, DIL_MAX, rows_max, D_ATTN),
                      lambda i: (i // tiles_per_seq, 0, i % tiles_per_seq, 0))
    nat = pl.BlockSpec((TOK_TILE, D_REC), lambda i: (i, 0))
    mid_shape = jax.ShapeDtypeStruct((BATCH, DIL_MID, SUB_MID, D_ATTN), BF16)
    mx_shape = jax.ShapeDtypeStruct((BATCH, DIL_MAX, SUB_MAX, D_ATTN), BF16)
    nat_shape = jax.ShapeDtypeStruct((N_TOK, D_REC), F32)
    return pl.pallas_call(
        _inproj_kernel,
        grid=(N_TOK // TOK_TILE,),
        in_specs=[pl.BlockSpec((TOK_TILE, D_MODEL), lambda i: (i, 0)),
                  pl.BlockSpec((1, 6, D_MODEL), lambda i: (i // tiles_per_seq, 0, 0)),
                  pl.BlockSpec((1, D_MODEL), lambda i: (0, 0)),
                  _resident(w_in.shape)],
        out_specs=[mid, mid, mid, mx, mx, mx, nat, nat, nat, nat],
        out_shape=[mid_shape] * 3 + [mx_shape] * 3 + [nat_shape] * 4,
        scratch_shapes=[pltpu.VMEM(w_in.shape, BF16),
                        pltpu.VMEM((D_ATTN // LANES, TOK_TILE, LANES), F32),
                        pltpu.VMEM((D_ATTN // LANES, DIL_MID, rows_mid, LANES), F32)],
        compiler_params=pltpu.CompilerParams(dimension_semantics=("arbitrary",),
                                             vmem_limit_bytes=VMEM_LIMIT),
        name="inproj",
    )(x2, mod3, g_pre, w_in)


def _attn_kernel(q4_ref, k4_ref, v4_ref, q16_ref, k16_ref, v16_ref, nw_ref, o_ref,
                 sel_scr, keep_scr, o_scr, m_scr, l_scr):
    bw = ATT_BLOCK
    lane = lax.broadcasted_iota(I32, (bw, LANES), 1)
    head0 = lane < HEAD_DIM_A
    head_masks = (jnp.where(head0, 1.0, 0.0).astype(BF16), jnp.where(head0, 0.0, 1.0).astype(BF16))

    rr = lax.broadcasted_iota(I32, (bw, bw), 0)
    cc = lax.broadcasted_iota(I32, (bw, bw), 1)
    piece = bw // DIL_MID
    rp = DIL_MID * (rr % piece) + rr // piece
    cp = DIL_MID * (cc % piece) + cc // piece
    for var, (r, c) in enumerate(((rr, cc), (rr, cc), (rp, cp), (rp, cp))):
        first = var % 2 == 1
        left = (c < r) if first else (c >= r)
        right = jnp.zeros_like(left) if first else jnp.logical_not(left)
        sel_scr[var] = jnp.where(left, 1.0, 0.0)
        keep_scr[var, 0] = jnp.where(left, 1.0, 0.0).astype(BF16)
        keep_scr[var, 1] = jnp.where(right, 1.0, 0.0).astype(BF16)

    def score_matmuls(loaded):
        return [_dot_nt(q * hm, k) for (q, k, _, _, _) in loaded for hm in head_masks]

    def finish_group(scores, loaded):
        probs = []
        for n, s in enumerate(scores):
            _, _, _, var, right_bias = loaded[n // 2]
            folded = jnp.where(sel_scr[var] > 0.5, s[:, 0:bw], s[:, bw:] + right_bias)
            m = jnp.max(folded, axis=-1, keepdims=True)
            p = jnp.exp2(folded - m)
            pb = p.astype(BF16)
            spread = jnp.concatenate([pb * keep_scr[var, 0], pb * keep_scr[var, 1]], axis=1)
            probs.append((spread, jnp.sum(p, axis=-1, keepdims=True), m))
        pvs = [_dot(p, loaded[n // 2][2]) for n, (p, _, _) in enumerate(probs)]
        results = []
        for u in range(len(loaded)):
            (_, l0, m0), (_, l1, m1) = probs[2 * u], probs[2 * u + 1]
            results.append((jnp.where(head0, pvs[2 * u], pvs[2 * u + 1]),
                            jnp.where(head0, jnp.broadcast_to(m0, (bw, LANES)),
                                      jnp.broadcast_to(m1, (bw, LANES))),
                            jnp.where(head0, jnp.broadcast_to(l0, (bw, LANES)),
                                      jnp.broadcast_to(l1, (bw, LANES)))))
        return results

    group = 8

    def first_block_bias(blk):
        return jnp.where(blk == 0, NEG_BIG, 0.0)

    def load_d1(g):
        loaded, starts = [], []
        for i in range(group):
            blk = g * group + i
            qs = pl.multiple_of(blk * piece, piece)
            ks = pl.multiple_of(jnp.maximum(blk - 1, 0) * piece, piece)
            q = jnp.concatenate([q4_ref[r, pl.ds(qs, piece), :] for r in range(DIL_MID)], axis=0)
            k = jnp.concatenate([k4_ref[r, pl.ds(ks + half * piece, piece), :]
                                 for half in range(2) for r in range(DIL_MID)], axis=0)
            v = jnp.concatenate([v4_ref[r, pl.ds(ks + half * piece, piece), :]
                                 for half in range(2) for r in range(DIL_MID)], axis=0)
            loaded.append((q, k, v, jnp.where(blk == 0, 3, 2), first_block_bias(blk)))
            starts.append(qs)

        def store(results):
            for qs, parts in zip(starts, results):
                for r in range(DIL_MID):
                    for scr, val in zip((o_scr, m_scr, l_scr), parts):
                        scr[0, r, pl.ds(qs, piece), :] = val[r * piece:(r + 1) * piece]

        return loaded, store

    def load_d4(g):
        loaded, dsts = [], []
        for i in range(group):
            r, blk = i % DIL_MID, g * (group // DIL_MID) + i // DIL_MID
            qs = pl.multiple_of(blk * bw, bw)
            ks = pl.multiple_of(jnp.maximum(blk - 1, 0) * bw, bw)
            loaded.append((q4_ref[r, pl.ds(qs, bw), :], k4_ref[r, pl.ds(ks, 2 * bw), :],
                           v4_ref[r, pl.ds(ks, 2 * bw), :], jnp.where(blk == 0, 1, 0),
                           first_block_bias(blk)))
            dsts.append((r, qs))

        def store(results):
            for (r, qs), parts in zip(dsts, results):
                for scr, val in zip((o_scr, m_scr, l_scr), parts):
                    scr[1, r, pl.ds(qs, bw), :] = val

        return loaded, store

    def load_d16(g):
        loaded, dsts = [], []
        for i in range(group):
            r, blk = g * (group // 2) + i // 2, i % 2
            loaded.append((q16_ref[r, blk * bw:(blk + 1) * bw, :], k16_ref[r], v16_ref[r],
                           1 - blk, NEG_BIG if blk == 0 else 0.0))
            dsts.append((r % DIL_MID, pl.ds(blk * (bw * DIL_MID) + r // DIL_MID, bw, stride=DIL_MID)))

        def store(results):
            for (sub, dst), parts in zip(dsts, results):
                for scr, val in zip((o_scr, m_scr, l_scr), parts):
                    scr[2, sub, dst, :] = val

        return loaded, store

    def run_group(load):
        def step(g, carry):
            loaded, store = load(g)
            store(finish_group(score_matmuls(loaded), loaded))
            return carry
        return step

    n_groups = SEQ // bw // group
    for load in (load_d1, load_d4, load_d16):
        lax.fori_loop(0, n_groups, run_group(load), 0)

    rows = 256
    hi = lax.broadcasted_iota(I32, (LANES, LANES), 0) // HEAD_DIM_A
    hj = lax.broadcasted_iota(I32, (LANES, LANES), 1) // HEAD_DIM_A
    head_sum = jnp.where(hi == hj, 1.0, 0.0).astype(BF16)
    nw = nw_ref[...]

    def head_sums(x):
        x_hi, x_lo = _split_bf16(x)
        return _dot(x_hi, head_sum) + _dot(x_lo, head_sum)

    def merge(t, carry):
        r = t // (SUB_MID // rows)
        start = pl.multiple_of((t % (SUB_MID // rows)) * rows, rows)
        sl = pl.ds(start, rows)
        own_score = head_sums(q4_ref[r, sl, :].astype(F32) * k4_ref[r, sl, :].astype(F32))
        ms = [m_scr[n, r, sl, :] for n in range(3)]
        mx = jnp.maximum(jnp.maximum(jnp.maximum(ms[0], ms[1]), ms[2]), own_score)
        ws = [jnp.exp2(m - mx) for m in ms]
        w_own = float(len(ms)) * jnp.exp2(own_score - mx)
        num = (ws[0] * o_scr[0, r, sl, :] + ws[1] * o_scr[1, r, sl, :] + ws[2] * o_scr[2, r, sl, :]
               + w_own * v4_ref[r, sl, :].astype(F32))
        den = ws[0] * l_scr[0, r, sl, :] + ws[1] * l_scr[1, r, sl, :] + ws[2] * l_scr[2, r, sl, :] + w_own
        o = num / den
        mean_sq = head_sums(o * o) * (1.0 / HEAD_DIM_A)
        o_ref[pl.ds(start * DIL_MID + r, rows, stride=DIL_MID), :] = o * lax.rsqrt(mean_sq + EPS) * nw
        return carry

    lax.fori_loop(0, SEQ // rows, merge, 0, unroll=8)


def _attention(q4, k4, v4, q16, k16, v16, attn_norm_w):
    hp = D_ATTN // LANES
    mid = pl.BlockSpec((None, DIL_MID, SUB_MID, LANES), lambda b, h: (b, 0, 0, h))
    mx = pl.BlockSpec((None, DIL_MAX, SUB_MAX, LANES), lambda b, h: (b, 0, 0, h))
    scr = pltpu.VMEM((3, DIL_MID, SUB_MID, LANES), F32)
    return pl.pallas_call(
        _attn_kernel,
        grid=(BATCH, hp),
        in_specs=[mid, mid, mid, mx, mx, mx, pl.BlockSpec((1, LANES), lambda b, h: (0, h))],
        out_specs=pl.BlockSpec((None, SEQ, LANES), lambda b, h: (b, 0, h)),
        out_shape=jax.ShapeDtypeStruct((BATCH, SEQ, D_ATTN), F32),
        scratch_shapes=[pltpu.VMEM((4, ATT_BLOCK, ATT_BLOCK), F32),
                        pltpu.VMEM((4, 2, ATT_BLOCK, ATT_BLOCK), BF16), scr, scr, scr],
        compiler_params=pltpu.CompilerParams(vmem_limit_bytes=VMEM_LIMIT),
        name="dilated_attn",
    )(q4, k4, v4, q16, k16, v16, attn_norm_w)


def _hgrn_kernel(qr_ref, fr_ref, ir_ref, gr_ref, lb_ref, nw_ref, o_ref,
                 qe_scr, oi_scr, delta_scr, dec_scr, st_scr):
    sup, c = HGRN_SUPER, HGRN_CHUNK
    nch = sup // c
    n_sup = SEQ // sup
    lbp = lb_ref[...]
    lmx = jnp.max(lbp, axis=0, keepdims=True)
    ex = jnp.exp(lbp - lmx)
    lb = ex[0:1] / (ex[0:1] + ex[1:2])
    nw = nw_ref[...]

    ri = lax.broadcasted_iota(I32, (sup, sup), 0)
    ci = lax.broadcasted_iota(I32, (sup, sup), 1)
    same_chunk = (ri // c) == (ci // c)
    causal = same_chunk & (ci <= ri)
    cum_op = jnp.where(causal, 1.0, 0.0).astype(BF16)

    def chunk_rows(kd, ch):
        parts = []
        if ch > 0:
            parts.append(jnp.zeros((ch * c, LANES), BF16))
        parts.append(kd[ch * c:(ch + 1) * c])
        if ch < nch - 1:
            parts.append(jnp.zeros(((nch - 1 - ch) * c, LANES), BF16))
        return jnp.concatenate(parts, axis=0)

    group = 4

    def independent(g, carry):
        ts = [g * group + i for i in range(group)]
        sls = [pl.ds(pl.multiple_of(t * sup, sup), sup) for t in ts]
        pre = []
        for sl in sls:
            f = lb + (1.0 - lb) * _sigmoid(fr_ref[sl, :])
            logf_hi, logf_lo = _split_bf16(jnp.log(f))
            pre.append((1.0 - f, _dot(cum_op, logf_hi) + _dot(cum_op, logf_lo)))
        mid = []
        for t, sl, (kk, b) in zip(ts, sls, pre):
            b_last = jnp.concatenate(
                [jnp.broadcast_to(b[(ch + 1) * c - 1:(ch + 1) * c], (c, LANES)) for ch in range(nch)],
                axis=0)
            q = qr_ref[sl, :]
            qeb = (q * _sigmoid(q) * jnp.exp(b)).astype(BF16)
            ke = (kk * jnp.exp(-b)).astype(BF16)
            kd = (kk * jnp.exp(b_last - b)).astype(BF16)
            qe_scr[sl, :] = qeb
            dec_rows = jnp.concatenate([b_last[ch * c:ch * c + 1] for ch in range(nch)], axis=0)
            dec_scr[pl.ds(pl.multiple_of(t * nch, nch), nch), :] = jnp.exp(dec_rows)
            v = ir_ref[sl, :]
            vt = v.T.astype(BF16)
            scores = _dot_nt(qeb, ke)
            for pair in range(nch // 2):
                rhs = jnp.concatenate([chunk_rows(kd, 2 * pair), chunk_rows(kd, 2 * pair + 1)], axis=1)
                d2 = _dot(vt, rhs)
                delta_scr[t * nch + 2 * pair] = d2[:, 0:LANES]
                delta_scr[t * nch + 2 * pair + 1] = d2[:, LANES:]
            mid.append((scores, v.astype(BF16)))
        for sl, (scores, vb) in zip(sls, mid):
            a = jnp.where(causal, scores, 0.0)
            oi_scr[sl, :] = _dot(a.astype(BF16), vb)
        return carry

    lax.fori_loop(0, n_sup // group, independent, 0)

    def recur(ch, st):
        st_scr[ch] = st.astype(BF16)
        return st * dec_scr[pl.ds(ch, 1), :] + delta_scr[ch]

    lax.fori_loop(0, SEQ // c, recur, jnp.zeros((LANES, LANES), F32), unroll=8)

    def finish(t, carry):
        sl = pl.ds(pl.multiple_of(t * sup, sup), sup)
        parts = [_dot_nt(qe_scr[pl.ds(pl.multiple_of(t * sup + ch * c, c), c), :], st_scr[t * nch + ch])
                 for ch in range(nch)]
        o = oi_scr[sl, :] + jnp.concatenate(parts, axis=0)
        g = gr_ref[sl, :]
        o_ref[sl, :] = (_rms(o) * nw * (g * _sigmoid(g))).astype(o_ref.dtype)
        return carry

    lax.fori_loop(0, n_sup, finish, 0, unroll=8)


def _hgrn(qr, fr, ir, gr, hgrn_lb, hgrn_norm_w):
    nh = D_REC // LANES
    n_chunks = SEQ // HGRN_CHUNK
    blk = pl.BlockSpec((None, SEQ, LANES), lambda b, h: (b, 0, h))
    return pl.pallas_call(
        _hgrn_kernel,
        grid=(BATCH, nh),
        in_specs=[blk, blk, blk, blk,
                  pl.BlockSpec((2, LANES), lambda b, h: (0, h)),
                  pl.BlockSpec((1, LANES), lambda b, h: (0, h))],
        out_specs=blk,
        out_shape=jax.ShapeDtypeStruct((BATCH, SEQ, D_REC), BF16),
        scratch_shapes=[pltpu.VMEM((SEQ, LANES), BF16),
                        pltpu.VMEM((SEQ, LANES), F32),
                        pltpu.VMEM((n_chunks, LANES, LANES), F32),
                        pltpu.VMEM((n_chunks, LANES), F32),
                        pltpu.VMEM((n_chunks, LANES, LANES), BF16)],
        compiler_params=pltpu.CompilerParams(vmem_limit_bytes=VMEM_LIMIT),
        name="hgrn2",
    )(qr, fr, ir, gr, hgrn_lb, hgrn_norm_w)


def _tile_batch(i):
    return i // (SEQ // TOK_TILE)


def _store_row_tiles(ref, val):
    rows, half = val.shape[0], D_MODEL // 2
    lo = lax.bitcast_convert_type(val[:, :half], U32)
    hi = lax.bitcast_convert_type(val[:, half:], U32)
    words = lax.shift_right_logical(lo, jnp.uint32(16)) | (hi & jnp.uint32(0xFFFF0000))
    for j in range(ROW_TILES):
        ref[pl.ds(j, rows, stride=ROW_TILES), :] = words[:, j * LANES:(j + 1) * LANES]


def _load_row_tiles(ref, rows):
    words = jnp.concatenate(
        [ref[pl.ds(j, rows, stride=ROW_TILES), :] for j in range(ROW_TILES)], axis=1)
    lo = lax.bitcast_convert_type(lax.shift_left(words, jnp.uint32(16)), F32)
    hi = lax.bitcast_convert_type(words & jnp.uint32(0xFFFF0000), F32)
    return jnp.concatenate([lo, hi], axis=1)


def _mid_kernel(ya_ref, yr_ref, x_ref, mod_ref, gpost_ref, gpre_ref, wo_ref, wr_ref, br_ref,
                x1_ref, h2_ref, idx_ref, gate_ref, rank_ref, cnt_ref, carry_ref, wo16):
    i = pl.program_id(0)
    _cast_weights_once(wo_ref, wo16)

    @pl.when(i == 0)
    def _():
        carry_ref[...] = jnp.zeros_like(carry_ref)

    mod = mod_ref[0]
    gate_m, shift_f, scale_f = mod[2:3], mod[3:4], mod[4:5]
    y = _dot(ya_ref[...].astype(BF16), wo16[0:D_ATTN, :]) + _dot(yr_ref[...], wo16[D_ATTN:, :])
    x1 = x_ref[...] + _rms(y) * (gate_m * gpost_ref[...])
    x1_ref[...] = x1
    h2 = _rms(x1) * (gpre_ref[...] * (1.0 + scale_f)) + shift_f
    h2_hi = h2.astype(BF16)
    h2_rounded = h2_hi.astype(F32)
    _store_row_tiles(h2_ref, h2_rounded)

    tm = h2.shape[0]
    h2_lo = (h2 - h2_rounded).astype(BF16)
    wr_hi, wr_lo = _split_bf16(wr_ref[...])
    parts = jnp.concatenate([_dot(h2_hi, jnp.concatenate([wr_hi, wr_lo], axis=1)),
                             _dot(h2_lo, wr_hi),
                             jnp.broadcast_to(br_ref[...], (tm, N_EXPERTS))], axis=1).T
    work = sum(parts[g * N_EXPERTS:(g + 1) * N_EXPERTS] for g in range(LANES // N_EXPERTS))
    eidx = lax.broadcasted_iota(I32, (N_EXPERTS, tm), 0).astype(F32)
    vals, idxs = [], []
    onehot = jnp.zeros((N_EXPERTS, tm), F32)
    for _ in range(TOP_K):
        m = jnp.max(work, axis=0, keepdims=True)
        sel = jnp.min(jnp.where(work == m, eidx, float(N_EXPERTS)), axis=0, keepdims=True)
        hit = eidx == sel
        work = jnp.where(hit, -jnp.inf, work)
        onehot = jnp.where(hit, 1.0, onehot)
        vals.append(m)
        idxs.append(sel)
    ex = [jnp.exp(vv - vals[0]) for vv in vals]
    inv_den = 1.0 / (ex[0] + ex[1] + ex[2] + ex[3])

    ri = lax.broadcasted_iota(I32, (tm, tm), 0)
    ci = lax.broadcasted_iota(I32, (tm, tm), 1)
    strict_upper = jnp.where(ri < ci, 1.0, 0.0).astype(BF16)
    before = _dot(onehot.astype(BF16), strict_upper) + carry_ref[...]
    ranks = [jnp.sum(jnp.where(eidx == idxs[kk], before, 0.0), axis=0, keepdims=True)
             for kk in range(TOP_K)]
    idx_ref[...] = jnp.concatenate(idxs, axis=0).astype(I32)
    rank_ref[...] = jnp.concatenate(ranks, axis=0).astype(I32)
    gate_ref[...] = jnp.concatenate([e * inv_den for e in ex], axis=0)
    total = carry_ref[...] + jnp.sum(onehot, axis=1, keepdims=True)
    carry_ref[...] = total
    cnt_ref[...] = total.astype(I32)


def _mid(ya, yr, x2, mod3, g_post, g_pre, w_out, w_router, b_router):
    tok = lambda w: pl.BlockSpec((TOK_TILE, w), lambda i: (i, 0))
    const = lambda s: pl.BlockSpec(s, lambda i: (0,) * len(s))
    lanes_tok = pl.BlockSpec((TOP_K, TOK_TILE), lambda i: (0, i))
    return pl.pallas_call(
        _mid_kernel,
        grid=(N_TOK // TOK_TILE,),
        in_specs=[tok(D_ATTN), tok(D_REC), tok(D_MODEL),
                  pl.BlockSpec((1, 6, D_MODEL), lambda i: (_tile_batch(i), 0, 0)),
                  const((1, D_MODEL)), const((1, D_MODEL)),
                  _resident((D_MODEL, D_MODEL)), const((D_MODEL, N_EXPERTS)), const((1, N_EXPERTS))],
        out_specs=[tok(D_MODEL),
                   pl.BlockSpec((TOK_TILE * ROW_TILES, LANES), lambda i: (i, 0)),
                   lanes_tok, lanes_tok, lanes_tok, const((N_EXPERTS, 1))],
        out_shape=[jax.ShapeDtypeStruct((N_TOK, D_MODEL), F32),
                   jax.ShapeDtypeStruct((N_TOK * ROW_TILES, LANES), U32),
                   jax.ShapeDtypeStruct((TOP_K, N_TOK), I32),
                   jax.ShapeDtypeStruct((TOP_K, N_TOK), F32),
                   jax.ShapeDtypeStruct((TOP_K, N_TOK), I32),
                   jax.ShapeDtypeStruct((N_EXPERTS, 1), I32)],
        scratch_shapes=[pltpu.VMEM((N_EXPERTS, 1), F32), pltpu.VMEM((D_MODEL, D_MODEL), BF16)],
        compiler_params=pltpu.CompilerParams(dimension_semantics=("arbitrary",),
                                             vmem_limit_bytes=VMEM_LIMIT),
        name="outproj_router",
    )(ya, yr, x2, mod3, g_post, g_pre, w_out, w_router, b_router)


def _sc_mesh():
    return plsc.VectorSubcoreMesh(core_axis_name="c", subcore_axis_name="s")


def _sc_worker_count():
    info = plsc.get_sparse_core_info()
    return info.num_cores, info.num_cores * info.num_subcores


def _sc_dispatch(h_rows, dest_win):
    n_cores, n_workers = _sc_worker_count()
    n_win = N_TOK // SC_WINDOW
    per_worker = n_win // n_workers

    @functools.partial(
        pl.kernel, mesh=_sc_mesh(),
        out_type=jax.ShapeDtypeStruct((N_SLOTS, ROW_TILES, LANES), U32),
        scratch_types=[pltpu.VMEM((TOP_K, SC_WINDOW), I32),
                       pltpu.VMEM((SC_WINDOW, ROW_TILES, LANES), U32),
                       pltpu.SemaphoreType.DMA],
        name="sc_dispatch")
    def run(h_hbm, dest_hbm, xs_hbm, idx_v, rows_v, sem):
        wid = lax.axis_index("s") * n_cores + lax.axis_index("c")

        @pl.loop(0, per_worker)
        def _(j):
            win = wid * per_worker + j
            pltpu.sync_copy(dest_hbm.at[win], idx_v)
            pltpu.sync_copy(h_hbm.at[pl.ds(win * SC_WINDOW, SC_WINDOW)], rows_v)
            copies = [pltpu.async_copy(rows_v, xs_hbm.at[idx_v.at[kk]], sem)
                      for kk in range(TOP_K)]
            for cp in copies:
                cp.wait()

    return run(h_rows, dest_win)


def _sc_combine(y_rows, dest_win, gate_words, first_win, n_win):
    n_cores, n_workers = _sc_worker_count()
    w = dest_win.shape[2]
    per_worker = n_win // n_workers
    depth = SC_SUM_DEPTH
    assert per_worker * n_workers == n_win and per_worker % depth == 0 and depth % 2 == 0, (n_win, n_workers)

    @functools.partial(
        pl.kernel, mesh=_sc_mesh(),
        out_type=jax.ShapeDtypeStruct((n_win * w, ROW_TILES, LANES), U32),
        scratch_types=[pltpu.VMEM((per_worker, TOP_K, w), I32),
                       pltpu.VMEM((per_worker, TOP_K, w * SC_LANES), U32),
                       pltpu.VMEM((depth, TOP_K, w, ROW_TILES, LANES), U32),
                       pltpu.VMEM((2, w, ROW_TILES, LANES), U32),
                       pltpu.SemaphoreType.DMA((depth,)),
                       pltpu.SemaphoreType.DMA((2,))],
        compiler_params=pltpu.CompilerParams(needs_layout_passes=False),
        name="sc_combine")
    def run(y_hbm, dest_hbm, gate_hbm, out_hbm, idx_v, gate_v, rows_v, out_v, in_sems, out_sems):
        wid = lax.axis_index("s") * n_cores + lax.axis_index("c")
        first = wid * per_worker
        pltpu.sync_copy(dest_hbm.at[pl.ds(first_win + first, per_worker)], idx_v)
        pltpu.sync_copy(gate_hbm.at[pl.ds(first_win + first, per_worker)], gate_v)

        def gather(it, slot):
            return [pltpu.make_async_copy(y_hbm.at[idx_v.at[it, kk]], rows_v.at[slot, kk], in_sems.at[slot])
                    for kk in range(TOP_K)]

        def put(it, slot):
            return pltpu.make_async_copy(out_v.at[slot], out_hbm.at[pl.ds((first + it) * w, w)],
                                         out_sems.at[slot])

        def step(it, slot):
            @pl.when(it + depth - 1 < per_worker)
            def _():
                for cp in gather(it + depth - 1, (slot + depth - 1) % depth):
                    cp.start()

            for cp in gather(it, slot):
                cp.wait()

            slot2 = slot % 2

            @pl.when(it >= 2)
            def _():
                put(it - 2, slot2).wait()

            @plsc.parallel_loop(0, w)
            def _(j):
                own = pl.ds(pl.multiple_of(j * SC_LANES, SC_LANES), SC_LANES)
                gates = [plsc.bitcast(gate_v[it, kk, own], BF16) for kk in range(TOP_K)]
                for t in range(ROW_TILES):
                    for c in range(LANES // SC_LANES):
                        lanes = pl.ds(c * SC_LANES, SC_LANES)
                        v = [plsc.bitcast(rows_v[slot, kk, j, t, lanes], BF16) * gates[kk]
                             for kk in range(TOP_K)]
                        out_v[slot2, j, t, lanes] = plsc.bitcast((v[0] + v[1]) + (v[2] + v[3]), U32)

            put(it, slot2).start()

        for ahead in range(depth - 1):
            for cp in gather(ahead, ahead):
                cp.start()

        @pl.loop(0, per_worker // depth)
        def _(p):
            for slot in range(depth):
                step(depth * p + slot, slot)

        put(per_worker - 2, 0).wait()
        put(per_worker - 1, 1).wait()

    return run(y_rows, dest_win, gate_words)


def _expert_kernel(be_ref, nu_ref, nx_ref, nv_ref, x_ref, wgu_hbm, bgu_ref, wd_hbm, bd_ref, y_ref,
                   wgu32, wd32, wgu16, wd16, sems):
    i = pl.program_id(0)
    e = be_ref[i]
    prev = be_ref[jnp.maximum(i - 1, 0)]

    def weight_copies(ex):
        return (pltpu.make_async_copy(wgu_hbm.at[ex], wgu32, sems.at[0]),
                pltpu.make_async_copy(wd_hbm.at[ex], wd32, sems.at[1]))

    @pl.when(i == 0)
    def _():
        for cp in weight_copies(e):
            cp.start(priority=1)

    @pl.when((i == 0) | (e != prev))
    def _():
        for cp in weight_copies(e):
            cp.wait()
        rows = 128

        def cast(r, carry):
            sl = pl.ds(pl.multiple_of(r * rows, rows), rows)
            wgu16[sl, :] = wgu32[sl, :].astype(BF16)
            wd16[sl, :] = wd32[sl, :].astype(BF16)
            return carry

        lax.fori_loop(0, D_MODEL // rows, cast, 0)
        nxt = nx_ref[i]

        @pl.when(nxt >= 0)
        def _():
            for cp in weight_copies(nxt):
                cp.start(priority=1)

    def run_rows(rows):
        x = _load_row_tiles(x_ref, rows).astype(BF16)
        bgu = bgu_ref[0]
        glu = _dot(x, wgu16[:, 0:D_FF]) + bgu[:, 0:D_FF]
        lin = _dot(x, wgu16[:, D_FF:]) + bgu[:, D_FF:]
        glu = jnp.minimum(glu, SWIGLU_LIMIT)
        lin = jnp.clip(lin, -SWIGLU_LIMIT, SWIGLU_LIMIT)
        act = glu * _sigmoid(SWIGLU_ALPHA * glu) * (lin + 1.0)
        y = _dot(act.astype(BF16), wd16[...]) + bd_ref[0]
        _store_row_tiles(y_ref, y.astype(BF16).astype(F32))

    valid = nv_ref[i]
    for rows in range(MOE_ROWS_STEP, MOE_BLOCK + 1, MOE_ROWS_STEP):
        pl.when((valid > rows - MOE_ROWS_STEP) & (valid <= rows))(functools.partial(run_rows, rows))


def _experts(blk_e, n_used, next_e, blk_valid, xs2, w_gu, b_gu3, w_down, b_down3):
    row_blk = pl.BlockSpec((MOE_BLOCK * ROW_TILES, LANES),
                           lambda i, be, nu, nx, nv: (jnp.minimum(i, nu[0] - 1), 0))
    grid_spec = pltpu.PrefetchScalarGridSpec(
        num_scalar_prefetch=4,
        grid=(N_BLOCKS,),
        in_specs=[row_blk,
                  pl.BlockSpec(memory_space=pl.ANY),
                  pl.BlockSpec((1, 1, 2 * D_FF), lambda i, be, nu, nx, nv: (be[i], 0, 0)),
                  pl.BlockSpec(memory_space=pl.ANY),
                  pl.BlockSpec((1, 1, D_MODEL), lambda i, be, nu, nx, nv: (be[i], 0, 0))],
        out_specs=row_blk,
        scratch_shapes=[pltpu.VMEM((D_MODEL, 2 * D_FF), F32),
                        pltpu.VMEM((D_FF, D_MODEL), F32),
                        pltpu.VMEM((D_MODEL, 2 * D_FF), BF16),
                        pltpu.VMEM((D_FF, D_MODEL), BF16),
                        pltpu.SemaphoreType.DMA((2,))],
    )
    return pl.pallas_call(
        _expert_kernel,
        grid_spec=grid_spec,
        out_shape=jax.ShapeDtypeStruct((N_SLOTS * ROW_TILES, LANES), U32),
        compiler_params=pltpu.CompilerParams(dimension_semantics=("arbitrary",),
                                             vmem_limit_bytes=VMEM_LIMIT),
        name="experts",
    )(blk_e, n_used, next_e, blk_valid, xs2, w_gu, b_gu3, w_down, b_down3)


def _final_kernel(ysum_ref, x1_ref, mod_ref, gpost_ref, *maybe_alias_and_out):
    o_ref = maybe_alias_and_out[-1]
    y = _load_row_tiles(ysum_ref, TOK_TILE)
    gate_f = mod_ref[0][5:6]
    o_ref[...] = x1_ref[...] + _rms(y) * (gate_f * gpost_ref[...])


def _final(first_tile, tiles, ysum, x1, mod3, g_post, out_so_far):
    tok = lambda w: pl.BlockSpec((TOK_TILE, w), lambda i: (first_tile + i, 0))
    in_specs = [pl.BlockSpec((TOK_TILE * ROW_TILES, LANES), lambda i: (i, 0)),
                tok(D_MODEL),
                pl.BlockSpec((1, 6, D_MODEL), lambda i: (_tile_batch(first_tile + i), 0, 0)),
                pl.BlockSpec((1, D_MODEL), lambda i: (0, 0))]
    args = [ysum, x1, mod3, g_post]
    aliases = {}
    if out_so_far is not None:
        in_specs.append(pl.BlockSpec(memory_space=pl.ANY))
        args.append(out_so_far)
        aliases = {len(args) - 1: 0}
    return pl.pallas_call(
        _final_kernel,
        grid=(tiles,),
        in_specs=in_specs,
        out_specs=tok(D_MODEL),
        out_shape=jax.ShapeDtypeStruct((N_TOK, D_MODEL), F32),
        input_output_aliases=aliases,
        compiler_params=pltpu.CompilerParams(vmem_limit_bytes=VMEM_LIMIT),
        name="combine_final",
    )(*args)


def _slot_kernel(idx_ref, rank_ref, start_ref, dest_ref):
    tn = idx_ref.shape[1]
    experts = lax.broadcasted_iota(I32, (N_EXPERTS, tn), 0)
    starts = start_ref[...].astype(F32)
    rows = [jnp.sum(jnp.where(experts == idx_ref[kk:kk + 1, :], starts, 0.0), axis=0, keepdims=True)
            for kk in range(TOP_K)]
    dest_ref[...] = jnp.concatenate(rows, axis=0).astype(I32) + rank_ref[...]


def _slots(idx, rank, pstart):
    tn = N_TOK // 2
    lanes_tok = pl.BlockSpec((TOP_K, tn), lambda i: (0, i))
    return pl.pallas_call(
        _slot_kernel,
        grid=(N_TOK // tn,),
        in_specs=[lanes_tok, lanes_tok, pl.BlockSpec((N_EXPERTS, 1), lambda i: (0, 0))],
        out_specs=lanes_tok,
        out_shape=jax.ShapeDtypeStruct((TOP_K, N_TOK), I32),
        name="moe_slots",
    )(idx, rank, pstart.reshape(N_EXPERTS, 1))


def _routing_tables(idx, rank, counts):
    counts = counts.reshape(N_EXPERTS)
    experts = jnp.arange(N_EXPERTS, dtype=I32)
    padded = ((counts + MOE_BLOCK - 1) // MOE_BLOCK) * MOE_BLOCK
    pend = jnp.cumsum(padded)
    pstart = pend - padded
    dest = _slots(idx, rank, pstart)
    dest_win = dest.reshape(TOP_K, N_TOK // SC_WINDOW, SC_WINDOW).transpose(1, 0, 2)
    n_used = (pend[-1] // MOE_BLOCK).astype(I32).reshape(1)
    blk_start = jnp.arange(N_BLOCKS, dtype=I32) * MOE_BLOCK
    blk_e = jnp.sum(blk_start[:, None] >= pend[None, :], axis=1).astype(I32)
    last_e = jnp.max(jnp.where(counts > 0, experts, 0))
    blk_e = jnp.minimum(blk_e, last_e)
    later = (experts[None, :] > experts[:, None]) & (counts[None, :] > 0)
    next_nonempty = jnp.min(jnp.where(later, experts[None, :], N_EXPERTS), axis=1)
    next_nonempty = jnp.where(next_nonempty == N_EXPERTS, -1, next_nonempty).astype(I32)
    of_block = lambda table: jnp.sum(jnp.where(blk_e[:, None] == experts, table, 0), axis=1).astype(I32)
    next_e = of_block(next_nonempty)
    blk_valid = jnp.clip(of_block(counts) - (blk_start - of_block(pstart)), 0, MOE_BLOCK)
    blk_valid = jnp.where(blk_start < pend[-1], blk_valid, 0).astype(I32)
    return dest, dest_win, blk_e, n_used, next_e, blk_valid


def kernel(x, c, w_ada, b_ada, g_pre_mix, g_post_mix, w_in, attn_norm_w, hgrn_lb, hgrn_norm_w,
           w_out, g_pre_ffn, g_post_ffn, w_router, b_router, w_gu, b_gu, w_down, b_down):
    c_pad = jnp.pad(c, ((0, SUBLANES - BATCH), (0, 0)))
    mod = _ada_mod(c_pad, w_ada[0], b_ada)
    mod3 = mod[:BATCH].reshape(BATCH, 6, D_MODEL)

    x2 = x.reshape(N_TOK, D_MODEL)
    q4, k4, v4, q16, k16, v16, qr, fr, ir, gr = _inproj(x2, mod3, g_pre_mix, w_in[0])
    nat = lambda t: t.reshape(BATCH, SEQ, D_REC)
    ya = _attention(q4, k4, v4, q16, k16, v16, attn_norm_w)
    yr = _hgrn(nat(qr), nat(fr), nat(ir), nat(gr), hgrn_lb, hgrn_norm_w)

    x1, h2, idx, gates, rank, counts = _mid(
        ya.reshape(N_TOK, D_ATTN), yr.reshape(N_TOK, D_REC), x2, mod3, g_post_mix, g_pre_ffn,
        w_out[0], w_router[0], b_router)

    dest, dest_win, blk_e, n_used, next_e, blk_valid = _routing_tables(idx, rank, counts)
    xs = _sc_dispatch(h2.reshape(N_TOK, ROW_TILES, LANES), dest_win)
    ys = _experts(blk_e, n_used, next_e, blk_valid, xs.reshape(N_SLOTS * ROW_TILES, LANES),
                  w_gu[0], b_gu[0].reshape(N_EXPERTS, 1, 2 * D_FF),
                  w_down[0], b_down[0].reshape(N_EXPERTS, 1, D_MODEL))
    ys3 = ys.reshape(N_SLOTS, ROW_TILES, LANES)
    by_window = lambda t: t.reshape(TOP_K, N_TOK // SC_SUM_WINDOW, SC_SUM_WINDOW).transpose(1, 0, 2)
    sum_win = by_window(dest)
    gate_bits = lax.bitcast_convert_type(gates.astype(BF16), jnp.uint16).astype(U32)
    gate_words = jnp.broadcast_to(by_window(gate_bits | (gate_bits << 16))[..., None],
                                  sum_win.shape + (SC_LANES,)).reshape(sum_win.shape[:2] + (-1,))
    out, done = None, 0
    for units in FINISH_PIECES:
        n_tok = units * FINISH_UNIT
        ysum = _sc_combine(ys3, sum_win, gate_words, done // SC_SUM_WINDOW, n_tok // SC_SUM_WINDOW)
        out = _final(done // TOK_TILE, n_tok // TOK_TILE, ysum.reshape(n_tok * ROW_TILES, LANES),
                     x1, mod3, g_post_ffn, out)
        done += n_tok
    return out.reshape(BATCH, SEQ, D_MODEL)
```

```python
import functools
import math

import jax
import jax.numpy as jnp
from jax import lax
from jax.experimental import pallas as pl
from jax.experimental.pallas import tpu as pltpu
from jax.experimental.pallas import tpu_sc as plsc

F32 = jnp.float32
BF16 = jnp.bfloat16
I32 = jnp.int32
U32 = jnp.uint32

D_MODEL = 1024
BATCH = 4
SEQ = 4096
N_TOK = BATCH * SEQ
D_ATTN = 512
HEAD_DIM_A = 64
ATT_BLOCK = 128
DIL_MID = 4
DIL_MAX = 16
SUB_MID = SEQ // DIL_MID
SUB_MAX = SEQ // DIL_MAX
D_REC = 512
HGRN_CHUNK = 32
HGRN_SUPER = 256
N_EXPERTS = 32
TOP_K = 4
D_FF = 1024
SWIGLU_LIMIT = 7.0
SWIGLU_ALPHA = 1.702
EPS = 1e-6
NEG_BIG = -1e30
Q_SCALE = HEAD_DIM_A ** -0.5 * math.log2(math.e)

LANES = 128
SUBLANES = 8
ROW_TILES = D_MODEL // 2 // LANES

TOK_TILE = 512
MOE_BLOCK = 1024
MOE_ROWS_STEP = 128
N_SLOTS = N_TOK * TOP_K + N_EXPERTS * MOE_BLOCK
N_BLOCKS = N_SLOTS // MOE_BLOCK
SC_WINDOW = 128
SC_SUM_WINDOW = 8
SC_SUM_DEPTH = 4
FINISH_UNIT = 1024
FINISH_PIECES = (2, 4, 4, 4, 2)
SC_LANES = 16
VMEM_LIMIT = 56 * 1024 * 1024


def _sigmoid(x):
    return 1.0 / (1.0 + jnp.exp(-x))


def _dot(a, b):
    return jnp.dot(a, b, preferred_element_type=F32)


def _dot_nt(a, b):
    return lax.dot_general(a, b, (((1,), (1,)), ((), ())), preferred_element_type=F32)


def _split_bf16(x):
    hi = x.astype(BF16)
    return hi, (x - hi.astype(F32)).astype(BF16)


def _rms(x):
    return x * lax.rsqrt(jnp.mean(x * x, axis=-1, keepdims=True) + EPS)


def _ada_kernel(c_ref, w_ref, b_ref, o_ref):
    c = c_ref[...]
    cond = c * _sigmoid(c)
    c_hi, c_lo = _split_bf16(cond)
    w_hi, w_lo = _split_bf16(w_ref[...])
    o_ref[...] = _dot(c_hi, w_hi) + _dot(c_lo, w_hi) + _dot(c_hi, w_lo) + b_ref[...]


def _ada_mod(c_pad, w_ada, b_ada):
    n = w_ada.shape[1]
    tn = 1536
    return pl.pallas_call(
        _ada_kernel,
        grid=(n // tn,),
        in_specs=[pl.BlockSpec((SUBLANES, D_MODEL), lambda j: (0, 0)),
                  pl.BlockSpec((D_MODEL, tn), lambda j: (0, j)),
                  pl.BlockSpec((1, tn), lambda j: (0, j))],
        out_specs=pl.BlockSpec((SUBLANES, tn), lambda j: (0, j)),
        out_shape=jax.ShapeDtypeStruct((SUBLANES, n), F32),
        compiler_params=pltpu.CompilerParams(vmem_limit_bytes=VMEM_LIMIT),
        name="ada_mod",
    )(c_pad, w_ada, b_ada)


def _cast_weights_once(w_ref, w16):
    rows = 128

    @pl.when(pl.program_id(0) == 0)
    def _():
        def cast(r, carry):
            sl = pl.ds(pl.multiple_of(r * rows, rows), rows)
            w16[sl, :] = w_ref[sl, :].astype(BF16)
            return carry

        lax.fori_loop(0, w_ref.shape[0] // rows, cast, 0)


def _resident(shape):
    return pl.BlockSpec(shape, lambda i: (0,) * len(shape), pipeline_mode=pl.Buffered(1))


def _inproj_kernel(x_ref, mod_ref, g_ref, w_ref, q4, k4, v4, q16, k16, v16, qr, fr, ir, gr,
                   w16, stage_nat, stage_mid):
    _cast_weights_once(w_ref, w16)
    mod = mod_ref[0]
    shift, scale = mod[0:1], mod[1:2]
    h = _rms(x_ref[...]) * (g_ref[...] * (1.0 + scale)) + shift
    hb = h.astype(BF16)
    slabs = D_ATTN // LANES
    rows_mid = TOK_TILE // DIL_MID
    rows_max = TOK_TILE // DIL_MAX

    def proj(j):
        return _dot(hb, w16[:, j * D_ATTN:(j + 1) * D_ATTN])

    for j, (o_mid, o_max) in enumerate(((q4, q16), (k4, k16), (v4, v16))):
        r = proj(j)
        if j == 0:
            r = r * Q_SCALE
        for cs in range(slabs):
            stage_nat[cs] = r[:, cs * LANES:(cs + 1) * LANES]
        for cs in range(slabs):
            lanes = slice(cs * LANES, (cs + 1) * LANES)
            for sub in range(DIL_MID):
                piece = stage_nat[cs, pl.ds(sub, rows_mid, stride=DIL_MID), :]
                o_mid[sub, :, lanes] = piece.astype(BF16)
                stage_mid[cs, sub] = piece
            for sub in range(DIL_MAX):
                piece = stage_mid[cs, sub % DIL_MID, pl.ds(sub // DIL_MID, rows_max, stride=DIL_MID), :]
                o_max[sub, :, lanes] = piece.astype(BF16)
    for j, o_ref in enumerate((qr, fr, ir, gr)):
        o_ref[...] = proj(3 + j)


def _inproj(x2, mod3, g_pre, w_in):
    tiles_per_seq = SEQ // TOK_TILE
    rows_mid = TOK_TILE // DIL_MID
    rows_max = TOK_TILE // DIL_MAX
    mid = pl.BlockSpec((None, DIL_MID, rows_mid, D_ATTN),
                       lambda i: (i // tiles_per_seq, 0, i % tiles_per_seq, 0))
    mx = pl.BlockSpec((None, DIL_MAX, rows_max, D_ATTN),
                      lambda i: (i // tiles_per_seq, 0, i % tiles_per_seq, 0))
    nat = pl.BlockSpec((TOK_TILE, D_REC), lambda i: (i, 0))
    mid_shape = jax.ShapeDtypeStruct((BATCH, DIL_MID, SUB_MID, D_ATTN), BF16)
    mx_shape = jax.ShapeDtypeStruct((BATCH, DIL_MAX, SUB_MAX, D_ATTN), BF16)
    nat_shape = jax.ShapeDtypeStruct((N_TOK, D_REC), F32)
    return pl.pallas_call(
        _inproj_kernel,
        grid=(N_TOK // TOK_TILE,),
        in_specs=[pl.BlockSpec((TOK_TILE, D_MODEL), lambda i: (i, 0)),
                  pl.BlockSpec((1, 6, D_MODEL), lambda i: (i // tiles_per_seq, 0, 0)),
                  pl.BlockSpec((1, D_MODEL), lambda i: (0, 0)),
                  _resident(w_in.shape)],
        out_specs=[mid, mid, mid, mx, mx, mx, nat, nat, nat, nat],
        out_shape=[mid_shape] * 3 + [mx_shape] * 3 + [nat_shape] * 4,
        scratch_shapes=[pltpu.VMEM(w_in.shape, BF16),
                        pltpu.VMEM((D_ATTN // LANES, TOK_TILE, LANES), F32),
                        pltpu.VMEM((D_ATTN // LANES, DIL_MID, rows_mid, LANES), F32)],
        compiler_params=pltpu.CompilerParams(dimension_semantics=("arbitrary",),
                                             vmem_limit_bytes=VMEM_LIMIT),
        name="inproj",
    )(x2, mod3, g_pre, w_in)


def _attn_kernel(q4_ref, k4_ref, v4_ref, q16_ref, k16_ref, v16_ref, nw_ref, o_ref,
                 sel_scr, keep_scr, o_scr, m_scr, l_scr):
    bw = ATT_BLOCK
    lane = lax.broadcasted_iota(I32, (bw, LANES), 1)
    head0 = lane < HEAD_DIM_A
    head_masks = (jnp.where(head0, 1.0, 0.0).astype(BF16), jnp.where(head0, 0.0, 1.0).astype(BF16))

    rr = lax.broadcasted_iota(I32, (bw, bw), 0)
    cc = lax.broadcasted_iota(I32, (bw, bw), 1)
    piece = bw // DIL_MID
    rp = DIL_MID * (rr % piece) + rr // piece
    cp = DIL_MID * (cc % piece) + cc // piece
    for var, (r, c) in enumerate(((rr, cc), (rr, cc), (rp, cp), (rp, cp))):
        first = var % 2 == 1
        left = (c < r) if first else (c >= r)
        right = jnp.zeros_like(left) if first else jnp.logical_not(left)
        sel_scr[var] = jnp.where(left, 1.0, 0.0)
        keep_scr[var, 0] = jnp.where(left, 1.0, 0.0).astype(BF16)
        keep_scr[var, 1] = jnp.where(right, 1.0, 0.0).astype(BF16)

    def score_matmuls(loaded):
        return [_dot_nt(q * hm, k) for (q, k, _, _, _) in loaded for hm in head_masks]

    def finish_group(scores, loaded):
        probs = []
        for n, s in enumerate(scores):
            _, _, _, var, right_bias = loaded[n // 2]
            folded = jnp.where(sel_scr[var] > 0.5, s[:, 0:bw], s[:, bw:] + right_bias)
            m = jnp.max(folded, axis=-1, keepdims=True)
            p = jnp.exp2(folded - m)
            pb = p.astype(BF16)
            spread = jnp.concatenate([pb * keep_scr[var, 0], pb * keep_scr[var, 1]], axis=1)
            probs.append((spread, jnp.sum(p, axis=-1, keepdims=True), m))
        pvs = [_dot(p, loaded[n // 2][2]) for n, (p, _, _) in enumerate(probs)]
        results = []
        for u in range(len(loaded)):
            (_, l0, m0), (_, l1, m1) = probs[2 * u], probs[2 * u + 1]
            results.append((jnp.where(head0, pvs[2 * u], pvs[2 * u + 1]),
                            jnp.where(head0, jnp.broadcast_to(m0, (bw, LANES)),
                                      jnp.broadcast_to(m1, (bw, LANES))),
                            jnp.where(head0, jnp.broadcast_to(l0, (bw, LANES)),
                                      jnp.broadcast_to(l1, (bw, LANES)))))
        return results

    group = 8

    def first_block_bias(blk):
        return jnp.where(blk == 0, NEG_BIG, 0.0)

    def load_d1(g):
        loaded, starts = [], []
        for i in range(group):
            blk = g * group + i
            qs = pl.multiple_of(blk * piece, piece)
            ks = pl.multiple_of(jnp.maximum(blk - 1, 0) * piece, piece)
            q = jnp.concatenate([q4_ref[r, pl.ds(qs, piece), :] for r in range(DIL_MID)], axis=0)
            k = jnp.concatenate([k4_ref[r, pl.ds(ks + half * piece, piece), :]
                                 for half in range(2) for r in range(DIL_MID)], axis=0)
            v = jnp.concatenate([v4_ref[r, pl.ds(ks + half * piece, piece), :]
                                 for half in range(2) for r in range(DIL_MID)], axis=0)
            loaded.append((q, k, v, jnp.where(blk == 0, 3, 2), first_block_bias(blk)))
            starts.append(qs)

        def store(results):
            for qs, parts in zip(starts, results):
                for r in range(DIL_MID):
                    for scr, val in zip((o_scr, m_scr, l_scr), parts):
                        scr[0, r, pl.ds(qs, piece), :] = val[r * piece:(r + 1) * piece]

        return loaded, store

    def load_d4(g):
        loaded, dsts = [], []
        for i in range(group):
            r, blk = i % DIL_MID, g * (group // DIL_MID) + i // DIL_MID
            qs = pl.multiple_of(blk * bw, bw)
            ks = pl.multiple_of(jnp.maximum(blk - 1, 0) * bw, bw)
            loaded.append((q4_ref[r, pl.ds(qs, bw), :], k4_ref[r, pl.ds(ks, 2 * bw), :],
                           v4_ref[r, pl.ds(ks, 2 * bw), :], jnp.where(blk == 0, 1, 0),
                           first_block_bias(blk)))
            dsts.append((r, qs))

        def store(results):
            for (r, qs), parts in zip(dsts, results):
                for scr, val in zip((o_scr, m_scr, l_scr), parts):
                    scr[1, r, pl.ds(qs, bw), :] = val

        return loaded, store

    def load_d16(g):
        loaded, dsts = [], []
        for i in range(group):
            r, blk = g * (group // 2) + i // 2, i % 2
            loaded.append((q16_ref[r, blk * bw:(blk + 1) * bw, :], k16_ref[r], v16_ref[r],
                           1 - blk, NEG_BIG if blk == 0 else 0.0))
            dsts.append((r % DIL_MID, pl.ds(blk * (bw * DIL_MID) + r // DIL_MID, bw, stride=DIL_MID)))

        def store(results):
            for (sub, dst), parts in zip(dsts, results):
                for scr, val in zip((o_scr, m_scr, l_scr), parts):
                    scr[2, sub, dst, :] = val

        return loaded, store

    def run_group(load):
        def step(g, carry):
            loaded, store = load(g)
            store(finish_group(score_matmuls(loaded), loaded))
            return carry
        return step

    n_groups = SEQ // bw // group
    for load in (load_d1, load_d4, load_d16):
        lax.fori_loop(0, n_groups, run_group(load), 0)

    rows = 256
    hi = lax.broadcasted_iota(I32, (LANES, LANES), 0) // HEAD_DIM_A
    hj = lax.broadcasted_iota(I32, (LANES, LANES), 1) // HEAD_DIM_A
    head_sum = jnp.where(hi == hj, 1.0, 0.0).astype(BF16)
    nw = nw_ref[...]

    def head_sums(x):
        x_hi, x_lo = _split_bf16(x)
        return _dot(x_hi, head_sum) + _dot(x_lo, head_sum)

    def merge(t, carry):
        r = t // (SUB_MID // rows)
        start = pl.multiple_of((t % (SUB_MID // rows)) * rows, rows)
        sl = pl.ds(start, rows)
        own_score = head_sums(q4_ref[r, sl, :].astype(F32) * k4_ref[r, sl, :].astype(F32))
        ms = [m_scr[n, r, sl, :] for n in range(3)]
        mx = jnp.maximum(jnp.maximum(jnp.maximum(ms[0], ms[1]), ms[2]), own_score)
        ws = [jnp.exp2(m - mx) for m in ms]
        w_own = float(len(ms)) * jnp.exp2(own_score - mx)
        num = (ws[0] * o_scr[0, r, sl, :] + ws[1] * o_scr[1, r, sl, :] + ws[2] * o_scr[2, r, sl, :]
               + w_own * v4_ref[r, sl, :].astype(F32))
        den = ws[0] * l_scr[0, r, sl, :] + ws[1] * l_scr[1, r, sl, :] + ws[2] * l_scr[2, r, sl, :] + w_own
        o = num / den
        mean_sq = head_sums(o * o) * (1.0 / HEAD_DIM_A)
        o_ref[pl.ds(start * DIL_MID + r, rows, stride=DIL_MID), :] = o * lax.rsqrt(mean_sq + EPS) * nw
        return carry

    lax.fori_loop(0, SEQ // rows, merge, 0, unroll=8)


def _attention(q4, k4, v4, q16, k16, v16, attn_norm_w):
    hp = D_ATTN // LANES
    mid = pl.BlockSpec((None, DIL_MID, SUB_MID, LANES), lambda b, h: (b, 0, 0, h))
    mx = pl.BlockSpec((None, DIL_MAX, SUB_MAX, LANES), lambda b, h: (b, 0, 0, h))
    scr = pltpu.VMEM((3, DIL_MID, SUB_MID, LANES), F32)
    return pl.pallas_call(
        _attn_kernel,
        grid=(BATCH, hp),
        in_specs=[mid, mid, mid, mx, mx, mx, pl.BlockSpec((1, LANES), lambda b, h: (0, h))],
        out_specs=pl.BlockSpec((None, SEQ, LANES), lambda b, h: (b, 0, h)),
        out_shape=jax.ShapeDtypeStruct((BATCH, SEQ, D_ATTN), F32),
        scratch_shapes=[pltpu.VMEM((4, ATT_BLOCK, ATT_BLOCK), F32),
                        pltpu.VMEM((4, 2, ATT_BLOCK, ATT_BLOCK), BF16), scr, scr, scr],
        compiler_params=pltpu.CompilerParams(vmem_limit_bytes=VMEM_LIMIT),
        name="dilated_attn",
    )(q4, k4, v4, q16, k16, v16, attn_norm_w)


def _hgrn_kernel(qr_ref, fr_ref, ir_ref, gr_ref, lb_ref, nw_ref, o_ref,
                 qe_scr, oi_scr, delta_scr, dec_scr, st_scr):
    sup, c = HGRN_SUPER, HGRN_CHUNK
    nch = sup // c
    n_sup = SEQ // sup
    lbp = lb_ref[...]
    lmx = jnp.max(lbp, axis=0, keepdims=True)
    ex = jnp.exp(lbp - lmx)
    lb = ex[0:1] / (ex[0:1] + ex[1:2])
    nw = nw_ref[...]

    ri = lax.broadcasted_iota(I32, (sup, sup), 0)
    ci = lax.broadcasted_iota(I32, (sup, sup), 1)
    same_chunk = (ri // c) == (ci // c)
    causal = same_chunk & (ci <= ri)
    cum_op = jnp.where(causal, 1.0, 0.0).astype(BF16)

    def chunk_rows(kd, ch):
        parts = []
        if ch > 0:
            parts.append(jnp.zeros((ch * c, LANES), BF16))
        parts.append(kd[ch * c:(ch + 1) * c])
        if ch < nch - 1:
            parts.append(jnp.zeros(((nch - 1 - ch) * c, LANES), BF16))
        return jnp.concatenate(parts, axis=0)

    group = 4

    def independent(g, carry):
        ts = [g * group + i for i in range(group)]
        sls = [pl.ds(pl.multiple_of(t * sup, sup), sup) for t in ts]
        pre = []
        for sl in sls:
            f = lb + (1.0 - lb) * _sigmoid(fr_ref[sl, :])
            logf_hi, logf_lo = _split_bf16(jnp.log(f))
            pre.append((1.0 - f, _dot(cum_op, logf_hi) + _dot(cum_op, logf_lo)))
        mid = []
        for t, sl, (kk, b) in zip(ts, sls, pre):
            b_last = jnp.concatenate(
                [jnp.broadcast_to(b[(ch + 1) * c - 1:(ch + 1) * c], (c, LANES)) for ch in range(nch)],
                axis=0)
            q = qr_ref[sl, :]
            qeb = (q * _sigmoid(q) * jnp.exp(b)).astype(BF16)
            ke = (kk * jnp.exp(-b)).astype(BF16)
            kd = (kk * jnp.exp(b_last - b)).astype(BF16)
            qe_scr[sl, :] = qeb
            dec_rows = jnp.concatenate([b_last[ch * c:ch * c + 1] for ch in range(nch)], axis=0)
            dec_scr[pl.ds(pl.multiple_of(t * nch, nch), nch), :] = jnp.exp(dec_rows)
            v = ir_ref[sl, :]
            vt = v.T.astype(BF16)
            scores = _dot_nt(qeb, ke)
            for pair in range(nch // 2):
                rhs = jnp.concatenate([chunk_rows(kd, 2 * pair), chunk_rows(kd, 2 * pair + 1)], axis=1)
                d2 = _dot(vt, rhs)
                delta_scr[t * nch + 2 * pair] = d2[:, 0:LANES]
                delta_scr[t * nch + 2 * pair + 1] = d2[:, LANES:]
            mid.append((scores, v.astype(BF16)))
        for sl, (scores, vb) in zip(sls, mid):
            a = jnp.where(causal, scores, 0.0)
            oi_scr[sl, :] = _dot(a.astype(BF16), vb)
        return carry

    lax.fori_loop(0, n_sup // group, independent, 0)

    def recur(ch, st):
        st_scr[ch] = st.astype(BF16)
        return st * dec_scr[pl.ds(ch, 1), :] + delta_scr[ch]

    lax.fori_loop(0, SEQ // c, recur, jnp.zeros((LANES, LANES), F32), unroll=8)

    def finish(t, carry):
        sl = pl.ds(pl.multiple_of(t * sup, sup), sup)
        parts = [_dot_nt(qe_scr[pl.ds(pl.multiple_of(t * sup + ch * c, c), c), :], st_scr[t * nch + ch])
                 for ch in range(nch)]
        o = oi_scr[sl, :] + jnp.concatenate(parts, axis=0)
        g = gr_ref[sl, :]
        o_ref[sl, :] = (_rms(o) * nw * (g * _sigmoid(g))).astype(o_ref.dtype)
        return carry

    lax.fori_loop(0, n_sup, finish, 0, unroll=8)


def _hgrn(qr, fr, ir, gr, hgrn_lb, hgrn_norm_w):
    nh = D_REC // LANES
    n_chunks = SEQ // HGRN_CHUNK
    blk = pl.BlockSpec((None, SEQ, LANES), lambda b, h: (b, 0, h))
    return pl.pallas_call(
        _hgrn_kernel,
        grid=(BATCH, nh),
        in_specs=[blk, blk, blk, blk,
                  pl.BlockSpec((2, LANES), lambda b, h: (0, h)),
                  pl.BlockSpec((1, LANES), lambda b, h: (0, h))],
        out_specs=blk,
        out_shape=jax.ShapeDtypeStruct((BATCH, SEQ, D_REC), BF16),
        scratch_shapes=[pltpu.VMEM((SEQ, LANES), BF16),
                        pltpu.VMEM((SEQ, LANES), F32),
                        pltpu.VMEM((n_chunks, LANES, LANES), F32),
                        pltpu.VMEM((n_chunks, LANES), F32),
                        pltpu.VMEM((n_chunks, LANES, LANES), BF16)],
        compiler_params=pltpu.CompilerParams(vmem_limit_bytes=VMEM_LIMIT),
        name="hgrn2",
    )(qr, fr, ir, gr, hgrn_lb, hgrn_norm_w)


def _tile_batch(i):
    return i // (SEQ // TOK_TILE)


def _store_row_tiles(ref, val):
    rows, half = val.shape[0], D_MODEL // 2
    lo = lax.bitcast_convert_type(val[:, :half], U32)
    hi = lax.bitcast_convert_type(val[:, half:], U32)
    words = lax.shift_right_logical(lo, jnp.uint32(16)) | (hi & jnp.uint32(0xFFFF0000))
    for j in range(ROW_TILES):
        ref[pl.ds(j, rows, stride=ROW_TILES), :] = words[:, j * LANES:(j + 1) * LANES]


def _load_row_tiles(ref, rows):
    words = jnp.concatenate(
        [ref[pl.ds(j, rows, stride=ROW_TILES), :] for j in range(ROW_TILES)], axis=1)
    lo = lax.bitcast_convert_type(lax.shift_left(words, jnp.uint32(16)), F32)
    hi = lax.bitcast_convert_type(words & jnp.uint32(0xFFFF0000), F32)
    return jnp.concatenate([lo, hi], axis=1)


def _mid_kernel(ya_ref, yr_ref, x_ref, mod_ref, gpost_ref, gpre_ref, wo_ref, wr_ref, br_ref,
                x1_ref, h2_ref, idx_ref, gate_ref, rank_ref, cnt_ref, carry_ref, wo16):
    i = pl.program_id(0)
    _cast_weights_once(wo_ref, wo16)

    @pl.when(i == 0)
    def _():
        carry_ref[...] = jnp.zeros_like(carry_ref)

    mod = mod_ref[0]
    gate_m, shift_f, scale_f = mod[2:3], mod[3:4], mod[4:5]
    y = _dot(ya_ref[...].astype(BF16), wo16[0:D_ATTN, :]) + _dot(yr_ref[...], wo16[D_ATTN:, :])
    x1 = x_ref[...] + _rms(y) * (gate_m * gpost_ref[...])
    x1_ref[...] = x1
    h2 = _rms(x1) * (gpre_ref[...] * (1.0 + scale_f)) + shift_f
    h2_hi = h2.astype(BF16)
    h2_rounded = h2_hi.astype(F32)
    _store_row_tiles(h2_ref, h2_rounded)

    tm = h2.shape[0]
    h2_lo = (h2 - h2_rounded).astype(BF16)
    wr_hi, wr_lo = _split_bf16(wr_ref[...])
    parts = jnp.concatenate([_dot(h2_hi, jnp.concatenate([wr_hi, wr_lo], axis=1)),
                             _dot(h2_lo, wr_hi),
                             jnp.broadcast_to(br_ref[...], (tm, N_EXPERTS))], axis=1).T
    work = sum(parts[g * N_EXPERTS:(g + 1) * N_EXPERTS] for g in range(LANES // N_EXPERTS))
    eidx = lax.broadcasted_iota(I32, (N_EXPERTS, tm), 0).astype(F32)
    vals, idxs = [], []
    onehot = jnp.zeros((N_EXPERTS, tm), F32)
    for _ in range(TOP_K):
        m = jnp.max(work, axis=0, keepdims=True)
        sel = jnp.min(jnp.where(work == m, eidx, float(N_EXPERTS)), axis=0, keepdims=True)
        hit = eidx == sel
        work = jnp.where(hit, -jnp.inf, work)
        onehot = jnp.where(hit, 1.0, onehot)
        vals.append(m)
        idxs.append(sel)
    ex = [jnp.exp(vv - vals[0]) for vv in vals]
    inv_den = 1.0 / (ex[0] + ex[1] + ex[2] + ex[3])

    ri = lax.broadcasted_iota(I32, (tm, tm), 0)
    ci = lax.broadcasted_iota(I32, (tm, tm), 1)
    strict_upper = jnp.where(ri < ci, 1.0, 0.0).astype(BF16)
    before = _dot(onehot.astype(BF16), strict_upper) + carry_ref[...]
    ranks = [jnp.sum(jnp.where(eidx == idxs[kk], before, 0.0), axis=0, keepdims=True)
             for kk in range(TOP_K)]
    idx_ref[...] = jnp.concatenate(idxs, axis=0).astype(I32)
    rank_ref[...] = jnp.concatenate(ranks, axis=0).astype(I32)
    gate_ref[...] = jnp.concatenate([e * inv_den for e in ex], axis=0)
    total = carry_ref[...] + jnp.sum(onehot, axis=1, keepdims=True)
    carry_ref[...] = total
    cnt_ref[...] = total.astype(I32)


def _mid(ya, yr, x2, mod3, g_post, g_pre, w_out, w_router, b_router):
    tok = lambda w: pl.BlockSpec((TOK_TILE, w), lambda i: (i, 0))
    const = lambda s: pl.BlockSpec(s, lambda i: (0,) * len(s))
    lanes_tok = pl.BlockSpec((TOP_K, TOK_TILE), lambda i: (0, i))
    return pl.pallas_call(
        _mid_kernel,
        grid=(N_TOK // TOK_TILE,),
        in_specs=[tok(D_ATTN), tok(D_REC), tok(D_MODEL),
                  pl.BlockSpec((1, 6, D_MODEL), lambda i: (_tile_batch(i), 0, 0)),
                  const((1, D_MODEL)), const((1, D_MODEL)),
                  _resident((D_MODEL, D_MODEL)), const((D_MODEL, N_EXPERTS)), const((1, N_EXPERTS))],
        out_specs=[tok(D_MODEL),
                   pl.BlockSpec((TOK_TILE * ROW_TILES, LANES), lambda i: (i, 0)),
                   lanes_tok, lanes_tok, lanes_tok, const((N_EXPERTS, 1))],
        out_shape=[jax.ShapeDtypeStruct((N_TOK, D_MODEL), F32),
                   jax.ShapeDtypeStruct((N_TOK * ROW_TILES, LANES), U32),
                   jax.ShapeDtypeStruct((TOP_K, N_TOK), I32),
                   jax.ShapeDtypeStruct((TOP_K, N_TOK), F32),
                   jax.ShapeDtypeStruct((TOP_K, N_TOK), I32),
                   jax.ShapeDtypeStruct((N_EXPERTS, 1), I32)],
        scratch_shapes=[pltpu.VMEM((N_EXPERTS, 1), F32), pltpu.VMEM((D_MODEL, D_MODEL), BF16)],
        compiler_params=pltpu.CompilerParams(dimension_semantics=("arbitrary",),
                                             vmem_limit_bytes=VMEM_LIMIT),
        name="outproj_router",
    )(ya, yr, x2, mod3, g_post, g_pre, w_out, w_router, b_router)


def _sc_mesh():
    return plsc.VectorSubcoreMesh(core_axis_name="c", subcore_axis_name="s")


def _sc_worker_count():
    info = plsc.get_sparse_core_info()
    return info.num_cores, info.num_cores * info.num_subcores


def _sc_dispatch(h_rows, dest_win):
    n_cores, n_workers = _sc_worker_count()
    n_win = N_TOK // SC_WINDOW
    per_worker = n_win // n_workers

    @functools.partial(
        pl.kernel, mesh=_sc_mesh(),
        out_type=jax.ShapeDtypeStruct((N_SLOTS, ROW_TILES, LANES), U32),
        scratch_types=[pltpu.VMEM((TOP_K, SC_WINDOW), I32),
                       pltpu.VMEM((SC_WINDOW, ROW_TILES, LANES), U32),
                       pltpu.SemaphoreType.DMA],
        name="sc_dispatch")
    def run(h_hbm, dest_hbm, xs_hbm, idx_v, rows_v, sem):
        wid = lax.axis_index("s") * n_cores + lax.axis_index("c")

        @pl.loop(0, per_worker)
        def _(j):
            win = wid * per_worker + j
            pltpu.sync_copy(dest_hbm.at[win], idx_v)
            pltpu.sync_copy(h_hbm.at[pl.ds(win * SC_WINDOW, SC_WINDOW)], rows_v)
            copies = [pltpu.async_copy(rows_v, xs_hbm.at[idx_v.at[kk]], sem)
                      for kk in range(TOP_K)]
            for cp in copies:
                cp.wait()

    return run(h_rows, dest_win)


def _sc_combine(y_rows, dest_win, gate_words, first_win, n_win):
    n_cores, n_workers = _sc_worker_count()
    w = dest_win.shape[2]
    per_worker = n_win // n_workers
    depth = SC_SUM_DEPTH
    assert per_worker * n_workers == n_win and per_worker % depth == 0 and depth % 2 == 0, (n_win, n_workers)

    @functools.partial(
        pl.kernel, mesh=_sc_mesh(),
        out_type=jax.ShapeDtypeStruct((n_win * w, ROW_TILES, LANES), U32),
        scratch_types=[pltpu.VMEM((per_worker, TOP_K, w), I32),
                       pltpu.VMEM((per_worker, TOP_K, w * SC_LANES), U32),
                       pltpu.VMEM((depth, TOP_K, w, ROW_TILES, LANES), U32),
                       pltpu.VMEM((2, w, ROW_TILES, LANES), U32),
                       pltpu.SemaphoreType.DMA((depth,)),
                       pltpu.SemaphoreType.DMA((2,))],
        compiler_params=pltpu.CompilerParams(needs_layout_passes=False),
        name="sc_combine")
    def run(y_hbm, dest_hbm, gate_hbm, out_hbm, idx_v, gate_v, rows_v, out_v, in_sems, out_sems):
        wid = lax.axis_index("s") * n_cores + lax.axis_index("c")
        first = wid * per_worker
        pltpu.sync_copy(dest_hbm.at[pl.ds(first_win + first, per_worker)], idx_v)
        pltpu.sync_copy(gate_hbm.at[pl.ds(first_win + first, per_worker)], gate_v)

        def gather(it, slot):
            return [pltpu.make_async_copy(y_hbm.at[idx_v.at[it, kk]], rows_v.at[slot, kk], in_sems.at[slot])
                    for kk in range(TOP_K)]

        def put(it, slot):
            return pltpu.make_async_copy(out_v.at[slot], out_hbm.at[pl.ds((first + it) * w, w)],
                                         out_sems.at[slot])

        def step(it, slot):
            @pl.when(it + depth - 1 < per_worker)
            def _():
                for cp in gather(it + depth - 1, (slot + depth - 1) % depth):
                    cp.start()

            for cp in gather(it, slot):
                cp.wait()

            slot2 = slot % 2

            @pl.when(it >= 2)
            def _():
                put(it - 2, slot2).wait()

            @plsc.parallel_loop(0, w)
            def _(j):
                own = pl.ds(pl.multiple_of(j * SC_LANES, SC_LANES), SC_LANES)
                gates = [plsc.bitcast(gate_v[it, kk, own], BF16) for kk in range(TOP_K)]
                for t in range(ROW_TILES):
                    for c in range(LANES // SC_LANES):
                        lanes = pl.ds(c * SC_LANES, SC_LANES)
                        v = [plsc.bitcast(rows_v[slot, kk, j, t, lanes], BF16) * gates[kk]
                             for kk in range(TOP_K)]
                        out_v[slot2, j, t, lanes] = plsc.bitcast((v[0] + v[1]) + (v[2] + v[3]), U32)

            put(it, slot2).start()

        for ahead in range(depth - 1):
            for cp in gather(ahead, ahead):
                cp.start()

        @pl.loop(0, per_worker // depth)
        def _(p):
            for slot in range(depth):
                step(depth * p + slot, slot)

        put(per_worker - 2, 0).wait()
        put(per_worker - 1, 1).wait()

    return run(y_rows, dest_win, gate_words)


def _expert_kernel(be_ref, nu_ref, nx_ref, nv_ref, x_ref, wgu_hbm, bgu_ref, wd_hbm, bd_ref, y_ref,
                   wgu32, wd32, wgu16, wd16, sems):
    i = pl.program_id(0)
    e = be_ref[i]
    prev = be_ref[jnp.maximum(i - 1, 0)]

    def weight_copies(ex):
        return (pltpu.make_async_copy(wgu_hbm.at[ex], wgu32, sems.at[0]),
                pltpu.make_async_copy(wd_hbm.at[ex], wd32, sems.at[1]))

    @pl.when(i == 0)
    def _():
        for cp in weight_copies(e):
            cp.start(priority=1)

    @pl.when((i == 0) | (e != prev))
    def _():
        for cp in weight_copies(e):
            cp.wait()
        rows = 128

        def cast(r, carry):
            sl = pl.ds(pl.multiple_of(r * rows, rows), rows)
            wgu16[sl, :] = wgu32[sl, :].astype(BF16)
            wd16[sl, :] = wd32[sl, :].astype(BF16)
            return carry

        lax.fori_loop(0, D_MODEL // rows, cast, 0)
        nxt = nx_ref[i]

        @pl.when(nxt >= 0)
        def _():
            for cp in weight_copies(nxt):
                cp.start(priority=1)

    def run_rows(rows):
        x = _load_row_tiles(x_ref, rows).astype(BF16)
        bgu = bgu_ref[0]
        glu = _dot(x, wgu16[:, 0:D_FF]) + bgu[:, 0:D_FF]
        lin = _dot(x, wgu16[:, D_FF:]) + bgu[:, D_FF:]
        glu = jnp.minimum(glu, SWIGLU_LIMIT)
        lin = jnp.clip(lin, -SWIGLU_LIMIT, SWIGLU_LIMIT)
        act = glu * _sigmoid(SWIGLU_ALPHA * glu) * (lin + 1.0)
        y = _dot(act.astype(BF16), wd16[...]) + bd_ref[0]
        _store_row_tiles(y_ref, y.astype(BF16).astype(F32))

    valid = nv_ref[i]
    for rows in range(MOE_ROWS_STEP, MOE_BLOCK + 1, MOE_ROWS_STEP):
        pl.when((valid > rows - MOE_ROWS_STEP) & (valid <= rows))(functools.partial(run_rows, rows))


def _experts(blk_e, n_used, next_e, blk_valid, xs2, w_gu, b_gu3, w_down, b_down3):
    row_blk = pl.BlockSpec((MOE_BLOCK * ROW_TILES, LANES),
                           lambda i, be, nu, nx, nv: (jnp.minimum(i, nu[0] - 1), 0))
    grid_spec = pltpu.PrefetchScalarGridSpec(
        num_scalar_prefetch=4,
        grid=(N_BLOCKS,),
        in_specs=[row_blk,
                  pl.BlockSpec(memory_space=pl.ANY),
                  pl.BlockSpec((1, 1, 2 * D_FF), lambda i, be, nu, nx, nv: (be[i], 0, 0)),
                  pl.BlockSpec(memory_space=pl.ANY),
                  pl.BlockSpec((1, 1, D_MODEL), lambda i, be, nu, nx, nv: (be[i], 0, 0))],
        out_specs=row_blk,
        scratch_shapes=[pltpu.VMEM((D_MODEL, 2 * D_FF), F32),
                        pltpu.VMEM((D_FF, D_MODEL), F32),
                        pltpu.VMEM((D_MODEL, 2 * D_FF), BF16),
                        pltpu.VMEM((D_FF, D_MODEL), BF16),
                        pltpu.SemaphoreType.DMA((2,))],
    )
    return pl.pallas_call(
        _expert_kernel,
        grid_spec=grid_spec,
        out_shape=jax.ShapeDtypeStruct((N_SLOTS * ROW_TILES, LANES), U32),
        compiler_params=pltpu.CompilerParams(dimension_semantics=("arbitrary",),
                                             vmem_limit_bytes=VMEM_LIMIT),
        name="experts",
    )(blk_e, n_used, next_e, blk_valid, xs2, w_gu, b_gu3, w_down, b_down3)


def _final_kernel(ysum_ref, x1_ref, mod_ref, gpost_ref, *maybe_alias_and_out):
    o_ref = maybe_alias_and_out[-1]
    y = _load_row_tiles(ysum_ref, TOK_TILE)
    gate_f = mod_ref[0][5:6]
    o_ref[...] = x1_ref[...] + _rms(y) * (gate_f * gpost_ref[...])


def _final(first_tile, tiles, ysum, x1, mod3, g_post, out_so_far):
    tok = lambda w: pl.BlockSpec((TOK_TILE, w), lambda i: (first_tile + i, 0))
    in_specs = [pl.BlockSpec((TOK_TILE * ROW_TILES, LANES), lambda i: (i, 0)),
                tok(D_MODEL),
                pl.BlockSpec((1, 6, D_MODEL), lambda i: (_tile_batch(first_tile + i), 0, 0)),
                pl.BlockSpec((1, D_MODEL), lambda i: (0, 0))]
    args = [ysum, x1, mod3, g_post]
    aliases = {}
    if out_so_far is not None:
        in_specs.append(pl.BlockSpec(memory_space=pl.ANY))
        args.append(out_so_far)
        aliases = {len(args) - 1: 0}
    return pl.pallas_call(
        _final_kernel,
        grid=(tiles,),
        in_specs=in_specs,
        out_specs=tok(D_MODEL),
        out_shape=jax.ShapeDtypeStruct((N_TOK, D_MODEL), F32),
        input_output_aliases=aliases,
        compiler_params=pltpu.CompilerParams(vmem_limit_bytes=VMEM_LIMIT),
        name="combine_final",
    )(*args)


def _slot_kernel(idx_ref, rank_ref, start_ref, dest_ref):
    tn = idx_ref.shape[1]
    experts = lax.broadcasted_iota(I32, (N_EXPERTS, tn), 0)
    starts = start_ref[...].astype(F32)
    rows = [jnp.sum(jnp.where(experts == idx_ref[kk:kk + 1, :], starts, 0.0), axis=0, keepdims=True)
            for kk in range(TOP_K)]
    dest_ref[...] = jnp.concatenate(rows, axis=0).astype(I32) + rank_ref[...]


def _slots(idx, rank, pstart):
    tn = N_TOK // 2
    lanes_tok = pl.BlockSpec((TOP_K, tn), lambda i: (0, i))
    return pl.pallas_call(
        _slot_kernel,
        grid=(N_TOK // tn,),
        in_specs=[lanes_tok, lanes_tok, pl.BlockSpec((N_EXPERTS, 1), lambda i: (0, 0))],
        out_specs=lanes_tok,
        out_shape=jax.ShapeDtypeStruct((TOP_K, N_TOK), I32),
        name="moe_slots",
    )(idx, rank, pstart.reshape(N_EXPERTS, 1))


def _routing_tables(idx, rank, counts):
    counts = counts.reshape(N_EXPERTS)
    experts = jnp.arange(N_EXPERTS, dtype=I32)
    padded = ((counts + MOE_BLOCK - 1) // MOE_BLOCK) * MOE_BLOCK
    pend = jnp.cumsum(padded)
    pstart = pend - padded
    dest = _slots(idx, rank, pstart)
    dest_win = dest.reshape(TOP_K, N_TOK // SC_WINDOW, SC_WINDOW).transpose(1, 0, 2)
    n_used = (pend[-1] // MOE_BLOCK).astype(I32).reshape(1)
    blk_start = jnp.arange(N_BLOCKS, dtype=I32) * MOE_BLOCK
    blk_e = jnp.sum(blk_start[:, None] >= pend[None, :], axis=1).astype(I32)
    last_e = jnp.max(jnp.where(counts > 0, experts, 0))
    blk_e = jnp.minimum(blk_e, last_e)
    later = (experts[None, :] > experts[:, None]) & (counts[None, :] > 0)
    next_nonempty = jnp.min(jnp.where(later, experts[None, :], N_EXPERTS), axis=1)
    next_nonempty = jnp.where(next_nonempty == N_EXPERTS, -1, next_nonempty).astype(I32)
    of_block = lambda table: jnp.sum(jnp.where(blk_e[:, None] == experts, table, 0), axis=1).astype(I32)
    next_e = of_block(next_nonempty)
    blk_valid = jnp.clip(of_block(counts) - (blk_start - of_block(pstart)), 0, MOE_BLOCK)
    blk_valid = jnp.where(blk_start < pend[-1], blk_valid, 0).astype(I32)
    return dest, dest_win, blk_e, n_used, next_e, blk_valid


def kernel(x, c, w_ada, b_ada, g_pre_mix, g_post_mix, w_in, attn_norm_w, hgrn_lb, hgrn_norm_w,
           w_out, g_pre_ffn, g_post_ffn, w_router, b_router, w_gu, b_gu, w_down, b_down):
    c_pad = jnp.pad(c, ((0, SUBLANES - BATCH), (0, 0)))
    mod = _ada_mod(c_pad, w_ada[0], b_ada)
    mod3 = mod[:BATCH].reshape(BATCH, 6, D_MODEL)

    x2 = x.reshape(N_TOK, D_MODEL)
    q4, k4, v4, q16, k16, v16, qr, fr, ir, gr = _inproj(x2, mod3, g_pre_mix, w_in[0])
    nat = lambda t: t.reshape(BATCH, SEQ, D_REC)
    ya = _attention(q4, k4, v4, q16, k16, v16, attn_norm_w)
    yr = _hgrn(nat(qr), nat(fr), nat(ir), nat(gr), hgrn_lb, hgrn_norm_w)

    x1, h2, idx, gates, rank, counts = _mid(
        ya.reshape(N_TOK, D_ATTN), yr.reshape(N_TOK, D_REC), x2, mod3, g_post_mix, g_pre_ffn,
        w_out[0], w_router[0], b_router)

    dest, dest_win, blk_e, n_used, next_e, blk_valid = _routing_tables(idx, rank, counts)
    xs = _sc_dispatch(h2.reshape(N_TOK, ROW_TILES, LANES), dest_win)
    ys = _experts(blk_e, n_used, next_e, blk_valid, xs.reshape(N_SLOTS * ROW_TILES, LANES),
                  w_gu[0], b_gu[0].reshape(N_EXPERTS, 1, 2 * D_FF),
                  w_down[0], b_down[0].reshape(N_EXPERTS, 1, D_MODEL))
    ys3 = ys.reshape(N_SLOTS, ROW_TILES, LANES)
    by_window = lambda t: t.reshape(TOP_K, N_TOK // SC_SUM_WINDOW, SC_SUM_WINDOW).transpose(1, 0, 2)
    sum_win = by_window(dest)
    gate_bits = lax.bitcast_convert_type(gates.astype(BF16), jnp.uint16).astype(U32)
    gate_words = jnp.broadcast_to(by_window(gate_bits | (gate_bits << 16))[..., None],
                                  sum_win.shape + (SC_LANES,)).reshape(sum_win.shape[:2] + (-1,))
    out, done = None, 0
    for units in FINISH_PIECES:
        n_tok = units * FINISH_UNIT
        ysum = _sc_combine(ys3, sum_win, gate_words, done // SC_SUM_WINDOW, n_tok // SC_SUM_WINDOW)
        out = _final(done // TOK_TILE, n_tok // TOK_TILE, ysum.reshape(n_tok * ROW_TILES, LANES),
                     x1, mod3, g_post_ffn, out)
        done += n_tok
    return out.reshape(BATCH, SEQ, D_MODEL)
```

```python
import functools
import math

import jax
import jax.numpy as jnp
from jax import lax
from jax.experimental import pallas as pl
from jax.experimental.pallas import tpu as pltpu
from jax.experimental.pallas import tpu_sc as plsc

F32 = jnp.float32
BF16 = jnp.bfloat16
I32 = jnp.int32
U32 = jnp.uint32

D_MODEL = 1024
BATCH = 4
SEQ = 4096
N_TOK = BATCH * SEQ
D_ATTN = 512
HEAD_DIM_A = 64
ATT_BLOCK = 128
DIL_MID = 4
DIL_MAX = 16
SUB_MID = SEQ // DIL_MID
SUB_MAX = SEQ // DIL_MAX
D_REC = 512
HGRN_CHUNK = 32
HGRN_SUPER = 256
N_EXPERTS = 32
TOP_K = 4
D_FF = 1024
SWIGLU_LIMIT = 7.0
SWIGLU_ALPHA = 1.702
EPS = 1e-6
NEG_BIG = -1e30
Q_SCALE = HEAD_DIM_A ** -0.5 * math.log2(math.e)

LANES = 128
SUBLANES = 8
ROW_TILES = D_MODEL // 2 // LANES

TOK_TILE = 512
MOE_BLOCK = 1024
MOE_ROWS_STEP = 128
N_SLOTS = N_TOK * TOP_K + N_EXPERTS * MOE_BLOCK
N_BLOCKS = N_SLOTS // MOE_BLOCK
SC_WINDOW = 128
SC_SUM_WINDOW = 8
SC_SUM_DEPTH = 4
FINISH_UNIT = 1024
FINISH_PIECES = (4, 4, 4, 4)
SC_LANES = 16
VMEM_LIMIT = 56 * 1024 * 1024


def _sigmoid(x):
    return 1.0 / (1.0 + jnp.exp(-x))


def _dot(a, b):
    return jnp.dot(a, b, preferred_element_type=F32)


def _dot_nt(a, b):
    return lax.dot_general(a, b, (((1,), (1,)), ((), ())), preferred_element_type=F32)


def _split_bf16(x):
    hi = x.astype(BF16)
    return hi, (x - hi.astype(F32)).astype(BF16)


def _rms(x):
    return x * lax.rsqrt(jnp.mean(x * x, axis=-1, keepdims=True) + EPS)


def _ada_kernel(c_ref, w_ref, b_ref, o_ref):
    c = c_ref[...]
    cond = c * _sigmoid(c)
    c_hi, c_lo = _split_bf16(cond)
    w_hi, w_lo = _split_bf16(w_ref[...])
    o_ref[...] = _dot(c_hi, w_hi) + _dot(c_lo, w_hi) + _dot(c_hi, w_lo) + b_ref[...]


def _ada_mod(c_pad, w_ada, b_ada):
    n = w_ada.shape[1]
    tn = 1536
    return pl.pallas_call(
        _ada_kernel,
        grid=(n // tn,),
        in_specs=[pl.BlockSpec((SUBLANES, D_MODEL), lambda j: (0, 0)),
                  pl.BlockSpec((D_MODEL, tn), lambda j: (0, j)),
                  pl.BlockSpec((1, tn), lambda j: (0, j))],
        out_specs=pl.BlockSpec((SUBLANES, tn), lambda j: (0, j)),
        out_shape=jax.ShapeDtypeStruct((SUBLANES, n), F32),
        compiler_params=pltpu.CompilerParams(vmem_limit_bytes=VMEM_LIMIT),
        name="ada_mod",
    )(c_pad, w_ada, b_ada)


def _cast_weights_once(w_ref, w16):
    rows = 128

    @pl.when(pl.program_id(0) == 0)
    def _():
        def cast(r, carry):
            sl = pl.ds(pl.multiple_of(r * rows, rows), rows)
            w16[sl, :] = w_ref[sl, :].astype(BF16)
            return carry

        lax.fori_loop(0, w_ref.shape[0] // rows, cast, 0)


def _resident(shape):
    return pl.BlockSpec(shape, lambda i: (0,) * len(shape), pipeline_mode=pl.Buffered(1))


def _inproj_kernel(x_ref, mod_ref, g_ref, w_ref, q4, k4, v4, q16, k16, v16, qr, fr, ir, gr,
                   w16, stage_nat, stage_mid):
    _cast_weights_once(w_ref, w16)
    mod = mod_ref[0]
    shift, scale = mod[0:1], mod[1:2]
    h = _rms(x_ref[...]) * (g_ref[...] * (1.0 + scale)) + shift
    hb = h.astype(BF16)
    slabs = D_ATTN // LANES
    rows_mid = TOK_TILE // DIL_MID
    rows_max = TOK_TILE // DIL_MAX

    def proj(j):
        return _dot(hb, w16[:, j * D_ATTN:(j + 1) * D_ATTN])

    for j, (o_mid, o_max) in enumerate(((q4, q16), (k4, k16), (v4, v16))):
        r = proj(j)
        if j == 0:
            r = r * Q_SCALE
        for cs in range(slabs):
            stage_nat[cs] = r[:, cs * LANES:(cs + 1) * LANES]
        for cs in range(slabs):
            lanes = slice(cs * LANES, (cs + 1) * LANES)
            for sub in range(DIL_MID):
                piece = stage_nat[cs, pl.ds(sub, rows_mid, stride=DIL_MID), :]
                o_mid[sub, :, lanes] = piece.astype(BF16)
                stage_mid[cs, sub] = piece
            for sub in range(DIL_MAX):
                piece = stage_mid[cs, sub % DIL_MID, pl.ds(sub // DIL_MID, rows_max, stride=DIL_MID), :]
                o_max[sub, :, lanes] = piece.astype(BF16)
    for j, o_ref in enumerate((qr, fr, ir, gr)):
        o_ref[...] = proj(3 + j)


def _inproj(x2, mod3, g_pre, w_in):
    tiles_per_seq = SEQ // TOK_TILE
    rows_mid = TOK_TILE // DIL_MID
    rows_max = TOK_TILE // DIL_MAX
    mid = pl.BlockSpec((None, DIL_MID, rows_mid, D_ATTN),
                       lambda i: (i // tiles_per_seq, 0, i % tiles_per_seq, 0))
    mx = pl.BlockSpec((None, DIL_MAX, rows_max, D_ATTN),
                      lambda i: (i // tiles_per_seq, 0, i % tiles_per_seq, 0))
    nat = pl.BlockSpec((TOK_TILE, D_REC), lambda i: (i, 0))
    mid_shape = jax.ShapeDtypeStruct((BATCH, DIL_MID, SUB_MID, D_ATTN), BF16)
    mx_shape = jax.ShapeDtypeStruct((BATCH, DIL_MAX, SUB_MAX, D_ATTN), BF16)
    nat_shape = jax.ShapeDtypeStruct((N_TOK, D_REC), F32)
    return pl.pallas_call(
        _inproj_kernel,
        grid=(N_TOK // TOK_TILE,),
        in_specs=[pl.BlockSpec((TOK_TILE, D_MODEL), lambda i: (i, 0)),
                  pl.BlockSpec((1, 6, D_MODEL), lambda i: (i // tiles_per_seq, 0, 0)),
                  pl.BlockSpec((1, D_MODEL), lambda i: (0, 0)),
                  _resident(w_in.shape)],
        out_specs=[mid, mid, mid, mx, mx, mx, nat, nat, nat, nat],
        out_shape=[mid_shape] * 3 + [mx_shape] * 3 + [nat_shape] * 4,
        scratch_shapes=[pltpu.VMEM(w_in.shape, BF16),
                        pltpu.VMEM((D_ATTN // LANES, TOK_TILE, LANES), F32),
                        pltpu.VMEM((D_ATTN // LANES, DIL_MID, rows_mid, LANES), F32)],
        compiler_params=pltpu.CompilerParams(dimension_semantics=("arbitrary",),
                                             vmem_limit_bytes=VMEM_LIMIT),
        name="inproj",
    )(x2, mod3, g_pre, w_in)


def _attn_kernel(q4_ref, k4_ref, v4_ref, q16_ref, k16_ref, v16_ref, nw_ref, o_ref,
                 sel_scr, keep_scr, o_scr, m_scr, l_scr):
    bw = ATT_BLOCK
    lane = lax.broadcasted_iota(I32, (bw, LANES), 1)
    head0 = lane < HEAD_DIM_A
    head_masks = (jnp.where(head0, 1.0, 0.0).astype(BF16), jnp.where(head0, 0.0, 1.0).astype(BF16))

    rr = lax.broadcasted_iota(I32, (bw, bw), 0)
    cc = lax.broadcasted_iota(I32, (bw, bw), 1)
    piece = bw // DIL_MID
    rp = DIL_MID * (rr % piece) + rr // piece
    cp = DIL_MID * (cc % piece) + cc // piece
    for var, (r, c) in enumerate(((rr, cc), (rr, cc), (rp, cp), (rp, cp))):
        first = var % 2 == 1
        left = (c < r) if first else (c >= r)
        right = jnp.zeros_like(left) if first else jnp.logical_not(left)
        sel_scr[var] = jnp.where(left, 1.0, 0.0)
        keep_scr[var, 0] = jnp.where(left, 1.0, 0.0).astype(BF16)
        keep_scr[var, 1] = jnp.where(right, 1.0, 0.0).astype(BF16)

    def score_matmuls(loaded):
        return [_dot_nt(q * hm, k) for (q, k, _, _, _) in loaded for hm in head_masks]

    def finish_group(scores, loaded):
        probs = []
        for n, s in enumerate(scores):
            _, _, _, var, right_bias = loaded[n // 2]
            folded = jnp.where(sel_scr[var] > 0.5, s[:, 0:bw], s[:, bw:] + right_bias)
            m = jnp.max(folded, axis=-1, keepdims=True)
            p = jnp.exp2(folded - m)
            pb = p.astype(BF16)
            spread = jnp.concatenate([pb * keep_scr[var, 0], pb * keep_scr[var, 1]], axis=1)
            probs.append((spread, jnp.sum(p, axis=-1, keepdims=True), m))
        pvs = [_dot(p, loaded[n // 2][2]) for n, (p, _, _) in enumerate(probs)]
        results = []
        for u in range(len(loaded)):
            (_, l0, m0), (_, l1, m1) = probs[2 * u], probs[2 * u + 1]
            results.append((jnp.where(head0, pvs[2 * u], pvs[2 * u + 1]),
                            jnp.where(head0, jnp.broadcast_to(m0, (bw, LANES)),
                                      jnp.broadcast_to(m1, (bw, LANES))),
                            jnp.where(head0, jnp.broadcast_to(l0, (bw, LANES)),
                                      jnp.broadcast_to(l1, (bw, LANES)))))
        return results

    group = 8

    def first_block_bias(blk):
        return jnp.where(blk == 0, NEG_BIG, 0.0)

    def load_d1(g):
        loaded, starts = [], []
        for i in range(group):
            blk = g * group + i
            qs = pl.multiple_of(blk * piece, piece)
            ks = pl.multiple_of(jnp.maximum(blk - 1, 0) * piece, piece)
            q = jnp.concatenate([q4_ref[r, pl.ds(qs, piece), :] for r in range(DIL_MID)], axis=0)
            k = jnp.concatenate([k4_ref[r, pl.ds(ks + half * piece, piece), :]
                                 for half in range(2) for r in range(DIL_MID)], axis=0)
            v = jnp.concatenate([v4_ref[r, pl.ds(ks + half * piece, piece), :]
                                 for half in range(2) for r in range(DIL_MID)], axis=0)
            loaded.append((q, k, v, jnp.where(blk == 0, 3, 2), first_block_bias(blk)))
            starts.append(qs)

        def store(results):
            for qs, parts in zip(starts, results):
                for r in range(DIL_MID):
                    for scr, val in zip((o_scr, m_scr, l_scr), parts):
                        scr[0, r, pl.ds(qs, piece), :] = val[r * piece:(r + 1) * piece]

        return loaded, store

    def load_d4(g):
        loaded, dsts = [], []
        for i in range(group):
            r, blk = i % DIL_MID, g * (group // DIL_MID) + i // DIL_MID
            qs = pl.multiple_of(blk * bw, bw)
            ks = pl.multiple_of(jnp.maximum(blk - 1, 0) * bw, bw)
            loaded.append((q4_ref[r, pl.ds(qs, bw), :], k4_ref[r, pl.ds(ks, 2 * bw), :],
                           v4_ref[r, pl.ds(ks, 2 * bw), :], jnp.where(blk == 0, 1, 0),
                           first_block_bias(blk)))
            dsts.append((r, qs))

        def store(results):
            for (r, qs), parts in zip(dsts, results):
                for scr, val in zip((o_scr, m_scr, l_scr), parts):
                    scr[1, r, pl.ds(qs, bw), :] = val

        return loaded, store

    def load_d16(g):
        loaded, dsts = [], []
        for i in range(group):
            r, blk = g * (group // 2) + i // 2, i % 2
            loaded.append((q16_ref[r, blk * bw:(blk + 1) * bw, :], k16_ref[r], v16_ref[r],
                           1 - blk, NEG_BIG if blk == 0 else 0.0))
            dsts.append((r % DIL_MID, pl.ds(blk * (bw * DIL_MID) + r // DIL_MID, bw, stride=DIL_MID)))

        def store(results):
            for (sub, dst), parts in zip(dsts, results):
                for scr, val in zip((o_scr, m_scr, l_scr), parts):
                    scr[2, sub, dst, :] = val

        return loaded, store

    def run_group(load):
        def step(g, carry):
            loaded, store = load(g)
            store(finish_group(score_matmuls(loaded), loaded))
            return carry
        return step

    n_groups = SEQ // bw // group
    for load in (load_d1, load_d4, load_d16):
        lax.fori_loop(0, n_groups, run_group(load), 0)

    rows = 256
    hi = lax.broadcasted_iota(I32, (LANES, LANES), 0) // HEAD_DIM_A
    hj = lax.broadcasted_iota(I32, (LANES, LANES), 1) // HEAD_DIM_A
    head_sum = jnp.where(hi == hj, 1.0, 0.0).astype(BF16)
    nw = nw_ref[...]

    def head_sums(x):
        x_hi, x_lo = _split_bf16(x)
        return _dot(x_hi, head_sum) + _dot(x_lo, head_sum)

    def merge(t, carry):
        r = t // (SUB_MID // rows)
        start = pl.multiple_of((t % (SUB_MID // rows)) * rows, rows)
        sl = pl.ds(start, rows)
        own_score = head_sums(q4_ref[r, sl, :].astype(F32) * k4_ref[r, sl, :].astype(F32))
        ms = [m_scr[n, r, sl, :] for n in range(3)]
        mx = jnp.maximum(jnp.maximum(jnp.maximum(ms[0], ms[1]), ms[2]), own_score)
        ws = [jnp.exp2(m - mx) for m in ms]
        w_own = float(len(ms)) * jnp.exp2(own_score - mx)
        num = (ws[0] * o_scr[0, r, sl, :] + ws[1] * o_scr[1, r, sl, :] + ws[2] * o_scr[2, r, sl, :]
               + w_own * v4_ref[r, sl, :].astype(F32))
        den = ws[0] * l_scr[0, r, sl, :] + ws[1] * l_scr[1, r, sl, :] + ws[2] * l_scr[2, r, sl, :] + w_own
        o = num / den
        mean_sq = head_sums(o * o) * (1.0 / HEAD_DIM_A)
        o_ref[pl.ds(start * DIL_MID + r, rows, stride=DIL_MID), :] = o * lax.rsqrt(mean_sq + EPS) * nw
        return carry

    lax.fori_loop(0, SEQ // rows, merge, 0, unroll=8)


def _attention(q4, k4, v4, q16, k16, v16, attn_norm_w):
    hp = D_ATTN // LANES
    mid = pl.BlockSpec((None, DIL_MID, SUB_MID, LANES), lambda b, h: (b, 0, 0, h))
    mx = pl.BlockSpec((None, DIL_MAX, SUB_MAX, LANES), lambda b, h: (b, 0, 0, h))
    scr = pltpu.VMEM((3, DIL_MID, SUB_MID, LANES), F32)
    return pl.pallas_call(
        _attn_kernel,
        grid=(BATCH, hp),
        in_specs=[mid, mid, mid, mx, mx, mx, pl.BlockSpec((1, LANES), lambda b, h: (0, h))],
        out_specs=pl.BlockSpec((None, SEQ, LANES), lambda b, h: (b, 0, h)),
        out_shape=jax.ShapeDtypeStruct((BATCH, SEQ, D_ATTN), F32),
        scratch_shapes=[pltpu.VMEM((4, ATT_BLOCK, ATT_BLOCK), F32),
                        pltpu.VMEM((4, 2, ATT_BLOCK, ATT_BLOCK), BF16), scr, scr, scr],
        compiler_params=pltpu.CompilerParams(vmem_limit_bytes=VMEM_LIMIT),
        name="dilated_attn",
    )(q4, k4, v4, q16, k16, v16, attn_norm_w)


def _hgrn_kernel(qr_ref, fr_ref, ir_ref, gr_ref, lb_ref, nw_ref, o_ref,
                 qe_scr, oi_scr, delta_scr, dec_scr, st_scr):
    sup, c = HGRN_SUPER, HGRN_CHUNK
    nch = sup // c
    n_sup = SEQ // sup
    lbp = lb_ref[...]
    lmx = jnp.max(lbp, axis=0, keepdims=True)
    ex = jnp.exp(lbp - lmx)
    lb = ex[0:1] / (ex[0:1] + ex[1:2])
    nw = nw_ref[...]

    ri = lax.broadcasted_iota(I32, (sup, sup), 0)
    ci = lax.broadcasted_iota(I32, (sup, sup), 1)
    same_chunk = (ri // c) == (ci // c)
    causal = same_chunk & (ci <= ri)
    cum_op = jnp.where(causal, 1.0, 0.0).astype(BF16)

    def chunk_rows(kd, ch):
        parts = []
        if ch > 0:
            parts.append(jnp.zeros((ch * c, LANES), BF16))
        parts.append(kd[ch * c:(ch + 1) * c])
        if ch < nch - 1:
            parts.append(jnp.zeros(((nch - 1 - ch) * c, LANES), BF16))
        return jnp.concatenate(parts, axis=0)

    group = 4

    def independent(g, carry):
        ts = [g * group + i for i in range(group)]
        sls = [pl.ds(pl.multiple_of(t * sup, sup), sup) for t in ts]
        pre = []
        for sl in sls:
            f = lb + (1.0 - lb) * _sigmoid(fr_ref[sl, :])
            logf_hi, logf_lo = _split_bf16(jnp.log(f))
            pre.append((1.0 - f, _dot(cum_op, logf_hi) + _dot(cum_op, logf_lo)))
        mid = []
        for t, sl, (kk, b) in zip(ts, sls, pre):
            b_last = jnp.concatenate(
                [jnp.broadcast_to(b[(ch + 1) * c - 1:(ch + 1) * c], (c, LANES)) for ch in range(nch)],
                axis=0)
            q = qr_ref[sl, :]
            qeb = (q * _sigmoid(q) * jnp.exp(b)).astype(BF16)
            ke = (kk * jnp.exp(-b)).astype(BF16)
            kd = (kk * jnp.exp(b_last - b)).astype(BF16)
            qe_scr[sl, :] = qeb
            dec_rows = jnp.concatenate([b_last[ch * c:ch * c + 1] for ch in range(nch)], axis=0)
            dec_scr[pl.ds(pl.multiple_of(t * nch, nch), nch), :] = jnp.exp(dec_rows)
            v = ir_ref[sl, :]
            vt = v.T.astype(BF16)
            scores = _dot_nt(qeb, ke)
            for pair in range(nch // 2):
                rhs = jnp.concatenate([chunk_rows(kd, 2 * pair), chunk_rows(kd, 2 * pair + 1)], axis=1)
                d2 = _dot(vt, rhs)
                delta_scr[t * nch + 2 * pair] = d2[:, 0:LANES]
                delta_scr[t * nch + 2 * pair + 1] = d2[:, LANES:]
            mid.append((scores, v.astype(BF16)))
        for sl, (scores, vb) in zip(sls, mid):
            a = jnp.where(causal, scores, 0.0)
            oi_scr[sl, :] = _dot(a.astype(BF16), vb)
        return carry

    lax.fori_loop(0, n_sup // group, independent, 0)

    def recur(ch, st):
        st_scr[ch] = st.astype(BF16)
        return st * dec_scr[pl.ds(ch, 1), :] + delta_scr[ch]

    lax.fori_loop(0, SEQ // c, recur, jnp.zeros((LANES, LANES), F32), unroll=8)

    def finish(t, carry):
        sl = pl.ds(pl.multiple_of(t * sup, sup), sup)
        parts = [_dot_nt(qe_scr[pl.ds(pl.multiple_of(t * sup + ch * c, c), c), :], st_scr[t * nch + ch])
                 for ch in range(nch)]
        o = oi_scr[sl, :] + jnp.concatenate(parts, axis=0)
        g = gr_ref[sl, :]
        o_ref[sl, :] = (_rms(o) * nw * (g * _sigmoid(g))).astype(o_ref.dtype)
        return carry

    lax.fori_loop(0, n_sup, finish, 0, unroll=8)


def _hgrn(qr, fr, ir, gr, hgrn_lb, hgrn_norm_w):
    nh = D_REC // LANES
    n_chunks = SEQ // HGRN_CHUNK
    blk = pl.BlockSpec((None, SEQ, LANES), lambda b, h: (b, 0, h))
    return pl.pallas_call(
        _hgrn_kernel,
        grid=(BATCH, nh),
        in_specs=[blk, blk, blk, blk,
                  pl.BlockSpec((2, LANES), lambda b, h: (0, h)),
                  pl.BlockSpec((1, LANES), lambda b, h: (0, h))],
        out_specs=blk,
        out_shape=jax.ShapeDtypeStruct((BATCH, SEQ, D_REC), BF16),
        scratch_shapes=[pltpu.VMEM((SEQ, LANES), BF16),
                        pltpu.VMEM((SEQ, LANES), F32),
                        pltpu.VMEM((n_chunks, LANES, LANES), F32),
                        pltpu.VMEM((n_chunks, LANES), F32),
                        pltpu.VMEM((n_chunks, LANES, LANES), BF16)],
        compiler_params=pltpu.CompilerParams(vmem_limit_bytes=VMEM_LIMIT),
        name="hgrn2",
    )(qr, fr, ir, gr, hgrn_lb, hgrn_norm_w)


def _tile_batch(i):
    return i // (SEQ // TOK_TILE)


def _store_row_tiles(ref, val):
    rows, half = val.shape[0], D_MODEL // 2
    lo = lax.bitcast_convert_type(val[:, :half], U32)
    hi = lax.bitcast_convert_type(val[:, half:], U32)
    words = lax.shift_right_logical(lo, jnp.uint32(16)) | (hi & jnp.uint32(0xFFFF0000))
    for j in range(ROW_TILES):
        ref[pl.ds(j, rows, stride=ROW_TILES), :] = words[:, j * LANES:(j + 1) * LANES]


def _load_row_tiles(ref, rows):
    words = jnp.concatenate(
        [ref[pl.ds(j, rows, stride=ROW_TILES), :] for j in range(ROW_TILES)], axis=1)
    lo = lax.bitcast_convert_type(lax.shift_left(words, jnp.uint32(16)), F32)
    hi = lax.bitcast_convert_type(words & jnp.uint32(0xFFFF0000), F32)
    return jnp.concatenate([lo, hi], axis=1)


def _mid_kernel(ya_ref, yr_ref, x_ref, mod_ref, gpost_ref, gpre_ref, wo_ref, wr_ref, br_ref,
                x1_ref, h2_ref, idx_ref, gate_ref, rank_ref, cnt_ref, carry_ref, wo16):
    i = pl.program_id(0)
    _cast_weights_once(wo_ref, wo16)

    @pl.when(i == 0)
    def _():
        carry_ref[...] = jnp.zeros_like(carry_ref)

    mod = mod_ref[0]
    gate_m, shift_f, scale_f = mod[2:3], mod[3:4], mod[4:5]
    y = _dot(ya_ref[...].astype(BF16), wo16[0:D_ATTN, :]) + _dot(yr_ref[...], wo16[D_ATTN:, :])
    x1 = x_ref[...] + _rms(y) * (gate_m * gpost_ref[...])
    x1_ref[...] = x1
    h2 = _rms(x1) * (gpre_ref[...] * (1.0 + scale_f)) + shift_f
    h2_hi = h2.astype(BF16)
    h2_rounded = h2_hi.astype(F32)
    _store_row_tiles(h2_ref, h2_rounded)

    tm = h2.shape[0]
    h2_lo = (h2 - h2_rounded).astype(BF16)
    wr_hi, wr_lo = _split_bf16(wr_ref[...])
    parts = jnp.concatenate([_dot(h2_hi, jnp.concatenate([wr_hi, wr_lo], axis=1)),
                             _dot(h2_lo, wr_hi),
                             jnp.broadcast_to(br_ref[...], (tm, N_EXPERTS))], axis=1).T
    work = sum(parts[g * N_EXPERTS:(g + 1) * N_EXPERTS] for g in range(LANES // N_EXPERTS))
    eidx = lax.broadcasted_iota(I32, (N_EXPERTS, tm), 0).astype(F32)
    vals, idxs = [], []
    onehot = jnp.zeros((N_EXPERTS, tm), F32)
    for _ in range(TOP_K):
        m = jnp.max(work, axis=0, keepdims=True)
        sel = jnp.min(jnp.where(work == m, eidx, float(N_EXPERTS)), axis=0, keepdims=True)
        hit = eidx == sel
        work = jnp.where(hit, -jnp.inf, work)
        onehot = jnp.where(hit, 1.0, onehot)
        vals.append(m)
        idxs.append(sel)
    ex = [jnp.exp(vv - vals[0]) for vv in vals]
    inv_den = 1.0 / (ex[0] + ex[1] + ex[2] + ex[3])

    ri = lax.broadcasted_iota(I32, (tm, tm), 0)
    ci = lax.broadcasted_iota(I32, (tm, tm), 1)
    strict_upper = jnp.where(ri < ci, 1.0, 0.0).astype(BF16)
    before = _dot(onehot.astype(BF16), strict_upper) + carry_ref[...]
    ranks = [jnp.sum(jnp.where(eidx == idxs[kk], before, 0.0), axis=0, keepdims=True)
             for kk in range(TOP_K)]
    idx_ref[...] = jnp.concatenate(idxs, axis=0).astype(I32)
    rank_ref[...] = jnp.concatenate(ranks, axis=0).astype(I32)
    gate_ref[...] = jnp.concatenate([e * inv_den for e in ex], axis=0)
    total = carry_ref[...] + jnp.sum(onehot, axis=1, keepdims=True)
    carry_ref[...] = total
    cnt_ref[...] = total.astype(I32)


def _mid(ya, yr, x2, mod3, g_post, g_pre, w_out, w_router, b_router):
    tok = lambda w: pl.BlockSpec((TOK_TILE, w), lambda i: (i, 0))
    const = lambda s: pl.BlockSpec(s, lambda i: (0,) * len(s))
    lanes_tok = pl.BlockSpec((TOP_K, TOK_TILE), lambda i: (0, i))
    return pl.pallas_call(
        _mid_kernel,
        grid=(N_TOK // TOK_TILE,),
        in_specs=[tok(D_ATTN), tok(D_REC), tok(D_MODEL),
                  pl.BlockSpec((1, 6, D_MODEL), lambda i: (_tile_batch(i), 0, 0)),
                  const((1, D_MODEL)), const((1, D_MODEL)),
                  _resident((D_MODEL, D_MODEL)), const((D_MODEL, N_EXPERTS)), const((1, N_EXPERTS))],
        out_specs=[tok(D_MODEL),
                   pl.BlockSpec((TOK_TILE * ROW_TILES, LANES), lambda i: (i, 0)),
                   lanes_tok, lanes_tok, lanes_tok, const((N_EXPERTS, 1))],
        out_shape=[jax.ShapeDtypeStruct((N_TOK, D_MODEL), F32),
                   jax.ShapeDtypeStruct((N_TOK * ROW_TILES, LANES), U32),
                   jax.ShapeDtypeStruct((TOP_K, N_TOK), I32),
                   jax.ShapeDtypeStruct((TOP_K, N_TOK), F32),
                   jax.ShapeDtypeStruct((TOP_K, N_TOK), I32),
                   jax.ShapeDtypeStruct((N_EXPERTS, 1), I32)],
        scratch_shapes=[pltpu.VMEM((N_EXPERTS, 1), F32), pltpu.VMEM((D_MODEL, D_MODEL), BF16)],
        compiler_params=pltpu.CompilerParams(dimension_semantics=("arbitrary",),
                                             vmem_limit_bytes=VMEM_LIMIT),
        name="outproj_router",
    )(ya, yr, x2, mod3, g_post, g_pre, w_out, w_router, b_router)


def _sc_mesh():
    return plsc.VectorSubcoreMesh(core_axis_name="c", subcore_axis_name="s")


def _sc_worker_count():
    info = plsc.get_sparse_core_info()
    return info.num_cores, info.num_cores * info.num_subcores


def _sc_dispatch(h_rows, dest_win):
    n_cores, n_workers = _sc_worker_count()
    n_win = N_TOK // SC_WINDOW
    per_worker = n_win // n_workers

    @functools.partial(
        pl.kernel, mesh=_sc_mesh(),
        out_type=jax.ShapeDtypeStruct((N_SLOTS, ROW_TILES, LANES), U32),
        scratch_types=[pltpu.VMEM((TOP_K, SC_WINDOW), I32),
                       pltpu.VMEM((SC_WINDOW, ROW_TILES, LANES), U32),
                       pltpu.SemaphoreType.DMA],
        name="sc_dispatch")
    def run(h_hbm, dest_hbm, xs_hbm, idx_v, rows_v, sem):
        wid = lax.axis_index("s") * n_cores + lax.axis_index("c")

        @pl.loop(0, per_worker)
        def _(j):
            win = wid * per_worker + j
            pltpu.sync_copy(dest_hbm.at[win], idx_v)
            pltpu.sync_copy(h_hbm.at[pl.ds(win * SC_WINDOW, SC_WINDOW)], rows_v)
            copies = [pltpu.async_copy(rows_v, xs_hbm.at[idx_v.at[kk]], sem)
                      for kk in range(TOP_K)]
            for cp in copies:
                cp.wait()

    return run(h_rows, dest_win)


def _sc_combine(y_rows, dest_win, gate_words, first_win, n_win):
    n_cores, n_workers = _sc_worker_count()
    w = dest_win.shape[2]
    per_worker = n_win // n_workers
    depth = SC_SUM_DEPTH
    assert per_worker * n_workers == n_win and per_worker % depth == 0 and depth % 2 == 0, (n_win, n_workers)

    @functools.partial(
        pl.kernel, mesh=_sc_mesh(),
        out_type=jax.ShapeDtypeStruct((n_win * w, ROW_TILES, LANES), U32),
        scratch_types=[pltpu.VMEM((per_worker, TOP_K, w), I32),
                       pltpu.VMEM((per_worker, TOP_K, w * SC_LANES), U32),
                       pltpu.VMEM((depth, TOP_K, w, ROW_TILES, LANES), U32),
                       pltpu.VMEM((2, w, ROW_TILES, LANES), U32),
                       pltpu.SemaphoreType.DMA((depth,)),
                       pltpu.SemaphoreType.DMA((2,))],
        compiler_params=pltpu.CompilerParams(needs_layout_passes=False),
        name="sc_combine")
    def run(y_hbm, dest_hbm, gate_hbm, out_hbm, idx_v, gate_v, rows_v, out_v, in_sems, out_sems):
        wid = lax.axis_index("s") * n_cores + lax.axis_index("c")
        first = wid * per_worker
        pltpu.sync_copy(dest_hbm.at[pl.ds(first_win + first, per_worker)], idx_v)
        pltpu.sync_copy(gate_hbm.at[pl.ds(first_win + first, per_worker)], gate_v)

        def gather(it, slot):
            return [pltpu.make_async_copy(y_hbm.at[idx_v.at[it, kk]], rows_v.at[slot, kk], in_sems.at[slot])
                    for kk in range(TOP_K)]

        def put(it, slot):
            return pltpu.make_async_copy(out_v.at[slot], out_hbm.at[pl.ds((first + it) * w, w)],
                                         out_sems.at[slot])

        def step(it, slot):
            @pl.when(it + depth - 1 < per_worker)
            def _():
                for cp in gather(it + depth - 1, (slot + depth - 1) % depth):
                    cp.start()

            for cp in gather(it, slot):
                cp.wait()

            slot2 = slot % 2

            @pl.when(it >= 2)
            def _():
                put(it - 2, slot2).wait()

            @plsc.parallel_loop(0, w)
            def _(j):
                own = pl.ds(pl.multiple_of(j * SC_LANES, SC_LANES), SC_LANES)
                gates = [plsc.bitcast(gate_v[it, kk, own], BF16) for kk in range(TOP_K)]
                for t in range(ROW_TILES):
                    for c in range(LANES // SC_LANES):
                        lanes = pl.ds(c * SC_LANES, SC_LANES)
                        v = [plsc.bitcast(rows_v[slot, kk, j, t, lanes], BF16) * gates[kk]
                             for kk in range(TOP_K)]
                        out_v[slot2, j, t, lanes] = plsc.bitcast((v[0] + v[1]) + (v[2] + v[3]), U32)

            put(it, slot2).start()

        for ahead in range(depth - 1):
            for cp in gather(ahead, ahead):
                cp.start()

        @pl.loop(0, per_worker // depth)
        def _(p):
            for slot in range(depth):
                step(depth * p + slot, slot)

        put(per_worker - 2, 0).wait()
        put(per_worker - 1, 1).wait()

    return run(y_rows, dest_win, gate_words)


def _expert_kernel(be_ref, nu_ref, nx_ref, nv_ref, x_ref, wgu_hbm, bgu_ref, wd_hbm, bd_ref, y_ref,
                   wgu32, wd32, wgu16, wd16, sems):
    i = pl.program_id(0)
    e = be_ref[i]
    prev = be_ref[jnp.maximum(i - 1, 0)]

    def weight_copies(ex):
        return (pltpu.make_async_copy(wgu_hbm.at[ex], wgu32, sems.at[0]),
                pltpu.make_async_copy(wd_hbm.at[ex], wd32, sems.at[1]))

    @pl.when(i == 0)
    def _():
        for thread, cp in enumerate(weight_copies(e)):
            cp.start(priority=thread)

    @pl.when((i == 0) | (e != prev))
    def _():
        for cp in weight_copies(e):
            cp.wait()
        rows = 128

        def cast(r, carry):
            sl = pl.ds(pl.multiple_of(r * rows, rows), rows)
            wgu16[sl, :] = wgu32[sl, :].astype(BF16)
            wd16[sl, :] = wd32[sl, :].astype(BF16)
            return carry

        lax.fori_loop(0, D_MODEL // rows, cast, 0)
        nxt = nx_ref[i]

        @pl.when(nxt >= 0)
        def _():
            for thread, cp in enumerate(weight_copies(nxt)):
                cp.start(priority=thread)

    def run_rows(rows):
        x = _load_row_tiles(x_ref, rows).astype(BF16)
        bgu = bgu_ref[0]
        glu = _dot(x, wgu16[:, 0:D_FF]) + bgu[:, 0:D_FF]
        lin = _dot(x, wgu16[:, D_FF:]) + bgu[:, D_FF:]
        glu = jnp.minimum(glu, SWIGLU_LIMIT)
        lin = jnp.clip(lin, -SWIGLU_LIMIT, SWIGLU_LIMIT)
        act = glu * _sigmoid(SWIGLU_ALPHA * glu) * (lin + 1.0)
        y = _dot(act.astype(BF16), wd16[...]) + bd_ref[0]
        _store_row_tiles(y_ref, y.astype(BF16).astype(F32))

    valid = nv_ref[i]
    for rows in range(MOE_ROWS_STEP, MOE_BLOCK + 1, MOE_ROWS_STEP):
        pl.when((valid > rows - MOE_ROWS_STEP) & (valid <= rows))(functools.partial(run_rows, rows))


def _experts(blk_e, n_used, next_e, blk_valid, xs2, w_gu, b_gu3, w_down, b_down3):
    row_blk = pl.BlockSpec((MOE_BLOCK * ROW_TILES, LANES),
                           lambda i, be, nu, nx, nv: (jnp.minimum(i, nu[0] - 1), 0))
    grid_spec = pltpu.PrefetchScalarGridSpec(
        num_scalar_prefetch=4,
        grid=(N_BLOCKS,),
        in_specs=[row_blk,
                  pl.BlockSpec(memory_space=pl.ANY),
                  pl.BlockSpec((1, 1, 2 * D_FF), lambda i, be, nu, nx, nv: (be[i], 0, 0)),
                  pl.BlockSpec(memory_space=pl.ANY),
                  pl.BlockSpec((1, 1, D_MODEL), lambda i, be, nu, nx, nv: (be[i], 0, 0))],
        out_specs=row_blk,
        scratch_shapes=[pltpu.VMEM((D_MODEL, 2 * D_FF), F32),
                        pltpu.VMEM((D_FF, D_MODEL), F32),
                        pltpu.VMEM((D_MODEL, 2 * D_FF), BF16),
                        pltpu.VMEM((D_FF, D_MODEL), BF16),
                        pltpu.SemaphoreType.DMA((2,))],
    )
    return pl.pallas_call(
        _expert_kernel,
        grid_spec=grid_spec,
        out_shape=jax.ShapeDtypeStruct((N_SLOTS * ROW_TILES, LANES), U32),
        compiler_params=pltpu.CompilerParams(dimension_semantics=("arbitrary",),
                                             vmem_limit_bytes=VMEM_LIMIT),
        name="experts",
    )(blk_e, n_used, next_e, blk_valid, xs2, w_gu, b_gu3, w_down, b_down3)


def _final_kernel(ysum_ref, x1_ref, mod_ref, gpost_ref, *maybe_alias_and_out):
    o_ref = maybe_alias_and_out[-1]
    y = _load_row_tiles(ysum_ref, TOK_TILE)
    gate_f = mod_ref[0][5:6]
    o_ref[...] = x1_ref[...] + _rms(y) * (gate_f * gpost_ref[...])


def _final(first_tile, tiles, ysum, x1, mod3, g_post, out_so_far):
    tok = lambda w: pl.BlockSpec((TOK_TILE, w), lambda i: (first_tile + i, 0))
    in_specs = [pl.BlockSpec((TOK_TILE * ROW_TILES, LANES), lambda i: (i, 0)),
                tok(D_MODEL),
                pl.BlockSpec((1, 6, D_MODEL), lambda i: (_tile_batch(first_tile + i), 0, 0)),
                pl.BlockSpec((1, D_MODEL), lambda i: (0, 0))]
    args = [ysum, x1, mod3, g_post]
    aliases = {}
    if out_so_far is not None:
        in_specs.append(pl.BlockSpec(memory_space=pl.ANY))
        args.append(out_so_far)
        aliases = {len(args) - 1: 0}
    return pl.pallas_call(
        _final_kernel,
        grid=(tiles,),
        in_specs=in_specs,
        out_specs=tok(D_MODEL),
        out_shape=jax.ShapeDtypeStruct((N_TOK, D_MODEL), F32),
        input_output_aliases=aliases,
        compiler_params=pltpu.CompilerParams(vmem_limit_bytes=VMEM_LIMIT),
        name="combine_final",
    )(*args)


def _slot_kernel(idx_ref, rank_ref, start_ref, dest_ref):
    tn = idx_ref.shape[1]
    experts = lax.broadcasted_iota(I32, (N_EXPERTS, tn), 0)
    starts = start_ref[...].astype(F32)
    rows = [jnp.sum(jnp.where(experts == idx_ref[kk:kk + 1, :], starts, 0.0), axis=0, keepdims=True)
            for kk in range(TOP_K)]
    dest_ref[...] = jnp.concatenate(rows, axis=0).astype(I32) + rank_ref[...]


def _slots(idx, rank, pstart):
    tn = N_TOK // 2
    lanes_tok = pl.BlockSpec((TOP_K, tn), lambda i: (0, i))
    return pl.pallas_call(
        _slot_kernel,
        grid=(N_TOK // tn,),
        in_specs=[lanes_tok, lanes_tok, pl.BlockSpec((N_EXPERTS, 1), lambda i: (0, 0))],
        out_specs=lanes_tok,
        out_shape=jax.ShapeDtypeStruct((TOP_K, N_TOK), I32),
        name="moe_slots",
    )(idx, rank, pstart.reshape(N_EXPERTS, 1))


def _routing_tables(idx, rank, counts):
    counts = counts.reshape(N_EXPERTS)
    experts = jnp.arange(N_EXPERTS, dtype=I32)
    padded = ((counts + MOE_BLOCK - 1) // MOE_BLOCK) * MOE_BLOCK
    pend = jnp.cumsum(padded)
    pstart = pend - padded
    dest = _slots(idx, rank, pstart)
    dest_win = dest.reshape(TOP_K, N_TOK // SC_WINDOW, SC_WINDOW).transpose(1, 0, 2)
    n_used = (pend[-1] // MOE_BLOCK).astype(I32).reshape(1)
    blk_start = jnp.arange(N_BLOCKS, dtype=I32) * MOE_BLOCK
    blk_e = jnp.sum(blk_start[:, None] >= pend[None, :], axis=1).astype(I32)
    last_e = jnp.max(jnp.where(counts > 0, experts, 0))
    blk_e = jnp.minimum(blk_e, last_e)
    later = (experts[None, :] > experts[:, None]) & (counts[None, :] > 0)
    next_nonempty = jnp.min(jnp.where(later, experts[None, :], N_EXPERTS), axis=1)
    next_nonempty = jnp.where(next_nonempty == N_EXPERTS, -1, next_nonempty).astype(I32)
    of_block = lambda table: jnp.sum(jnp.where(blk_e[:, None] == experts, table, 0), axis=1).astype(I32)
    next_e = of_block(next_nonempty)
    blk_valid = jnp.clip(of_block(counts) - (blk_start - of_block(pstart)), 0, MOE_BLOCK)
    blk_valid = jnp.where(blk_start < pend[-1], blk_valid, 0).astype(I32)
    return dest, dest_win, blk_e, n_used, next_e, blk_valid


def kernel(x, c, w_ada, b_ada, g_pre_mix, g_post_mix, w_in, attn_norm_w, hgrn_lb, hgrn_norm_w,
           w_out, g_pre_ffn, g_post_ffn, w_router, b_router, w_gu, b_gu, w_down, b_down):
    c_pad = jnp.pad(c, ((0, SUBLANES - BATCH), (0, 0)))
    mod = _ada_mod(c_pad, w_ada[0], b_ada)
    mod3 = mod[:BATCH].reshape(BATCH, 6, D_MODEL)

    x2 = x.reshape(N_TOK, D_MODEL)
    q4, k4, v4, q16, k16, v16, qr, fr, ir, gr = _inproj(x2, mod3, g_pre_mix, w_in[0])
    nat = lambda t: t.reshape(BATCH, SEQ, D_REC)
    ya = _attention(q4, k4, v4, q16, k16, v16, attn_norm_w)
    yr = _hgrn(nat(qr), nat(fr), nat(ir), nat(gr), hgrn_lb, hgrn_norm_w)

    x1, h2, idx, gates, rank, counts = _mid(
        ya.reshape(N_TOK, D_ATTN), yr.reshape(N_TOK, D_REC), x2, mod3, g_post_mix, g_pre_ffn,
        w_out[0], w_router[0], b_router)

    dest, dest_win, blk_e, n_used, next_e, blk_valid = _routing_tables(idx, rank, counts)
    xs = _sc_dispatch(h2.reshape(N_TOK, ROW_TILES, LANES), dest_win)
    ys = _experts(blk_e, n_used, next_e, blk_valid, xs.reshape(N_SLOTS * ROW_TILES, LANES),
                  w_gu[0], b_gu[0].reshape(N_EXPERTS, 1, 2 * D_FF),
                  w_down[0], b_down[0].reshape(N_EXPERTS, 1, D_MODEL))
    ys3 = ys.reshape(N_SLOTS, ROW_TILES, LANES)
    by_window = lambda t: t.reshape(TOP_K, N_TOK // SC_SUM_WINDOW, SC_SUM_WINDOW).transpose(1, 0, 2)
    sum_win = by_window(dest)
    gate_bits = lax.bitcast_convert_type(gates.astype(BF16), jnp.uint16).astype(U32)
    gate_words = jnp.broadcast_to(by_window(gate_bits | (gate_bits << 16))[..., None],
                                  sum_win.shape + (SC_LANES,)).reshape(sum_win.shape[:2] + (-1,))
    out, done = None, 0
    for units in FINISH_PIECES:
        n_tok = units * FINISH_UNIT
        ysum = _sc_combine(ys3, sum_win, gate_words, done // SC_SUM_WINDOW, n_tok // SC_SUM_WINDOW)
        out = _final(done // TOK_TILE, n_tok // TOK_TILE, ysum.reshape(n_tok * ROW_TILES, LANES),
                     x1, mod3, g_post_ffn, out)
        done += n_tok
    return out.reshape(BATCH, SEQ, D_MODEL)
```
